```python
import math
import jax, jax.numpy as jnp
from jax import lax
import numpy as np

D_MODEL = 2048
BATCH = 1
SEQ = 8192
DEPTH = 2
DEC_BATCH = 8
DEC_SEQ = 64
PAST_LEN = 2048

CHUNK = 64
HEAD_DIM = 128
N_SB_HEADS = D_MODEL // (2 * HEAD_DIM)
N_DSA_HEADS = D_MODEL // (2 * HEAD_DIM)
SB_WIDTH = N_SB_HEADS * HEAD_DIM
DSA_WIDTH = N_DSA_HEADS * HEAD_DIM
MIX_WIDTH = SB_WIDTH + DSA_WIDTH
N_IDX_HEADS = 16
IDX_DIM = 64
TOPK_MAX = 256
N_BUCKETS = 32
MAX_DISTANCE = 128
CONV_WIDTH = 31
D_FF = 4 * D_MODEL
N_MEM = 256
N_CROSS_HEADS = 4
CROSS_HEAD_DIM = 128
CROSS_WIDTH = N_CROSS_HEADS * CROSS_HEAD_DIM
Q_BLOCK = 128
N_AB = (DEPTH + 1) // 2
N_CONV = DEPTH // 2
RMS_EPS = 1e-6
AB_WIDTHS = (SB_WIDTH, SB_WIDTH, SB_WIDTH, DSA_WIDTH, DSA_WIDTH, DSA_WIDTH, N_IDX_HEADS * IDX_DIM, IDX_DIM, N_IDX_HEADS)
AB_SPLITS = tuple(int(s) for s in np.cumsum(AB_WIDTHS)[:-1])
D_IN_AB = int(sum(AB_WIDTHS))
ATTN_SCALE = HEAD_DIM ** -0.5
IDX_SCALE = IDX_DIM ** -0.5
IDX_HEAD_SCALE = N_IDX_HEADS ** -0.5
CROSS_SCALE = CROSS_HEAD_DIM ** -0.5

kernel_name = 'streaming_hybrid_sb_dsa_conformer_step'


def _rms_norm(x, g):
    xf = x.astype(jnp.float32)
    y = xf * lax.rsqrt(jnp.mean(jnp.square(xf), axis=-1, keepdims=True) + RMS_EPS)
    return y.astype(x.dtype) * g


def _t5_bucket(rel):
    nb = N_BUCKETS // 2
    max_exact = nb // 2
    side = jnp.where(rel > 0, nb, 0)
    n = jnp.abs(rel)
    nf = jnp.maximum(n, 1).astype(jnp.float32)
    large = max_exact + (jnp.log(nf / max_exact) / math.log(MAX_DISTANCE / max_exact) * (nb - max_exact)).astype(jnp.int32)
    large = jnp.minimum(large, nb - 1)
    return side + jnp.where(n < max_exact, n, large)


def _mix_ab(h, w_in, w_out, rel_bias, past_sb_k, past_sb_v, past_k, past_v, past_ki, topk):
    B, T, _ = h.shape
    P = past_sb_k.shape[1]
    S = P + T
    proj = jnp.einsum('btd,de->bte', h, w_in)
    q_sb, k_sb, v_sb, q_d, k_d, v_d, q_i, k_i, w_i = jnp.split(proj, AB_SPLITS, axis=-1)
    q_sb = q_sb.reshape(B, T, N_SB_HEADS, HEAD_DIM)
    k_sb = k_sb.reshape(B, T, N_SB_HEADS, HEAD_DIM)
    v_sb = v_sb.reshape(B, T, N_SB_HEADS, HEAD_DIM)
    q_d = q_d.reshape(B, T, N_DSA_HEADS, HEAD_DIM)
    k_d = k_d.reshape(B, T, N_DSA_HEADS, HEAD_DIM)
    v_d = v_d.reshape(B, T, N_DSA_HEADS, HEAD_DIM)
    q_i = q_i.reshape(B, T, N_IDX_HEADS, IDX_DIM)
    ksb_all = jnp.concatenate([past_sb_k, k_sb], axis=1)
    vsb_all = jnp.concatenate([past_sb_v, v_sb], axis=1)
    kd_all = jnp.concatenate([past_k, k_d], axis=1)
    vd_all = jnp.concatenate([past_v, v_d], axis=1)
    ki_all = jnp.concatenate([past_ki, k_i], axis=1)
    q_pos = P + jnp.arange(T, dtype=jnp.int32)
    k_pos = jnp.arange(S, dtype=jnp.int32)
    bq = min(Q_BLOCK, T)
    nb = T // bq

    def blocks(a):
        return a.reshape((B * nb, bq) + a.shape[2:])

    bid = jnp.repeat(jnp.arange(B, dtype=jnp.int32), nb)
    qpos_blk = jnp.tile(q_pos.reshape(nb, bq), (B, 1))

    def one_block(args):
        b, qp, qs, qd, qi, wi = args
        kb = ksb_all[b]
        vb = vsb_all[b]
        z = jnp.einsum('qhd,shd->hqs', qs, kb).astype(jnp.float32) * ATTN_SCALE
        causal = qp[:, None] > k_pos[None, :]
        log_1mb = jnp.where(causal, jax.nn.log_sigmoid(-z), 0.0)
        log_keep = lax.cumsum(log_1mb, axis=2, reverse=True) - log_1mb
        a_sb = jnp.where(causal, jnp.exp(jax.nn.log_sigmoid(z) + log_keep), 0.0)
        o_sb = jnp.einsum('hqs,shd->qhd', a_sb.astype(vb.dtype), vb)
        idx_score = jnp.einsum('qhd,sd->qhs', qi, ki_all[b]) * IDX_SCALE
        index = jnp.einsum('qh,qhs->qs', wi * IDX_HEAD_SCALE, jax.nn.relu(idx_score)).astype(jnp.float32)
        visible = (qp[:, None] // CHUNK) >= (k_pos[None, :] // CHUNK)
        index = jnp.where(visible, index, -jnp.inf)
        _, sel = lax.top_k(index, topk)
        valid = jnp.take_along_axis(visible, sel, axis=1)
        k_sel = kd_all[b][sel]
        v_sel = vd_all[b][sel]
        logits = jnp.einsum('qhd,qkhd->qhk', qd, k_sel).astype(jnp.float32) * ATTN_SCALE
        bias = rel_bias[_t5_bucket(sel - qp[:, None])]
        logits = logits + jnp.transpose(bias, (0, 2, 1)).astype(jnp.float32)
        logits = jnp.where(valid[:, None, :], logits, -jnp.inf)
        p = jax.nn.softmax(logits, axis=-1)
        o_d = jnp.einsum('qhk,qkhd->qhd', p.astype(v_sel.dtype), v_sel)
        return o_sb, o_d

    o_sb, o_d = lax.map(one_block, (bid, qpos_blk, blocks(q_sb), blocks(q_d), blocks(q_i), blocks(w_i)))
    o = jnp.concatenate([o_sb.reshape(B, T, SB_WIDTH), o_d.reshape(B, T, DSA_WIDTH)], axis=-1)
    out = jnp.einsum('bte,ed->btd', o, w_out)
    return out, (k_sb, v_sb, k_d, v_d, k_i)


def _conv_module(h, w_pw1, b_pw1, w_dw, b_dw, g_norm, w_pw2, b_pw2, conv_buf):
    a = jnp.einsum('btd,de->bte', h, w_pw1) + b_pw1
    u = a[..., :D_MODEL] * jax.nn.sigmoid(a[..., D_MODEL:])
    upad = jnp.concatenate([conv_buf, u], axis=1)
    y = lax.conv_general_dilated(upad, w_dw[:, None, :], window_strides=(1,), padding='VALID',
                                 dimension_numbers=('NWC', 'WIO', 'NWC'),
                                 feature_group_count=D_MODEL) + b_dw
    y = jax.nn.silu(_rms_norm(y, g_norm))
    out = jnp.einsum('btd,de->bte', y, w_pw2) + b_pw2
    return out, upad[:, -(CONV_WIDTH - 1):]


def _cross_attention(h, mem_k, mem_v, w_cq, w_co):
    B, T, _ = h.shape
    q = jnp.einsum('btd,de->bte', h, w_cq).reshape(B, T, N_CROSS_HEADS, CROSS_HEAD_DIM)
    logits = jnp.einsum('bthd,bmhd->bhtm', q, mem_k).astype(jnp.float32) * CROSS_SCALE
    p = jax.nn.softmax(logits, axis=-1).astype(mem_v.dtype)
    o = jnp.einsum('bhtm,bmhd->bthd', p, mem_v).reshape(B, T, CROSS_WIDTH)
    return jnp.einsum('bte,ed->btd', o, w_co)


def _sq_relu_mlp(h, w_up, w_down):
    return jnp.einsum('btf,fd->btd', jnp.square(jax.nn.relu(jnp.einsum('btd,df->btf', h, w_up))), w_down)


def setup_inputs(seed: int = 0) -> dict:
    key = jax.random.key(seed)
    ks = iter(jax.random.split(key, 40))

    def nrm(shape, scale):
        return jax.random.normal(next(ks), shape, jnp.float32) * scale

    def gain(shape):
        return 1.0 + nrm(shape, 0.02)

    D = D_MODEL
    return {
        'x_prompt': nrm((BATCH, SEQ, D), 1.0),
        'x_sample': nrm((DEC_BATCH, DEC_SEQ, D), 1.0),
        'mem_prompt': nrm((BATCH, N_MEM, D), 1.0),
        'cache_sb_k': nrm((N_AB, DEC_BATCH, PAST_LEN, N_SB_HEADS, HEAD_DIM), 1.0),
        'cache_sb_v': nrm((N_AB, DEC_BATCH, PAST_LEN, N_SB_HEADS, HEAD_DIM), 1.0),
        'cache_dsa_k': nrm((N_AB, DEC_BATCH, PAST_LEN, N_DSA_HEADS, HEAD_DIM), 1.0),
        'cache_dsa_v': nrm((N_AB, DEC_BATCH, PAST_LEN, N_DSA_HEADS, HEAD_DIM), 1.0),
        'cache_idx_k': nrm((N_AB, DEC_BATCH, PAST_LEN, IDX_DIM), 1.0),
        'cache_mem_k': nrm((DEPTH, DEC_BATCH, N_MEM, N_CROSS_HEADS, CROSS_HEAD_DIM), 1.0),
        'cache_mem_v': nrm((DEPTH, DEC_BATCH, N_MEM, N_CROSS_HEADS, CROSS_HEAD_DIM), 1.0),
        'state_conv': nrm((N_CONV, DEC_BATCH, CONV_WIDTH - 1, D), 0.5),
        'norm_mix': gain((DEPTH, D)),
        'norm_cross': gain((DEPTH, D)),
        'norm_mlp': gain((DEPTH, D)),
        'norm_final': gain((D,)),
        'w_in_ab': nrm((N_AB, D, D_IN_AB), D ** -0.5),
        'w_out_ab': nrm((N_AB, MIX_WIDTH, D), MIX_WIDTH ** -0.5),
        'rel_bias': nrm((N_BUCKETS, N_DSA_HEADS), 0.2),
        'w_pw1': nrm((N_CONV, D, 2 * D), D ** -0.5),
        'b_pw1': nrm((N_CONV, 2 * D), 0.02),
        'w_dw': nrm((N_CONV, CONV_WIDTH, D), CONV_WIDTH ** -0.5),
        'b_dw': nrm((N_CONV, D), 0.02),
        'g_conv_norm': gain((N_CONV, D)),
        'w_pw2': nrm((N_CONV, D, D), D ** -0.5),
        'b_pw2': nrm((N_CONV, D), 0.02),
        'w_cq': nrm((DEPTH, D, CROSS_WIDTH), D ** -0.5),
        'w_mk': nrm((DEPTH, D, CROSS_WIDTH), D ** -0.5),
        'w_mv': nrm((DEPTH, D, CROSS_WIDTH), D ** -0.5),
        'w_co': nrm((DEPTH, CROSS_WIDTH, D), CROSS_WIDTH ** -0.5),
        'w_up': nrm((DEPTH, D, D_FF), D ** -0.5),
        'w_down': nrm((DEPTH, D_FF, D), D_FF ** -0.5),
    }


def reference(x_prompt, x_sample, mem_prompt, cache_sb_k, cache_sb_v, cache_dsa_k, cache_dsa_v,
              cache_idx_k, cache_mem_k, cache_mem_v, state_conv, norm_mix, norm_cross, norm_mlp,
              norm_final, w_in_ab, w_out_ab, rel_bias, w_pw1, b_pw1, w_dw, b_dw, g_conv_norm,
              w_pw2, b_pw2, w_cq, w_mk, w_mv, w_co, w_up, w_down):

    def run(x, mem_k, mem_v, past_sb_k, past_sb_v, past_k, past_v, past_ki, conv_state):
        topk = min(TOPK_MAX, (past_sb_k.shape[2] + x.shape[1]) // 4)
        rows, bufs = [], []
        for l in range(DEPTH):
            i = l // 2
            h = _rms_norm(x, norm_mix[l])
            if l % 2 == 0:
                mix, new_rows = _mix_ab(h, w_in_ab[i], w_out_ab[i], rel_bias, past_sb_k[i], past_sb_v[i],
                                        past_k[i], past_v[i], past_ki[i], topk)
                rows.append(new_rows)
            else:
                mix, new_buf = _conv_module(h, w_pw1[i], b_pw1[i], w_dw[i], b_dw[i], g_conv_norm[i],
                                            w_pw2[i], b_pw2[i], conv_state[i])
                bufs.append(new_buf)
            x = x + mix
            x = x + _cross_attention(_rms_norm(x, norm_cross[l]), mem_k[l], mem_v[l], w_cq[l], w_co[l])
            x = x + _sq_relu_mlp(_rms_norm(x, norm_mlp[l]), w_up[l], w_down[l])
        y = _rms_norm(x, norm_final)
        sb_k, sb_v, d_k, d_v, d_ki = (jnp.stack(a) for a in zip(*rows))
        return y, sb_k, sb_v, d_k, d_v, d_ki, jnp.stack(bufs)

    b0 = x_prompt.shape[0]
    dt = x_prompt.dtype
    p_mem_k = jnp.einsum('bmd,lde->lbme', mem_prompt, w_mk).reshape(DEPTH, b0, -1, N_CROSS_HEADS, CROSS_HEAD_DIM)
    p_mem_v = jnp.einsum('bmd,lde->lbme', mem_prompt, w_mv).reshape(DEPTH, b0, -1, N_CROSS_HEADS, CROSS_HEAD_DIM)
    empty_sb = jnp.zeros((N_AB, b0, 0, N_SB_HEADS, HEAD_DIM), dt)
    empty_d = jnp.zeros((N_AB, b0, 0, N_DSA_HEADS, HEAD_DIM), dt)
    empty_ki = jnp.zeros((N_AB, b0, 0, IDX_DIM), dt)
    zero_conv = jnp.zeros((N_CONV, b0, CONV_WIDTH - 1, D_MODEL), dt)
    y_prompt, p_sb_k, p_sb_v, p_dsa_k, p_dsa_v, p_idx_k, p_conv = run(
        x_prompt, p_mem_k, p_mem_v, empty_sb, empty_sb, empty_d, empty_d, empty_ki, zero_conv)

    y_sample, s_sb_k, s_sb_v, s_dsa_k, s_dsa_v, s_idx_k, s_conv = run(
        x_sample, cache_mem_k, cache_mem_v, cache_sb_k, cache_sb_v, cache_dsa_k, cache_dsa_v,
        cache_idx_k, state_conv)

    return (y_prompt, y_sample, p_sb_k, p_sb_v, p_dsa_k, p_dsa_v, p_idx_k, p_mem_k, p_mem_v, p_conv,
            s_sb_k, s_sb_v, s_dsa_k, s_dsa_v, s_idx_k, s_conv)
```

```python
import functools
import math

import numpy as np
import jax
import jax.numpy as jnp
from jax import lax
from jax.experimental import pallas as pl
from jax.experimental.pallas import tpu as pltpu

F32 = jnp.float32
BF16 = jnp.bfloat16

HEAD_DIM = 128
IDX_DIM = 64
N_IDX_HEADS = 16
CHUNK = 64
TOPK_MAX = 256
N_BUCKETS = 32
MAX_DISTANCE = 128
CONV_WIDTH = 31
CONV_HALO = 32
RMS_EPS = 1e-6
MASKED_LOGIT = -1e30
VMEM_LIMIT_BYTES = 56 * 1024 * 1024


def _params(*semantics):
    return pltpu.CompilerParams(dimension_semantics=semantics, vmem_limit_bytes=VMEM_LIMIT_BYTES)


def _pick_tile(n, pref):
    if n <= pref:
        return n
    t = pref
    while n % t:
        t //= 2
    return t


def _rms_rows(x, g):
    ms = jnp.mean(x * x, axis=-1, keepdims=True)
    return x * lax.rsqrt(ms + RMS_EPS) * g


def _mm_kernel(*refs, has_norm, has_bias, has_res, glu):
    it = iter(refs)
    x_ref = next(it)
    g_ref = next(it) if has_norm else None
    w_ref = next(it)
    w2_ref = next(it) if glu else None
    b_ref = next(it) if has_bias else None
    b2_ref = next(it) if (has_bias and glu) else None
    r_ref = next(it) if has_res else None
    o_ref = next(it)
    xn_ref = next(it) if has_norm else None

    if has_norm:
        @pl.when(pl.program_id(1) == 0)
        def _():
            xn_ref[...] = _rms_rows(x_ref[...], g_ref[...]).astype(BF16)
        xb = xn_ref[...]
    else:
        xb = x_ref[...].astype(BF16)

    y = jnp.dot(xb, w_ref[...], preferred_element_type=F32)
    if has_bias:
        y = y + b_ref[...]
    if glu:
        gate = jnp.dot(xb, w2_ref[...], preferred_element_type=F32)
        if has_bias:
            gate = gate + b2_ref[...]
        y = y * (1.0 / (1.0 + jnp.exp(-gate)))
    if has_res:
        y = y + r_ref[...]
    o_ref[...] = y.astype(o_ref.dtype)


def _matmul(x, w, *, gain=None, bias=None, residual=None, glu=False, out_dtype=F32, tm=512, tn=512):
    M, K = x.shape
    n_out = w.shape[1] // 2 if glu else w.shape[1]
    tm = _pick_tile(M, tm)
    tn = _pick_tile(n_out, tn)
    nj = n_out // tn
    has_norm, has_bias, has_res = gain is not None, bias is not None, residual is not None

    args = [x]
    specs = [pl.BlockSpec((tm, K), lambda i, j: (i, 0))]
    if has_norm:
        args.append(gain.reshape(1, K))
        specs.append(pl.BlockSpec((1, K), lambda i, j: (0, 0)))
    args.append(w)
    specs.append(pl.BlockSpec((K, tn), lambda i, j: (0, j)))
    if glu:
        args.append(w)
        specs.append(pl.BlockSpec((K, tn), lambda i, j: (0, j + nj)))
    if has_bias:
        b2d = bias.reshape(1, -1)
        args.append(b2d)
        specs.append(pl.BlockSpec((1, tn), lambda i, j: (0, j)))
        if glu:
            args.append(b2d)
            specs.append(pl.BlockSpec((1, tn), lambda i, j: (0, j + nj)))
    if has_res:
        args.append(residual)
        specs.append(pl.BlockSpec((tm, tn), lambda i, j: (i, j)))

    return pl.pallas_call(
        functools.partial(_mm_kernel, has_norm=has_norm, has_bias=has_bias, has_res=has_res, glu=glu),
        grid=(M // tm, nj),
        in_specs=specs,
        out_specs=pl.BlockSpec((tm, tn), lambda i, j: (i, j)),
        out_shape=jax.ShapeDtypeStruct((M, n_out), out_dtype),
        scratch_shapes=[pltpu.VMEM((tm, K), BF16)] if has_norm else [],
        compiler_params=_params("parallel", "arbitrary"),
        name="norm_matmul",
    )(*args)


def _mlp_kernel(x_ref, g_ref, wu_ref, wd_ref, o_ref, xn_ref, acc_ref):
    f = pl.program_id(1)

    @pl.when(f == 0)
    def _():
        xn_ref[...] = _rms_rows(x_ref[...], g_ref[...]).astype(BF16)
        acc_ref[...] = jnp.zeros_like(acc_ref)

    h = jnp.dot(xn_ref[...], wu_ref[...], preferred_element_type=F32)
    h = jnp.maximum(h, 0.0)
    h = (h * h).astype(BF16)
    acc_ref[...] += jnp.dot(h, wd_ref[...], preferred_element_type=F32)

    @pl.when(f == pl.num_programs(1) - 1)
    def _():
        o_ref[...] = x_ref[...] + acc_ref[...]


def _mlp(x, gain, w_up, w_down, *, tm=512, tf=512):
    M, D = x.shape
    F = w_up.shape[1]
    tm = _pick_tile(M, tm)
    tf = _pick_tile(F, tf)
    return pl.pallas_call(
        _mlp_kernel,
        grid=(M // tm, F // tf),
        in_specs=[
            pl.BlockSpec((tm, D), lambda i, f: (i, 0)),
            pl.BlockSpec((1, D), lambda i, f: (0, 0)),
            pl.BlockSpec((D, tf), lambda i, f: (0, f)),
            pl.BlockSpec((tf, D), lambda i, f: (f, 0)),
        ],
        out_specs=pl.BlockSpec((tm, D), lambda i, f: (i, 0)),
        out_shape=jax.ShapeDtypeStruct((M, D), F32),
        scratch_shapes=[pltpu.VMEM((tm, D), BF16), pltpu.VMEM((tm, D), F32)],
        compiler_params=_params("parallel", "arbitrary"),
        name="mlp",
    )(x, gain.reshape(1, D), w_up, w_down)


def _norm_kernel(x_ref, g_ref, o_ref):
    o_ref[...] = _rms_rows(x_ref[...], g_ref[...])


def _rmsnorm(x, gain, *, tm=512):
    M, D = x.shape
    tm = _pick_tile(M, tm)
    return pl.pallas_call(
        _norm_kernel,
        grid=(M // tm,),
        in_specs=[pl.BlockSpec((tm, D), lambda i: (i, 0)), pl.BlockSpec((1, D), lambda i: (0, 0))],
        out_specs=pl.BlockSpec((tm, D), lambda i: (i, 0)),
        out_shape=jax.ShapeDtypeStruct((M, D), F32),
        compiler_params=_params("parallel"),
        name="final_norm",
    )(x, gain.reshape(1, D))


def _nt_dot(a, b):
    return lax.dot_general(a, b, (((1,), (1,)), ((), ())), preferred_element_type=F32)


def _sb_kernel(q_ref, k_ref, v_ref, u_ref, o_ref, acc_ref, carry_ref, *, tq, tk, past, scale):
    i = pl.program_id(2)
    q = q_ref[...]
    q_start = past + i * tq
    jd = q_start // tk
    reps = tk // HEAD_DIM

    acc_ref[...] = jnp.zeros_like(acc_ref)
    carry_ref[...] = jnp.zeros_like(carry_ref)

    def step(j, masked):
        row0 = pl.multiple_of(j * tk, tk)
        k = k_ref[pl.ds(row0, tk), :]
        v = v_ref[pl.ds(row0, tk), :]
        z = _nt_dot(q, k) * scale
        log_1mb = -(jnp.maximum(z, 0.0) + jnp.log(1.0 + jnp.exp(-jnp.abs(z))))
        log_b = z + log_1mb
        if masked:
            t_pos = q_start + lax.broadcasted_iota(jnp.int32, (tq, tk), 0)
            s_pos = j * tk + lax.broadcasted_iota(jnp.int32, (tq, tk), 1)
            causal = t_pos > s_pos
            log_1mb = jnp.where(causal, log_1mb, 0.0)
        hi = log_1mb.astype(BF16)
        lo = (log_1mb - hi.astype(F32)).astype(BF16)
        u = u_ref[...]
        cs = jnp.dot(hi, u, preferred_element_type=F32) + jnp.dot(lo, u, preferred_element_type=F32)
        carry = carry_ref[...]
        log_keep = cs[:, :tk] + jnp.concatenate([carry] * reps, axis=1)
        a = jnp.exp(log_b + log_keep)
        if masked:
            a = jnp.where(causal, a, 0.0)
        acc_ref[...] += jnp.dot(a.astype(BF16), v, preferred_element_type=F32)
        carry_ref[...] = carry + cs[:, tk:]

    step(jd, True)

    def body(n, c):
        step(jd - 1 - n, False)
        return c

    lax.fori_loop(0, jd, body, 0)
    o_ref[...] = acc_ref[...].astype(o_ref.dtype)


def _suffix_sum_matrix(tk):
    r = np.arange(tk)
    upper = (r[:, None] > r[None, :]).astype(np.float32)
    return jnp.asarray(np.concatenate([upper, np.ones((tk, HEAD_DIM), np.float32)], axis=1), dtype=BF16)


def _sb_attention(q, k_all, v_all, *, past, tq, tk):
    B, T, W = q.shape
    H = W // HEAD_DIM
    s_pad = k_all.shape[1]
    assert T % tq == 0 and s_pad % tk == 0 and tk % tq == 0 and past % tk == 0
    return pl.pallas_call(
        functools.partial(_sb_kernel, tq=tq, tk=tk, past=past, scale=HEAD_DIM ** -0.5),
        grid=(B, H, T // tq),
        in_specs=[
            pl.BlockSpec((None, tq, HEAD_DIM), lambda b, h, i: (b, i, h)),
            pl.BlockSpec((None, s_pad, HEAD_DIM), lambda b, h, i: (b, 0, h)),
            pl.BlockSpec((None, s_pad, HEAD_DIM), lambda b, h, i: (b, 0, h)),
            pl.BlockSpec((tk, tk + HEAD_DIM), lambda b, h, i: (0, 0)),
        ],
        out_specs=pl.BlockSpec((None, tq, HEAD_DIM), lambda b, h, i: (b, i, h)),
        out_shape=jax.ShapeDtypeStruct((B, T, W), BF16),
        scratch_shapes=[pltpu.VMEM((tq, HEAD_DIM), F32), pltpu.VMEM((tq, HEAD_DIM), F32)],
        compiler_params=_params("parallel", "parallel", "arbitrary"),
        name="sb_attention",
    )(q, k_all, v_all, _suffix_sum_matrix(tk))


def _sortable_key(x):
    b = lax.bitcast_convert_type(x, jnp.int32)
    return b ^ ((b >> 31) & jnp.int32(0x7FFFFFFF))


_KEY_NEG_INF = int(np.int32(np.array(-np.inf, np.float32).view(np.int32) ^ 0x7FFFFFFF))
_KEY_LOWEST_FINITE = int(np.int32(np.array(np.finfo(np.float32).min, np.float32).view(np.int32) ^ 0x7FFFFFFF))
_INT32_MIN = -2 ** 31


def _dsa_kernel(qi_ref, wi_ref, ki2_ref, qd_ref, kd_ref, vd_ref, bnear_ref, bfar_ref, o_ref,
                keys_ref, thr_ref, wb_ref, m_ref, l_ref, acc_ref, *, tq, tk, past, topk, scale):
    i = pl.program_id(1)
    h = pl.program_id(2)
    q_start = past + i * tq
    jd = q_start // tk
    reps = tk // HEAD_DIM

    @pl.when(h == 0)
    def _select():
        w = wi_ref[...] * (IDX_DIM ** -0.5 * N_IDX_HEADS ** -0.5)
        for hh in range(N_IDX_HEADS):
            wb_ref[hh] = jnp.broadcast_to(w[:, hh:hh + 1], (tq, HEAD_DIM))

        def idx_tile(j, c):
            kt = ki2_ref[j]
            tot = jnp.zeros((tq, tk), F32)
            for p in range(N_IDX_HEADS // 2):
                sc = jnp.dot(qi_ref[:, p * 128:(p + 1) * 128], kt, preferred_element_type=F32)
                w0 = jnp.concatenate([wb_ref[2 * p]] * reps, axis=1)
                w1 = jnp.concatenate([wb_ref[2 * p + 1]] * reps, axis=1)
                tot = tot + w0 * jnp.maximum(sc[:, :tk], 0.0) + w1 * jnp.maximum(sc[:, tk:], 0.0)
            keys_ref[j] = _sortable_key(tot)
            return c

        lax.fori_loop(0, jd + 1, idx_tile, 0)

        t_pos = q_start + lax.broadcasted_iota(jnp.int32, (tq, tk), 0)
        s_pos = jd * tk + lax.broadcasted_iota(jnp.int32, (tq, tk), 1)
        visible = (t_pos // CHUNK) >= (s_pos // CHUNK)
        keys_ref[jd] = jnp.where(visible, keys_ref[jd], jnp.int32(_KEY_NEG_INF))

        def count_ge(cand):
            def body(j, cnt):
                kj = keys_ref[j]
                for r in range(reps):
                    cnt = cnt + jnp.where(kj[:, r * 128:(r + 1) * 128] >= cand, 1.0, 0.0)
                return cnt
            cnt = lax.fori_loop(0, jd + 1, body, jnp.zeros((tq, HEAD_DIM), F32))
            return jnp.sum(cnt, axis=-1, keepdims=True)

        kf = float(topk)
        zero = jnp.zeros((tq, HEAD_DIM), jnp.int32)
        res = jnp.where(count_ge(zero) >= kf, zero, jnp.int32(_INT32_MIN))
        for bit in range(30, -1, -1):
            cand = res + jnp.int32(1 << bit)
            res = jnp.where(count_ge(cand) >= kf, cand, res)
        thr_ref[...] = jnp.maximum(res, jnp.int32(_KEY_LOWEST_FINITE))

    m_ref[...] = jnp.full_like(m_ref, MASKED_LOGIT)
    l_ref[...] = jnp.zeros_like(l_ref)
    acc_ref[...] = jnp.zeros_like(acc_ref)
    q = qd_ref[...]
    thr = jnp.concatenate([thr_ref[...]] * reps, axis=1)

    def flash(j, bias):
        row0 = pl.multiple_of(j * tk, tk)
        k = kd_ref[pl.ds(row0, tk), :]
        v = vd_ref[pl.ds(row0, tk), :]
        s = _nt_dot(q, k) * scale + bias
        s = jnp.where(keys_ref[j] >= thr, s, MASKED_LOGIT)
        m_old = m_ref[...]
        m_new = jnp.maximum(m_old, jnp.max(s, axis=-1, keepdims=True))
        alpha = jnp.exp(m_old - m_new)
        p = jnp.exp(s - m_new)
        l_ref[...] = alpha * l_ref[...] + jnp.sum(p, axis=-1, keepdims=True)
        acc_ref[...] = alpha * acc_ref[...] + jnp.dot(p.astype(BF16), v, preferred_element_type=F32)
        m_ref[...] = m_new

    flash(jd, bnear_ref[0])

    @pl.when(jd >= 1)
    def _():
        flash(jd - 1, bnear_ref[1])

    far = jnp.broadcast_to(bfar_ref[...], (tq, HEAD_DIM))
    far = jnp.concatenate([far] * reps, axis=1)

    def body(j, c):
        flash(j, far)
        return c

    lax.fori_loop(0, jnp.maximum(jd - 1, 0), body, 0)
    o_ref[...] = (acc_ref[...] / l_ref[...]).astype(o_ref.dtype)


def _t5_bucket(rel):
    nb = N_BUCKETS // 2
    max_exact = nb // 2
    side = jnp.where(rel > 0, nb, 0)
    n = jnp.abs(rel)
    nf = jnp.maximum(n, 1).astype(F32)
    large = max_exact + (jnp.log(nf / max_exact) / math.log(MAX_DISTANCE / max_exact) * (nb - max_exact)).astype(jnp.int32)
    large = jnp.minimum(large, nb - 1)
    return side + jnp.where(n < max_exact, n, large)


def _bias_tables(rel_bias, *, tq, tk, q_off):
    r = jnp.arange(tq, dtype=jnp.int32)[:, None] + q_off
    c = jnp.arange(tk, dtype=jnp.int32)[None, :]
    rel = jnp.stack([c - r, c - r - tk])
    near = jnp.transpose(rel_bias[_t5_bucket(rel)], (3, 0, 1, 2)).astype(F32)
    far_bucket = _t5_bucket(jnp.int32(-2 * tk))
    far = jnp.broadcast_to(rel_bias[far_bucket][:, None, None], (rel_bias.shape[1], 1, HEAD_DIM)).astype(F32)
    return near, far


def _dsa_attention(q_idx, w_idx, k_idx_all, q, k_all, v_all, rel_bias, *, past, tq, tk):
    B, T, W = q.shape
    H = W // HEAD_DIM
    s_pad = k_all.shape[1]
    nkt = s_pad // tk
    nq = T // tq
    assert T % tq == 0 and s_pad % tk == 0 and tk % CHUNK == 0 and tq % CHUNK == 0 and past % tk == 0
    assert tq == tk or nq == 1
    assert tk >= MAX_DISTANCE
    topk = min(TOPK_MAX, (past + T) // 4)

    kt = jnp.transpose(k_idx_all.reshape(B, nkt, tk, IDX_DIM), (0, 1, 3, 2))
    z = jnp.zeros_like(kt)
    ki2 = jnp.concatenate([jnp.concatenate([kt, z], axis=3), jnp.concatenate([z, kt], axis=3)], axis=2)
    near, far = _bias_tables(rel_bias, tq=tq, tk=tk, q_off=0)

    return pl.pallas_call(
        functools.partial(_dsa_kernel, tq=tq, tk=tk, past=past, topk=topk, scale=HEAD_DIM ** -0.5),
        grid=(B, nq, H),
        in_specs=[
            pl.BlockSpec((None, tq, N_IDX_HEADS * IDX_DIM), lambda b, i, h: (b, i, 0)),
            pl.BlockSpec((None, tq, N_IDX_HEADS), lambda b, i, h: (b, i, 0)),
            pl.BlockSpec((None, nkt, 2 * IDX_DIM, 2 * tk), lambda b, i, h: (b, 0, 0, 0)),
            pl.BlockSpec((None, tq, HEAD_DIM), lambda b, i, h: (b, i, h)),
            pl.BlockSpec((None, s_pad, HEAD_DIM), lambda b, i, h: (b, 0, h)),
            pl.BlockSpec((None, s_pad, HEAD_DIM), lambda b, i, h: (b, 0, h)),
            pl.BlockSpec((None, 2, tq, tk), lambda b, i, h: (h, 0, 0, 0)),
            pl.BlockSpec((None, 1, HEAD_DIM), lambda b, i, h: (h, 0, 0)),
        ],
        out_specs=pl.BlockSpec((None, tq, HEAD_DIM), lambda b, i, h: (b, i, h)),
        out_shape=jax.ShapeDtypeStruct((B, T, W), BF16),
        scratch_shapes=[
            pltpu.VMEM((nkt, tq, tk), jnp.int32),
            pltpu.VMEM((tq, HEAD_DIM), jnp.int32),
            pltpu.VMEM((N_IDX_HEADS, tq, HEAD_DIM), F32),
            pltpu.VMEM((tq, 1), F32),
            pltpu.VMEM((tq, 1), F32),
            pltpu.VMEM((tq, HEAD_DIM), F32),
        ],
        compiler_params=_params("parallel", "arbitrary", "arbitrary"),
        name="dsa_attention",
    )(q_idx, w_idx, ki2, q, k_all, v_all, near, far)


def _cross_kernel(q_ref, k_ref, v_ref, o_ref, *, n_heads, scale):
    for hh in range(n_heads):
        sl = slice(hh * HEAD_DIM, (hh + 1) * HEAD_DIM)
        s = _nt_dot(q_ref[:, sl], k_ref[:, sl]) * scale
        p = jnp.exp(s - jnp.max(s, axis=-1, keepdims=True))
        denom = jnp.sum(p, axis=-1, keepdims=True)
        o = jnp.dot(p.astype(BF16), v_ref[:, sl], preferred_element_type=F32)
        o_ref[:, sl] = (o / denom).astype(o_ref.dtype)


def _cross_attention(q, mem_k, mem_v, *, tq=512):
    B, T, W = q.shape
    n_mem = mem_k.shape[1]
    tq = _pick_tile(T, tq)
    return pl.pallas_call(
        functools.partial(_cross_kernel, n_heads=W // HEAD_DIM, scale=HEAD_DIM ** -0.5),
        grid=(B, T // tq),
        in_specs=[
            pl.BlockSpec((None, tq, W), lambda b, i: (b, i, 0)),
            pl.BlockSpec((None, n_mem, W), lambda b, i: (b, 0, 0)),
            pl.BlockSpec((None, n_mem, W), lambda b, i: (b, 0, 0)),
        ],
        out_specs=pl.BlockSpec((None, tq, W), lambda b, i: (b, i, 0)),
        out_shape=jax.ShapeDtypeStruct((B, T, W), BF16),
        compiler_params=_params("parallel", "parallel"),
        name="cross_attention",
    )(q, mem_k, mem_v)


def _conv_kernel(a_ref, b_ref, w_ref, bias_ref, g_ref, o_ref, buf_ref, y_ref, *, tt, lane_tile):
    D = a_ref.shape[-1]
    buf_ref[0:tt, :] = a_ref[...]
    buf_ref[tt:tt + CONV_HALO, :] = b_ref[...]
    shift = CONV_HALO - (CONV_WIDTH - 1)
    for c0 in range(0, D, lane_tile):
        cs = slice(c0, c0 + lane_tile)
        acc = jnp.zeros((tt, lane_tile), F32)
        for k in range(CONV_WIDTH):
            acc = acc + buf_ref[k + shift:k + shift + tt, cs] * w_ref[k:k + 1, cs]
        y_ref[:, cs] = acc + bias_ref[:, cs]
    y = _rms_rows(y_ref[...], g_ref[...])
    o_ref[...] = (y * (1.0 / (1.0 + jnp.exp(-y)))).astype(o_ref.dtype)


def _conv_norm_swish(u_staged, w_dw, b_dw, g_norm, *, tt):
    B, tp, D = u_staged.shape
    T = tp - CONV_HALO
    assert T % tt == 0 and tt % CONV_HALO == 0
    hb = tt // CONV_HALO
    return pl.pallas_call(
        functools.partial(_conv_kernel, tt=tt, lane_tile=512),
        grid=(B, T // tt),
        in_specs=[
            pl.BlockSpec((None, tt, D), lambda b, i: (b, i, 0)),
            pl.BlockSpec((None, CONV_HALO, D), lambda b, i: (b, (i + 1) * hb, 0)),
            pl.BlockSpec((CONV_WIDTH, D), lambda b, i: (0, 0)),
            pl.BlockSpec((1, D), lambda b, i: (0, 0)),
            pl.BlockSpec((1, D), lambda b, i: (0, 0)),
        ],
        out_specs=pl.BlockSpec((None, tt, D), lambda b, i: (b, i, 0)),
        out_shape=jax.ShapeDtypeStruct((B, T, D), BF16),
        scratch_shapes=[pltpu.VMEM((tt + CONV_HALO, D), F32), pltpu.VMEM((tt, D), F32)],
        compiler_params=_params("parallel", "parallel"),
        name="conv_norm_swish",
    )(u_staged, u_staged, w_dw, b_dw.reshape(1, D), g_norm.reshape(1, D))


def _pad_rows(a, n):
    if a.shape[1] == n:
        return a
    return jnp.concatenate([a, jnp.zeros((a.shape[0], n - a.shape[1]) + a.shape[2:], a.dtype)], axis=1)


def kernel(x_prompt, x_sample, mem_prompt, cache_sb_k, cache_sb_v, cache_dsa_k, cache_dsa_v, cache_idx_k, cache_mem_k, cache_mem_v, state_conv, norm_mix, norm_cross, norm_mlp, norm_final, w_in_ab, w_out_ab, rel_bias, w_pw1, b_pw1, w_dw, b_dw, g_conv_norm, w_pw2, b_pw2, w_cq, w_mk, w_mv, w_co, w_up, w_down):
    Bp, Tp, D = x_prompt.shape
    Bs, Ts, _ = x_sample.shape
    past = cache_sb_k.shape[2]
    depth = norm_mix.shape[0]
    n_mem = mem_prompt.shape[1]
    sbw = cache_sb_k.shape[3] * HEAD_DIM
    dsw = cache_dsa_k.shape[3] * HEAD_DIM
    qiw = N_IDX_HEADS * IDX_DIM
    cw = w_cq.shape[2]
    Mp, Ms = Bp * Tp, Bs * Ts
    main_w = 3 * sbw + 3 * dsw + qiw

    x = jnp.concatenate([x_prompt.reshape(Mp, D), x_sample.reshape(Ms, D)], axis=0)
    mem_flat = mem_prompt.reshape(Bp * n_mem, D)

    tq_p = tk_p = _pick_tile(Tp, 256)
    tq_s, tk_s = Ts, 256
    s_pad_s = -(-(past + Ts) // tk_s) * tk_s

    outs = {k: [] for k in ("p_sb_k", "p_sb_v", "p_d_k", "p_d_v", "p_ki", "p_mem_k", "p_mem_v", "p_conv",
                            "s_sb_k", "s_sb_v", "s_d_k", "s_d_v", "s_ki", "s_conv")}

    def split_rows(a):
        return a[:Mp].reshape(Bp, Tp, -1), a[Mp:].reshape(Bs, Ts, -1)

    for l in range(depth):
        i = l // 2
        if l % 2 == 0:
            w_in = w_in_ab[i]
            w_main = w_in[:, :main_w].astype(BF16)
            w_tail = jnp.pad(w_in[:, main_w:], ((0, 0), (0, HEAD_DIM - (w_in.shape[1] - main_w)))).astype(BF16)
            proj = _matmul(x, w_main, gain=norm_mix[l])
            tail = _matmul(x, w_tail, gain=norm_mix[l])
            cols = np.cumsum([0, sbw, sbw, sbw, dsw, dsw, dsw, qiw])
            q_sb, k_sb, v_sb, q_d, k_d, v_d, q_i = (proj[:, cols[n]:cols[n + 1]] for n in range(7))
            k_i, w_i = tail[:, :IDX_DIM], tail[:, IDX_DIM:IDX_DIM + N_IDX_HEADS]

            hd = (cache_sb_k.shape[3], HEAD_DIM)
            for name, a in (("sb_k", k_sb), ("sb_v", v_sb), ("d_k", k_d), ("d_v", v_d)):
                ap, as_ = split_rows(a)
                outs["p_" + name].append(ap.reshape(Bp, Tp, *hd))
                outs["s_" + name].append(as_.reshape(Bs, Ts, *hd))
            kip, kis = split_rows(k_i)
            outs["p_ki"].append(kip)
            outs["s_ki"].append(kis)

            def bf(a):
                return a.astype(BF16)

            qsp, qss = split_rows(bf(q_sb))
            ksp, kss = split_rows(bf(k_sb))
            vsp, vss = split_rows(bf(v_sb))
            qdp, qds = split_rows(bf(q_d))
            kdp, kds = split_rows(bf(k_d))
            vdp, vds = split_rows(bf(v_d))
            qip, qis = split_rows(bf(q_i))
            kibp, kibs = split_rows(bf(k_i))
            wip, wis = split_rows(w_i)

            o_sb_p = _sb_attention(qsp, ksp, vsp, past=0, tq=tq_p, tk=tk_p)
            o_d_p = _dsa_attention(qip, wip, kibp, qdp, kdp, vdp, rel_bias, past=0, tq=tq_p, tk=tk_p)

            def with_past(cache, new):
                full = jnp.concatenate([bf(cache[i]).reshape(Bs, past, -1), new], axis=1)
                return _pad_rows(full, s_pad_s)

            o_sb_s = _sb_attention(qss, with_past(cache_sb_k, kss), with_past(cache_sb_v, vss),
                                   past=past, tq=tq_s, tk=tk_s)
            o_d_s = _dsa_attention(qis, wis, with_past(cache_idx_k, kibs), qds, with_past(cache_dsa_k, kds),
                                   with_past(cache_dsa_v, vds), rel_bias, past=past, tq=tq_s, tk=tk_s)

            o = jnp.concatenate([
                jnp.concatenate([o_sb_p, o_d_p], axis=-1).reshape(Mp, sbw + dsw),
                jnp.concatenate([o_sb_s, o_d_s], axis=-1).reshape(Ms, sbw + dsw)], axis=0)
            x = _matmul(o, w_out_ab[i].astype(BF16), residual=x)
        else:
            u = _matmul(x, w_pw1[i].astype(BF16), gain=norm_mix[l], bias=b_pw1[i], glu=True)
            u_p, u_s = split_rows(u)
            front = CONV_HALO - (CONV_WIDTH - 1)
            staged_p = jnp.concatenate([jnp.zeros((Bp, CONV_HALO, D), F32), u_p], axis=1)
            staged_s = jnp.concatenate([jnp.zeros((Bs, front, D), F32), state_conv[i], u_s], axis=1)
            y_p = _conv_norm_swish(staged_p, w_dw[i], b_dw[i], g_conv_norm[i], tt=_pick_tile(Tp, 256))
            y_s = _conv_norm_swish(staged_s, w_dw[i], b_dw[i], g_conv_norm[i], tt=Ts)
            outs["p_conv"].append(staged_p[:, -(CONV_WIDTH - 1):])
            outs["s_conv"].append(staged_s[:, -(CONV_WIDTH - 1):])
            y = jnp.concatenate([y_p.reshape(Mp, D), y_s.reshape(Ms, D)], axis=0)
            x = _matmul(y, w_pw2[i].astype(BF16), bias=b_pw2[i], residual=x)

        mk = _matmul(mem_flat, w_mk[l].astype(BF16))
        mv = _matmul(mem_flat, w_mv[l].astype(BF16))
        outs["p_mem_k"].append(mk.reshape(Bp, n_mem, cw // HEAD_DIM, HEAD_DIM))
        outs["p_mem_v"].append(mv.reshape(Bp, n_mem, cw // HEAD_DIM, HEAD_DIM))
        qc = _matmul(x, w_cq[l].astype(BF16), gain=norm_cross[l], out_dtype=BF16)
        qc_p, qc_s = split_rows(qc)
        oc_p = _cross_attention(qc_p, mk.astype(BF16).reshape(Bp, n_mem, cw), mv.astype(BF16).reshape(Bp, n_mem, cw))
        oc_s = _cross_attention(qc_s, cache_mem_k[l].astype(BF16).reshape(Bs, n_mem, cw),
                                cache_mem_v[l].astype(BF16).reshape(Bs, n_mem, cw))
        oc = jnp.concatenate([oc_p.reshape(Mp, cw), oc_s.reshape(Ms, cw)], axis=0)
        x = _matmul(oc, w_co[l].astype(BF16), residual=x)

        x = _mlp(x, norm_mlp[l], w_up[l].astype(BF16), w_down[l].astype(BF16))

    y = _rmsnorm(x, norm_final)
    y_p, y_s = split_rows(y)
    st = lambda k: jnp.stack(outs[k])
    return (y_p, y_s, st("p_sb_k"), st("p_sb_v"), st("p_d_k"), st("p_d_v"), st("p_ki"), st("p_mem_k"),
            st("p_mem_v"), st("p_conv"), st("s_sb_k"), st("s_sb_v"), st("s_d_k"), st("s_d_v"), st("s_ki"),
            st("s_conv"))
```

```python
import functools
import math

import numpy as np
import jax
import jax.numpy as jnp
from jax import lax
from jax.experimental import pallas as pl
from jax.experimental.pallas import tpu as pltpu

F32 = jnp.float32
BF16 = jnp.bfloat16

HEAD_DIM = 128
IDX_DIM = 64
N_IDX_HEADS = 16
CHUNK = 64
TOPK_MAX = 256
N_BUCKETS = 32
MAX_DISTANCE = 128
CONV_WIDTH = 31
CONV_HALO = 32
RMS_EPS = 1e-6
MASKED_LOGIT = -1e30
LOG2E = math.log2(math.e)
_INT32_MIN = -2 ** 31
VMEM_LIMIT_BYTES = 56 * 1024 * 1024
SB_HEADS_PER_STEP = 8
DSA_HEADS_PER_STEP = 4


def _params(*semantics):
    return pltpu.CompilerParams(dimension_semantics=semantics, vmem_limit_bytes=VMEM_LIMIT_BYTES)


def _pick_tile(n, pref):
    if n <= pref:
        return n
    t = pref
    while n % t:
        t //= 2
    return t


def _rms_rows(x, g):
    ms = jnp.mean(x * x, axis=-1, keepdims=True)
    return x * lax.rsqrt(ms + RMS_EPS) * g


def _mm_kernel(*refs, has_norm, has_bias, has_res, glu):
    it = iter(refs)
    x_ref = next(it)
    g_ref = next(it) if has_norm else None
    w_ref = next(it)
    w2_ref = next(it) if glu else None
    b_ref = next(it) if has_bias else None
    b2_ref = next(it) if (has_bias and glu) else None
    r_ref = next(it) if has_res else None
    o_ref = next(it)
    xn_ref = next(it) if has_norm else None

    if has_norm:
        @pl.when(pl.program_id(1) == 0)
        def _():
            xn_ref[...] = _rms_rows(x_ref[...], g_ref[...]).astype(BF16)
        xb = xn_ref[...]
    else:
        xb = x_ref[...].astype(BF16)

    y = jnp.dot(xb, w_ref[...], preferred_element_type=F32)
    if has_bias:
        y = y + b_ref[...]
    if glu:
        gate = jnp.dot(xb, w2_ref[...], preferred_element_type=F32)
        if has_bias:
            gate = gate + b2_ref[...]
        y = y * (1.0 / (1.0 + jnp.exp(-gate)))
    if has_res:
        y = y + r_ref[...]
    o_ref[...] = y.astype(o_ref.dtype)


def _matmul(x, w, *, gain=None, bias=None, residual=None, glu=False, out_dtype=F32, tm=512, tn=512):
    M, K = x.shape
    n_out = w.shape[1] // 2 if glu else w.shape[1]
    tm = _pick_tile(M, tm)
    tn = _pick_tile(n_out, tn)
    nj = n_out // tn
    has_norm, has_bias, has_res = gain is not None, bias is not None, residual is not None

    args = [x]
    specs = [pl.BlockSpec((tm, K), lambda i, j: (i, 0))]
    if has_norm:
        args.append(gain.reshape(1, K))
        specs.append(pl.BlockSpec((1, K), lambda i, j: (0, 0)))
    args.append(w)
    specs.append(pl.BlockSpec((K, tn), lambda i, j: (0, j)))
    if glu:
        args.append(w)
        specs.append(pl.BlockSpec((K, tn), lambda i, j: (0, j + nj)))
    if has_bias:
        b2d = bias.reshape(1, -1)
        args.append(b2d)
        specs.append(pl.BlockSpec((1, tn), lambda i, j: (0, j)))
        if glu:
            args.append(b2d)
            specs.append(pl.BlockSpec((1, tn), lambda i, j: (0, j + nj)))
    if has_res:
        args.append(residual)
        specs.append(pl.BlockSpec((tm, tn), lambda i, j: (i, j)))

    return pl.pallas_call(
        functools.partial(_mm_kernel, has_norm=has_norm, has_bias=has_bias, has_res=has_res, glu=glu),
        grid=(M // tm, nj),
        in_specs=specs,
        out_specs=pl.BlockSpec((tm, tn), lambda i, j: (i, j)),
        out_shape=jax.ShapeDtypeStruct((M, n_out), out_dtype),
        scratch_shapes=[pltpu.VMEM((tm, K), BF16)] if has_norm else [],
        compiler_params=_params("parallel", "arbitrary"),
        name="norm_matmul",
    )(*args)


def _mlp_kernel(x_ref, g_ref, wu_ref, wd_ref, o_ref, xn_ref, acc_ref):
    f = pl.program_id(1)

    @pl.when(f == 0)
    def _():
        xn_ref[...] = _rms_rows(x_ref[...], g_ref[...]).astype(BF16)
        acc_ref[...] = jnp.zeros_like(acc_ref)

    h = jnp.dot(xn_ref[...], wu_ref[...], preferred_element_type=F32)
    h = jnp.maximum(h, 0.0)
    h = (h * h).astype(BF16)
    acc_ref[...] += jnp.dot(h, wd_ref[...], preferred_element_type=F32)

    @pl.when(f == pl.num_programs(1) - 1)
    def _():
        o_ref[...] = x_ref[...] + acc_ref[...]


def _mlp(x, gain, w_up, w_down, *, tm=512, tf=512):
    M, D = x.shape
    F = w_up.shape[1]
    tm = _pick_tile(M, tm)
    tf = _pick_tile(F, tf)
    return pl.pallas_call(
        _mlp_kernel,
        grid=(M // tm, F // tf),
        in_specs=[
            pl.BlockSpec((tm, D), lambda i, f: (i, 0)),
            pl.BlockSpec((1, D), lambda i, f: (0, 0)),
            pl.BlockSpec((D, tf), lambda i, f: (0, f)),
            pl.BlockSpec((tf, D), lambda i, f: (f, 0)),
        ],
        out_specs=pl.BlockSpec((tm, D), lambda i, f: (i, 0)),
        out_shape=jax.ShapeDtypeStruct((M, D), F32),
        scratch_shapes=[pltpu.VMEM((tm, D), BF16), pltpu.VMEM((tm, D), F32)],
        compiler_params=_params("parallel", "arbitrary"),
        name="mlp",
    )(x, gain.reshape(1, D), w_up, w_down)


def _norm_kernel(x_ref, g_ref, o_ref):
    o_ref[...] = _rms_rows(x_ref[...], g_ref[...])


def _rmsnorm(x, gain, *, tm=512):
    M, D = x.shape
    tm = _pick_tile(M, tm)
    return pl.pallas_call(
        _norm_kernel,
        grid=(M // tm,),
        in_specs=[pl.BlockSpec((tm, D), lambda i: (i, 0)), pl.BlockSpec((1, D), lambda i: (0, 0))],
        out_specs=pl.BlockSpec((tm, D), lambda i: (i, 0)),
        out_shape=jax.ShapeDtypeStruct((M, D), F32),
        compiler_params=_params("parallel"),
        name="final_norm",
    )(x, gain.reshape(1, D))


def _nt_dot(a, b):
    return lax.dot_general(a, b, (((1,), (1,)), ((), ())), preferred_element_type=F32)


def _sb_kernel(q_ref, k_ref, v_ref, u_ref, o_ref, acc_ref, carry_ref, *, tq, tk, hp, past, scale2):
    i = pl.program_id(2)
    q_start = past + i * tq
    jd = q_start // tk
    reps = tk // HEAD_DIM

    acc_ref[...] = jnp.zeros_like(acc_ref)
    carry_ref[...] = jnp.zeros_like(carry_ref)

    def step(j, masked):
        row0 = pl.multiple_of(j * tk, tk)
        if masked:
            t_pos = q_start + lax.broadcasted_iota(jnp.int32, (tq, tk), 0)
            s_pos = j * tk + lax.broadcasted_iota(jnp.int32, (tq, tk), 1)
            causal = t_pos > s_pos
        heads = [slice(hh * HEAD_DIM, (hh + 1) * HEAD_DIM) for hh in range(hp)]
        z2s = [_nt_dot(q_ref[:, hs], k_ref[pl.ds(row0, tk), hs]) * scale2 for hs in heads]
        ns, log2_betas, css = [], [], []
        for z2 in z2s:
            neg_abs = lax.bitcast_convert_type(
                lax.bitcast_convert_type(z2, jnp.int32) | jnp.int32(_INT32_MIN), F32)
            n = jnp.maximum(z2, 0.0) + jnp.log(1.0 + jnp.exp2(neg_abs)) * LOG2E
            log2_betas.append(z2 - n)
            if masked:
                n = jnp.where(causal, n, 0.0)
            hi = n.astype(BF16)
            lo = (n - hi.astype(F32)).astype(BF16)
            css.append(jnp.dot(jnp.concatenate([hi, lo], axis=1), u_ref[...], preferred_element_type=F32))
            ns.append(n)
        for hs, n, log2_beta, cs in zip(heads, ns, log2_betas, css):
            carry = carry_ref[:, hs]
            a = jnp.exp2(log2_beta - cs - jnp.concatenate([carry] * reps, axis=1))
            if masked:
                a = jnp.where(causal, a, 0.0)
            acc_ref[:, hs] += jnp.dot(a.astype(BF16), v_ref[pl.ds(row0, tk), hs], preferred_element_type=F32)
            carry_ref[:, hs] = carry + jnp.broadcast_to(cs[:, 0:1] + n[:, 0:1], (tq, HEAD_DIM))

    step(jd, True)

    def body(t, c):
        step(jd - 1 - t, False)
        return c

    lax.fori_loop(0, jd, body, 0)
    o_ref[...] = acc_ref[...].astype(o_ref.dtype)


def _suffix_sum_matrix(tk):
    r = np.arange(tk)
    upper = (r[:, None] > r[None, :]).astype(np.float32)
    return jnp.asarray(np.concatenate([upper, upper], axis=0), dtype=BF16)


def _sb_attention(q, k_all, v_all, *, past, tq, tk, hp):
    B, T, W = q.shape
    H = W // HEAD_DIM
    s_pad = k_all.shape[1]
    assert T % tq == 0 and s_pad % tk == 0 and tk % tq == 0 and past % tk == 0 and H % hp == 0
    wp = hp * HEAD_DIM
    resident = pl.Buffered(1) if T // tq > 1 else None
    return pl.pallas_call(
        functools.partial(_sb_kernel, tq=tq, tk=tk, hp=hp, past=past, scale2=HEAD_DIM ** -0.5 * LOG2E),
        grid=(B, H // hp, T // tq),
        in_specs=[
            pl.BlockSpec((None, tq, wp), lambda b, h, i: (b, i, h)),
            pl.BlockSpec((None, s_pad, wp), lambda b, h, i: (b, 0, h), pipeline_mode=resident),
            pl.BlockSpec((None, s_pad, wp), lambda b, h, i: (b, 0, h), pipeline_mode=resident),
            pl.BlockSpec((2 * tk, tk), lambda b, h, i: (0, 0), pipeline_mode=resident),
        ],
        out_specs=pl.BlockSpec((None, tq, wp), lambda b, h, i: (b, i, h)),
        out_shape=jax.ShapeDtypeStruct((B, T, W), BF16),
        scratch_shapes=[pltpu.VMEM((tq, wp), F32), pltpu.VMEM((tq, wp), F32)],
        compiler_params=_params("parallel", "parallel", "arbitrary"),
        name="sb_attention",
    )(q, k_all, v_all, _suffix_sum_matrix(tk))


def _sortable_key(x):
    b = lax.bitcast_convert_type(x, jnp.int32)
    return b ^ ((b >> 31) & jnp.int32(0x7FFFFFFF))


_KEY_NEG_INF = int(np.int32(np.array(-np.inf, np.float32).view(np.int32) ^ 0x7FFFFFFF))
_KEY_LOWEST_FINITE = int(np.int32(np.array(np.finfo(np.float32).min, np.float32).view(np.int32) ^ 0x7FFFFFFF))


COUNT_ROW_GROUP = 64


def _dsa_select_kernel(qi_ref, wi_ref, ki2_ref, mask_ref, keys_ref, wb_ref, *, tq, tk, nkt, past, topk):
    i = pl.program_id(1)
    q_start = past + i * tq
    jd = q_start // tk
    reps = tk // HEAD_DIM

    w = wi_ref[...] * (IDX_DIM ** -0.5 * N_IDX_HEADS ** -0.5)
    for hh in range(N_IDX_HEADS):
        wb_ref[hh] = jnp.broadcast_to(w[:, hh:hh + 1], (tq, HEAD_DIM))

    def idx_tile(j, c):
        kt = ki2_ref[j]
        tot = jnp.zeros((tq, tk), F32)
        for p in range(N_IDX_HEADS // 2):
            sc = jnp.dot(qi_ref[:, p * 128:(p + 1) * 128], kt, preferred_element_type=F32)
            w0 = jnp.concatenate([wb_ref[2 * p]] * reps, axis=1)
            w1 = jnp.concatenate([wb_ref[2 * p + 1]] * reps, axis=1)
            tot = tot + w0 * jnp.maximum(sc[:, :tk], 0.0) + w1 * jnp.maximum(sc[:, tk:], 0.0)
        keys_ref[j] = _sortable_key(tot)
        return c

    lax.fori_loop(0, jd + 1, idx_tile, 0)

    t_pos = q_start + lax.broadcasted_iota(jnp.int32, (tq, tk), 0)
    s_pos = jd * tk + lax.broadcasted_iota(jnp.int32, (tq, tk), 1)
    visible = (t_pos // CHUNK) >= (s_pos // CHUNK)
    keys_ref[jd] = jnp.where(visible, keys_ref[jd], jnp.int32(_KEY_NEG_INF))

    rg = min(COUNT_ROW_GROUP, tq)

    def count_ge(cand):
        parts = []
        for g in range(tq // rg):
            cg = cand[g * rg:(g + 1) * rg]

            def body(j, cnt, g=g, cg=cg):
                kj = keys_ref[j, pl.ds(g * rg, rg), :]
                for r in range(reps):
                    cnt = cnt + jnp.where(kj[:, r * 128:(r + 1) * 128] >= cg, 1.0, 0.0)
                return cnt

            parts.append(lax.fori_loop(0, jd + 1, body, jnp.zeros((rg, HEAD_DIM), F32)))
        return jnp.sum(jnp.concatenate(parts, axis=0), axis=-1, keepdims=True)

    kf = float(topk)
    zero = jnp.zeros((tq, HEAD_DIM), jnp.int32)
    res = jnp.where(count_ge(zero) >= kf, zero, jnp.int32(_INT32_MIN))
    for bit in range(30, -1, -1):
        cand = res + jnp.int32(1 << bit)
        res = jnp.where(count_ge(cand) >= kf, cand, res)
    thr = jnp.maximum(res, jnp.int32(_KEY_LOWEST_FINITE))
    thr = jnp.concatenate([thr] * reps, axis=1)

    def write_mask(j, c):
        mask_ref[j] = jnp.where(keys_ref[j] >= thr, 0.0, MASKED_LOGIT).astype(mask_ref.dtype)
        return c

    lax.fori_loop(0, jd + 1, write_mask, 0)

    def write_hidden(j, c):
        mask_ref[j] = jnp.full((tq, tk), MASKED_LOGIT, mask_ref.dtype)
        return c

    lax.fori_loop(jd + 1, nkt, write_hidden, 0)


def _dsa_flash_kernel(q_ref, k_ref, v_ref, mask_ref, bnear_ref, o_ref, m_ref, l_ref, acc_ref,
                      *, tq, tk, hp, past, scale2):
    i = pl.program_id(2)
    jd = (past + i * tq) // tk
    reps = tk // HEAD_DIM
    heads = [slice(hh * HEAD_DIM, (hh + 1) * HEAD_DIM) for hh in range(hp)]

    m_ref[...] = jnp.full_like(m_ref, MASKED_LOGIT)
    l_ref[...] = jnp.zeros_like(l_ref)
    acc_ref[...] = jnp.zeros_like(acc_ref)

    def step(j, near):
        row0 = pl.multiple_of(j * tk, tk)
        maskf = mask_ref[j].astype(F32)
        logits = []
        for hh, hs in enumerate(heads):
            s = _nt_dot(q_ref[:, hs], k_ref[pl.ds(row0, tk), hs]) * scale2 + maskf
            if near is not None:
                s = s + bnear_ref[hh, near]
            logits.append(s)
        probs, alphas = [], []
        for hs, s in zip(heads, logits):
            m_old = m_ref[:, hs]
            m_new = jnp.maximum(m_old, jnp.max(s, axis=-1, keepdims=True))
            alpha = jnp.exp2(m_old - m_new)
            p = jnp.exp2(s - jnp.concatenate([m_new] * reps, axis=1))
            l_ref[:, hs] = alpha * l_ref[:, hs] + jnp.sum(p, axis=-1, keepdims=True)
            m_ref[:, hs] = m_new
            probs.append(p.astype(BF16))
            alphas.append(alpha)
        for hs, p, alpha in zip(heads, probs, alphas):
            pv = jnp.dot(p, v_ref[pl.ds(row0, tk), hs], preferred_element_type=F32)
            acc_ref[:, hs] = alpha * acc_ref[:, hs] + pv

    step(jd, 0)

    @pl.when(jd >= 1)
    def _():
        step(jd - 1, 1)

    def body(j, c):
        step(j, None)
        return c

    lax.fori_loop(0, jnp.maximum(jd - 1, 0), body, 0)
    o_ref[...] = (acc_ref[...] / l_ref[...]).astype(o_ref.dtype)


def _t5_bucket(rel):
    nb = N_BUCKETS // 2
    max_exact = nb // 2
    side = jnp.where(rel > 0, nb, 0)
    n = jnp.abs(rel)
    nf = jnp.maximum(n, 1).astype(F32)
    large = max_exact + (jnp.log(nf / max_exact) / math.log(MAX_DISTANCE / max_exact) * (nb - max_exact)).astype(jnp.int32)
    large = jnp.minimum(large, nb - 1)
    return side + jnp.where(n < max_exact, n, large)


def _near_bias(rel_bias, *, tq, tk):
    r = jnp.arange(tq, dtype=jnp.int32)[:, None]
    c = jnp.arange(tk, dtype=jnp.int32)[None, :]
    bucket = _t5_bucket(jnp.stack([c - r, c - r - tk]))
    far_bucket = _t5_bucket(jnp.int32(-2 * tk))
    table = (rel_bias - rel_bias[far_bucket][None, :]) * LOG2E
    onehot = (bucket[..., None] == jnp.arange(N_BUCKETS, dtype=jnp.int32)).astype(F32)
    return jnp.einsum("dqkb,bh->hdqk", onehot, table, precision=lax.Precision.HIGHEST)


def _dsa_attention(q_idx, w_idx, k_idx_all, q, k_all, v_all, rel_bias, *, past, tq, tk, hp):
    B, T, W = q.shape
    H = W // HEAD_DIM
    s_pad = k_all.shape[1]
    nkt = s_pad // tk
    nq = T // tq
    assert T % tq == 0 and s_pad % tk == 0 and tk % CHUNK == 0 and tq % CHUNK == 0 and past % tk == 0
    assert tq == tk or nq == 1
    assert tk >= MAX_DISTANCE and H % hp == 0
    topk = min(TOPK_MAX, (past + T) // 4)

    kt = jnp.transpose(k_idx_all.reshape(B, nkt, tk, IDX_DIM), (0, 1, 3, 2))
    z = jnp.zeros_like(kt)
    ki2 = jnp.concatenate([jnp.concatenate([kt, z], axis=3), jnp.concatenate([z, kt], axis=3)], axis=2)
    resident = pl.Buffered(1) if nq > 1 else None

    mask = pl.pallas_call(
        functools.partial(_dsa_select_kernel, tq=tq, tk=tk, nkt=nkt, past=past, topk=topk),
        grid=(B, nq),
        in_specs=[
            pl.BlockSpec((None, tq, N_IDX_HEADS * IDX_DIM), lambda b, i: (b, i, 0)),
            pl.BlockSpec((None, tq, N_IDX_HEADS), lambda b, i: (b, i, 0)),
            pl.BlockSpec((None, nkt, 2 * IDX_DIM, 2 * tk), lambda b, i: (b, 0, 0, 0), pipeline_mode=resident),
        ],
        out_specs=pl.BlockSpec((None, None, nkt, tq, tk), lambda b, i: (b, i, 0, 0, 0)),
        out_shape=jax.ShapeDtypeStruct((B, nq, nkt, tq, tk), BF16),
        scratch_shapes=[
            pltpu.VMEM((nkt, tq, tk), jnp.int32),
            pltpu.VMEM((N_IDX_HEADS, tq, HEAD_DIM), F32),
        ],
        compiler_params=_params("parallel", "parallel"),
        name="dsa_select",
    )(q_idx, w_idx, ki2)

    wp = hp * HEAD_DIM
    return pl.pallas_call(
        functools.partial(_dsa_flash_kernel, tq=tq, tk=tk, hp=hp, past=past, scale2=HEAD_DIM ** -0.5 * LOG2E),
        grid=(B, H // hp, nq),
        in_specs=[
            pl.BlockSpec((None, tq, wp), lambda b, h, i: (b, i, h)),
            pl.BlockSpec((None, s_pad, wp), lambda b, h, i: (b, 0, h), pipeline_mode=resident),
            pl.BlockSpec((None, s_pad, wp), lambda b, h, i: (b, 0, h), pipeline_mode=resident),
            pl.BlockSpec((None, None, nkt, tq, tk), lambda b, h, i: (b, i, 0, 0, 0)),
            pl.BlockSpec((hp, 2, tq, tk), lambda b, h, i: (h, 0, 0, 0), pipeline_mode=resident),
        ],
        out_specs=pl.BlockSpec((None, tq, wp), lambda b, h, i: (b, i, h)),
        out_shape=jax.ShapeDtypeStruct((B, T, W), BF16),
        scratch_shapes=[pltpu.VMEM((tq, wp), F32), pltpu.VMEM((tq, wp), F32), pltpu.VMEM((tq, wp), F32)],
        compiler_params=_params("parallel", "parallel", "arbitrary"),
        name="dsa_flash",
    )(q, k_all, v_all, mask, _near_bias(rel_bias, tq=tq, tk=tk))


def _cross_kernel(q_ref, k_ref, v_ref, o_ref, *, n_heads, scale):
    for hh in range(n_heads):
        sl = slice(hh * HEAD_DIM, (hh + 1) * HEAD_DIM)
        s = _nt_dot(q_ref[:, sl], k_ref[:, sl]) * scale
        p = jnp.exp(s - jnp.max(s, axis=-1, keepdims=True))
        denom = jnp.sum(p, axis=-1, keepdims=True)
        o = jnp.dot(p.astype(BF16), v_ref[:, sl], preferred_element_type=F32)
        o_ref[:, sl] = (o / denom).astype(o_ref.dtype)


def _cross_attention(q, mem_k, mem_v, *, tq=512):
    B, T, W = q.shape
    n_mem = mem_k.shape[1]
    tq = _pick_tile(T, tq)
    return pl.pallas_call(
        functools.partial(_cross_kernel, n_heads=W // HEAD_DIM, scale=HEAD_DIM ** -0.5),
        grid=(B, T // tq),
        in_specs=[
            pl.BlockSpec((None, tq, W), lambda b, i: (b, i, 0)),
            pl.BlockSpec((None, n_mem, W), lambda b, i: (b, 0, 0)),
            pl.BlockSpec((None, n_mem, W), lambda b, i: (b, 0, 0)),
        ],
        out_specs=pl.BlockSpec((None, tq, W), lambda b, i: (b, i, 0)),
        out_shape=jax.ShapeDtypeStruct((B, T, W), BF16),
        compiler_params=_params("parallel", "parallel"),
        name="cross_attention",
    )(q, mem_k, mem_v)


def _conv_kernel(a_ref, b_ref, w_ref, bias_ref, g_ref, o_ref, buf_ref, y_ref, *, tt, lane_tile):
    D = a_ref.shape[-1]
    buf_ref[0:tt, :] = a_ref[...]
    buf_ref[tt:tt + CONV_HALO, :] = b_ref[...]
    shift = CONV_HALO - (CONV_WIDTH - 1)
    for c0 in range(0, D, lane_tile):
        cs = slice(c0, c0 + lane_tile)
        acc = jnp.zeros((tt, lane_tile), F32)
        for k in range(CONV_WIDTH):
            acc = acc + buf_ref[k + shift:k + shift + tt, cs] * w_ref[k:k + 1, cs]
        y_ref[:, cs] = acc + bias_ref[:, cs]
    y = _rms_rows(y_ref[...], g_ref[...])
    o_ref[...] = (y * (1.0 / (1.0 + jnp.exp(-y)))).astype(o_ref.dtype)


def _conv_norm_swish(u_staged, w_dw, b_dw, g_norm, *, tt):
    B, tp, D = u_staged.shape
    T = tp - CONV_HALO
    assert T % tt == 0 and tt % CONV_HALO == 0
    hb = tt // CONV_HALO
    return pl.pallas_call(
        functools.partial(_conv_kernel, tt=tt, lane_tile=512),
        grid=(B, T // tt),
        in_specs=[
            pl.BlockSpec((None, tt, D), lambda b, i: (b, i, 0)),
            pl.BlockSpec((None, CONV_HALO, D), lambda b, i: (b, (i + 1) * hb, 0)),
            pl.BlockSpec((CONV_WIDTH, D), lambda b, i: (0, 0)),
            pl.BlockSpec((1, D), lambda b, i: (0, 0)),
            pl.BlockSpec((1, D), lambda b, i: (0, 0)),
        ],
        out_specs=pl.BlockSpec((None, tt, D), lambda b, i: (b, i, 0)),
        out_shape=jax.ShapeDtypeStruct((B, T, D), BF16),
        scratch_shapes=[pltpu.VMEM((tt + CONV_HALO, D), F32), pltpu.VMEM((tt, D), F32)],
        compiler_params=_params("parallel", "parallel"),
        name="conv_norm_swish",
    )(u_staged, u_staged, w_dw, b_dw.reshape(1, D), g_norm.reshape(1, D))


def _pad_rows(a, n):
    if a.shape[1] == n:
        return a
    return jnp.concatenate([a, jnp.zeros((a.shape[0], n - a.shape[1]) + a.shape[2:], a.dtype)], axis=1)


def kernel(x_prompt, x_sample, mem_prompt, cache_sb_k, cache_sb_v, cache_dsa_k, cache_dsa_v, cache_idx_k, cache_mem_k, cache_mem_v, state_conv, norm_mix, norm_cross, norm_mlp, norm_final, w_in_ab, w_out_ab, rel_bias, w_pw1, b_pw1, w_dw, b_dw, g_conv_norm, w_pw2, b_pw2, w_cq, w_mk, w_mv, w_co, w_up, w_down):
    Bp, Tp, D = x_prompt.shape
    Bs, Ts, _ = x_sample.shape
    past = cache_sb_k.shape[2]
    depth = norm_mix.shape[0]
    n_mem = mem_prompt.shape[1]
    sbw = cache_sb_k.shape[3] * HEAD_DIM
    dsw = cache_dsa_k.shape[3] * HEAD_DIM
    qiw = N_IDX_HEADS * IDX_DIM
    cw = w_cq.shape[2]
    Mp, Ms = Bp * Tp, Bs * Ts
    main_w = 3 * sbw + 3 * dsw + qiw

    x = jnp.concatenate([x_prompt.reshape(Mp, D), x_sample.reshape(Ms, D)], axis=0)
    mem_flat = mem_prompt.reshape(Bp * n_mem, D)

    tq_p = tk_p = _pick_tile(Tp, 256)
    tq_s, tk_s = Ts, 256
    s_pad_s = -(-(past + Ts) // tk_s) * tk_s

    outs = {k: [] for k in ("p_sb_k", "p_sb_v", "p_d_k", "p_d_v", "p_ki", "p_mem_k", "p_mem_v", "p_conv",
                            "s_sb_k", "s_sb_v", "s_d_k", "s_d_v", "s_ki", "s_conv")}

    def split_rows(a):
        return a[:Mp].reshape(Bp, Tp, -1), a[Mp:].reshape(Bs, Ts, -1)

    for l in range(depth):
        i = l // 2
        if l % 2 == 0:
            w_in = w_in_ab[i]
            w_main = w_in[:, :main_w].astype(BF16)
            w_tail = jnp.pad(w_in[:, main_w:], ((0, 0), (0, HEAD_DIM - (w_in.shape[1] - main_w)))).astype(BF16)
            proj = _matmul(x, w_main, gain=norm_mix[l])
            tail = _matmul(x, w_tail, gain=norm_mix[l])
            cols = np.cumsum([0, sbw, sbw, sbw, dsw, dsw, dsw, qiw])
            q_sb, k_sb, v_sb, q_d, k_d, v_d, q_i = (proj[:, cols[n]:cols[n + 1]] for n in range(7))
            k_i, w_i = tail[:, :IDX_DIM], tail[:, IDX_DIM:IDX_DIM + N_IDX_HEADS]

            hd = (cache_sb_k.shape[3], HEAD_DIM)
            for name, a in (("sb_k", k_sb), ("sb_v", v_sb), ("d_k", k_d), ("d_v", v_d)):
                ap, as_ = split_rows(a)
                outs["p_" + name].append(ap.reshape(Bp, Tp, *hd))
                outs["s_" + name].append(as_.reshape(Bs, Ts, *hd))
            kip, kis = split_rows(k_i)
            outs["p_ki"].append(kip)
            outs["s_ki"].append(kis)

            def bf(a):
                return a.astype(BF16)

            qsp, qss = split_rows(bf(q_sb))
            ksp, kss = split_rows(bf(k_sb))
            vsp, vss = split_rows(bf(v_sb))
            qdp, qds = split_rows(bf(q_d))
            kdp, kds = split_rows(bf(k_d))
            vdp, vds = split_rows(bf(v_d))
            qip, qis = split_rows(bf(q_i))
            kibp, kibs = split_rows(bf(k_i))
            wip, wis = split_rows(w_i)

            o_sb_p = _sb_attention(qsp, ksp, vsp, past=0, tq=tq_p, tk=tk_p, hp=SB_HEADS_PER_STEP)
            o_d_p = _dsa_attention(qip, wip, kibp, qdp, kdp, vdp, rel_bias, past=0, tq=tq_p, tk=tk_p,
                                   hp=DSA_HEADS_PER_STEP)

            def with_past(cache, new):
                full = jnp.concatenate([bf(cache[i]).reshape(Bs, past, -1), new], axis=1)
                return _pad_rows(full, s_pad_s)

            o_sb_s = _sb_attention(qss, with_past(cache_sb_k, kss), with_past(cache_sb_v, vss),
                                   past=past, tq=tq_s, tk=tk_s, hp=SB_HEADS_PER_STEP)
            o_d_s = _dsa_attention(qis, wis, with_past(cache_idx_k, kibs), qds, with_past(cache_dsa_k, kds),
                                   with_past(cache_dsa_v, vds), rel_bias, past=past, tq=tq_s, tk=tk_s,
                                   hp=DSA_HEADS_PER_STEP)

            o = jnp.concatenate([
                jnp.concatenate([o_sb_p, o_d_p], axis=-1).reshape(Mp, sbw + dsw),
                jnp.concatenate([o_sb_s, o_d_s], axis=-1).reshape(Ms, sbw + dsw)], axis=0)
            x = _matmul(o, w_out_ab[i].astype(BF16), residual=x)
        else:
            u = _matmul(x, w_pw1[i].astype(BF16), gain=norm_mix[l], bias=b_pw1[i], glu=True)
            u_p, u_s = split_rows(u)
            front = CONV_HALO - (CONV_WIDTH - 1)
            staged_p = jnp.concatenate([jnp.zeros((Bp, CONV_HALO, D), F32), u_p], axis=1)
            staged_s = jnp.concatenate([jnp.zeros((Bs, front, D), F32), state_conv[i], u_s], axis=1)
            y_p = _conv_norm_swish(staged_p, w_dw[i], b_dw[i], g_conv_norm[i], tt=_pick_tile(Tp, 256))
            y_s = _conv_norm_swish(staged_s, w_dw[i], b_dw[i], g_conv_norm[i], tt=Ts)
            outs["p_conv"].append(staged_p[:, -(CONV_WIDTH - 1):])
            outs["s_conv"].append(staged_s[:, -(CONV_WIDTH - 1):])
            y = jnp.concatenate([y_p.reshape(Mp, D), y_s.reshape(Ms, D)], axis=0)
            x = _matmul(y, w_pw2[i].astype(BF16), bias=b_pw2[i], residual=x)

        mk = _matmul(mem_flat, w_mk[l].astype(BF16))
        mv = _matmul(mem_flat, w_mv[l].astype(BF16))
        outs["p_mem_k"].append(mk.reshape(Bp, n_mem, cw // HEAD_DIM, HEAD_DIM))
        outs["p_mem_v"].append(mv.reshape(Bp, n_mem, cw // HEAD_DIM, HEAD_DIM))
        qc = _matmul(x, w_cq[l].astype(BF16), gain=norm_cross[l], out_dtype=BF16)
        qc_p, qc_s = split_rows(qc)
        oc_p = _cross_attention(qc_p, mk.astype(BF16).reshape(Bp, n_mem, cw), mv.astype(BF16).reshape(Bp, n_mem, cw))
        oc_s = _cross_attention(qc_s, cache_mem_k[l].astype(BF16).reshape(Bs, n_mem, cw),
                                cache_mem_v[l].astype(BF16).reshape(Bs, n_mem, cw))
        oc = jnp.concatenate([oc_p.reshape(Mp, cw), oc_s.reshape(Ms, cw)], axis=0)
        x = _matmul(oc, w_co[l].astype(BF16), residual=x)

        x = _mlp(x, norm_mlp[l], w_up[l].astype(BF16), w_down[l].astype(BF16))

    y = _rmsnorm(x, norm_final)
    y_p, y_s = split_rows(y)
    st = lambda k: jnp.stack(outs[k])
    return (y_p, y_s, st("p_sb_k"), st("p_sb_v"), st("p_d_k"), st("p_d_v"), st("p_ki"), st("p_mem_k"),
            st("p_mem_v"), st("p_conv"), st("s_sb_k"), st("s_sb_v"), st("s_d_k"), st("s_d_v"), st("s_ki"),
            st("s_conv"))
```

```python
import functools
import math
from typing import NamedTuple

import numpy as np
import jax
import jax.numpy as jnp
from jax import lax
from jax.experimental import pallas as pl
from jax.experimental.pallas import tpu as pltpu

F32 = jnp.float32
BF16 = jnp.bfloat16

HEAD_DIM = 128
IDX_DIM = 64
N_IDX_HEADS = 16
CHUNK = 64
TOPK_MAX = 256
N_BUCKETS = 32
MAX_DISTANCE = 128
CONV_WIDTH = 31
CONV_HALO = 32
RMS_EPS = 1e-6
MASKED_LOGIT = -1e30
LOG2E = math.log2(math.e)
_INT32_MIN = -2 ** 31
VMEM_LIMIT_BYTES = 56 * 1024 * 1024
SB_HEADS_PER_STEP = 8
DSA_HEADS_PER_STEP = 4
ATTN_TILE = 256


class _Group(NamedTuple):
    batch: int
    seq: int
    row0: int
    past: int


def _params(*semantics):
    return pltpu.CompilerParams(dimension_semantics=semantics, vmem_limit_bytes=VMEM_LIMIT_BYTES)


def _pick_tile(n, pref):
    if n <= pref:
        return n
    t = pref
    while n % t:
        t //= 2
    return t


def _chained(prev):
    if prev is None:
        return [], [], None
    return [prev], [pl.BlockSpec(memory_space=pl.ANY)], True


def _rms_rows(x, g):
    ms = jnp.mean(x * x, axis=-1, keepdims=True)
    return x * lax.rsqrt(ms + RMS_EPS) * g


def _mm_kernel(*refs, has_norm, has_bias, has_res, glu):
    it = iter(refs)
    x_ref = next(it)
    g_ref = next(it) if has_norm else None
    w_ref = next(it)
    w2_ref = next(it) if glu else None
    b_ref = next(it) if has_bias else None
    b2_ref = next(it) if (has_bias and glu) else None
    r_ref = next(it) if has_res else None
    o_ref = next(it)
    xn_ref = next(it) if has_norm else None

    if has_norm:
        @pl.when(pl.program_id(1) == 0)
        def _():
            xn_ref[...] = _rms_rows(x_ref[...], g_ref[...]).astype(BF16)
        xb = xn_ref[...]
    else:
        xb = x_ref[...].astype(BF16)

    y = jnp.dot(xb, w_ref[...], preferred_element_type=F32)
    if has_bias:
        y = y + b_ref[...]
    if glu:
        gate = jnp.dot(xb, w2_ref[...], preferred_element_type=F32)
        if has_bias:
            gate = gate + b2_ref[...]
        y = y * (1.0 / (1.0 + jnp.exp(-gate)))
    if has_res:
        y = y + r_ref[...]
    o_ref[...] = y.astype(o_ref.dtype)


def _matmul(x, w, *, gain=None, bias=None, residual=None, glu=False, out_dtype=F32, tm=512, tn=512):
    M, K = x.shape
    n_out = w.shape[1] // 2 if glu else w.shape[1]
    tm = _pick_tile(M, tm)
    tn = _pick_tile(n_out, tn)
    nj = n_out // tn
    has_norm, has_bias, has_res = gain is not None, bias is not None, residual is not None

    args = [x]
    specs = [pl.BlockSpec((tm, K), lambda i, j: (i, 0))]
    if has_norm:
        args.append(gain.reshape(1, K))
        specs.append(pl.BlockSpec((1, K), lambda i, j: (0, 0)))
    args.append(w)
    specs.append(pl.BlockSpec((K, tn), lambda i, j: (0, j)))
    if glu:
        args.append(w)
        specs.append(pl.BlockSpec((K, tn), lambda i, j: (0, j + nj)))
    if has_bias:
        b2d = bias.reshape(1, -1)
        args.append(b2d)
        specs.append(pl.BlockSpec((1, tn), lambda i, j: (0, j)))
        if glu:
            args.append(b2d)
            specs.append(pl.BlockSpec((1, tn), lambda i, j: (0, j + nj)))
    if has_res:
        args.append(residual)
        specs.append(pl.BlockSpec((tm, tn), lambda i, j: (i, j)))

    return pl.pallas_call(
        functools.partial(_mm_kernel, has_norm=has_norm, has_bias=has_bias, has_res=has_res, glu=glu),
        grid=(M // tm, nj),
        in_specs=specs,
        out_specs=pl.BlockSpec((tm, tn), lambda i, j: (i, j)),
        out_shape=jax.ShapeDtypeStruct((M, n_out), out_dtype),
        scratch_shapes=[pltpu.VMEM((tm, K), BF16)] if has_norm else [],
        compiler_params=_params("parallel", "arbitrary"),
        name="norm_matmul",
    )(*args)


def _in_proj_kernel(x_ref, g_ref, w_ref, o16_ref, o32_ref, xn_ref, *, n32_tiles):
    j = pl.program_id(1)

    @pl.when(j == 0)
    def _():
        xn_ref[...] = _rms_rows(x_ref[...], g_ref[...]).astype(BF16)

    y = jnp.dot(xn_ref[...], w_ref[...], preferred_element_type=F32)
    o16_ref[...] = y.astype(BF16)

    @pl.when(j < n32_tiles)
    def _():
        o32_ref[...] = y


def _in_proj(x, w, gain, *, n32, tm=512, tn=512):
    M, K = x.shape
    N = w.shape[1]
    tm = _pick_tile(M, tm)
    assert N % tn == 0 and n32 % tn == 0
    n32_tiles = n32 // tn
    return pl.pallas_call(
        functools.partial(_in_proj_kernel, n32_tiles=n32_tiles),
        grid=(M // tm, N // tn),
        in_specs=[
            pl.BlockSpec((tm, K), lambda i, j: (i, 0)),
            pl.BlockSpec((1, K), lambda i, j: (0, 0)),
            pl.BlockSpec((K, tn), lambda i, j: (0, j)),
        ],
        out_specs=[
            pl.BlockSpec((tm, tn), lambda i, j: (i, j)),
            pl.BlockSpec((tm, tn), lambda i, j: (i, jnp.minimum(j, n32_tiles - 1))),
        ],
        out_shape=[jax.ShapeDtypeStruct((M, N), BF16), jax.ShapeDtypeStruct((M, n32), F32)],
        scratch_shapes=[pltpu.VMEM((tm, K), BF16)],
        compiler_params=_params("parallel", "arbitrary"),
        name="in_proj",
    )(x, gain.reshape(1, K), w)


def _mlp_kernel(x_ref, g_ref, wu_ref, wd_ref, o_ref, xn_ref, acc_ref):
    f = pl.program_id(1)

    @pl.when(f == 0)
    def _():
        xn_ref[...] = _rms_rows(x_ref[...], g_ref[...]).astype(BF16)
        acc_ref[...] = jnp.zeros_like(acc_ref)

    h = jnp.dot(xn_ref[...], wu_ref[...], preferred_element_type=F32)
    h = jnp.maximum(h, 0.0)
    h = (h * h).astype(BF16)
    acc_ref[...] += jnp.dot(h, wd_ref[...], preferred_element_type=F32)

    @pl.when(f == pl.num_programs(1) - 1)
    def _():
        o_ref[...] = x_ref[...] + acc_ref[...]


def _mlp(x, gain, w_up, w_down, *, tm=512, tf=512):
    M, D = x.shape
    F = w_up.shape[1]
    tm = _pick_tile(M, tm)
    tf = _pick_tile(F, tf)
    return pl.pallas_call(
        _mlp_kernel,
        grid=(M // tm, F // tf),
        in_specs=[
            pl.BlockSpec((tm, D), lambda i, f: (i, 0)),
            pl.BlockSpec((1, D), lambda i, f: (0, 0)),
            pl.BlockSpec((D, tf), lambda i, f: (0, f)),
            pl.BlockSpec((tf, D), lambda i, f: (f, 0)),
        ],
        out_specs=pl.BlockSpec((tm, D), lambda i, f: (i, 0)),
        out_shape=jax.ShapeDtypeStruct((M, D), F32),
        scratch_shapes=[pltpu.VMEM((tm, D), BF16), pltpu.VMEM((tm, D), F32)],
        compiler_params=_params("parallel", "arbitrary"),
        name="mlp",
    )(x, gain.reshape(1, D), w_up, w_down)


def _final_norm_kernel(x_ref, g_ref, op_ref, os_ref, *, n_prompt_tiles):
    y = _rms_rows(x_ref[...], g_ref[...])

    @pl.when(pl.program_id(0) < n_prompt_tiles)
    def _():
        op_ref[...] = y

    @pl.when(pl.program_id(0) >= n_prompt_tiles)
    def _():
        os_ref[...] = y


def _final_norm(x, gain, m_prompt):
    M, D = x.shape
    m_sample = M - m_prompt
    tm = _pick_tile(m_sample, 512)
    assert m_prompt % tm == 0 and m_sample % tm == 0
    npt = m_prompt // tm
    return pl.pallas_call(
        functools.partial(_final_norm_kernel, n_prompt_tiles=npt),
        grid=(M // tm,),
        in_specs=[pl.BlockSpec((tm, D), lambda i: (i, 0)), pl.BlockSpec((1, D), lambda i: (0, 0))],
        out_specs=[pl.BlockSpec((tm, D), lambda i: (jnp.minimum(i, npt - 1), 0)),
                   pl.BlockSpec((tm, D), lambda i: (jnp.maximum(i - npt, 0), 0))],
        out_shape=[jax.ShapeDtypeStruct((m_prompt, D), F32), jax.ShapeDtypeStruct((m_sample, D), F32)],
        compiler_params=_params("arbitrary"),
        name="final_norm",
    )(x, gain.reshape(1, D))


def _nt_dot(a, b):
    return lax.dot_general(a, b, (((1,), (1,)), ((), ())), preferred_element_type=F32)


def _kv_operands(grp, proj, k_col, v_col, k_cache, v_cache, *, tk, hp):
    wp = hp * HEAD_DIM
    if grp.past == 0:
        blk0 = grp.row0 // grp.seq
        args = [proj, proj]
        specs = [pl.BlockSpec((grp.seq, wp), lambda b, h, i, c=k_col // wp: (blk0 + b, c + h),
                              pipeline_mode=pl.Buffered(1)),
                 pl.BlockSpec((grp.seq, wp), lambda b, h, i, c=v_col // wp: (blk0 + b, c + h),
                              pipeline_mode=pl.Buffered(1))]
        return args, specs

    def new_tile(col):
        rows = lax.slice(proj, (grp.row0, col), (grp.row0 + grp.batch * grp.seq, col + k_cache.shape[2] * HEAD_DIM))
        rows = rows.reshape(grp.batch, grp.seq, -1)
        return jnp.pad(rows, ((0, 0), (0, tk - grp.seq), (0, 0)))

    args = [new_tile(k_col), new_tile(v_col)]
    specs = [pl.BlockSpec((None, tk, wp), lambda b, h, i: (b, 0, h))] * 2
    n_heads = k_cache.shape[2]
    for cache in (k_cache, v_cache):
        args.append(cache.reshape(grp.batch, grp.past * n_heads, HEAD_DIM))
        specs.append(pl.BlockSpec((None, grp.past * n_heads, HEAD_DIM), lambda b, h, i: (b, 0, 0)))
    return args, specs


def _split_kv_refs(refs, has_past):
    if not has_past:
        return refs[0], refs[1], None, None, refs[2:]
    return refs[0], refs[1], refs[2], refs[3], refs[4:]


def _tile_reader(new_ref, past_ref, *, tk, hp, n_heads):
    def read(j, hh, diag):
        hs = slice(hh * HEAD_DIM, (hh + 1) * HEAD_DIM)
        if past_ref is None:
            return new_ref[pl.ds(pl.multiple_of(j * tk, tk), tk), hs]
        if diag:
            return new_ref[:, hs]
        head = pl.program_id(1) * hp + hh
        return past_ref[pl.ds(j * (tk * n_heads) + head, tk, stride=n_heads), :].astype(BF16)
    return read


def _sb_kernel(q_ref, *refs, tq, tk, hp, n_heads, past, has_past, scale2):
    k_new, v_new, k_past, v_past, rest = _split_kv_refs(refs, has_past)
    u_ref, o_ref, acc_ref, carry_ref = rest[0], rest[-3], rest[-2], rest[-1]
    read_k = _tile_reader(k_new, k_past, tk=tk, hp=hp, n_heads=n_heads)
    read_v = _tile_reader(v_new, v_past, tk=tk, hp=hp, n_heads=n_heads)

    i = pl.program_id(2)
    q_start = past + i * tq
    jd = q_start // tk
    reps = tk // HEAD_DIM

    acc_ref[...] = jnp.zeros_like(acc_ref)
    carry_ref[...] = jnp.zeros_like(carry_ref)

    def step(j, masked):
        if masked:
            t_pos = q_start + lax.broadcasted_iota(jnp.int32, (tq, tk), 0)
            s_pos = j * tk + lax.broadcasted_iota(jnp.int32, (tq, tk), 1)
            causal = t_pos > s_pos
        heads = [slice(hh * HEAD_DIM, (hh + 1) * HEAD_DIM) for hh in range(hp)]
        z2s = [_nt_dot(q_ref[:, hs], read_k(j, hh, masked)) * scale2 for hh, hs in enumerate(heads)]
        ns, log2_betas, css = [], [], []
        for z2 in z2s:
            neg_abs = lax.bitcast_convert_type(
                lax.bitcast_convert_type(z2, jnp.int32) | jnp.int32(_INT32_MIN), F32)
            n = jnp.maximum(z2, 0.0) + jnp.log(1.0 + jnp.exp2(neg_abs)) * LOG2E
            log2_betas.append(z2 - n)
            if masked:
                n = jnp.where(causal, n, 0.0)
            hi = n.astype(BF16)
            lo = (n - hi.astype(F32)).astype(BF16)
            css.append(jnp.dot(jnp.concatenate([hi, lo], axis=1), u_ref[...], preferred_element_type=F32))
            ns.append(n)
        for hh, (hs, n, log2_beta, cs) in enumerate(zip(heads, ns, log2_betas, css)):
            carry = carry_ref[:, hs]
            a = jnp.exp2(log2_beta - cs - jnp.concatenate([carry] * reps, axis=1))
            if masked:
                a = jnp.where(causal, a, 0.0)
            acc_ref[:, hs] += jnp.dot(a.astype(BF16), read_v(j, hh, masked), preferred_element_type=F32)
            carry_ref[:, hs] = carry + jnp.broadcast_to(cs[:, 0:1] + n[:, 0:1], (tq, HEAD_DIM))

    step(jd, True)

    def body(t, c):
        step(jd - 1 - t, False)
        return c

    lax.fori_loop(0, jd, body, 0)
    o_ref[...] = acc_ref[...].astype(o_ref.dtype)


def _suffix_sum_matrix(tk):
    r = np.arange(tk)
    upper = (r[:, None] > r[None, :]).astype(np.float32)
    return jnp.asarray(np.concatenate([upper, upper], axis=0), dtype=BF16)


def _attn_tiles(grp):
    tq = _pick_tile(grp.seq, ATTN_TILE)
    tk = ATTN_TILE if grp.past else tq
    nq = grp.seq // tq
    assert grp.seq % tq == 0 and tk % tq == 0 and grp.past % tk == 0 and grp.row0 % tq == 0
    assert grp.past == 0 or nq == 1
    assert grp.past > 0 or grp.row0 % grp.seq == 0
    return tq, tk, nq


def _sb_attention(grp, proj, cols, k_cache, v_cache, o_prev, o_shape, o_col, *, hp):
    H = k_cache.shape[2]
    tq, tk, nq = _attn_tiles(grp)
    wp = hp * HEAD_DIM
    q_blk0 = grp.row0 // tq
    kv_args, kv_specs = _kv_operands(grp, proj, cols[1], cols[2], k_cache, v_cache, tk=tk, hp=hp)
    prev_args, prev_specs, aliased = _chained(o_prev)
    n_in = 1 + len(kv_args) + 1
    return pl.pallas_call(
        functools.partial(_sb_kernel, tq=tq, tk=tk, hp=hp, n_heads=H, past=grp.past, has_past=grp.past > 0,
                          scale2=HEAD_DIM ** -0.5 * LOG2E),
        grid=(grp.batch, H // hp, nq),
        in_specs=[pl.BlockSpec((tq, wp), lambda b, h, i: (q_blk0 + b * nq + i, cols[0] // wp + h))]
        + kv_specs
        + [pl.BlockSpec((2 * tk, tk), lambda b, h, i: (0, 0), pipeline_mode=pl.Buffered(1))]
        + prev_specs,
        out_specs=pl.BlockSpec((tq, wp), lambda b, h, i: (q_blk0 + b * nq + i, o_col // wp + h)),
        out_shape=jax.ShapeDtypeStruct(o_shape, BF16),
        input_output_aliases={n_in: 0} if aliased else {},
        scratch_shapes=[pltpu.VMEM((tq, wp), F32), pltpu.VMEM((tq, wp), F32)],
        compiler_params=_params("parallel", "parallel", "arbitrary"),
        name="sb_attention",
    )(proj, *kv_args, _suffix_sum_matrix(tk), *prev_args)


def _sortable_key(x):
    b = lax.bitcast_convert_type(x, jnp.int32)
    return b ^ ((b >> 31) & jnp.int32(0x7FFFFFFF))


COUNT_ROW_GROUP = 128


def _dsa_select_kernel(qi_ref, wi_ref, ki2_ref, mask_ref, hi_ref, lo_ref, wb_ref, *, tq, tk, nkt, past, topk):
    i = pl.program_id(1)
    q_start = past + i * tq
    jd = q_start // tk
    reps = tk // HEAD_DIM
    i16 = jnp.int16
    lowest = i16(-2 ** 15)

    w = wi_ref[:, IDX_DIM:IDX_DIM + N_IDX_HEADS] * (IDX_DIM ** -0.5 * N_IDX_HEADS ** -0.5)
    for hh in range(N_IDX_HEADS):
        wb_ref[hh] = jnp.broadcast_to(w[:, hh:hh + 1], (tq, HEAD_DIM))

    t_pos = q_start + lax.broadcasted_iota(jnp.int32, (tq, tk), 0)
    s_pos = jd * tk + lax.broadcasted_iota(jnp.int32, (tq, tk), 1)
    visible = (t_pos // CHUNK) >= (s_pos // CHUNK)

    def store_keys(j, key):
        hi_ref[j] = (key >> 16).astype(i16)
        lo_ref[j] = key.astype(i16) ^ lowest

    def idx_tile(j):
        kt = ki2_ref[j]
        tot = jnp.zeros((tq, tk), F32)
        for p in range(N_IDX_HEADS // 2):
            sc = jnp.dot(qi_ref[:, p * 128:(p + 1) * 128], kt, preferred_element_type=F32)
            w0 = jnp.concatenate([wb_ref[2 * p]] * reps, axis=1)
            w1 = jnp.concatenate([wb_ref[2 * p + 1]] * reps, axis=1)
            tot = tot + w0 * jnp.maximum(sc[:, :tk], 0.0) + w1 * jnp.maximum(sc[:, tk:], 0.0)
        return _sortable_key(tot)

    def idx_body(j, c):
        store_keys(j, idx_tile(j))
        return c

    lax.fori_loop(0, jd, idx_body, 0)
    store_keys(jd, jnp.where(visible, idx_tile(jd), jnp.int32(_INT32_MIN)))

    n_vis = jd + 1

    @pl.when(n_vis % 2 == 1)
    def _():
        store_keys(n_vis, jnp.full((tq, tk), _INT32_MIN, jnp.int32))

    n_pairs = (n_vis + 1) // 2
    rg = min(COUNT_ROW_GROUP, tq)

    def count(data_ref, cand, strict=False):
        parts = []
        for g in range(tq // rg):
            cg = cand[g * rg:(g + 1) * rg]

            def body(t, cnt, g=g, cg=cg):
                for jj in (2 * t, 2 * t + 1):
                    kj = data_ref[jj, pl.ds(g * rg, rg), :]
                    for r in range(reps):
                        blk = kj[:, r * 128:(r + 1) * 128]
                        hit = (blk > cg) if strict else (blk >= cg)
                        cnt = cnt + jnp.where(hit, i16(1), i16(0))
                return cnt

            parts.append(lax.fori_loop(0, n_pairs, body, jnp.zeros((rg, HEAD_DIM), i16)))
        return jnp.sum(jnp.concatenate(parts, axis=0).astype(F32), axis=-1, keepdims=True)

    def search(data_ref, need):
        zero = jnp.zeros((tq, HEAD_DIM), jnp.int32)
        res = jnp.where(count(data_ref, zero.astype(i16)) >= need, zero, jnp.int32(-2 ** 15))
        for bit in range(14, -1, -1):
            cand = res + jnp.int32(1 << bit)
            res = jnp.where(count(data_ref, cand.astype(i16)) >= need, cand, res)
        return res

    kf = float(topk)
    hi_thr = search(hi_ref, kf).astype(i16)
    need = kf - count(hi_ref, hi_thr, strict=True)
    hi_thr_t = jnp.concatenate([hi_thr] * reps, axis=1)

    def keep_ties(j, c):
        lo_ref[j] = jnp.where(hi_ref[j] == hi_thr_t, lo_ref[j], lowest)
        return c

    lax.fori_loop(0, 2 * n_pairs, keep_ties, 0)
    lo_thr = search(lo_ref, need).astype(i16)
    lo_thr_t = jnp.concatenate([lo_thr] * reps, axis=1)

    def tile_mask(j):
        zero, masked = jnp.asarray(0.0, mask_ref.dtype), jnp.asarray(MASKED_LOGIT, mask_ref.dtype)
        return jnp.where(hi_ref[j] > hi_thr_t, zero, jnp.where(lo_ref[j] >= lo_thr_t, zero, masked))

    def write_mask(j, c):
        mask_ref[j] = tile_mask(j)
        return c

    lax.fori_loop(0, jd, write_mask, 0)
    mask_ref[jd] = jnp.where(visible, tile_mask(jd).astype(F32), MASKED_LOGIT).astype(mask_ref.dtype)

    def write_hidden(j, c):
        mask_ref[j] = jnp.full((tq, tk), MASKED_LOGIT, mask_ref.dtype)
        return c

    lax.fori_loop(jd + 1, nkt, write_hidden, 0)


def _dsa_flash_kernel(q_ref, *refs, tq, tk, hp, n_heads, past, has_past, scale2):
    k_new, v_new, k_past, v_past, rest = _split_kv_refs(refs, has_past)
    mask_ref, bnear_ref = rest[0], rest[1]
    o_ref, m_ref, l_ref, acc_ref = rest[-4], rest[-3], rest[-2], rest[-1]
    read_k = _tile_reader(k_new, k_past, tk=tk, hp=hp, n_heads=n_heads)
    read_v = _tile_reader(v_new, v_past, tk=tk, hp=hp, n_heads=n_heads)

    i = pl.program_id(2)
    jd = (past + i * tq) // tk
    reps = tk // HEAD_DIM
    heads = [slice(hh * HEAD_DIM, (hh + 1) * HEAD_DIM) for hh in range(hp)]

    m_ref[...] = jnp.full_like(m_ref, MASKED_LOGIT)
    l_ref[...] = jnp.zeros_like(l_ref)
    acc_ref[...] = jnp.zeros_like(acc_ref)

    def step(j, near):
        diag = near == 0
        maskf = mask_ref[j].astype(F32)
        logits = []
        for hh, hs in enumerate(heads):
            s = _nt_dot(q_ref[:, hs], read_k(j, hh, diag)) * scale2 + maskf
            if near is not None:
                s = s + bnear_ref[hh, near]
            logits.append(s)
        probs, alphas = [], []
        for hs, s in zip(heads, logits):
            m_old = m_ref[:, hs]
            m_new = jnp.maximum(m_old, jnp.max(s, axis=-1, keepdims=True))
            alpha = jnp.exp2(m_old - m_new)
            p = jnp.exp2(s - jnp.concatenate([m_new] * reps, axis=1))
            l_ref[:, hs] = alpha * l_ref[:, hs] + jnp.sum(p, axis=-1, keepdims=True)
            m_ref[:, hs] = m_new
            probs.append(p.astype(BF16))
            alphas.append(alpha)
        for hh, (hs, p, alpha) in enumerate(zip(heads, probs, alphas)):
            pv = jnp.dot(p, read_v(j, hh, diag), preferred_element_type=F32)
            acc_ref[:, hs] = alpha * acc_ref[:, hs] + pv

    step(jd, 0)

    @pl.when(jd >= 1)
    def _():
        step(jd - 1, 1)

    def body(j, c):
        step(j, None)
        return c

    lax.fori_loop(0, jnp.maximum(jd - 1, 0), body, 0)
    o_ref[...] = (acc_ref[...] / l_ref[...]).astype(o_ref.dtype)


def _t5_bucket(rel):
    nb = N_BUCKETS // 2
    max_exact = nb // 2
    side = jnp.where(rel > 0, nb, 0)
    n = jnp.abs(rel)
    nf = jnp.maximum(n, 1).astype(F32)
    large = max_exact + (jnp.log(nf / max_exact) / math.log(MAX_DISTANCE / max_exact) * (nb - max_exact)).astype(jnp.int32)
    large = jnp.minimum(large, nb - 1)
    return side + jnp.where(n < max_exact, n, large)


def _near_bias(rel_bias, *, tq, tk):
    r = jnp.arange(tq, dtype=jnp.int32)[:, None]
    c = jnp.arange(tk, dtype=jnp.int32)[None, :]
    bucket = _t5_bucket(jnp.stack([c - r, c - r - tk]))
    far_bucket = _t5_bucket(jnp.int32(-2 * tk))
    table = (rel_bias - rel_bias[far_bucket][None, :]) * LOG2E
    onehot = (bucket[..., None] == jnp.arange(N_BUCKETS, dtype=jnp.int32)).astype(F32)
    return jnp.einsum("dqkb,bh->hdqk", onehot, table, precision=lax.Precision.HIGHEST)


def _dsa_attention(grp, proj, tail, cols, k_idx_all, k_cache, v_cache, rel_bias, o_prev, o_shape, o_col, *, hp):
    H = k_cache.shape[2]
    tq, tk, nq = _attn_tiles(grp)
    s_pad = k_idx_all.shape[1]
    nkt = s_pad // tk
    assert tk % CHUNK == 0 and tq % CHUNK == 0 and s_pad % tk == 0
    assert tk >= MAX_DISTANCE and H % hp == 0
    topk = min(TOPK_MAX, (grp.past + grp.seq) // 4)
    q_blk0 = grp.row0 // tq
    qiw = N_IDX_HEADS * IDX_DIM

    kt = jnp.transpose(k_idx_all.reshape(grp.batch, nkt, tk, IDX_DIM), (0, 1, 3, 2))
    z = jnp.zeros_like(kt)
    ki2 = jnp.concatenate([jnp.concatenate([kt, z], axis=3), jnp.concatenate([z, kt], axis=3)], axis=2)
    resident = pl.Buffered(1) if nq > 1 else None

    mask = pl.pallas_call(
        functools.partial(_dsa_select_kernel, tq=tq, tk=tk, nkt=nkt, past=grp.past, topk=topk),
        grid=(grp.batch, nq),
        in_specs=[
            pl.BlockSpec((tq, qiw), lambda b, i: (q_blk0 + b * nq + i, cols[0] // qiw)),
            pl.BlockSpec((tq, HEAD_DIM), lambda b, i: (q_blk0 + b * nq + i, 0)),
            pl.BlockSpec((None, nkt, 2 * IDX_DIM, 2 * tk), lambda b, i: (b, 0, 0, 0), pipeline_mode=resident),
        ],
        out_specs=pl.BlockSpec((None, None, nkt, tq, tk), lambda b, i: (b, i, 0, 0, 0)),
        out_shape=jax.ShapeDtypeStruct((grp.batch, nq, nkt, tq, tk), BF16),
        scratch_shapes=[
            pltpu.VMEM((nkt + 1, tq, tk), jnp.int16),
            pltpu.VMEM((nkt + 1, tq, tk), jnp.int16),
            pltpu.VMEM((N_IDX_HEADS, tq, HEAD_DIM), F32),
        ],
        compiler_params=_params("parallel", "parallel"),
        name="dsa_select",
    )(proj, tail, ki2)

    wp = hp * HEAD_DIM
    kv_args, kv_specs = _kv_operands(grp, proj, cols[2], cols[3], k_cache, v_cache, tk=tk, hp=hp)
    prev_args, prev_specs, aliased = _chained(o_prev)
    n_in = 1 + len(kv_args) + 2
    return pl.pallas_call(
        functools.partial(_dsa_flash_kernel, tq=tq, tk=tk, hp=hp, n_heads=H, past=grp.past, has_past=grp.past > 0,
                          scale2=HEAD_DIM ** -0.5 * LOG2E),
        grid=(grp.batch, H // hp, nq),
        in_specs=[pl.BlockSpec((tq, wp), lambda b, h, i: (q_blk0 + b * nq + i, cols[1] // wp + h))]
        + kv_specs
        + [pl.BlockSpec((None, None, nkt, tq, tk), lambda b, h, i: (b, i, 0, 0, 0)),
           pl.BlockSpec((hp, 2, tq, tk), lambda b, h, i: (h, 0, 0, 0), pipeline_mode=resident)]
        + prev_specs,
        out_specs=pl.BlockSpec((tq, wp), lambda b, h, i: (q_blk0 + b * nq + i, o_col // wp + h)),
        out_shape=jax.ShapeDtypeStruct(o_shape, BF16),
        input_output_aliases={n_in: 0} if aliased else {},
        scratch_shapes=[pltpu.VMEM((tq, wp), F32), pltpu.VMEM((tq, wp), F32), pltpu.VMEM((tq, wp), F32)],
        compiler_params=_params("parallel", "parallel", "arbitrary"),
        name="dsa_flash",
    )(proj, *kv_args, mask, _near_bias(rel_bias, tq=tq, tk=tk), *prev_args)


def _cross_kernel(q_ref, k_ref, v_ref, *rest, n_heads, scale):
    o_ref = rest[-1]
    for hh in range(n_heads):
        sl = slice(hh * HEAD_DIM, (hh + 1) * HEAD_DIM)
        s = _nt_dot(q_ref[:, sl], k_ref[:, sl]) * scale
        p = jnp.exp(s - jnp.max(s, axis=-1, keepdims=True))
        denom = jnp.sum(p, axis=-1, keepdims=True)
        o = jnp.dot(p.astype(BF16), v_ref[:, sl], preferred_element_type=F32)
        o_ref[:, sl] = (o / denom).astype(o_ref.dtype)


def _cross_attention(grp, q, mem_k, mem_v, o_prev, *, tq=512):
    M, W = q.shape
    n_mem = mem_k.shape[1]
    tq = _pick_tile(grp.seq, tq)
    nq = grp.seq // tq
    assert grp.row0 % tq == 0
    blk0 = grp.row0 // tq
    prev_args, prev_specs, aliased = _chained(o_prev)
    return pl.pallas_call(
        functools.partial(_cross_kernel, n_heads=W // HEAD_DIM, scale=HEAD_DIM ** -0.5),
        grid=(grp.batch, nq),
        in_specs=[
            pl.BlockSpec((tq, W), lambda b, i: (blk0 + b * nq + i, 0)),
            pl.BlockSpec((None, n_mem, W), lambda b, i: (b, 0, 0)),
            pl.BlockSpec((None, n_mem, W), lambda b, i: (b, 0, 0)),
        ] + prev_specs,
        out_specs=pl.BlockSpec((tq, W), lambda b, i: (blk0 + b * nq + i, 0)),
        out_shape=jax.ShapeDtypeStruct((M, W), BF16),
        input_output_aliases={3: 0} if aliased else {},
        compiler_params=_params("parallel", "parallel"),
        name="cross_attention",
    )(q, mem_k, mem_v, *prev_args)


def _conv_kernel(a_ref, prev_ref, state_ref, w_ref, bias_ref, g_ref, *rest, tt, lane_tile):
    o_ref, buf_ref, y_ref = rest[-3], rest[-2], rest[-1]
    D = a_ref.shape[-1]

    @pl.when(pl.program_id(1) == 0)
    def _():
        buf_ref[0:CONV_HALO, :] = state_ref[...]

    @pl.when(pl.program_id(1) > 0)
    def _():
        buf_ref[0:CONV_HALO, :] = prev_ref[...]

    buf_ref[CONV_HALO:CONV_HALO + tt, :] = a_ref[...]
    first = CONV_HALO - (CONV_WIDTH - 1)
    for c0 in range(0, D, lane_tile):
        cs = slice(c0, c0 + lane_tile)
        acc = jnp.zeros((tt, lane_tile), F32)
        for k in range(CONV_WIDTH):
            acc = acc + buf_ref[first + k:first + k + tt, cs] * w_ref[k:k + 1, cs]
        y_ref[:, cs] = acc + bias_ref[:, cs]
    y = _rms_rows(y_ref[...], g_ref[...])
    o_ref[...] = (y * (1.0 / (1.0 + jnp.exp(-y)))).astype(o_ref.dtype)


def _conv_norm_swish(grp, u, state, w_dw, b_dw, g_norm, o_prev):
    M, D = u.shape
    tt = _pick_tile(grp.seq, 256)
    nt = grp.seq // tt
    assert grp.seq % tt == 0 and tt % CONV_HALO == 0 and grp.row0 % tt == 0
    blk0 = grp.row0 // tt
    hb = tt // CONV_HALO
    prev_args, prev_specs, aliased = _chained(o_prev)
    return pl.pallas_call(
        functools.partial(_conv_kernel, tt=tt, lane_tile=512),
        grid=(grp.batch, nt),
        in_specs=[
            pl.BlockSpec((tt, D), lambda b, i: (blk0 + b * nt + i, 0)),
            pl.BlockSpec((CONV_HALO, D), lambda b, i: (jnp.maximum((blk0 + b * nt + i) * hb - 1, 0), 0)),
            pl.BlockSpec((None, CONV_HALO, D), lambda b, i: (b, 0, 0)),
            pl.BlockSpec((CONV_WIDTH, D), lambda b, i: (0, 0)),
            pl.BlockSpec((1, D), lambda b, i: (0, 0)),
            pl.BlockSpec((1, D), lambda b, i: (0, 0)),
        ] + prev_specs,
        out_specs=pl.BlockSpec((tt, D), lambda b, i: (blk0 + b * nt + i, 0)),
        out_shape=jax.ShapeDtypeStruct((M, D), BF16),
        input_output_aliases={6: 0} if aliased else {},
        scratch_shapes=[pltpu.VMEM((tt + CONV_HALO, D), F32), pltpu.VMEM((tt, D), F32)],
        compiler_params=_params("parallel", "arbitrary"),
        name="conv_norm_swish",
    )(u, u, state, w_dw, b_dw.reshape(1, D), g_norm.reshape(1, D), *prev_args)


def kernel(x_prompt, x_sample, mem_prompt, cache_sb_k, cache_sb_v, cache_dsa_k, cache_dsa_v, cache_idx_k, cache_mem_k, cache_mem_v, state_conv, norm_mix, norm_cross, norm_mlp, norm_final, w_in_ab, w_out_ab, rel_bias, w_pw1, b_pw1, w_dw, b_dw, g_conv_norm, w_pw2, b_pw2, w_cq, w_mk, w_mv, w_co, w_up, w_down):
    Bp, Tp, D = x_prompt.shape
    Bs, Ts, _ = x_sample.shape
    past = cache_sb_k.shape[2]
    depth = norm_mix.shape[0]
    n_mem = mem_prompt.shape[1]
    n_sb, n_dsa = cache_sb_k.shape[3], cache_dsa_k.shape[3]
    sbw, dsw = n_sb * HEAD_DIM, n_dsa * HEAD_DIM
    qiw = N_IDX_HEADS * IDX_DIM
    cw = w_cq.shape[2]
    Mp, Ms = Bp * Tp, Bs * Ts
    M = Mp + Ms
    prompt = _Group(Bp, Tp, 0, 0)
    sample = _Group(Bs, Ts, Mp, past)

    x = jnp.concatenate([x_prompt.reshape(Mp, D), x_sample.reshape(Ms, D)], axis=0)
    mem_flat = mem_prompt.reshape(Bp * n_mem, D)

    src = dict(zip(("q_sb", "k_sb", "v_sb", "q_d", "k_d", "v_d", "q_i", "k_i", "w_i"),
                   np.cumsum([0, sbw, sbw, sbw, dsw, dsw, dsw, qiw, IDX_DIM])))
    order = ("k_sb", "v_sb", "k_d", "v_d", "q_sb", "q_d", "q_i")
    width = dict(q_sb=sbw, k_sb=sbw, v_sb=sbw, q_d=dsw, k_d=dsw, v_d=dsw, q_i=qiw)
    col = dict(zip(order, np.cumsum([0] + [width[n] for n in order])[:-1]))
    col = {k: int(v) for k, v in col.items()}
    n32 = 2 * sbw + 2 * dsw

    outs = {k: [] for k in ("p_sb_k", "p_sb_v", "p_d_k", "p_d_v", "p_ki", "p_mem_k", "p_mem_v", "p_conv",
                            "s_sb_k", "s_sb_v", "s_d_k", "s_d_v", "s_ki", "s_conv")}

    def split_rows(a):
        return a[:Mp].reshape(Bp, Tp, -1), a[Mp:].reshape(Bs, Ts, -1)

    for l in range(depth):
        i = l // 2
        if l % 2 == 0:
            w_in = w_in_ab[i]
            w_main = jnp.concatenate([w_in[:, src[n]:src[n] + width[n]] for n in order], axis=1).astype(BF16)
            tail_w = w_in.shape[1] - int(src["k_i"])
            w_tail = jnp.pad(w_in[:, src["k_i"]:], ((0, 0), (0, HEAD_DIM - tail_w))).astype(BF16)
            proj, kv32 = _in_proj(x, w_main, norm_mix[l], n32=n32)
            tail = _matmul(x, w_tail, gain=norm_mix[l])

            for name, key, nh in (("sb_k", "k_sb", n_sb), ("sb_v", "v_sb", n_sb),
                                  ("d_k", "k_d", n_dsa), ("d_v", "v_d", n_dsa)):
                ap, as_ = split_rows(kv32[:, col[key]:col[key] + width[key]])
                outs["p_" + name].append(ap.reshape(Bp, Tp, nh, HEAD_DIM))
                outs["s_" + name].append(as_.reshape(Bs, Ts, nh, HEAD_DIM))
            k_i = tail[:, :IDX_DIM]
            kip, kis = split_rows(k_i)
            outs["p_ki"].append(kip)
            outs["s_ki"].append(kis)

            tk_s = ATTN_TILE
            s_pad_s = -(-(past + Ts) // tk_s) * tk_s
            kidx_p = kip.astype(BF16)
            kidx_s = jnp.concatenate([cache_idx_k[i].astype(BF16), kis.astype(BF16),
                                      jnp.zeros((Bs, s_pad_s - past - Ts, IDX_DIM), BF16)], axis=1)

            o_shape = (M, sbw + dsw)
            sb_cols = (col["q_sb"], col["k_sb"], col["v_sb"])
            d_cols = (col["q_i"], col["q_d"], col["k_d"], col["v_d"])
            o = _sb_attention(prompt, proj, sb_cols, cache_sb_k[i], cache_sb_v[i], None, o_shape, 0,
                              hp=SB_HEADS_PER_STEP)
            o = _dsa_attention(prompt, proj, tail, d_cols, kidx_p, cache_dsa_k[i], cache_dsa_v[i], rel_bias,
                               o, o_shape, sbw, hp=DSA_HEADS_PER_STEP)
            o = _sb_attention(sample, proj, sb_cols, cache_sb_k[i], cache_sb_v[i], o, o_shape, 0,
                              hp=SB_HEADS_PER_STEP)
            o = _dsa_attention(sample, proj, tail, d_cols, kidx_s, cache_dsa_k[i], cache_dsa_v[i], rel_bias,
                               o, o_shape, sbw, hp=DSA_HEADS_PER_STEP)
            x = _matmul(o, w_out_ab[i].astype(BF16), residual=x)
        else:
            u = _matmul(x, w_pw1[i].astype(BF16), gain=norm_mix[l], bias=b_pw1[i], glu=True)
            front = CONV_HALO - (CONV_WIDTH - 1)
            state_p = jnp.zeros((Bp, CONV_HALO, D), F32)
            state_s = jnp.concatenate([jnp.zeros((Bs, front, D), F32), state_conv[i]], axis=1)
            y = _conv_norm_swish(prompt, u, state_p, w_dw[i], b_dw[i], g_conv_norm[i], None)
            y = _conv_norm_swish(sample, u, state_s, w_dw[i], b_dw[i], g_conv_norm[i], y)
            u_p, u_s = split_rows(u)
            keep = CONV_WIDTH - 1
            outs["p_conv"].append(jnp.concatenate([state_p, u_p], axis=1)[:, -keep:] if Tp < keep else u_p[:, -keep:])
            outs["s_conv"].append(jnp.concatenate([state_s, u_s], axis=1)[:, -keep:] if Ts < keep else u_s[:, -keep:])
            x = _matmul(y, w_pw2[i].astype(BF16), bias=b_pw2[i], residual=x)

        mk = _matmul(mem_flat, w_mk[l].astype(BF16))
        mv = _matmul(mem_flat, w_mv[l].astype(BF16))
        outs["p_mem_k"].append(mk.reshape(Bp, n_mem, cw // HEAD_DIM, HEAD_DIM))
        outs["p_mem_v"].append(mv.reshape(Bp, n_mem, cw // HEAD_DIM, HEAD_DIM))
        qc = _matmul(x, w_cq[l].astype(BF16), gain=norm_cross[l], out_dtype=BF16)
        oc = _cross_attention(prompt, qc, mk.astype(BF16).reshape(Bp, n_mem, cw),
                              mv.astype(BF16).reshape(Bp, n_mem, cw), None)
        oc = _cross_attention(sample, qc, cache_mem_k[l].astype(BF16).reshape(Bs, n_mem, cw),
                              cache_mem_v[l].astype(BF16).reshape(Bs, n_mem, cw), oc)
        x = _matmul(oc, w_co[l].astype(BF16), residual=x)

        x = _mlp(x, norm_mlp[l], w_up[l].astype(BF16), w_down[l].astype(BF16))

    y_p, y_s = _final_norm(x, norm_final, Mp)
    st = lambda k: jnp.stack(outs[k])
    return (y_p.reshape(Bp, Tp, D), y_s.reshape(Bs, Ts, D), st("p_sb_k"), st("p_sb_v"), st("p_d_k"), st("p_d_v"),
            st("p_ki"), st("p_mem_k"), st("p_mem_v"), st("p_conv"), st("s_sb_k"), st("s_sb_v"), st("s_d_k"),
            st("s_d_v"), st("s_ki"), st("s_conv"))
```

```python
import functools
import math
from typing import NamedTuple

import numpy as np
import jax
import jax.numpy as jnp
from jax import lax
from jax.experimental import pallas as pl
from jax.experimental.pallas import tpu as pltpu

F32 = jnp.float32
BF16 = jnp.bfloat16

HEAD_DIM = 128
IDX_DIM = 64
N_IDX_HEADS = 16
CHUNK = 64
TOPK_MAX = 256
N_BUCKETS = 32
MAX_DISTANCE = 128
CONV_WIDTH = 31
CONV_HALO = 32
RMS_EPS = 1e-6
MASKED_LOGIT = -1e30
LOG2E = math.log2(math.e)
_INT32_MIN = -2 ** 31
VMEM_LIMIT_BYTES = 56 * 1024 * 1024
SB_HEADS_PER_STEP = 8
DSA_HEADS_PER_STEP = 4
ATTN_TILE = 256


class _Group(NamedTuple):
    batch: int
    seq: int
    row0: int
    past: int


def _params(*semantics):
    return pltpu.CompilerParams(dimension_semantics=semantics, vmem_limit_bytes=VMEM_LIMIT_BYTES)


def _pick_tile(n, pref):
    if n <= pref:
        return n
    t = pref
    while n % t:
        t //= 2
    return t


def _chained(prev):
    if prev is None:
        return [], [], None
    return [prev], [pl.BlockSpec(memory_space=pl.ANY)], True


def _rms_rows(x, g):
    ms = jnp.mean(x * x, axis=-1, keepdims=True)
    return x * lax.rsqrt(ms + RMS_EPS) * g


def _mm_kernel(*refs, has_norm, has_bias, has_res, glu):
    it = iter(refs)
    x_ref = next(it)
    g_ref = next(it) if has_norm else None
    w_ref = next(it)
    w2_ref = next(it) if glu else None
    b_ref = next(it) if has_bias else None
    b2_ref = next(it) if (has_bias and glu) else None
    r_ref = next(it) if has_res else None
    o_ref = next(it)
    xn_ref = next(it) if has_norm else None

    if has_norm:
        @pl.when(pl.program_id(1) == 0)
        def _():
            xn_ref[...] = _rms_rows(x_ref[...], g_ref[...]).astype(BF16)
        xb = xn_ref[...]
    else:
        xb = x_ref[...].astype(BF16)

    y = jnp.dot(xb, w_ref[...], preferred_element_type=F32)
    if has_bias:
        y = y + b_ref[...]
    if glu:
        gate = jnp.dot(xb, w2_ref[...], preferred_element_type=F32)
        if has_bias:
            gate = gate + b2_ref[...]
        y = y * (1.0 / (1.0 + jnp.exp(-gate)))
    if has_res:
        y = y + r_ref[...]
    o_ref[...] = y.astype(o_ref.dtype)


def _matmul(x, w, *, layer=0, gain=None, bias=None, residual=None, glu=False, out_dtype=F32, tm=512, tn=512):
    M, K = x.shape
    n_out = w.shape[2] // 2 if glu else w.shape[2]
    tm = _pick_tile(M, tm)
    tn = _pick_tile(n_out, tn)
    nj = n_out // tn
    has_norm, has_bias, has_res = gain is not None, bias is not None, residual is not None

    args = [x]
    specs = [pl.BlockSpec((tm, K), lambda i, j: (i, 0))]
    if has_norm:
        args.append(gain.reshape(1, K))
        specs.append(pl.BlockSpec((1, K), lambda i, j: (0, 0)))
    args.append(w)
    specs.append(pl.BlockSpec((None, K, tn), lambda i, j: (layer, 0, j)))
    if glu:
        args.append(w)
        specs.append(pl.BlockSpec((None, K, tn), lambda i, j: (layer, 0, j + nj)))
    if has_bias:
        b2d = bias.reshape(1, -1)
        args.append(b2d)
        specs.append(pl.BlockSpec((1, tn), lambda i, j: (0, j)))
        if glu:
            args.append(b2d)
            specs.append(pl.BlockSpec((1, tn), lambda i, j: (0, j + nj)))
    if has_res:
        args.append(residual)
        specs.append(pl.BlockSpec((tm, tn), lambda i, j: (i, j)))

    return pl.pallas_call(
        functools.partial(_mm_kernel, has_norm=has_norm, has_bias=has_bias, has_res=has_res, glu=glu),
        grid=(M // tm, nj),
        in_specs=specs,
        out_specs=pl.BlockSpec((tm, tn), lambda i, j: (i, j)),
        out_shape=jax.ShapeDtypeStruct((M, n_out), out_dtype),
        scratch_shapes=[pltpu.VMEM((tm, K), BF16)] if has_norm else [],
        compiler_params=_params("parallel", "arbitrary"),
        name="norm_matmul",
    )(*args)


def _in_proj_kernel(x_ref, g_ref, w_ref, s_ref, o16_ref, o32_ref, xn_ref, *, n32_tiles):
    j = pl.program_id(1)

    @pl.when(j == 0)
    def _():
        xn_ref[...] = _rms_rows(x_ref[...], g_ref[...]).astype(BF16)

    y = jnp.dot(xn_ref[...], w_ref[...], preferred_element_type=F32)
    o16_ref[...] = (y * s_ref[...]).astype(BF16)

    @pl.when(j < n32_tiles)
    def _():
        o32_ref[...] = y


def _in_proj(x, w, gain, col_scale, *, n32, tm=512, tn=512):
    M, K = x.shape
    N = w.shape[1]
    tm = _pick_tile(M, tm)
    assert N % tn == 0 and n32 % tn == 0
    n32_tiles = n32 // tn
    return pl.pallas_call(
        functools.partial(_in_proj_kernel, n32_tiles=n32_tiles),
        grid=(M // tm, N // tn),
        in_specs=[
            pl.BlockSpec((tm, K), lambda i, j: (i, 0)),
            pl.BlockSpec((1, K), lambda i, j: (0, 0)),
            pl.BlockSpec((K, tn), lambda i, j: (0, j)),
            pl.BlockSpec((1, tn), lambda i, j: (0, j)),
        ],
        out_specs=[
            pl.BlockSpec((tm, tn), lambda i, j: (i, j)),
            pl.BlockSpec((tm, tn), lambda i, j: (i, jnp.minimum(j, n32_tiles - 1))),
        ],
        out_shape=[jax.ShapeDtypeStruct((M, N), BF16), jax.ShapeDtypeStruct((M, n32), F32)],
        scratch_shapes=[pltpu.VMEM((tm, K), BF16)],
        compiler_params=_params("parallel", "arbitrary"),
        name="in_proj",
    )(x, gain.reshape(1, K), w, col_scale.reshape(1, N))


def _mlp_kernel(x_ref, g_ref, wu_ref, wd_ref, o_ref, xn_ref, acc_ref):
    f = pl.program_id(1)

    @pl.when(f == 0)
    def _():
        xn_ref[...] = _rms_rows(x_ref[...], g_ref[...]).astype(BF16)
        acc_ref[...] = jnp.zeros_like(acc_ref)

    h = jnp.dot(xn_ref[...], wu_ref[...], preferred_element_type=F32)
    h = jnp.maximum(h, 0.0)
    h = (h * h).astype(BF16)
    acc_ref[...] += jnp.dot(h, wd_ref[...], preferred_element_type=F32)

    @pl.when(f == pl.num_programs(1) - 1)
    def _():
        o_ref[...] = x_ref[...] + acc_ref[...]


def _mlp(x, gain, w_up, w_down, layer, *, tm=512, tf=512):
    M, D = x.shape
    F = w_up.shape[2]
    tm = _pick_tile(M, tm)
    tf = _pick_tile(F, tf)
    return pl.pallas_call(
        _mlp_kernel,
        grid=(M // tm, F // tf),
        in_specs=[
            pl.BlockSpec((tm, D), lambda i, f: (i, 0)),
            pl.BlockSpec((1, D), lambda i, f: (0, 0)),
            pl.BlockSpec((None, D, tf), lambda i, f: (layer, 0, f)),
            pl.BlockSpec((None, tf, D), lambda i, f: (layer, f, 0)),
        ],
        out_specs=pl.BlockSpec((tm, D), lambda i, f: (i, 0)),
        out_shape=jax.ShapeDtypeStruct((M, D), F32),
        scratch_shapes=[pltpu.VMEM((tm, D), BF16), pltpu.VMEM((tm, D), F32)],
        compiler_params=_params("parallel", "arbitrary"),
        name="mlp",
    )(x, gain.reshape(1, D), w_up, w_down)


def _final_norm_kernel(x_ref, g_ref, op_ref, os_ref, *, n_prompt_tiles):
    y = _rms_rows(x_ref[...], g_ref[...])

    @pl.when(pl.program_id(0) < n_prompt_tiles)
    def _():
        op_ref[...] = y

    @pl.when(pl.program_id(0) >= n_prompt_tiles)
    def _():
        os_ref[...] = y


def _final_norm(x, gain, m_prompt):
    M, D = x.shape
    m_sample = M - m_prompt
    tm = _pick_tile(m_sample, 512)
    assert m_prompt % tm == 0 and m_sample % tm == 0
    npt = m_prompt // tm
    return pl.pallas_call(
        functools.partial(_final_norm_kernel, n_prompt_tiles=npt),
        grid=(M // tm,),
        in_specs=[pl.BlockSpec((tm, D), lambda i: (i, 0)), pl.BlockSpec((1, D), lambda i: (0, 0))],
        out_specs=[pl.BlockSpec((tm, D), lambda i: (jnp.minimum(i, npt - 1), 0)),
                   pl.BlockSpec((tm, D), lambda i: (jnp.maximum(i - npt, 0), 0))],
        out_shape=[jax.ShapeDtypeStruct((m_prompt, D), F32), jax.ShapeDtypeStruct((m_sample, D), F32)],
        compiler_params=_params("arbitrary"),
        name="final_norm",
    )(x, gain.reshape(1, D))


def _nt_dot(a, b):
    return lax.dot_general(a, b, (((1,), (1,)), ((), ())), preferred_element_type=F32)


def _kv_operands(grp, proj, k_col, v_col, k_cache, v_cache, *, tk, hp):
    wp = hp * HEAD_DIM
    if grp.past == 0:
        blk0 = grp.row0 // grp.seq
        args = [proj, proj]
        specs = [pl.BlockSpec((grp.seq, wp), lambda b, h, i, c=k_col // wp: (blk0 + b, c + h),
                              pipeline_mode=pl.Buffered(1)),
                 pl.BlockSpec((grp.seq, wp), lambda b, h, i, c=v_col // wp: (blk0 + b, c + h),
                              pipeline_mode=pl.Buffered(1))]
        return args, specs

    def new_tile(col):
        rows = lax.slice(proj, (grp.row0, col), (grp.row0 + grp.batch * grp.seq, col + k_cache.shape[2] * HEAD_DIM))
        rows = rows.reshape(grp.batch, grp.seq, -1)
        return jnp.pad(rows, ((0, 0), (0, tk - grp.seq), (0, 0)))

    args = [new_tile(k_col), new_tile(v_col)]
    specs = [pl.BlockSpec((None, tk, wp), lambda b, h, i: (b, 0, h))] * 2
    n_heads = k_cache.shape[2]
    for cache in (k_cache, v_cache):
        args.append(cache.reshape(grp.batch, grp.past * n_heads, HEAD_DIM))
        specs.append(pl.BlockSpec((None, grp.past * n_heads, HEAD_DIM), lambda b, h, i: (b, 0, 0)))
    return args, specs


def _split_kv_refs(refs, has_past):
    if not has_past:
        return refs[0], refs[1], None, None, refs[2:]
    return refs[0], refs[1], refs[2], refs[3], refs[4:]


def _tile_reader(new_ref, past_ref, *, tk, hp, n_heads):
    def read(j, hh, diag):
        hs = slice(hh * HEAD_DIM, (hh + 1) * HEAD_DIM)
        if past_ref is None:
            return new_ref[pl.ds(pl.multiple_of(j * tk, tk), tk), hs]
        if diag:
            return new_ref[:, hs]
        head = pl.program_id(1) * hp + hh
        return past_ref[pl.ds(j * (tk * n_heads) + head, tk, stride=n_heads), :].astype(BF16)
    return read


def _sb_kernel(q_ref, *refs, tq, tk, hp, n_heads, past, has_past):
    k_new, v_new, k_past, v_past, rest = _split_kv_refs(refs, has_past)
    u_ref, o_ref, acc_ref, carry_ref = rest[0], rest[-3], rest[-2], rest[-1]
    read_k = _tile_reader(k_new, k_past, tk=tk, hp=hp, n_heads=n_heads)
    read_v = _tile_reader(v_new, v_past, tk=tk, hp=hp, n_heads=n_heads)

    i = pl.program_id(2)
    q_start = past + i * tq
    jd = q_start // tk
    reps = tk // HEAD_DIM

    acc_ref[...] = jnp.zeros_like(acc_ref)
    carry_ref[...] = jnp.zeros_like(carry_ref)

    def step(j, masked):
        if masked:
            t_pos = q_start + lax.broadcasted_iota(jnp.int32, (tq, tk), 0)
            s_pos = j * tk + lax.broadcasted_iota(jnp.int32, (tq, tk), 1)
            causal = t_pos > s_pos
        heads = [slice(hh * HEAD_DIM, (hh + 1) * HEAD_DIM) for hh in range(hp)]
        z2s = [_nt_dot(q_ref[:, hs], read_k(j, hh, masked)) for hh, hs in enumerate(heads)]
        ns, log2_betas, css = [], [], []
        for z2 in z2s:
            neg_abs = lax.bitcast_convert_type(
                lax.bitcast_convert_type(z2, jnp.int32) | jnp.int32(_INT32_MIN), F32)
            n = jnp.maximum(z2, 0.0) + jnp.log(1.0 + jnp.exp2(neg_abs)) * LOG2E
            log2_betas.append(z2 - n)
            if masked:
                n = jnp.where(causal, n, 0.0)
            hi = n.astype(BF16)
            lo = (n - hi.astype(F32)).astype(BF16)
            css.append(jnp.dot(jnp.concatenate([hi, lo], axis=1), u_ref[...], preferred_element_type=F32))
            ns.append(n)
        for hh, (hs, n, log2_beta, cs) in enumerate(zip(heads, ns, log2_betas, css)):
            carry = carry_ref[:, hs]
            a = jnp.exp2(log2_beta - cs - jnp.concatenate([carry] * reps, axis=1))
            if masked:
                a = jnp.where(causal, a, 0.0)
            acc_ref[:, hs] += jnp.dot(a.astype(BF16), read_v(j, hh, masked), preferred_element_type=F32)
            carry_ref[:, hs] = carry + jnp.broadcast_to(cs[:, 0:1] + n[:, 0:1], (tq, HEAD_DIM))

    step(jd, True)

    def body(t, c):
        step(jd - 1 - t, False)
        return c

    lax.fori_loop(0, jd, body, 0)
    o_ref[...] = acc_ref[...].astype(o_ref.dtype)


def _suffix_sum_matrix(tk):
    r = np.arange(tk)
    upper = (r[:, None] > r[None, :]).astype(np.float32)
    return jnp.asarray(np.concatenate([upper, upper], axis=0), dtype=BF16)


def _attn_tiles(grp):
    tq = _pick_tile(grp.seq, ATTN_TILE)
    tk = ATTN_TILE if grp.past else tq
    nq = grp.seq // tq
    assert grp.seq % tq == 0 and tk % tq == 0 and grp.past % tk == 0 and grp.row0 % tq == 0
    assert grp.past == 0 or nq == 1
    assert grp.past > 0 or grp.row0 % grp.seq == 0
    return tq, tk, nq


def _sb_attention(grp, proj, cols, k_cache, v_cache, o_prev, o_shape, o_col, *, hp):
    H = k_cache.shape[2]
    tq, tk, nq = _attn_tiles(grp)
    wp = hp * HEAD_DIM
    q_blk0 = grp.row0 // tq
    kv_args, kv_specs = _kv_operands(grp, proj, cols[1], cols[2], k_cache, v_cache, tk=tk, hp=hp)
    prev_args, prev_specs, aliased = _chained(o_prev)
    n_in = 1 + len(kv_args) + 1
    return pl.pallas_call(
        functools.partial(_sb_kernel, tq=tq, tk=tk, hp=hp, n_heads=H, past=grp.past, has_past=grp.past > 0),
        grid=(grp.batch, H // hp, nq),
        in_specs=[pl.BlockSpec((tq, wp), lambda b, h, i: (q_blk0 + b * nq + i, cols[0] // wp + h))]
        + kv_specs
        + [pl.BlockSpec((2 * tk, tk), lambda b, h, i: (0, 0), pipeline_mode=pl.Buffered(1))]
        + prev_specs,
        out_specs=pl.BlockSpec((tq, wp), lambda b, h, i: (q_blk0 + b * nq + i, o_col // wp + h)),
        out_shape=jax.ShapeDtypeStruct(o_shape, BF16),
        input_output_aliases={n_in: 0} if aliased else {},
        scratch_shapes=[pltpu.VMEM((tq, wp), F32), pltpu.VMEM((tq, wp), F32)],
        compiler_params=_params("parallel", "parallel", "arbitrary"),
        name="sb_attention",
    )(proj, *kv_args, _suffix_sum_matrix(tk), *prev_args)


def _sortable_key(x):
    b = lax.bitcast_convert_type(x, jnp.int32)
    return b ^ ((b >> 31) & jnp.int32(0x7FFFFFFF))


_KEY_NEG_INF = int(np.int32(np.array(-np.inf, np.float32).view(np.int32) ^ 0x7FFFFFFF))
_KEY_LOWEST_FINITE = int(np.int32(np.array(np.finfo(np.float32).min, np.float32).view(np.int32) ^ 0x7FFFFFFF))
COUNT_ROW_GROUP = 128


def _dsa_select_kernel(qi_ref, wi_ref, ki2_ref, mask_ref, keys_ref, wb_ref, *, tq, tk, nkt, past, topk):
    i = pl.program_id(1)
    q_start = past + i * tq
    jd = q_start // tk
    reps = tk // HEAD_DIM

    w = wi_ref[:, IDX_DIM:IDX_DIM + N_IDX_HEADS] * (IDX_DIM ** -0.5 * N_IDX_HEADS ** -0.5)
    for hh in range(N_IDX_HEADS):
        wb_ref[hh] = jnp.broadcast_to(w[:, hh:hh + 1], (tq, HEAD_DIM))

    def idx_tile(j):
        kt = ki2_ref[j]
        tot = jnp.zeros((tq, tk), F32)
        for p in range(N_IDX_HEADS // 2):
            sc = jnp.dot(qi_ref[:, p * 128:(p + 1) * 128], kt, preferred_element_type=F32)
            w0 = jnp.concatenate([wb_ref[2 * p]] * reps, axis=1)
            w1 = jnp.concatenate([wb_ref[2 * p + 1]] * reps, axis=1)
            tot = tot + w0 * jnp.maximum(sc[:, :tk], 0.0) + w1 * jnp.maximum(sc[:, tk:], 0.0)
        return _sortable_key(tot)

    def idx_body(j, c):
        keys_ref[j] = idx_tile(j)
        return c

    lax.fori_loop(0, jd, idx_body, 0)
    t_pos = q_start + lax.broadcasted_iota(jnp.int32, (tq, tk), 0)
    s_pos = jd * tk + lax.broadcasted_iota(jnp.int32, (tq, tk), 1)
    visible = (t_pos // CHUNK) >= (s_pos // CHUNK)
    keys_ref[jd] = jnp.where(visible, idx_tile(jd), jnp.int32(_KEY_NEG_INF))

    n_vis = jd + 1

    @pl.when(n_vis % 2 == 1)
    def _():
        keys_ref[n_vis] = jnp.full((tq, tk), _INT32_MIN, jnp.int32)

    n_pairs = (n_vis + 1) // 2
    rg = min(COUNT_ROW_GROUP, tq)

    def count_ge(cand):
        parts = []
        for g in range(tq // rg):
            cg = cand[g * rg:(g + 1) * rg]

            def body(t, cnt, g=g, cg=cg):
                for jj in (2 * t, 2 * t + 1):
                    kj = keys_ref[jj, pl.ds(g * rg, rg), :]
                    for r in range(reps):
                        cnt = cnt + jnp.where(kj[:, r * 128:(r + 1) * 128] >= cg, 1.0, 0.0)
                return cnt

            parts.append(lax.fori_loop(0, n_pairs, body, jnp.zeros((rg, HEAD_DIM), F32)))
        return jnp.sum(jnp.concatenate(parts, axis=0), axis=-1, keepdims=True)

    kf = float(topk)
    zero = jnp.zeros((tq, HEAD_DIM), jnp.int32)
    res = jnp.where(count_ge(zero) >= kf, zero, jnp.int32(_INT32_MIN))
    for bit in range(30, -1, -1):
        cand = res + jnp.int32(1 << bit)
        res = jnp.where(count_ge(cand) >= kf, cand, res)
    thr = jnp.maximum(res, jnp.int32(_KEY_LOWEST_FINITE))
    thr = jnp.concatenate([thr] * reps, axis=1)

    def write_mask(j, c):
        mask_ref[j] = jnp.where(keys_ref[j] >= thr, 0.0, MASKED_LOGIT).astype(mask_ref.dtype)
        return c

    lax.fori_loop(0, jd + 1, write_mask, 0)

    def write_hidden(j, c):
        mask_ref[j] = jnp.full((tq, tk), MASKED_LOGIT, mask_ref.dtype)
        return c

    lax.fori_loop(jd + 1, nkt, write_hidden, 0)


def _dsa_flash_kernel(q_ref, *refs, tq, tk, hp, n_heads, past, has_past):
    k_new, v_new, k_past, v_past, rest = _split_kv_refs(refs, has_past)
    mask_ref, bnear_ref = rest[0], rest[1]
    o_ref, m_ref, l_ref, acc_ref = rest[-4], rest[-3], rest[-2], rest[-1]
    read_k = _tile_reader(k_new, k_past, tk=tk, hp=hp, n_heads=n_heads)
    read_v = _tile_reader(v_new, v_past, tk=tk, hp=hp, n_heads=n_heads)

    i = pl.program_id(2)
    jd = (past + i * tq) // tk
    reps = tk // HEAD_DIM
    heads = [slice(hh * HEAD_DIM, (hh + 1) * HEAD_DIM) for hh in range(hp)]

    m_ref[...] = jnp.full_like(m_ref, MASKED_LOGIT)
    l_ref[...] = jnp.zeros_like(l_ref)
    acc_ref[...] = jnp.zeros_like(acc_ref)

    def step(j, near):
        diag = near == 0
        maskf = mask_ref[j].astype(F32)
        logits = []
        for hh, hs in enumerate(heads):
            s = _nt_dot(q_ref[:, hs], read_k(j, hh, diag)) + maskf
            if near is not None:
                s = s + bnear_ref[hh, near]
            logits.append(s)
        probs, alphas = [], []
        for hs, s in zip(heads, logits):
            m_old = m_ref[:, hs]
            m_new = jnp.maximum(m_old, jnp.max(s, axis=-1, keepdims=True))
            alpha = jnp.exp2(m_old - m_new)
            p = jnp.exp2(s - jnp.concatenate([m_new] * reps, axis=1))
            l_ref[:, hs] = alpha * l_ref[:, hs] + jnp.sum(p, axis=-1, keepdims=True)
            m_ref[:, hs] = m_new
            probs.append(p.astype(BF16))
            alphas.append(alpha)
        for hh, (hs, p, alpha) in enumerate(zip(heads, probs, alphas)):
            pv = jnp.dot(p, read_v(j, hh, diag), preferred_element_type=F32)
            acc_ref[:, hs] = alpha * acc_ref[:, hs] + pv

    step(jd, 0)

    @pl.when(jd >= 1)
    def _():
        step(jd - 1, 1)

    def body(j, c):
        step(j, None)
        return c

    lax.fori_loop(0, jnp.maximum(jd - 1, 0), body, 0)
    o_ref[...] = (acc_ref[...] / l_ref[...]).astype(o_ref.dtype)


def _t5_bucket(rel):
    nb = N_BUCKETS // 2
    max_exact = nb // 2
    side = jnp.where(rel > 0, nb, 0)
    n = jnp.abs(rel)
    nf = jnp.maximum(n, 1).astype(F32)
    large = max_exact + (jnp.log(nf / max_exact) / math.log(MAX_DISTANCE / max_exact) * (nb - max_exact)).astype(jnp.int32)
    large = jnp.minimum(large, nb - 1)
    return side + jnp.where(n < max_exact, n, large)


def _near_bias(rel_bias, *, tq, tk):
    r = jnp.arange(tq, dtype=jnp.int32)[:, None]
    c = jnp.arange(tk, dtype=jnp.int32)[None, :]
    bucket = _t5_bucket(jnp.stack([c - r, c - r - tk]))
    far_bucket = _t5_bucket(jnp.int32(-2 * tk))
    table = (rel_bias - rel_bias[far_bucket][None, :]) * LOG2E
    onehot = (bucket[..., None] == jnp.arange(N_BUCKETS, dtype=jnp.int32)).astype(F32)
    return jnp.einsum("dqkb,bh->hdqk", onehot, table, precision=lax.Precision.HIGHEST)


def _dsa_attention(grp, proj, tail, cols, k_idx_all, k_cache, v_cache, rel_bias, o_prev, o_shape, o_col, *, hp):
    H = k_cache.shape[2]
    tq, tk, nq = _attn_tiles(grp)
    s_pad = k_idx_all.shape[1]
    nkt = s_pad // tk
    assert tk % CHUNK == 0 and tq % CHUNK == 0 and s_pad % tk == 0
    assert tk >= MAX_DISTANCE and H % hp == 0
    topk = min(TOPK_MAX, (grp.past + grp.seq) // 4)
    q_blk0 = grp.row0 // tq
    qiw = N_IDX_HEADS * IDX_DIM

    kt = jnp.transpose(k_idx_all.reshape(grp.batch, nkt, tk, IDX_DIM), (0, 1, 3, 2))
    z = jnp.zeros_like(kt)
    ki2 = jnp.concatenate([jnp.concatenate([kt, z], axis=3), jnp.concatenate([z, kt], axis=3)], axis=2)
    resident = pl.Buffered(1) if nq > 1 else None

    mask = pl.pallas_call(
        functools.partial(_dsa_select_kernel, tq=tq, tk=tk, nkt=nkt, past=grp.past, topk=topk),
        grid=(grp.batch, nq),
        in_specs=[
            pl.BlockSpec((tq, qiw), lambda b, i: (q_blk0 + b * nq + i, cols[0] // qiw)),
            pl.BlockSpec((tq, HEAD_DIM), lambda b, i: (q_blk0 + b * nq + i, 0)),
            pl.BlockSpec((None, nkt, 2 * IDX_DIM, 2 * tk), lambda b, i: (b, 0, 0, 0), pipeline_mode=resident),
        ],
        out_specs=pl.BlockSpec((None, None, nkt, tq, tk), lambda b, i: (b, i, 0, 0, 0)),
        out_shape=jax.ShapeDtypeStruct((grp.batch, nq, nkt, tq, tk), BF16),
        scratch_shapes=[
            pltpu.VMEM((nkt + 1, tq, tk), jnp.int32),
            pltpu.VMEM((N_IDX_HEADS, tq, HEAD_DIM), F32),
        ],
        compiler_params=_params("parallel", "parallel"),
        name="dsa_select",
    )(proj, tail, ki2)

    wp = hp * HEAD_DIM
    kv_args, kv_specs = _kv_operands(grp, proj, cols[2], cols[3], k_cache, v_cache, tk=tk, hp=hp)
    prev_args, prev_specs, aliased = _chained(o_prev)
    n_in = 1 + len(kv_args) + 2
    return pl.pallas_call(
        functools.partial(_dsa_flash_kernel, tq=tq, tk=tk, hp=hp, n_heads=H, past=grp.past, has_past=grp.past > 0),
        grid=(grp.batch, H // hp, nq),
        in_specs=[pl.BlockSpec((tq, wp), lambda b, h, i: (q_blk0 + b * nq + i, cols[1] // wp + h))]
        + kv_specs
        + [pl.BlockSpec((None, None, nkt, tq, tk), lambda b, h, i: (b, i, 0, 0, 0)),
           pl.BlockSpec((hp, 2, tq, tk), lambda b, h, i: (h, 0, 0, 0), pipeline_mode=resident)]
        + prev_specs,
        out_specs=pl.BlockSpec((tq, wp), lambda b, h, i: (q_blk0 + b * nq + i, o_col // wp + h)),
        out_shape=jax.ShapeDtypeStruct(o_shape, BF16),
        input_output_aliases={n_in: 0} if aliased else {},
        scratch_shapes=[pltpu.VMEM((tq, wp), F32), pltpu.VMEM((tq, wp), F32), pltpu.VMEM((tq, wp), F32)],
        compiler_params=_params("parallel", "parallel", "arbitrary"),
        name="dsa_flash",
    )(proj, *kv_args, mask, _near_bias(rel_bias, tq=tq, tk=tk), *prev_args)


def _cross_kernel(q_ref, k_ref, v_ref, *rest, n_heads, scale):
    o_ref = rest[-1]
    for hh in range(n_heads):
        sl = slice(hh * HEAD_DIM, (hh + 1) * HEAD_DIM)
        s = _nt_dot(q_ref[:, sl], k_ref[:, sl]) * scale
        p = jnp.exp(s - jnp.max(s, axis=-1, keepdims=True))
        denom = jnp.sum(p, axis=-1, keepdims=True)
        o = jnp.dot(p.astype(BF16), v_ref[:, sl], preferred_element_type=F32)
        o_ref[:, sl] = (o / denom).astype(o_ref.dtype)


def _cross_attention(grp, q, mem_k, mem_v, o_prev, *, tq=512):
    M, W = q.shape
    n_mem = mem_k.shape[1]
    tq = _pick_tile(grp.seq, tq)
    nq = grp.seq // tq
    assert grp.row0 % tq == 0
    blk0 = grp.row0 // tq
    prev_args, prev_specs, aliased = _chained(o_prev)
    return pl.pallas_call(
        functools.partial(_cross_kernel, n_heads=W // HEAD_DIM, scale=HEAD_DIM ** -0.5),
        grid=(grp.batch, nq),
        in_specs=[
            pl.BlockSpec((tq, W), lambda b, i: (blk0 + b * nq + i, 0)),
            pl.BlockSpec((None, n_mem, W), lambda b, i: (b, 0, 0)),
            pl.BlockSpec((None, n_mem, W), lambda b, i: (b, 0, 0)),
        ] + prev_specs,
        out_specs=pl.BlockSpec((tq, W), lambda b, i: (blk0 + b * nq + i, 0)),
        out_shape=jax.ShapeDtypeStruct((M, W), BF16),
        input_output_aliases={3: 0} if aliased else {},
        compiler_params=_params("parallel", "parallel"),
        name="cross_attention",
    )(q, mem_k, mem_v, *prev_args)


CONV_ROW_CHUNK = 128
SUBLANES = 8


def _conv_kernel(a_ref, prev_ref, state_ref, w_ref, bias_ref, g_ref, *rest, tt):
    o_ref, buf_ref, q_ref, y_ref = rest[-4], rest[-3], rest[-2], rest[-1]
    D = a_ref.shape[-1]

    @pl.when(pl.program_id(1) == 0)
    def _():
        buf_ref[0:CONV_HALO, :] = state_ref[...]

    @pl.when(pl.program_id(1) > 0)
    def _():
        buf_ref[0:CONV_HALO, :] = prev_ref[...]

    buf_ref[CONV_HALO:CONV_HALO + tt, :] = a_ref[...]
    buf_ref[CONV_HALO + tt:CONV_HALO + tt + SUBLANES, :] = jnp.zeros((SUBLANES, D), F32)

    first = CONV_HALO - (CONV_WIDTH - 1)
    rc = min(CONV_ROW_CHUNK, tt)
    for c0 in range(0, D, HEAD_DIM):
        cs = slice(c0, c0 + HEAD_DIM)
        for t0 in range(0, tt, rc):
            out = jnp.zeros((rc, HEAD_DIM), F32)
            for r in range(SUBLANES):
                q = None
                for k in range(CONV_WIDTH):
                    if (first + k) % SUBLANES != r:
                        continue
                    base = t0 + first + k - r
                    term = buf_ref[base:base + rc + SUBLANES, cs] * w_ref[k:k + 1, cs]
                    q = term if q is None else q + term
                if r == 0:
                    out = out + q[0:rc]
                else:
                    q_ref[...] = q
                    out = out + q_ref[r:r + rc, :]
            y_ref[t0:t0 + rc, cs] = out + bias_ref[:, cs]
    y = _rms_rows(y_ref[...], g_ref[...])
    o_ref[...] = (y * (1.0 / (1.0 + jnp.exp(-y)))).astype(o_ref.dtype)


def _conv_norm_swish(grp, u, state, w_dw, b_dw, g_norm, o_prev):
    M, D = u.shape
    tt = _pick_tile(grp.seq, 256)
    nt = grp.seq // tt
    assert grp.seq % tt == 0 and tt % CONV_HALO == 0 and grp.row0 % tt == 0
    blk0 = grp.row0 // tt
    hb = tt // CONV_HALO
    prev_args, prev_specs, aliased = _chained(o_prev)
    return pl.pallas_call(
        functools.partial(_conv_kernel, tt=tt),
        grid=(grp.batch, nt),
        in_specs=[
            pl.BlockSpec((tt, D), lambda b, i: (blk0 + b * nt + i, 0)),
            pl.BlockSpec((CONV_HALO, D), lambda b, i: (jnp.maximum((blk0 + b * nt + i) * hb - 1, 0), 0)),
            pl.BlockSpec((None, CONV_HALO, D), lambda b, i: (b, 0, 0)),
            pl.BlockSpec((CONV_WIDTH, D), lambda b, i: (0, 0)),
            pl.BlockSpec((1, D), lambda b, i: (0, 0)),
            pl.BlockSpec((1, D), lambda b, i: (0, 0)),
        ] + prev_specs,
        out_specs=pl.BlockSpec((tt, D), lambda b, i: (blk0 + b * nt + i, 0)),
        out_shape=jax.ShapeDtypeStruct((M, D), BF16),
        input_output_aliases={6: 0} if aliased else {},
        scratch_shapes=[pltpu.VMEM((tt + CONV_HALO + SUBLANES, D), F32),
                        pltpu.VMEM((min(CONV_ROW_CHUNK, tt) + SUBLANES, HEAD_DIM), F32),
                        pltpu.VMEM((tt, D), F32)],
        compiler_params=_params("parallel", "arbitrary"),
        name="conv_norm_swish",
    )(u, u, state, w_dw, b_dw.reshape(1, D), g_norm.reshape(1, D), *prev_args)


def kernel(x_prompt, x_sample, mem_prompt, cache_sb_k, cache_sb_v, cache_dsa_k, cache_dsa_v, cache_idx_k, cache_mem_k, cache_mem_v, state_conv, norm_mix, norm_cross, norm_mlp, norm_final, w_in_ab, w_out_ab, rel_bias, w_pw1, b_pw1, w_dw, b_dw, g_conv_norm, w_pw2, b_pw2, w_cq, w_mk, w_mv, w_co, w_up, w_down):
    Bp, Tp, D = x_prompt.shape
    Bs, Ts, _ = x_sample.shape
    past = cache_sb_k.shape[2]
    depth = norm_mix.shape[0]
    n_mem = mem_prompt.shape[1]
    n_sb, n_dsa = cache_sb_k.shape[3], cache_dsa_k.shape[3]
    sbw, dsw = n_sb * HEAD_DIM, n_dsa * HEAD_DIM
    qiw = N_IDX_HEADS * IDX_DIM
    cw = w_cq.shape[2]
    Mp, Ms = Bp * Tp, Bs * Ts
    M = Mp + Ms
    prompt = _Group(Bp, Tp, 0, 0)
    sample = _Group(Bs, Ts, Mp, past)

    x = jnp.concatenate([x_prompt.reshape(Mp, D), x_sample.reshape(Ms, D)], axis=0)
    mem_flat = mem_prompt.reshape(Bp * n_mem, D)

    src = dict(zip(("q_sb", "k_sb", "v_sb", "q_d", "k_d", "v_d", "q_i", "k_i", "w_i"),
                   np.cumsum([0, sbw, sbw, sbw, dsw, dsw, dsw, qiw, IDX_DIM])))
    order = ("k_sb", "v_sb", "k_d", "v_d", "q_sb", "q_d", "q_i")
    width = dict(q_sb=sbw, k_sb=sbw, v_sb=sbw, q_d=dsw, k_d=dsw, v_d=dsw, q_i=qiw)
    col = dict(zip(order, np.cumsum([0] + [width[n] for n in order])[:-1]))
    col = {k: int(v) for k, v in col.items()}
    n32 = 2 * sbw + 2 * dsw

    outs = {k: [] for k in ("p_sb_k", "p_sb_v", "p_d_k", "p_d_v", "p_ki", "p_mem_k", "p_mem_v", "p_conv",
                            "s_sb_k", "s_sb_v", "s_d_k", "s_d_v", "s_ki", "s_conv")}

    def split_rows(a):
        return a[:Mp].reshape(Bp, Tp, -1), a[Mp:].reshape(Bs, Ts, -1)

    def last_rows(a, grp, state, n):
        if grp.seq >= n:
            return jnp.stack([a[grp.row0 + (b + 1) * grp.seq - n:grp.row0 + (b + 1) * grp.seq]
                              for b in range(grp.batch)])
        rows = a[grp.row0:grp.row0 + grp.batch * grp.seq].reshape(grp.batch, grp.seq, -1)
        return jnp.concatenate([state, rows], axis=1)[:, -n:]

    w_out16, w_pw1_16, w_pw2_16 = w_out_ab.astype(BF16), w_pw1.astype(BF16), w_pw2.astype(BF16)
    w_cq16, w_mk16, w_mv16, w_co16 = (w.astype(BF16) for w in (w_cq, w_mk, w_mv, w_co))
    w_up16, w_down16 = w_up.astype(BF16), w_down.astype(BF16)
    q_scale = HEAD_DIM ** -0.5 * LOG2E
    col_scale = jnp.concatenate([jnp.full((width[n],), q_scale if n in ("q_sb", "q_d") else 1.0, F32)
                                 for n in order])

    for l in range(depth):
        i = l // 2
        if l % 2 == 0:
            w_in = w_in_ab[i]
            w_main = jnp.concatenate([w_in[:, src[n]:src[n] + width[n]] for n in order], axis=1).astype(BF16)
            tail_w = w_in.shape[1] - int(src["k_i"])
            w_tail = jnp.pad(w_in[:, src["k_i"]:], ((0, 0), (0, HEAD_DIM - tail_w))).astype(BF16)
            proj, kv32 = _in_proj(x, w_main, norm_mix[l], col_scale, n32=n32)
            tail = _matmul(x, w_tail[None], gain=norm_mix[l])

            for name, key, nh in (("sb_k", "k_sb", n_sb), ("sb_v", "v_sb", n_sb),
                                  ("d_k", "k_d", n_dsa), ("d_v", "v_d", n_dsa)):
                ap, as_ = split_rows(kv32[:, col[key]:col[key] + width[key]])
                outs["p_" + name].append(ap.reshape(Bp, Tp, nh, HEAD_DIM))
                outs["s_" + name].append(as_.reshape(Bs, Ts, nh, HEAD_DIM))
            k_i = tail[:, :IDX_DIM]
            kip, kis = split_rows(k_i)
            outs["p_ki"].append(kip)
            outs["s_ki"].append(kis)

            tk_s = ATTN_TILE
            s_pad_s = -(-(past + Ts) // tk_s) * tk_s
            kidx_p = kip.astype(BF16)
            kidx_s = jnp.concatenate([cache_idx_k[i].astype(BF16), kis.astype(BF16),
                                      jnp.zeros((Bs, s_pad_s - past - Ts, IDX_DIM), BF16)], axis=1)

            o_shape = (M, sbw + dsw)
            sb_cols = (col["q_sb"], col["k_sb"], col["v_sb"])
            d_cols = (col["q_i"], col["q_d"], col["k_d"], col["v_d"])
            o = _sb_attention(prompt, proj, sb_cols, cache_sb_k[i], cache_sb_v[i], None, o_shape, 0,
                              hp=SB_HEADS_PER_STEP)
            o = _dsa_attention(prompt, proj, tail, d_cols, kidx_p, cache_dsa_k[i], cache_dsa_v[i], rel_bias,
                               o, o_shape, sbw, hp=DSA_HEADS_PER_STEP)
            o = _sb_attention(sample, proj, sb_cols, cache_sb_k[i], cache_sb_v[i], o, o_shape, 0,
                              hp=SB_HEADS_PER_STEP)
            o = _dsa_attention(sample, proj, tail, d_cols, kidx_s, cache_dsa_k[i], cache_dsa_v[i], rel_bias,
                               o, o_shape, sbw, hp=DSA_HEADS_PER_STEP)
            x = _matmul(o, w_out16, layer=i, residual=x)
        else:
            u = _matmul(x, w_pw1_16, layer=i, gain=norm_mix[l], bias=b_pw1[i], glu=True)
            front = CONV_HALO - (CONV_WIDTH - 1)
            state_p = jnp.zeros((Bp, CONV_HALO, D), F32)
            state_s = jnp.concatenate([jnp.zeros((Bs, front, D), F32), state_conv[i]], axis=1)
            y = _conv_norm_swish(prompt, u, state_p, w_dw[i], b_dw[i], g_conv_norm[i], None)
            y = _conv_norm_swish(sample, u, state_s, w_dw[i], b_dw[i], g_conv_norm[i], y)
            outs["p_conv"].append(last_rows(u, prompt, state_p, CONV_WIDTH - 1))
            outs["s_conv"].append(last_rows(u, sample, state_s, CONV_WIDTH - 1))
            x = _matmul(y, w_pw2_16, layer=i, bias=b_pw2[i], residual=x)

        mk = _matmul(mem_flat, w_mk16, layer=l)
        mv = _matmul(mem_flat, w_mv16, layer=l)
        outs["p_mem_k"].append(mk.reshape(Bp, n_mem, cw // HEAD_DIM, HEAD_DIM))
        outs["p_mem_v"].append(mv.reshape(Bp, n_mem, cw // HEAD_DIM, HEAD_DIM))
        qc = _matmul(x, w_cq16, layer=l, gain=norm_cross[l], out_dtype=BF16)
        oc = _cross_attention(prompt, qc, mk.astype(BF16).reshape(Bp, n_mem, cw),
                              mv.astype(BF16).reshape(Bp, n_mem, cw), None)
        oc = _cross_attention(sample, qc, cache_mem_k[l].astype(BF16).reshape(Bs, n_mem, cw),
                              cache_mem_v[l].astype(BF16).reshape(Bs, n_mem, cw), oc)
        x = _matmul(oc, w_co16, layer=l, residual=x)

        x = _mlp(x, norm_mlp[l], w_up16, w_down16, l)

    y_p, y_s = _final_norm(x, norm_final, Mp)
    st = lambda k: jnp.stack(outs[k])
    return (y_p.reshape(Bp, Tp, D), y_s.reshape(Bs, Ts, D), st("p_sb_k"), st("p_sb_v"), st("p_d_k"), st("p_d_v"),
            st("p_ki"), st("p_mem_k"), st("p_mem_v"), st("p_conv"), st("s_sb_k"), st("s_sb_v"), st("s_d_k"),
            st("s_d_v"), st("s_ki"), st("s_conv"))
```

```python
import functools
import math
from typing import NamedTuple

import numpy as np
import jax
import jax.numpy as jnp
from jax import lax
from jax.experimental import pallas as pl
from jax.experimental.pallas import tpu as pltpu

F32 = jnp.float32
BF16 = jnp.bfloat16

HEAD_DIM = 128
IDX_DIM = 64
N_IDX_HEADS = 16
CHUNK = 64
TOPK_MAX = 256
N_BUCKETS = 32
MAX_DISTANCE = 128
CONV_WIDTH = 31
CONV_HALO = 32
RMS_EPS = 1e-6
MASKED_LOGIT = -1e30
LOG2E = math.log2(math.e)
_INT32_MIN = -2 ** 31
VMEM_LIMIT_BYTES = 56 * 1024 * 1024
SB_HEADS_PER_STEP = 8
DSA_HEADS_PER_STEP = 4
SB_INTERLEAVE = 4
ATTN_TILE = 256


class _Group(NamedTuple):
    batch: int
    seq: int
    row0: int
    past: int


def _params(*semantics):
    return pltpu.CompilerParams(dimension_semantics=semantics, vmem_limit_bytes=VMEM_LIMIT_BYTES)


def _pick_tile(n, pref):
    if n <= pref:
        return n
    t = pref
    while n % t:
        t //= 2
    return t


def _row_tile(m, pref):
    t = min(m, pref) // 8 * 8
    while m % t:
        t -= 8
    return t


def _chained(prev):
    if prev is None:
        return [], [], None
    return [prev], [pl.BlockSpec(memory_space=pl.ANY)], True


def _rms_rows(x, g):
    ms = jnp.mean(x * x, axis=-1, keepdims=True)
    return x * lax.rsqrt(ms + RMS_EPS) * g


def _mm_kernel(*refs, has_norm, has_bias, has_res, glu):
    it = iter(refs)
    x_ref = next(it)
    g_ref = next(it) if has_norm else None
    w_ref = next(it)
    w2_ref = next(it) if glu else None
    b_ref = next(it) if has_bias else None
    b2_ref = next(it) if (has_bias and glu) else None
    r_ref = next(it) if has_res else None
    o_ref = next(it)
    xn_ref = next(it) if has_norm else None

    if has_norm:
        @pl.when(pl.program_id(1) == 0)
        def _():
            xn_ref[...] = _rms_rows(x_ref[...], g_ref[...]).astype(BF16)
        xb = xn_ref[...]
    else:
        xb = x_ref[...].astype(BF16)

    y = jnp.dot(xb, w_ref[...], preferred_element_type=F32)
    if has_bias:
        y = y + b_ref[...]
    if glu:
        gate = jnp.dot(xb, w2_ref[...], preferred_element_type=F32)
        if has_bias:
            gate = gate + b2_ref[...]
        y = y * (1.0 / (1.0 + jnp.exp(-gate)))
    if has_res:
        y = y + r_ref[...]
    o_ref[...] = y.astype(o_ref.dtype)


def _matmul(x, w, *, layer=0, gain=None, bias=None, residual=None, glu=False, out_dtype=F32, tm=1152, tn=512):
    M, K = x.shape
    n_out = w.shape[2] // 2 if glu else w.shape[2]
    tm = _row_tile(M, tm)
    tn = _pick_tile(n_out, tn)
    nj = n_out // tn
    has_norm, has_bias, has_res = gain is not None, bias is not None, residual is not None

    args = [x]
    specs = [pl.BlockSpec((tm, K), lambda i, j: (i, 0))]
    if has_norm:
        args.append(gain.reshape(1, K))
        specs.append(pl.BlockSpec((1, K), lambda i, j: (0, 0)))
    args.append(w)
    specs.append(pl.BlockSpec((None, K, tn), lambda i, j: (layer, 0, j)))
    if glu:
        args.append(w)
        specs.append(pl.BlockSpec((None, K, tn), lambda i, j: (layer, 0, j + nj)))
    if has_bias:
        b2d = bias.reshape(1, -1)
        args.append(b2d)
        specs.append(pl.BlockSpec((1, tn), lambda i, j: (0, j)))
        if glu:
            args.append(b2d)
            specs.append(pl.BlockSpec((1, tn), lambda i, j: (0, j + nj)))
    if has_res:
        args.append(residual)
        specs.append(pl.BlockSpec((tm, tn), lambda i, j: (i, j)))

    return pl.pallas_call(
        functools.partial(_mm_kernel, has_norm=has_norm, has_bias=has_bias, has_res=has_res, glu=glu),
        grid=(M // tm, nj),
        in_specs=specs,
        out_specs=pl.BlockSpec((tm, tn), lambda i, j: (i, j)),
        out_shape=jax.ShapeDtypeStruct((M, n_out), out_dtype),
        scratch_shapes=[pltpu.VMEM((tm, K), BF16)] if has_norm else [],
        compiler_params=_params("parallel", "arbitrary"),
        name="norm_matmul",
    )(*args)


def _in_proj_kernel(x_ref, g_ref, w_ref, s_ref, o16_ref, o32_ref, xn_ref, *, n32_tiles):
    j = pl.program_id(1)

    @pl.when(j == 0)
    def _():
        xn_ref[...] = _rms_rows(x_ref[...], g_ref[...]).astype(BF16)

    y = jnp.dot(xn_ref[...], w_ref[...], preferred_element_type=F32)
    o16_ref[...] = (y * s_ref[...]).astype(BF16)

    @pl.when(j < n32_tiles)
    def _():
        o32_ref[...] = y


def _in_proj(x, w, gain, col_scale, *, n32, tm=1152, tn=512):
    M, K = x.shape
    N = w.shape[1]
    tm = _row_tile(M, tm)
    assert N % tn == 0 and n32 % tn == 0
    n32_tiles = n32 // tn
    return pl.pallas_call(
        functools.partial(_in_proj_kernel, n32_tiles=n32_tiles),
        grid=(M // tm, N // tn),
        in_specs=[
            pl.BlockSpec((tm, K), lambda i, j: (i, 0)),
            pl.BlockSpec((1, K), lambda i, j: (0, 0)),
            pl.BlockSpec((K, tn), lambda i, j: (0, j)),
            pl.BlockSpec((1, tn), lambda i, j: (0, j)),
        ],
        out_specs=[
            pl.BlockSpec((tm, tn), lambda i, j: (i, j)),
            pl.BlockSpec((tm, tn), lambda i, j: (i, jnp.minimum(j, n32_tiles - 1))),
        ],
        out_shape=[jax.ShapeDtypeStruct((M, N), BF16), jax.ShapeDtypeStruct((M, n32), F32)],
        scratch_shapes=[pltpu.VMEM((tm, K), BF16)],
        compiler_params=_params("parallel", "arbitrary"),
        name="in_proj",
    )(x, gain.reshape(1, K), w, col_scale.reshape(1, N))


def _mlp_kernel(x_ref, g_ref, wu_ref, wd_ref, o_ref, xn_ref, acc_ref):
    f = pl.program_id(1)

    @pl.when(f == 0)
    def _():
        xn_ref[...] = _rms_rows(x_ref[...], g_ref[...]).astype(BF16)
        acc_ref[...] = jnp.zeros_like(acc_ref)

    h = jnp.dot(xn_ref[...], wu_ref[...], preferred_element_type=F32)
    h = jnp.maximum(h, 0.0)
    h = (h * h).astype(BF16)
    acc_ref[...] += jnp.dot(h, wd_ref[...], preferred_element_type=F32)

    @pl.when(f == pl.num_programs(1) - 1)
    def _():
        o_ref[...] = x_ref[...] + acc_ref[...]


def _mlp(x, gain, w_up, w_down, layer, *, tm=512, tf=512):
    M, D = x.shape
    F = w_up.shape[2]
    tm = _pick_tile(M, tm)
    tf = _pick_tile(F, tf)
    return pl.pallas_call(
        _mlp_kernel,
        grid=(M // tm, F // tf),
        in_specs=[
            pl.BlockSpec((tm, D), lambda i, f: (i, 0)),
            pl.BlockSpec((1, D), lambda i, f: (0, 0)),
            pl.BlockSpec((None, D, tf), lambda i, f: (layer, 0, f)),
            pl.BlockSpec((None, tf, D), lambda i, f: (layer, f, 0)),
        ],
        out_specs=pl.BlockSpec((tm, D), lambda i, f: (i, 0)),
        out_shape=jax.ShapeDtypeStruct((M, D), F32),
        scratch_shapes=[pltpu.VMEM((tm, D), BF16), pltpu.VMEM((tm, D), F32)],
        compiler_params=_params("parallel", "arbitrary"),
        name="mlp",
    )(x, gain.reshape(1, D), w_up, w_down)


def _final_norm_kernel(x_ref, g_ref, op_ref, os_ref, *, n_prompt_tiles):
    y = _rms_rows(x_ref[...], g_ref[...])

    @pl.when(pl.program_id(0) < n_prompt_tiles)
    def _():
        op_ref[...] = y

    @pl.when(pl.program_id(0) >= n_prompt_tiles)
    def _():
        os_ref[...] = y


def _final_norm(x, gain, m_prompt):
    M, D = x.shape
    m_sample = M - m_prompt
    tm = _pick_tile(m_sample, 512)
    assert m_prompt % tm == 0 and m_sample % tm == 0
    npt = m_prompt // tm
    return pl.pallas_call(
        functools.partial(_final_norm_kernel, n_prompt_tiles=npt),
        grid=(M // tm,),
        in_specs=[pl.BlockSpec((tm, D), lambda i: (i, 0)), pl.BlockSpec((1, D), lambda i: (0, 0))],
        out_specs=[pl.BlockSpec((tm, D), lambda i: (jnp.minimum(i, npt - 1), 0)),
                   pl.BlockSpec((tm, D), lambda i: (jnp.maximum(i - npt, 0), 0))],
        out_shape=[jax.ShapeDtypeStruct((m_prompt, D), F32), jax.ShapeDtypeStruct((m_sample, D), F32)],
        compiler_params=_params("arbitrary"),
        name="final_norm",
    )(x, gain.reshape(1, D))


def _nt_dot(a, b):
    return lax.dot_general(a, b, (((1,), (1,)), ((), ())), preferred_element_type=F32)


def _kv_operands(grp, proj, k_col, v_col, k_cache, v_cache, *, tk, hp):
    wp = hp * HEAD_DIM
    if grp.past == 0:
        blk0 = grp.row0 // grp.seq
        args = [proj, proj]
        specs = [pl.BlockSpec((grp.seq, wp), lambda b, h, i, c=k_col // wp: (blk0 + b, c + h),
                              pipeline_mode=pl.Buffered(1)),
                 pl.BlockSpec((grp.seq, wp), lambda b, h, i, c=v_col // wp: (blk0 + b, c + h),
                              pipeline_mode=pl.Buffered(1))]
        return args, specs

    def new_tile(col):
        rows = lax.slice(proj, (grp.row0, col), (grp.row0 + grp.batch * grp.seq, col + k_cache.shape[2] * HEAD_DIM))
        rows = rows.reshape(grp.batch, grp.seq, -1)
        return jnp.pad(rows, ((0, 0), (0, tk - grp.seq), (0, 0)))

    args = [new_tile(k_col), new_tile(v_col)]
    specs = [pl.BlockSpec((None, tk, wp), lambda b, h, i: (b, 0, h))] * 2
    n_heads = k_cache.shape[2]
    for cache in (k_cache, v_cache):
        args.append(cache.reshape(grp.batch, grp.past * n_heads, HEAD_DIM))
        specs.append(pl.BlockSpec((None, grp.past * n_heads, HEAD_DIM), lambda b, h, i: (b, 0, 0)))
    return args, specs


def _split_kv_refs(refs, has_past):
    if not has_past:
        return refs[0], refs[1], None, None, refs[2:]
    return refs[0], refs[1], refs[2], refs[3], refs[4:]


def _tile_reader(new_ref, past_ref, *, tk, hp, n_heads):
    def read(j, hh, diag, n=1):
        hs = slice(hh * HEAD_DIM, (hh + 1) * HEAD_DIM)
        if past_ref is None:
            return new_ref[pl.ds(pl.multiple_of(j * tk, tk), n * tk), hs]
        if diag:
            return new_ref[:, hs]
        head = pl.program_id(1) * hp + hh
        return past_ref[pl.ds(j * (tk * n_heads) + head, n * tk, stride=n_heads), :].astype(BF16)
    return read


def _sb_kernel(q_ref, *refs, tq, tk, hp, n_heads, past, has_past):
    k_new, v_new, k_past, v_past, rest = _split_kv_refs(refs, has_past)
    u_ref, o_ref, acc_ref, carry_ref = rest[0], rest[-3], rest[-2], rest[-1]
    read_k = _tile_reader(k_new, k_past, tk=tk, hp=hp, n_heads=n_heads)
    read_v = _tile_reader(v_new, v_past, tk=tk, hp=hp, n_heads=n_heads)

    i = pl.program_id(2)
    q_start = past + i * tq
    jd = q_start // tk
    reps = tk // HEAD_DIM

    acc_ref[...] = jnp.zeros_like(acc_ref)
    carry_ref[...] = jnp.zeros_like(carry_ref)

    def step(j, masked):
        if masked:
            t_pos = q_start + lax.broadcasted_iota(jnp.int32, (tq, tk), 0)
            s_pos = j * tk + lax.broadcasted_iota(jnp.int32, (tq, tk), 1)
            causal = t_pos > s_pos
        for h0 in range(0, hp, SB_INTERLEAVE):
            group_step(j, masked, causal if masked else None, range(h0, h0 + SB_INTERLEAVE))

    def group_step(j, masked, causal, group):
        heads = [(hh, slice(hh * HEAD_DIM, (hh + 1) * HEAD_DIM)) for hh in group]
        z2s = [_nt_dot(q_ref[:, hs], read_k(j, hh, masked)) for hh, hs in heads]
        ns, log2_betas, css = [], [], []
        for z2 in z2s:
            neg_abs = lax.bitcast_convert_type(
                lax.bitcast_convert_type(z2, jnp.int32) | jnp.int32(_INT32_MIN), F32)
            n = jnp.maximum(z2, 0.0) + jnp.log(1.0 + jnp.exp2(neg_abs)) * LOG2E
            log2_betas.append(z2 - n)
            if masked:
                n = jnp.where(causal, n, 0.0)
            n16 = n.astype(BF16)
            css.append(jnp.dot(n16, u_ref[...], preferred_element_type=F32))
            ns.append(n16)
        for (hh, hs), n16, log2_beta, cs in zip(heads, ns, log2_betas, css):
            a = jnp.exp2(log2_beta - cs)
            if masked:
                a = jnp.where(causal, a, 0.0)
            carry = carry_ref[:, hs]
            pv = jnp.dot(a.astype(BF16), read_v(j, hh, masked), preferred_element_type=F32)
            acc_ref[:, hs] += jnp.exp2(-carry) * pv
            carry_ref[:, hs] = carry + jnp.broadcast_to(cs[:, 0:1] + n16[:, 0:1].astype(F32), (tq, HEAD_DIM))

    step(jd, True)

    def body(t, c):
        step(jd - 1 - t, False)
        return c

    lax.fori_loop(0, jd, body, 0)
    o_ref[...] = acc_ref[...].astype(o_ref.dtype)


def _suffix_sum_matrix(tk):
    r = np.arange(tk)
    return jnp.asarray((r[:, None] > r[None, :]).astype(np.float32), dtype=BF16)


def _attn_tiles(grp):
    tq = _pick_tile(grp.seq, ATTN_TILE)
    tk = ATTN_TILE if grp.past else tq
    nq = grp.seq // tq
    assert grp.seq % tq == 0 and tk % tq == 0 and grp.past % tk == 0 and grp.row0 % tq == 0
    assert grp.past == 0 or nq == 1
    assert grp.past > 0 or grp.row0 % grp.seq == 0
    return tq, tk, nq


def _sb_attention(grp, proj, cols, k_cache, v_cache, o_prev, o_shape, o_col, *, hp):
    H = k_cache.shape[2]
    tq, tk, nq = _attn_tiles(grp)
    wp = hp * HEAD_DIM
    q_blk0 = grp.row0 // tq
    kv_args, kv_specs = _kv_operands(grp, proj, cols[1], cols[2], k_cache, v_cache, tk=tk, hp=hp)
    prev_args, prev_specs, aliased = _chained(o_prev)
    n_in = 1 + len(kv_args) + 1
    return pl.pallas_call(
        functools.partial(_sb_kernel, tq=tq, tk=tk, hp=hp, n_heads=H, past=grp.past, has_past=grp.past > 0),
        grid=(grp.batch, H // hp, nq),
        in_specs=[pl.BlockSpec((tq, wp), lambda b, h, i: (q_blk0 + b * nq + i, cols[0] // wp + h))]
        + kv_specs
        + [pl.BlockSpec((tk, tk), lambda b, h, i: (0, 0), pipeline_mode=pl.Buffered(1))]
        + prev_specs,
        out_specs=pl.BlockSpec((tq, wp), lambda b, h, i: (q_blk0 + b * nq + i, o_col // wp + h)),
        out_shape=jax.ShapeDtypeStruct(o_shape, BF16),
        input_output_aliases={n_in: 0} if aliased else {},
        scratch_shapes=[pltpu.VMEM((tq, wp), F32), pltpu.VMEM((tq, wp), F32)],
        compiler_params=_params("parallel", "parallel", "arbitrary"),
        name="sb_attention",
    )(proj, *kv_args, _suffix_sum_matrix(tk), *prev_args)


def _sortable_key(x):
    b = lax.bitcast_convert_type(x, jnp.int32)
    return b ^ ((b >> 31) & jnp.int32(0x7FFFFFFF))


_KEY_NEG_INF = int(np.int32(np.array(-np.inf, np.float32).view(np.int32) ^ 0x7FFFFFFF))
_KEY_LOWEST_FINITE = int(np.int32(np.array(np.finfo(np.float32).min, np.float32).view(np.int32) ^ 0x7FFFFFFF))
COUNT_ROW_GROUP = 128


def _dsa_select_kernel(qi_ref, wi_ref, ki2_ref, mask_ref, keys_ref, wb_ref, *, tq, tk, nkt, past, topk):
    i = pl.program_id(1)
    q_start = past + i * tq
    jd = q_start // tk
    reps = tk // HEAD_DIM

    w = wi_ref[:, IDX_DIM:IDX_DIM + N_IDX_HEADS] * (IDX_DIM ** -0.5 * N_IDX_HEADS ** -0.5)
    for hh in range(N_IDX_HEADS):
        wb_ref[hh] = jnp.broadcast_to(w[:, hh:hh + 1], (tq, HEAD_DIM))

    def idx_tile(j):
        kt = ki2_ref[j]
        tot = jnp.zeros((tq, tk), F32)
        for p in range(N_IDX_HEADS // 2):
            sc = jnp.dot(qi_ref[:, p * 128:(p + 1) * 128], kt, preferred_element_type=F32)
            w0 = jnp.concatenate([wb_ref[2 * p]] * reps, axis=1)
            w1 = jnp.concatenate([wb_ref[2 * p + 1]] * reps, axis=1)
            tot = tot + w0 * jnp.maximum(sc[:, :tk], 0.0) + w1 * jnp.maximum(sc[:, tk:], 0.0)
        return _sortable_key(tot)

    def idx_body(j, c):
        keys_ref[j] = idx_tile(j)
        return c

    lax.fori_loop(0, jd, idx_body, 0)
    t_pos = q_start + lax.broadcasted_iota(jnp.int32, (tq, tk), 0)
    s_pos = jd * tk + lax.broadcasted_iota(jnp.int32, (tq, tk), 1)
    visible = (t_pos // CHUNK) >= (s_pos // CHUNK)
    keys_ref[jd] = jnp.where(visible, idx_tile(jd), jnp.int32(_KEY_NEG_INF))

    n_vis = jd + 1

    @pl.when(n_vis % 2 == 1)
    def _():
        keys_ref[n_vis] = jnp.full((tq, tk), _INT32_MIN, jnp.int32)

    n_pairs = (n_vis + 1) // 2
    rg = min(COUNT_ROW_GROUP, tq)

    def count_ge(cand):
        parts = []
        for g in range(tq // rg):
            cg = cand[g * rg:(g + 1) * rg]

            def body(t, cnt, g=g, cg=cg):
                for jj in (2 * t, 2 * t + 1):
                    kj = keys_ref[jj, pl.ds(g * rg, rg), :]
                    for r in range(reps):
                        cnt = cnt + jnp.where(kj[:, r * 128:(r + 1) * 128] >= cg, 1.0, 0.0)
                return cnt

            parts.append(lax.fori_loop(0, n_pairs, body, jnp.zeros((rg, HEAD_DIM), F32)))
        return jnp.sum(jnp.concatenate(parts, axis=0), axis=-1, keepdims=True)

    kf = float(topk)
    zero = jnp.zeros((tq, HEAD_DIM), jnp.int32)
    res = jnp.where(count_ge(zero) >= kf, zero, jnp.int32(_INT32_MIN))
    for bit in range(30, -1, -1):
        cand = res + jnp.int32(1 << bit)
        res = jnp.where(count_ge(cand) >= kf, cand, res)
    thr = jnp.maximum(res, jnp.int32(_KEY_LOWEST_FINITE))
    thr = jnp.concatenate([thr] * reps, axis=1)

    def write_mask(j, c):
        mask_ref[j] = jnp.where(keys_ref[j] >= thr, 0.0, MASKED_LOGIT).astype(mask_ref.dtype)
        return c

    lax.fori_loop(0, jd + 1, write_mask, 0)

    def write_hidden(j, c):
        mask_ref[j] = jnp.full((tq, tk), MASKED_LOGIT, mask_ref.dtype)
        return c

    lax.fori_loop(jd + 1, nkt, write_hidden, 0)


def _dsa_flash_kernel(q_ref, *refs, tq, tk, hp, n_heads, past, has_past):
    k_new, v_new, k_past, v_past, rest = _split_kv_refs(refs, has_past)
    mask_ref, bnear_ref = rest[0], rest[1]
    o_ref, m_ref, l_ref, acc_ref = rest[-4], rest[-3], rest[-2], rest[-1]
    read_k = _tile_reader(k_new, k_past, tk=tk, hp=hp, n_heads=n_heads)
    read_v = _tile_reader(v_new, v_past, tk=tk, hp=hp, n_heads=n_heads)

    i = pl.program_id(2)
    jd = (past + i * tq) // tk
    reps = tk // HEAD_DIM
    heads = [slice(hh * HEAD_DIM, (hh + 1) * HEAD_DIM) for hh in range(hp)]

    m_ref[...] = jnp.full_like(m_ref, MASKED_LOGIT)
    l_ref[...] = jnp.zeros_like(l_ref)
    acc_ref[...] = jnp.zeros_like(acc_ref)

    def step(j, near, n=1):
        diag = near == 0
        maskf = jnp.concatenate([mask_ref[j + t].astype(F32) for t in range(n)], axis=1)
        logits = []
        for hh, hs in enumerate(heads):
            s = _nt_dot(q_ref[:, hs], read_k(j, hh, diag, n)) + maskf
            if near is not None:
                s = s + bnear_ref[hh, near]
            logits.append(s)
        probs, alphas = [], []
        for hs, s in zip(heads, logits):
            m_old = m_ref[:, hs]
            m_new = jnp.maximum(m_old, jnp.max(s, axis=-1, keepdims=True))
            alpha = jnp.exp2(m_old - m_new)
            p = jnp.exp2(s - jnp.concatenate([m_new] * (n * reps), axis=1))
            l_ref[:, hs] = alpha * l_ref[:, hs] + sum(p[:, r * HEAD_DIM:(r + 1) * HEAD_DIM]
                                                      for r in range(n * reps))
            m_ref[:, hs] = m_new
            probs.append(p.astype(BF16))
            alphas.append(alpha)
        for hh, (hs, p, alpha) in enumerate(zip(heads, probs, alphas)):
            pv = jnp.dot(p, read_v(j, hh, diag, n), preferred_element_type=F32)
            acc_ref[:, hs] = alpha * acc_ref[:, hs] + pv

    step(jd, 0)

    @pl.when(jd >= 1)
    def _():
        step(jd - 1, 1)

    n_far = jnp.maximum(jd - 1, 0)

    @pl.when(n_far % 2 == 1)
    def _():
        step(n_far - 1, None)

    def body(t, c):
        step(2 * t, None, 2)
        return c

    lax.fori_loop(0, n_far // 2, body, 0)
    for hs in heads:
        o_ref[:, hs] = (acc_ref[:, hs] / jnp.sum(l_ref[:, hs], axis=-1, keepdims=True)).astype(o_ref.dtype)


def _t5_bucket(rel):
    nb = N_BUCKETS // 2
    max_exact = nb // 2
    side = jnp.where(rel > 0, nb, 0)
    n = jnp.abs(rel)
    nf = jnp.maximum(n, 1).astype(F32)
    large = max_exact + (jnp.log(nf / max_exact) / math.log(MAX_DISTANCE / max_exact) * (nb - max_exact)).astype(jnp.int32)
    large = jnp.minimum(large, nb - 1)
    return side + jnp.where(n < max_exact, n, large)


def _near_bias(rel_bias, *, tq, tk):
    r = jnp.arange(tq, dtype=jnp.int32)[:, None]
    c = jnp.arange(tk, dtype=jnp.int32)[None, :]
    bucket = _t5_bucket(jnp.stack([c - r, c - r - tk]))
    far_bucket = _t5_bucket(jnp.int32(-2 * tk))
    table = (rel_bias - rel_bias[far_bucket][None, :]) * LOG2E
    onehot = (bucket[..., None] == jnp.arange(N_BUCKETS, dtype=jnp.int32)).astype(F32)
    return jnp.einsum("dqkb,bh->hdqk", onehot, table, precision=lax.Precision.HIGHEST)


def _dsa_attention(grp, proj, tail, cols, k_idx_all, k_cache, v_cache, rel_bias, o_prev, o_shape, o_col, *, hp):
    H = k_cache.shape[2]
    tq, tk, nq = _attn_tiles(grp)
    s_pad = k_idx_all.shape[1]
    nkt = s_pad // tk
    assert tk % CHUNK == 0 and tq % CHUNK == 0 and s_pad % tk == 0
    assert tk >= MAX_DISTANCE and H % hp == 0
    topk = min(TOPK_MAX, (grp.past + grp.seq) // 4)
    q_blk0 = grp.row0 // tq
    qiw = N_IDX_HEADS * IDX_DIM

    kt = jnp.transpose(k_idx_all.reshape(grp.batch, nkt, tk, IDX_DIM), (0, 1, 3, 2))
    z = jnp.zeros_like(kt)
    ki2 = jnp.concatenate([jnp.concatenate([kt, z], axis=3), jnp.concatenate([z, kt], axis=3)], axis=2)
    resident = pl.Buffered(1) if nq > 1 else None

    mask = pl.pallas_call(
        functools.partial(_dsa_select_kernel, tq=tq, tk=tk, nkt=nkt, past=grp.past, topk=topk),
        grid=(grp.batch, nq),
        in_specs=[
            pl.BlockSpec((tq, qiw), lambda b, i: (q_blk0 + b * nq + i, cols[0] // qiw)),
            pl.BlockSpec((tq, HEAD_DIM), lambda b, i: (q_blk0 + b * nq + i, 0)),
            pl.BlockSpec((None, nkt, 2 * IDX_DIM, 2 * tk), lambda b, i: (b, 0, 0, 0), pipeline_mode=resident),
        ],
        out_specs=pl.BlockSpec((None, None, nkt, tq, tk), lambda b, i: (b, i, 0, 0, 0)),
        out_shape=jax.ShapeDtypeStruct((grp.batch, nq, nkt, tq, tk), BF16),
        scratch_shapes=[
            pltpu.VMEM((nkt + 1, tq, tk), jnp.int32),
            pltpu.VMEM((N_IDX_HEADS, tq, HEAD_DIM), F32),
        ],
        compiler_params=_params("parallel", "parallel"),
        name="dsa_select",
    )(proj, tail, ki2)

    wp = hp * HEAD_DIM
    kv_args, kv_specs = _kv_operands(grp, proj, cols[2], cols[3], k_cache, v_cache, tk=tk, hp=hp)
    prev_args, prev_specs, aliased = _chained(o_prev)
    n_in = 1 + len(kv_args) + 2
    return pl.pallas_call(
        functools.partial(_dsa_flash_kernel, tq=tq, tk=tk, hp=hp, n_heads=H, past=grp.past, has_past=grp.past > 0),
        grid=(grp.batch, H // hp, nq),
        in_specs=[pl.BlockSpec((tq, wp), lambda b, h, i: (q_blk0 + b * nq + i, cols[1] // wp + h))]
        + kv_specs
        + [pl.BlockSpec((None, None, nkt, tq, tk), lambda b, h, i: (b, i, 0, 0, 0)),
           pl.BlockSpec((hp, 2, tq, tk), lambda b, h, i: (h, 0, 0, 0), pipeline_mode=resident)]
        + prev_specs,
        out_specs=pl.BlockSpec((tq, wp), lambda b, h, i: (q_blk0 + b * nq + i, o_col // wp + h)),
        out_shape=jax.ShapeDtypeStruct(o_shape, BF16),
        input_output_aliases={n_in: 0} if aliased else {},
        scratch_shapes=[pltpu.VMEM((tq, wp), F32), pltpu.VMEM((tq, wp), F32), pltpu.VMEM((tq, wp), F32)],
        compiler_params=_params("parallel", "parallel", "arbitrary"),
        name="dsa_flash",
    )(proj, *kv_args, mask, _near_bias(rel_bias, tq=tq, tk=tk), *prev_args)


def _cross_kernel(q_ref, k_ref, v_ref, *rest, n_heads, scale):
    o_ref = rest[-1]
    for hh in range(n_heads):
        sl = slice(hh * HEAD_DIM, (hh + 1) * HEAD_DIM)
        s = _nt_dot(q_ref[:, sl], k_ref[:, sl]) * scale
        p = jnp.exp(s - jnp.max(s, axis=-1, keepdims=True))
        denom = jnp.sum(p, axis=-1, keepdims=True)
        o = jnp.dot(p.astype(BF16), v_ref[:, sl], preferred_element_type=F32)
        o_ref[:, sl] = (o / denom).astype(o_ref.dtype)


def _cross_attention(grp, q, mem_k, mem_v, o_prev, *, tq=512):
    M, W = q.shape
    n_mem = mem_k.shape[1]
    tq = _pick_tile(grp.seq, tq)
    nq = grp.seq // tq
    assert grp.row0 % tq == 0
    blk0 = grp.row0 // tq
    prev_args, prev_specs, aliased = _chained(o_prev)
    return pl.pallas_call(
        functools.partial(_cross_kernel, n_heads=W // HEAD_DIM, scale=HEAD_DIM ** -0.5),
        grid=(grp.batch, nq),
        in_specs=[
            pl.BlockSpec((tq, W), lambda b, i: (blk0 + b * nq + i, 0)),
            pl.BlockSpec((None, n_mem, W), lambda b, i: (b, 0, 0)),
            pl.BlockSpec((None, n_mem, W), lambda b, i: (b, 0, 0)),
        ] + prev_specs,
        out_specs=pl.BlockSpec((tq, W), lambda b, i: (blk0 + b * nq + i, 0)),
        out_shape=jax.ShapeDtypeStruct((M, W), BF16),
        input_output_aliases={3: 0} if aliased else {},
        compiler_params=_params("parallel", "parallel"),
        name="cross_attention",
    )(q, mem_k, mem_v, *prev_args)


CONV_ROW_CHUNK = 128
SUBLANES = 8


def _conv_kernel(a_ref, prev_ref, state_ref, w_ref, bias_ref, g_ref, *rest, tt):
    o_ref, buf_ref, q_ref, y_ref = rest[-4], rest[-3], rest[-2], rest[-1]
    D = a_ref.shape[-1]

    @pl.when(pl.program_id(1) == 0)
    def _():
        buf_ref[0:CONV_HALO, :] = state_ref[...]

    @pl.when(pl.program_id(1) > 0)
    def _():
        buf_ref[0:CONV_HALO, :] = prev_ref[...]

    buf_ref[CONV_HALO:CONV_HALO + tt, :] = a_ref[...]
    buf_ref[CONV_HALO + tt:CONV_HALO + tt + SUBLANES, :] = jnp.zeros((SUBLANES, D), F32)

    first = CONV_HALO - (CONV_WIDTH - 1)
    rc = min(CONV_ROW_CHUNK, tt)
    for c0 in range(0, D, HEAD_DIM):
        cs = slice(c0, c0 + HEAD_DIM)
        for t0 in range(0, tt, rc):
            out = jnp.zeros((rc, HEAD_DIM), F32)
            for r in range(SUBLANES):
                q = None
                for k in range(CONV_WIDTH):
                    if (first + k) % SUBLANES != r:
                        continue
                    base = t0 + first + k - r
                    term = buf_ref[base:base + rc + SUBLANES, cs] * w_ref[k:k + 1, cs]
                    q = term if q is None else q + term
                if r == 0:
                    out = out + q[0:rc]
                else:
                    q_ref[...] = q
                    out = out + q_ref[r:r + rc, :]
            y_ref[t0:t0 + rc, cs] = out + bias_ref[:, cs]
    y = _rms_rows(y_ref[...], g_ref[...])
    o_ref[...] = (y * (1.0 / (1.0 + jnp.exp(-y)))).astype(o_ref.dtype)


def _conv_norm_swish(grp, u, state, w_dw, b_dw, g_norm, o_prev):
    M, D = u.shape
    tt = _pick_tile(grp.seq, 256)
    nt = grp.seq // tt
    assert grp.seq % tt == 0 and tt % CONV_HALO == 0 and grp.row0 % tt == 0
    blk0 = grp.row0 // tt
    hb = tt // CONV_HALO
    prev_args, prev_specs, aliased = _chained(o_prev)
    return pl.pallas_call(
        functools.partial(_conv_kernel, tt=tt),
        grid=(grp.batch, nt),
        in_specs=[
            pl.BlockSpec((tt, D), lambda b, i: (blk0 + b * nt + i, 0)),
            pl.BlockSpec((CONV_HALO, D), lambda b, i: (jnp.maximum((blk0 + b * nt + i) * hb - 1, 0), 0)),
            pl.BlockSpec((None, CONV_HALO, D), lambda b, i: (b, 0, 0)),
            pl.BlockSpec((CONV_WIDTH, D), lambda b, i: (0, 0)),
            pl.BlockSpec((1, D), lambda b, i: (0, 0)),
            pl.BlockSpec((1, D), lambda b, i: (0, 0)),
        ] + prev_specs,
        out_specs=pl.BlockSpec((tt, D), lambda b, i: (blk0 + b * nt + i, 0)),
        out_shape=jax.ShapeDtypeStruct((M, D), BF16),
        input_output_aliases={6: 0} if aliased else {},
        scratch_shapes=[pltpu.VMEM((tt + CONV_HALO + SUBLANES, D), F32),
                        pltpu.VMEM((min(CONV_ROW_CHUNK, tt) + SUBLANES, HEAD_DIM), F32),
                        pltpu.VMEM((tt, D), F32)],
        compiler_params=_params("parallel", "arbitrary"),
        name="conv_norm_swish",
    )(u, u, state, w_dw, b_dw.reshape(1, D), g_norm.reshape(1, D), *prev_args)


def kernel(x_prompt, x_sample, mem_prompt, cache_sb_k, cache_sb_v, cache_dsa_k, cache_dsa_v, cache_idx_k, cache_mem_k, cache_mem_v, state_conv, norm_mix, norm_cross, norm_mlp, norm_final, w_in_ab, w_out_ab, rel_bias, w_pw1, b_pw1, w_dw, b_dw, g_conv_norm, w_pw2, b_pw2, w_cq, w_mk, w_mv, w_co, w_up, w_down):
    Bp, Tp, D = x_prompt.shape
    Bs, Ts, _ = x_sample.shape
    past = cache_sb_k.shape[2]
    depth = norm_mix.shape[0]
    n_mem = mem_prompt.shape[1]
    n_sb, n_dsa = cache_sb_k.shape[3], cache_dsa_k.shape[3]
    sbw, dsw = n_sb * HEAD_DIM, n_dsa * HEAD_DIM
    qiw = N_IDX_HEADS * IDX_DIM
    cw = w_cq.shape[2]
    Mp, Ms = Bp * Tp, Bs * Ts
    M = Mp + Ms
    prompt = _Group(Bp, Tp, 0, 0)
    sample = _Group(Bs, Ts, Mp, past)

    x = jnp.concatenate([x_prompt.reshape(Mp, D), x_sample.reshape(Ms, D)], axis=0)
    mem_flat = mem_prompt.reshape(Bp * n_mem, D)

    src = dict(zip(("q_sb", "k_sb", "v_sb", "q_d", "k_d", "v_d", "q_i", "k_i", "w_i"),
                   np.cumsum([0, sbw, sbw, sbw, dsw, dsw, dsw, qiw, IDX_DIM])))
    order = ("k_sb", "v_sb", "k_d", "v_d", "q_sb", "q_d", "q_i")
    width = dict(q_sb=sbw, k_sb=sbw, v_sb=sbw, q_d=dsw, k_d=dsw, v_d=dsw, q_i=qiw)
    col = dict(zip(order, np.cumsum([0] + [width[n] for n in order])[:-1]))
    col = {k: int(v) for k, v in col.items()}
    n32 = 2 * sbw + 2 * dsw

    outs = {k: [] for k in ("p_sb_k", "p_sb_v", "p_d_k", "p_d_v", "p_ki", "p_mem_k", "p_mem_v", "p_conv",
                            "s_sb_k", "s_sb_v", "s_d_k", "s_d_v", "s_ki", "s_conv")}

    def split_rows(a):
        return a[:Mp].reshape(Bp, Tp, -1), a[Mp:].reshape(Bs, Ts, -1)

    def last_rows(a, grp, state, n):
        if grp.seq >= n:
            return jnp.stack([a[grp.row0 + (b + 1) * grp.seq - n:grp.row0 + (b + 1) * grp.seq]
                              for b in range(grp.batch)])
        rows = a[grp.row0:grp.row0 + grp.batch * grp.seq].reshape(grp.batch, grp.seq, -1)
        return jnp.concatenate([state, rows], axis=1)[:, -n:]

    w_out16, w_pw1_16, w_pw2_16 = w_out_ab.astype(BF16), w_pw1.astype(BF16), w_pw2.astype(BF16)
    w_cq16, w_mk16, w_mv16, w_co16 = (w.astype(BF16) for w in (w_cq, w_mk, w_mv, w_co))
    w_up16, w_down16 = w_up.astype(BF16), w_down.astype(BF16)
    q_scale = HEAD_DIM ** -0.5 * LOG2E
    col_scale = jnp.concatenate([jnp.full((width[n],), q_scale if n in ("q_sb", "q_d") else 1.0, F32)
                                 for n in order])

    for l in range(depth):
        i = l // 2
        if l % 2 == 0:
            w_in = w_in_ab[i]
            w_main = jnp.concatenate([w_in[:, src[n]:src[n] + width[n]] for n in order], axis=1).astype(BF16)
            tail_w = w_in.shape[1] - int(src["k_i"])
            w_tail = jnp.pad(w_in[:, src["k_i"]:], ((0, 0), (0, HEAD_DIM - tail_w))).astype(BF16)
            proj, kv32 = _in_proj(x, w_main, norm_mix[l], col_scale, n32=n32)
            tail = _matmul(x, w_tail[None], gain=norm_mix[l])

            for name, key, nh in (("sb_k", "k_sb", n_sb), ("sb_v", "v_sb", n_sb),
                                  ("d_k", "k_d", n_dsa), ("d_v", "v_d", n_dsa)):
                ap, as_ = split_rows(kv32[:, col[key]:col[key] + width[key]])
                outs["p_" + name].append(ap.reshape(Bp, Tp, nh, HEAD_DIM))
                outs["s_" + name].append(as_.reshape(Bs, Ts, nh, HEAD_DIM))
            k_i = tail[:, :IDX_DIM]
            kip, kis = split_rows(k_i)
            outs["p_ki"].append(kip)
            outs["s_ki"].append(kis)

            tk_s = ATTN_TILE
            s_pad_s = -(-(past + Ts) // tk_s) * tk_s
            kidx_p = kip.astype(BF16)
            kidx_s = jnp.concatenate([cache_idx_k[i].astype(BF16), kis.astype(BF16),
                                      jnp.zeros((Bs, s_pad_s - past - Ts, IDX_DIM), BF16)], axis=1)

            o_shape = (M, sbw + dsw)
            sb_cols = (col["q_sb"], col["k_sb"], col["v_sb"])
            d_cols = (col["q_i"], col["q_d"], col["k_d"], col["v_d"])
            o = _sb_attention(prompt, proj, sb_cols, cache_sb_k[i], cache_sb_v[i], None, o_shape, 0,
                              hp=SB_HEADS_PER_STEP)
            o = _dsa_attention(prompt, proj, tail, d_cols, kidx_p, cache_dsa_k[i], cache_dsa_v[i], rel_bias,
                               o, o_shape, sbw, hp=DSA_HEADS_PER_STEP)
            o = _sb_attention(sample, proj, sb_cols, cache_sb_k[i], cache_sb_v[i], o, o_shape, 0,
                              hp=SB_HEADS_PER_STEP)
            o = _dsa_attention(sample, proj, tail, d_cols, kidx_s, cache_dsa_k[i], cache_dsa_v[i], rel_bias,
                               o, o_shape, sbw, hp=DSA_HEADS_PER_STEP)
            x = _matmul(o, w_out16, layer=i, residual=x)
        else:
            u = _matmul(x, w_pw1_16, layer=i, gain=norm_mix[l], bias=b_pw1[i], glu=True)
            front = CONV_HALO - (CONV_WIDTH - 1)
            state_p = jnp.zeros((Bp, CONV_HALO, D), F32)
            state_s = jnp.concatenate([jnp.zeros((Bs, front, D), F32), state_conv[i]], axis=1)
            y = _conv_norm_swish(prompt, u, state_p, w_dw[i], b_dw[i], g_conv_norm[i], None)
            y = _conv_norm_swish(sample, u, state_s, w_dw[i], b_dw[i], g_conv_norm[i], y)
            outs["p_conv"].append(last_rows(u, prompt, state_p, CONV_WIDTH - 1))
            outs["s_conv"].append(last_rows(u, sample, state_s, CONV_WIDTH - 1))
            x = _matmul(y, w_pw2_16, layer=i, bias=b_pw2[i], residual=x)

        mk = _matmul(mem_flat, w_mk16, layer=l)
        mv = _matmul(mem_flat, w_mv16, layer=l)
        outs["p_mem_k"].append(mk.reshape(Bp, n_mem, cw // HEAD_DIM, HEAD_DIM))
        outs["p_mem_v"].append(mv.reshape(Bp, n_mem, cw // HEAD_DIM, HEAD_DIM))
        qc = _matmul(x, w_cq16, layer=l, gain=norm_cross[l], out_dtype=BF16)
        oc = _cross_attention(prompt, qc, mk.astype(BF16).reshape(Bp, n_mem, cw),
                              mv.astype(BF16).reshape(Bp, n_mem, cw), None)
        oc = _cross_attention(sample, qc, cache_mem_k[l].astype(BF16).reshape(Bs, n_mem, cw),
                              cache_mem_v[l].astype(BF16).reshape(Bs, n_mem, cw), oc)
        x = _matmul(oc, w_co16, layer=l, residual=x)

        x = _mlp(x, norm_mlp[l], w_up16, w_down16, l)

    y_p, y_s = _final_norm(x, norm_final, Mp)
    st = lambda k: jnp.stack(outs[k])
    return (y_p.reshape(Bp, Tp, D), y_s.reshape(Bs, Ts, D), st("p_sb_k"), st("p_sb_v"), st("p_d_k"), st("p_d_v"),
            st("p_ki"), st("p_mem_k"), st("p_mem_v"), st("p_conv"), st("s_sb_k"), st("s_sb_v"), st("s_d_k"),
            st("s_d_v"), st("s_ki"), st("s_conv"))
```

```python
import functools
import math
from typing import NamedTuple

import numpy as np
import jax
import jax.numpy as jnp
from jax import lax
from jax.experimental import pallas as pl
from jax.experimental.pallas import tpu as pltpu

F32 = jnp.float32
BF16 = jnp.bfloat16

HEAD_DIM = 128
IDX_DIM = 64
N_IDX_HEADS = 16
CHUNK = 64
TOPK_MAX = 256
N_BUCKETS = 32
MAX_DISTANCE = 128
CONV_WIDTH = 31
CONV_HALO = 32
RMS_EPS = 1e-6
MASKED_LOGIT = -1e30
LOG2E = math.log2(math.e)
_INT32_MIN = -2 ** 31
VMEM_LIMIT_BYTES = 56 * 1024 * 1024
SB_HEADS_PER_STEP = 8
DSA_HEADS_PER_STEP = 4
SB_INTERLEAVE = 4
ATTN_TILE = 256


class _Group(NamedTuple):
    batch: int
    seq: int
    row0: int
    past: int


def _params(*semantics):
    return pltpu.CompilerParams(dimension_semantics=semantics, vmem_limit_bytes=VMEM_LIMIT_BYTES)


def _pick_tile(n, pref):
    if n <= pref:
        return n
    t = pref
    while n % t:
        t //= 2
    return t


def _row_tile(m, pref):
    t = min(m, pref) // 8 * 8
    while m % t:
        t -= 8
    return t


def _chained(prev):
    if prev is None:
        return [], [], None
    return [prev], [pl.BlockSpec(memory_space=pl.ANY)], True


def _rms_rows(x, g):
    ms = jnp.mean(x * x, axis=-1, keepdims=True)
    return x * lax.rsqrt(ms + RMS_EPS) * g


def _mm_kernel(*refs, has_norm, has_bias, has_res, glu):
    it = iter(refs)
    x_ref = next(it)
    g_ref = next(it) if has_norm else None
    w_ref = next(it)
    w2_ref = next(it) if glu else None
    b_ref = next(it) if has_bias else None
    b2_ref = next(it) if (has_bias and glu) else None
    r_ref = next(it) if has_res else None
    o_ref = next(it)
    xn_ref = next(it) if has_norm else None

    if has_norm:
        @pl.when(pl.program_id(1) == 0)
        def _():
            xn_ref[...] = _rms_rows(x_ref[...], g_ref[...]).astype(BF16)
        xb = xn_ref[...]
    else:
        xb = x_ref[...].astype(BF16)

    y = jnp.dot(xb, w_ref[...], preferred_element_type=F32)
    if has_bias:
        y = y + b_ref[...]
    if glu:
        gate = jnp.dot(xb, w2_ref[...], preferred_element_type=F32)
        if has_bias:
            gate = gate + b2_ref[...]
        y = y * (1.0 / (1.0 + jnp.exp(-gate)))
    if has_res:
        y = y + r_ref[...]
    o_ref[...] = y.astype(o_ref.dtype)


def _matmul(x, w, *, layer=0, gain=None, bias=None, residual=None, glu=False, out_dtype=F32, tm=1152, tn=512):
    M, K = x.shape
    n_out = w.shape[2] // 2 if glu else w.shape[2]
    tm = _row_tile(M, tm)
    tn = _pick_tile(n_out, tn)
    nj = n_out // tn
    has_norm, has_bias, has_res = gain is not None, bias is not None, residual is not None

    args = [x]
    specs = [pl.BlockSpec((tm, K), lambda i, j: (i, 0))]
    if has_norm:
        args.append(gain.reshape(1, K))
        specs.append(pl.BlockSpec((1, K), lambda i, j: (0, 0)))
    args.append(w)
    specs.append(pl.BlockSpec((None, K, tn), lambda i, j: (layer, 0, j)))
    if glu:
        args.append(w)
        specs.append(pl.BlockSpec((None, K, tn), lambda i, j: (layer, 0, j + nj)))
    if has_bias:
        b2d = bias.reshape(1, -1)
        args.append(b2d)
        specs.append(pl.BlockSpec((1, tn), lambda i, j: (0, j)))
        if glu:
            args.append(b2d)
            specs.append(pl.BlockSpec((1, tn), lambda i, j: (0, j + nj)))
    if has_res:
        args.append(residual)
        specs.append(pl.BlockSpec((tm, tn), lambda i, j: (i, j)))

    return pl.pallas_call(
        functools.partial(_mm_kernel, has_norm=has_norm, has_bias=has_bias, has_res=has_res, glu=glu),
        grid=(M // tm, nj),
        in_specs=specs,
        out_specs=pl.BlockSpec((tm, tn), lambda i, j: (i, j)),
        out_shape=jax.ShapeDtypeStruct((M, n_out), out_dtype),
        scratch_shapes=[pltpu.VMEM((tm, K), BF16)] if has_norm else [],
        compiler_params=_params("parallel", "arbitrary"),
        name="norm_matmul",
    )(*args)


def _res_proj_kernel(y_ref, w_ref, *refs, has_bias):
    it = iter(refs)
    b_ref = next(it) if has_bias else None
    r_ref, g_ref, wq_ref, o_ref, q_ref, x1_ref = (next(it) for _ in range(6))
    j = pl.program_id(1)
    y = jnp.dot(y_ref[...], w_ref[...], preferred_element_type=F32) + r_ref[...]
    if has_bias:
        y = y + b_ref[...]
    o_ref[...] = y
    x1_ref[j] = y

    @pl.when(j == pl.num_programs(1) - 1)
    def _():
        x1 = jnp.concatenate([x1_ref[t] for t in range(x1_ref.shape[0])], axis=1)
        q_ref[...] = jnp.dot(_rms_rows(x1, g_ref[...]).astype(BF16), wq_ref[...],
                             preferred_element_type=F32).astype(q_ref.dtype)


def _residual_matmul_then_proj(y, w, layer, residual, bias, gain, wq, q_layer, *, tm=1152, tn=512):
    M, K = y.shape
    N, nq = w.shape[2], wq.shape[2]
    tm = _row_tile(M, tm)
    assert N % tn == 0
    nj = N // tn
    has_bias = bias is not None
    args = [y, w]
    specs = [pl.BlockSpec((tm, K), lambda i, j: (i, 0)), pl.BlockSpec((None, K, tn), lambda i, j: (layer, 0, j))]
    if has_bias:
        args.append(bias.reshape(1, N))
        specs.append(pl.BlockSpec((1, tn), lambda i, j: (0, j)))
    args += [residual, gain.reshape(1, N), wq]
    specs += [pl.BlockSpec((tm, tn), lambda i, j: (i, j)), pl.BlockSpec((1, N), lambda i, j: (0, 0)),
              pl.BlockSpec((None, N, nq), lambda i, j: (q_layer, 0, 0))]
    return pl.pallas_call(
        functools.partial(_res_proj_kernel, has_bias=has_bias),
        grid=(M // tm, nj),
        in_specs=specs,
        out_specs=[pl.BlockSpec((tm, tn), lambda i, j: (i, j)), pl.BlockSpec((tm, nq), lambda i, j: (i, 0))],
        out_shape=[jax.ShapeDtypeStruct((M, N), F32), jax.ShapeDtypeStruct((M, nq), BF16)],
        scratch_shapes=[pltpu.VMEM((nj, tm, tn), F32)],
        compiler_params=_params("parallel", "arbitrary"),
        name="residual_matmul_proj",
    )(*args)


def _in_proj_kernel(x_ref, g_ref, w_ref, s_ref, o16_ref, o32_ref, xn_ref, *, n32_tiles):
    j = pl.program_id(1)

    @pl.when(j == 0)
    def _():
        xn_ref[...] = _rms_rows(x_ref[...], g_ref[...]).astype(BF16)

    y = jnp.dot(xn_ref[...], w_ref[...], preferred_element_type=F32)
    o16_ref[...] = (y * s_ref[...]).astype(BF16)

    @pl.when(j < n32_tiles)
    def _():
        o32_ref[...] = y


def _in_proj(x, w, gain, col_scale, *, n32, tm=1152, tn=512):
    M, K = x.shape
    N = w.shape[1]
    tm = _row_tile(M, tm)
    assert N % tn == 0 and n32 % tn == 0
    n32_tiles = n32 // tn
    return pl.pallas_call(
        functools.partial(_in_proj_kernel, n32_tiles=n32_tiles),
        grid=(M // tm, N // tn),
        in_specs=[
            pl.BlockSpec((tm, K), lambda i, j: (i, 0)),
            pl.BlockSpec((1, K), lambda i, j: (0, 0)),
            pl.BlockSpec((K, tn), lambda i, j: (0, j)),
            pl.BlockSpec((1, tn), lambda i, j: (0, j)),
        ],
        out_specs=[
            pl.BlockSpec((tm, tn), lambda i, j: (i, j)),
            pl.BlockSpec((tm, tn), lambda i, j: (i, jnp.minimum(j, n32_tiles - 1))),
        ],
        out_shape=[jax.ShapeDtypeStruct((M, N), BF16), jax.ShapeDtypeStruct((M, n32), F32)],
        scratch_shapes=[pltpu.VMEM((tm, K), BF16)],
        compiler_params=_params("parallel", "arbitrary"),
        name="in_proj",
    )(x, gain.reshape(1, K), w, col_scale.reshape(1, N))


def _mlp_kernel(x_ref, oc_ref, wco_ref, g_ref, wu_ref, wd_ref, *refs, n_prompt_tiles):
    final = n_prompt_tiles is not None
    if final:
        gf_ref, op_ref, os_ref, x2_ref, xn_ref, acc_ref = refs
    else:
        o_ref, x2_ref, xn_ref, acc_ref = refs
    i, f = pl.program_id(0), pl.program_id(1)

    @pl.when(f == 0)
    def _():
        x2 = x_ref[...] + jnp.dot(oc_ref[...], wco_ref[...], preferred_element_type=F32)
        x2_ref[...] = x2
        xn_ref[...] = _rms_rows(x2, g_ref[...]).astype(BF16)
        acc_ref[...] = jnp.zeros_like(acc_ref)

    h = jnp.dot(xn_ref[...], wu_ref[...], preferred_element_type=F32)
    h = jnp.maximum(h, 0.0)
    h = (h * h).astype(BF16)
    acc_ref[...] += jnp.dot(h, wd_ref[...], preferred_element_type=F32)

    last = f == pl.num_programs(1) - 1
    if not final:
        @pl.when(last)
        def _():
            o_ref[...] = x2_ref[...] + acc_ref[...]
    else:
        @pl.when(last & (i < n_prompt_tiles))
        def _():
            op_ref[...] = _rms_rows(x2_ref[...] + acc_ref[...], gf_ref[...])

        @pl.when(last & (i >= n_prompt_tiles))
        def _():
            os_ref[...] = _rms_rows(x2_ref[...] + acc_ref[...], gf_ref[...])


def _cross_out_mlp(x, oc, w_co, gain, w_up, w_down, layer, *, final_gain=None, m_prompt=None, tm=512, tf=512):
    M, D = x.shape
    F = w_up.shape[2]
    cw = oc.shape[1]
    final = final_gain is not None
    tm = _pick_tile(M - m_prompt, tm) if final else _pick_tile(M, tm)
    tf = _pick_tile(F, tf)
    args = [x, oc, w_co, gain.reshape(1, D), w_up, w_down]
    specs = [
        pl.BlockSpec((tm, D), lambda i, f: (i, 0)),
        pl.BlockSpec((tm, cw), lambda i, f: (i, 0)),
        pl.BlockSpec((None, cw, D), lambda i, f: (layer, 0, 0)),
        pl.BlockSpec((1, D), lambda i, f: (0, 0)),
        pl.BlockSpec((None, D, tf), lambda i, f: (layer, 0, f)),
        pl.BlockSpec((None, tf, D), lambda i, f: (layer, f, 0)),
    ]
    if final:
        assert m_prompt % tm == 0 and (M - m_prompt) % tm == 0
        npt = m_prompt // tm
        args.append(final_gain.reshape(1, D))
        specs.append(pl.BlockSpec((1, D), lambda i, f: (0, 0)))
        out_specs = [pl.BlockSpec((tm, D), lambda i, f: (jnp.minimum(i, npt - 1), 0)),
                     pl.BlockSpec((tm, D), lambda i, f: (jnp.maximum(i - npt, 0), 0))]
        out_shape = [jax.ShapeDtypeStruct((m_prompt, D), F32), jax.ShapeDtypeStruct((M - m_prompt, D), F32)]
        semantics = ("arbitrary", "arbitrary")
    else:
        npt = None
        out_specs = pl.BlockSpec((tm, D), lambda i, f: (i, 0))
        out_shape = jax.ShapeDtypeStruct((M, D), F32)
        semantics = ("parallel", "arbitrary")
    return pl.pallas_call(
        functools.partial(_mlp_kernel, n_prompt_tiles=npt),
        grid=(M // tm, F // tf),
        in_specs=specs,
        out_specs=out_specs,
        out_shape=out_shape,
        scratch_shapes=[pltpu.VMEM((tm, D), F32), pltpu.VMEM((tm, D), BF16), pltpu.VMEM((tm, D), F32)],
        compiler_params=_params(*semantics),
        name="cross_out_mlp",
    )(*args)


def _nt_dot(a, b):
    return lax.dot_general(a, b, (((1,), (1,)), ((), ())), preferred_element_type=F32)


def _kv_operands(grp, proj, k_col, v_col, k_cache, v_cache, *, tk, hp):
    wp = hp * HEAD_DIM
    if grp.past == 0:
        blk0 = grp.row0 // grp.seq
        args = [proj, proj]
        specs = [pl.BlockSpec((grp.seq, wp), lambda b, h, i, c=k_col // wp: (blk0 + b, c + h),
                              pipeline_mode=pl.Buffered(1)),
                 pl.BlockSpec((grp.seq, wp), lambda b, h, i, c=v_col // wp: (blk0 + b, c + h),
                              pipeline_mode=pl.Buffered(1))]
        return args, specs

    def new_tile(col):
        rows = lax.slice(proj, (grp.row0, col), (grp.row0 + grp.batch * grp.seq, col + k_cache.shape[2] * HEAD_DIM))
        rows = rows.reshape(grp.batch, grp.seq, -1)
        return jnp.pad(rows, ((0, 0), (0, tk - grp.seq), (0, 0)))

    args = [new_tile(k_col), new_tile(v_col)]
    specs = [pl.BlockSpec((None, tk, wp), lambda b, h, i: (b, 0, h))] * 2
    n_heads = k_cache.shape[2]
    for cache in (k_cache, v_cache):
        args.append(cache.reshape(grp.batch, grp.past * n_heads, HEAD_DIM))
        specs.append(pl.BlockSpec((None, grp.past * n_heads, HEAD_DIM), lambda b, h, i: (b, 0, 0)))
    return args, specs


def _split_kv_refs(refs, has_past):
    if not has_past:
        return refs[0], refs[1], None, None, refs[2:]
    return refs[0], refs[1], refs[2], refs[3], refs[4:]


def _tile_reader(new_ref, past_ref, *, tk, hp, n_heads):
    def read(j, hh, diag, n=1):
        hs = slice(hh * HEAD_DIM, (hh + 1) * HEAD_DIM)
        if past_ref is None:
            return new_ref[pl.ds(pl.multiple_of(j * tk, tk), n * tk), hs]
        if diag:
            return new_ref[:, hs]
        head = pl.program_id(1) * hp + hh
        return past_ref[pl.ds(j * (tk * n_heads) + head, n * tk, stride=n_heads), :].astype(BF16)
    return read


def _sb_kernel(q_ref, *refs, tq, tk, hp, n_heads, past, has_past):
    k_new, v_new, k_past, v_past, rest = _split_kv_refs(refs, has_past)
    u_ref, o_ref, acc_ref, carry_ref = rest[0], rest[-3], rest[-2], rest[-1]
    read_k = _tile_reader(k_new, k_past, tk=tk, hp=hp, n_heads=n_heads)
    read_v = _tile_reader(v_new, v_past, tk=tk, hp=hp, n_heads=n_heads)

    i = pl.program_id(2)
    q_start = past + i * tq
    jd = q_start // tk
    reps = tk // HEAD_DIM

    acc_ref[...] = jnp.zeros_like(acc_ref)
    carry_ref[...] = jnp.zeros_like(carry_ref)

    def step(j, masked):
        if masked:
            t_pos = q_start + lax.broadcasted_iota(jnp.int32, (tq, tk), 0)
            s_pos = j * tk + lax.broadcasted_iota(jnp.int32, (tq, tk), 1)
            causal = t_pos > s_pos
        for h0 in range(0, hp, SB_INTERLEAVE):
            group_step(j, masked, causal if masked else None, range(h0, h0 + SB_INTERLEAVE))

    def group_step(j, masked, causal, group):
        heads = [(hh, slice(hh * HEAD_DIM, (hh + 1) * HEAD_DIM)) for hh in group]
        z2s = [_nt_dot(q_ref[:, hs], read_k(j, hh, masked)) for hh, hs in heads]
        ns, log2_betas, css = [], [], []
        for z2 in z2s:
            neg_abs = lax.bitcast_convert_type(
                lax.bitcast_convert_type(z2, jnp.int32) | jnp.int32(_INT32_MIN), F32)
            n = jnp.maximum(z2, 0.0) + jnp.log(1.0 + jnp.exp2(neg_abs)) * LOG2E
            log2_betas.append(z2 - n)
            if masked:
                n = jnp.where(causal, n, 0.0)
            n16 = n.astype(BF16)
            css.append(jnp.dot(n16, u_ref[...], preferred_element_type=F32))
            ns.append(n16)
        for (hh, hs), n16, log2_beta, cs in zip(heads, ns, log2_betas, css):
            a = jnp.exp2(log2_beta - cs)
            if masked:
                a = jnp.where(causal, a, 0.0)
            carry = carry_ref[:, hs]
            pv = jnp.dot(a.astype(BF16), read_v(j, hh, masked), preferred_element_type=F32)
            acc_ref[:, hs] += jnp.exp2(-carry) * pv
            carry_ref[:, hs] = carry + jnp.broadcast_to(cs[:, 0:1] + n16[:, 0:1].astype(F32), (tq, HEAD_DIM))

    step(jd, True)

    def body(t, c):
        step(jd - 1 - t, False)
        return c

    lax.fori_loop(0, jd, body, 0)
    o_ref[...] = acc_ref[...].astype(o_ref.dtype)


def _suffix_sum_matrix(tk):
    r = np.arange(tk)
    return jnp.asarray((r[:, None] > r[None, :]).astype(np.float32), dtype=BF16)


def _attn_tiles(grp):
    tq = _pick_tile(grp.seq, ATTN_TILE)
    tk = ATTN_TILE if grp.past else tq
    nq = grp.seq // tq
    assert grp.seq % tq == 0 and tk % tq == 0 and grp.past % tk == 0 and grp.row0 % tq == 0
    assert grp.past == 0 or nq == 1
    assert grp.past > 0 or grp.row0 % grp.seq == 0
    return tq, tk, nq


def _sb_attention(grp, proj, cols, k_cache, v_cache, o_prev, o_shape, o_col, *, hp):
    H = k_cache.shape[2]
    tq, tk, nq = _attn_tiles(grp)
    wp = hp * HEAD_DIM
    q_blk0 = grp.row0 // tq
    kv_args, kv_specs = _kv_operands(grp, proj, cols[1], cols[2], k_cache, v_cache, tk=tk, hp=hp)
    prev_args, prev_specs, aliased = _chained(o_prev)
    n_in = 1 + len(kv_args) + 1
    return pl.pallas_call(
        functools.partial(_sb_kernel, tq=tq, tk=tk, hp=hp, n_heads=H, past=grp.past, has_past=grp.past > 0),
        grid=(grp.batch, H // hp, nq),
        in_specs=[pl.BlockSpec((tq, wp), lambda b, h, i: (q_blk0 + b * nq + i, cols[0] // wp + h))]
        + kv_specs
        + [pl.BlockSpec((tk, tk), lambda b, h, i: (0, 0), pipeline_mode=pl.Buffered(1))]
        + prev_specs,
        out_specs=pl.BlockSpec((tq, wp), lambda b, h, i: (q_blk0 + b * nq + i, o_col // wp + h)),
        out_shape=jax.ShapeDtypeStruct(o_shape, BF16),
        input_output_aliases={n_in: 0} if aliased else {},
        scratch_shapes=[pltpu.VMEM((tq, wp), F32), pltpu.VMEM((tq, wp), F32)],
        compiler_params=_params("parallel", "parallel", "arbitrary"),
        name="sb_attention",
    )(proj, *kv_args, _suffix_sum_matrix(tk), *prev_args)


def _sortable_key(x):
    b = lax.bitcast_convert_type(x, jnp.int32)
    return b ^ ((b >> 31) & jnp.int32(0x7FFFFFFF))


_KEY_NEG_INF = int(np.int32(np.array(-np.inf, np.float32).view(np.int32) ^ 0x7FFFFFFF))
_KEY_LOWEST_FINITE = int(np.int32(np.array(np.finfo(np.float32).min, np.float32).view(np.int32) ^ 0x7FFFFFFF))
COUNT_ROW_GROUP = 128


def _dsa_select_kernel(qi_ref, wi_ref, ki2_ref, mask_ref, keys_ref, wb_ref, *, tq, tk, nkt, past, topk):
    i = pl.program_id(1)
    q_start = past + i * tq
    jd = q_start // tk
    reps = tk // HEAD_DIM

    w = wi_ref[:, IDX_DIM:IDX_DIM + N_IDX_HEADS] * (IDX_DIM ** -0.5 * N_IDX_HEADS ** -0.5)
    for hh in range(N_IDX_HEADS):
        wb_ref[hh] = jnp.broadcast_to(w[:, hh:hh + 1], (tq, HEAD_DIM))

    def idx_tile(j):
        kt = ki2_ref[j]
        tot = jnp.zeros((tq, tk), F32)
        for p in range(N_IDX_HEADS // 2):
            sc = jnp.dot(qi_ref[:, p * 128:(p + 1) * 128], kt, preferred_element_type=F32)
            w0 = jnp.concatenate([wb_ref[2 * p]] * reps, axis=1)
            w1 = jnp.concatenate([wb_ref[2 * p + 1]] * reps, axis=1)
            tot = tot + w0 * jnp.maximum(sc[:, :tk], 0.0) + w1 * jnp.maximum(sc[:, tk:], 0.0)
        return _sortable_key(tot)

    def idx_body(j, c):
        keys_ref[j] = idx_tile(j)
        return c

    lax.fori_loop(0, jd, idx_body, 0)
    t_pos = q_start + lax.broadcasted_iota(jnp.int32, (tq, tk), 0)
    s_pos = jd * tk + lax.broadcasted_iota(jnp.int32, (tq, tk), 1)
    visible = (t_pos // CHUNK) >= (s_pos // CHUNK)
    keys_ref[jd] = jnp.where(visible, idx_tile(jd), jnp.int32(_KEY_NEG_INF))

    n_vis = jd + 1

    @pl.when(n_vis % 2 == 1)
    def _():
        keys_ref[n_vis] = jnp.full((tq, tk), _INT32_MIN, jnp.int32)

    n_pairs = (n_vis + 1) // 2
    rg = min(COUNT_ROW_GROUP, tq)

    def count_ge(cand):
        parts = []
        for g in range(tq // rg):
            cg = cand[g * rg:(g + 1) * rg]

            def body(t, cnt, g=g, cg=cg):
                for jj in (2 * t, 2 * t + 1):
                    kj = keys_ref[jj, pl.ds(g * rg, rg), :]
                    for r in range(reps):
                        cnt = cnt + jnp.where(kj[:, r * 128:(r + 1) * 128] >= cg, 1.0, 0.0)
                return cnt

            parts.append(lax.fori_loop(0, n_pairs, body, jnp.zeros((rg, HEAD_DIM), F32)))
        return jnp.sum(jnp.concatenate(parts, axis=0), axis=-1, keepdims=True)

    kf = float(topk)
    zero = jnp.zeros((tq, HEAD_DIM), jnp.int32)
    res = jnp.where(count_ge(zero) >= kf, zero, jnp.int32(_INT32_MIN))
    for bit in range(30, -1, -1):
        cand = res + jnp.int32(1 << bit)
        res = jnp.where(count_ge(cand) >= kf, cand, res)
    thr = jnp.maximum(res, jnp.int32(_KEY_LOWEST_FINITE))
    thr = jnp.concatenate([thr] * reps, axis=1)

    def write_mask(j, c):
        mask_ref[j] = jnp.where(keys_ref[j] >= thr, 0.0, MASKED_LOGIT).astype(mask_ref.dtype)
        return c

    lax.fori_loop(0, jd + 1, write_mask, 0)

    def write_hidden(j, c):
        mask_ref[j] = jnp.full((tq, tk), MASKED_LOGIT, mask_ref.dtype)
        return c

    lax.fori_loop(jd + 1, nkt, write_hidden, 0)


def _dsa_flash_kernel(q_ref, *refs, tq, tk, hp, n_heads, past, has_past):
    k_new, v_new, k_past, v_past, rest = _split_kv_refs(refs, has_past)
    mask_ref, bnear_ref = rest[0], rest[1]
    o_ref, m_ref, l_ref, acc_ref = rest[-4], rest[-3], rest[-2], rest[-1]
    read_k = _tile_reader(k_new, k_past, tk=tk, hp=hp, n_heads=n_heads)
    read_v = _tile_reader(v_new, v_past, tk=tk, hp=hp, n_heads=n_heads)

    i = pl.program_id(2)
    jd = (past + i * tq) // tk
    reps = tk // HEAD_DIM
    heads = [slice(hh * HEAD_DIM, (hh + 1) * HEAD_DIM) for hh in range(hp)]

    m_ref[...] = jnp.full_like(m_ref, MASKED_LOGIT)
    l_ref[...] = jnp.zeros_like(l_ref)
    acc_ref[...] = jnp.zeros_like(acc_ref)

    def step(j, near, n=1):
        diag = near == 0
        maskf = jnp.concatenate([mask_ref[j + t].astype(F32) for t in range(n)], axis=1)
        logits = []
        for hh, hs in enumerate(heads):
            s = _nt_dot(q_ref[:, hs], read_k(j, hh, diag, n)) + maskf
            if near is not None:
                s = s + bnear_ref[hh, near]
            logits.append(s)
        probs, alphas = [], []
        for hs, s in zip(heads, logits):
            m_old = m_ref[:, hs]
            m_new = jnp.maximum(m_old, jnp.max(s, axis=-1, keepdims=True))
            alpha = jnp.exp2(m_old - m_new)
            p = jnp.exp2(s - jnp.concatenate([m_new] * (n * reps), axis=1))
            l_ref[:, hs] = alpha * l_ref[:, hs] + sum(p[:, r * HEAD_DIM:(r + 1) * HEAD_DIM]
                                                      for r in range(n * reps))
            m_ref[:, hs] = m_new
            probs.append(p.astype(BF16))
            alphas.append(alpha)
        for hh, (hs, p, alpha) in enumerate(zip(heads, probs, alphas)):
            pv = jnp.dot(p, read_v(j, hh, diag, n), preferred_element_type=F32)
            acc_ref[:, hs] = alpha * acc_ref[:, hs] + pv

    step(jd, 0)

    @pl.when(jd >= 1)
    def _():
        step(jd - 1, 1)

    n_far = jnp.maximum(jd - 1, 0)

    @pl.when(n_far % 2 == 1)
    def _():
        step(n_far - 1, None)

    def body(t, c):
        step(2 * t, None, 2)
        return c

    lax.fori_loop(0, n_far // 2, body, 0)
    for hs in heads:
        o_ref[:, hs] = (acc_ref[:, hs] / jnp.sum(l_ref[:, hs], axis=-1, keepdims=True)).astype(o_ref.dtype)


def _t5_bucket(rel):
    nb = N_BUCKETS // 2
    max_exact = nb // 2
    side = jnp.where(rel > 0, nb, 0)
    n = jnp.abs(rel)
    nf = jnp.maximum(n, 1).astype(F32)
    large = max_exact + (jnp.log(nf / max_exact) / math.log(MAX_DISTANCE / max_exact) * (nb - max_exact)).astype(jnp.int32)
    large = jnp.minimum(large, nb - 1)
    return side + jnp.where(n < max_exact, n, large)


def _near_bias(rel_bias, *, tq, tk):
    r = jnp.arange(tq, dtype=jnp.int32)[:, None]
    c = jnp.arange(tk, dtype=jnp.int32)[None, :]
    bucket = _t5_bucket(jnp.stack([c - r, c - r - tk]))
    far_bucket = _t5_bucket(jnp.int32(-2 * tk))
    table = (rel_bias - rel_bias[far_bucket][None, :]) * LOG2E
    onehot = (bucket[..., None] == jnp.arange(N_BUCKETS, dtype=jnp.int32)).astype(F32)
    return jnp.einsum("dqkb,bh->hdqk", onehot, table, precision=lax.Precision.HIGHEST)


def _dsa_attention(grp, proj, tail, cols, k_idx_all, k_cache, v_cache, rel_bias, o_prev, o_shape, o_col, *, hp):
    H = k_cache.shape[2]
    tq, tk, nq = _attn_tiles(grp)
    s_pad = k_idx_all.shape[1]
    nkt = s_pad // tk
    assert tk % CHUNK == 0 and tq % CHUNK == 0 and s_pad % tk == 0
    assert tk >= MAX_DISTANCE and H % hp == 0
    topk = min(TOPK_MAX, (grp.past + grp.seq) // 4)
    q_blk0 = grp.row0 // tq
    qiw = N_IDX_HEADS * IDX_DIM

    kt = jnp.transpose(k_idx_all.reshape(grp.batch, nkt, tk, IDX_DIM), (0, 1, 3, 2))
    z = jnp.zeros_like(kt)
    ki2 = jnp.concatenate([jnp.concatenate([kt, z], axis=3), jnp.concatenate([z, kt], axis=3)], axis=2)
    resident = pl.Buffered(1) if nq > 1 else None

    mask = pl.pallas_call(
        functools.partial(_dsa_select_kernel, tq=tq, tk=tk, nkt=nkt, past=grp.past, topk=topk),
        grid=(grp.batch, nq),
        in_specs=[
            pl.BlockSpec((tq, qiw), lambda b, i: (q_blk0 + b * nq + i, cols[0] // qiw)),
            pl.BlockSpec((tq, HEAD_DIM), lambda b, i: (q_blk0 + b * nq + i, 0)),
            pl.BlockSpec((None, nkt, 2 * IDX_DIM, 2 * tk), lambda b, i: (b, 0, 0, 0), pipeline_mode=resident),
        ],
        out_specs=pl.BlockSpec((None, None, nkt, tq, tk), lambda b, i: (b, i, 0, 0, 0)),
        out_shape=jax.ShapeDtypeStruct((grp.batch, nq, nkt, tq, tk), BF16),
        scratch_shapes=[
            pltpu.VMEM((nkt + 1, tq, tk), jnp.int32),
            pltpu.VMEM((N_IDX_HEADS, tq, HEAD_DIM), F32),
        ],
        compiler_params=_params("parallel", "parallel"),
        name="dsa_select",
    )(proj, tail, ki2)

    wp = hp * HEAD_DIM
    kv_args, kv_specs = _kv_operands(grp, proj, cols[2], cols[3], k_cache, v_cache, tk=tk, hp=hp)
    prev_args, prev_specs, aliased = _chained(o_prev)
    n_in = 1 + len(kv_args) + 2
    return pl.pallas_call(
        functools.partial(_dsa_flash_kernel, tq=tq, tk=tk, hp=hp, n_heads=H, past=grp.past, has_past=grp.past > 0),
        grid=(grp.batch, H // hp, nq),
        in_specs=[pl.BlockSpec((tq, wp), lambda b, h, i: (q_blk0 + b * nq + i, cols[1] // wp + h))]
        + kv_specs
        + [pl.BlockSpec((None, None, nkt, tq, tk), lambda b, h, i: (b, i, 0, 0, 0)),
           pl.BlockSpec((hp, 2, tq, tk), lambda b, h, i: (h, 0, 0, 0), pipeline_mode=resident)]
        + prev_specs,
        out_specs=pl.BlockSpec((tq, wp), lambda b, h, i: (q_blk0 + b * nq + i, o_col // wp + h)),
        out_shape=jax.ShapeDtypeStruct(o_shape, BF16),
        input_output_aliases={n_in: 0} if aliased else {},
        scratch_shapes=[pltpu.VMEM((tq, wp), F32), pltpu.VMEM((tq, wp), F32), pltpu.VMEM((tq, wp), F32)],
        compiler_params=_params("parallel", "parallel", "arbitrary"),
        name="dsa_flash",
    )(proj, *kv_args, mask, _near_bias(rel_bias, tq=tq, tk=tk), *prev_args)


def _cross_kernel(q_ref, k_ref, v_ref, *rest, n_heads, scale):
    o_ref = rest[-1]
    for hh in range(n_heads):
        sl = slice(hh * HEAD_DIM, (hh + 1) * HEAD_DIM)
        s = _nt_dot(q_ref[:, sl], k_ref[:, sl]) * scale
        p = jnp.exp(s - jnp.max(s, axis=-1, keepdims=True))
        denom = jnp.sum(p, axis=-1, keepdims=True)
        o = jnp.dot(p.astype(BF16), v_ref[:, sl], preferred_element_type=F32)
        o_ref[:, sl] = (o / denom).astype(o_ref.dtype)


def _cross_attention(grp, q, mem_k, mem_v, o_prev, *, tq=512):
    M, W = q.shape
    n_mem = mem_k.shape[1]
    tq = _pick_tile(grp.seq, tq)
    nq = grp.seq // tq
    assert grp.row0 % tq == 0
    blk0 = grp.row0 // tq
    prev_args, prev_specs, aliased = _chained(o_prev)
    return pl.pallas_call(
        functools.partial(_cross_kernel, n_heads=W // HEAD_DIM, scale=HEAD_DIM ** -0.5),
        grid=(grp.batch, nq),
        in_specs=[
            pl.BlockSpec((tq, W), lambda b, i: (blk0 + b * nq + i, 0)),
            pl.BlockSpec((None, n_mem, W), lambda b, i: (b, 0, 0)),
            pl.BlockSpec((None, n_mem, W), lambda b, i: (b, 0, 0)),
        ] + prev_specs,
        out_specs=pl.BlockSpec((tq, W), lambda b, i: (blk0 + b * nq + i, 0)),
        out_shape=jax.ShapeDtypeStruct((M, W), BF16),
        input_output_aliases={3: 0} if aliased else {},
        compiler_params=_params("parallel", "parallel"),
        name="cross_attention",
    )(q, mem_k, mem_v, *prev_args)


CONV_ROW_CHUNK = 128
SUBLANES = 8


def _conv_kernel(a_ref, prev_ref, state_ref, w_ref, bias_ref, g_ref, *rest, tt):
    o_ref, buf_ref, q_ref, y_ref = rest[-4], rest[-3], rest[-2], rest[-1]
    D = a_ref.shape[-1]

    @pl.when(pl.program_id(1) == 0)
    def _():
        buf_ref[0:CONV_HALO, :] = state_ref[...]

    @pl.when(pl.program_id(1) > 0)
    def _():
        buf_ref[0:CONV_HALO, :] = prev_ref[...]

    buf_ref[CONV_HALO:CONV_HALO + tt, :] = a_ref[...]
    buf_ref[CONV_HALO + tt:CONV_HALO + tt + SUBLANES, :] = jnp.zeros((SUBLANES, D), F32)

    first = CONV_HALO - (CONV_WIDTH - 1)
    rc = min(CONV_ROW_CHUNK, tt)
    for c0 in range(0, D, HEAD_DIM):
        cs = slice(c0, c0 + HEAD_DIM)
        for t0 in range(0, tt, rc):
            out = jnp.zeros((rc, HEAD_DIM), F32)
            for r in range(SUBLANES):
                q = None
                for k in range(CONV_WIDTH):
                    if (first + k) % SUBLANES != r:
                        continue
                    base = t0 + first + k - r
                    term = buf_ref[base:base + rc + SUBLANES, cs] * w_ref[k:k + 1, cs]
                    q = term if q is None else q + term
                if r == 0:
                    out = out + q[0:rc]
                else:
                    q_ref[...] = q
                    out = out + q_ref[r:r + rc, :]
            y_ref[t0:t0 + rc, cs] = out + bias_ref[:, cs]
    y = _rms_rows(y_ref[...], g_ref[...])
    o_ref[...] = (y * (1.0 / (1.0 + jnp.exp(-y)))).astype(o_ref.dtype)


def _conv_norm_swish(grp, u, state, w_dw, b_dw, g_norm, o_prev):
    M, D = u.shape
    tt = _pick_tile(grp.seq, 256)
    nt = grp.seq // tt
    assert grp.seq % tt == 0 and tt % CONV_HALO == 0 and grp.row0 % tt == 0
    blk0 = grp.row0 // tt
    hb = tt // CONV_HALO
    prev_args, prev_specs, aliased = _chained(o_prev)
    return pl.pallas_call(
        functools.partial(_conv_kernel, tt=tt),
        grid=(grp.batch, nt),
        in_specs=[
            pl.BlockSpec((tt, D), lambda b, i: (blk0 + b * nt + i, 0)),
            pl.BlockSpec((CONV_HALO, D), lambda b, i: (jnp.maximum((blk0 + b * nt + i) * hb - 1, 0), 0)),
            pl.BlockSpec((None, CONV_HALO, D), lambda b, i: (b, 0, 0)),
            pl.BlockSpec((CONV_WIDTH, D), lambda b, i: (0, 0)),
            pl.BlockSpec((1, D), lambda b, i: (0, 0)),
            pl.BlockSpec((1, D), lambda b, i: (0, 0)),
        ] + prev_specs,
        out_specs=pl.BlockSpec((tt, D), lambda b, i: (blk0 + b * nt + i, 0)),
        out_shape=jax.ShapeDtypeStruct((M, D), BF16),
        input_output_aliases={6: 0} if aliased else {},
        scratch_shapes=[pltpu.VMEM((tt + CONV_HALO + SUBLANES, D), F32),
                        pltpu.VMEM((min(CONV_ROW_CHUNK, tt) + SUBLANES, HEAD_DIM), F32),
                        pltpu.VMEM((tt, D), F32)],
        compiler_params=_params("parallel", "arbitrary"),
        name="conv_norm_swish",
    )(u, u, state, w_dw, b_dw.reshape(1, D), g_norm.reshape(1, D), *prev_args)


def kernel(x_prompt, x_sample, mem_prompt, cache_sb_k, cache_sb_v, cache_dsa_k, cache_dsa_v, cache_idx_k, cache_mem_k, cache_mem_v, state_conv, norm_mix, norm_cross, norm_mlp, norm_final, w_in_ab, w_out_ab, rel_bias, w_pw1, b_pw1, w_dw, b_dw, g_conv_norm, w_pw2, b_pw2, w_cq, w_mk, w_mv, w_co, w_up, w_down):
    Bp, Tp, D = x_prompt.shape
    Bs, Ts, _ = x_sample.shape
    past = cache_sb_k.shape[2]
    depth = norm_mix.shape[0]
    n_mem = mem_prompt.shape[1]
    n_sb, n_dsa = cache_sb_k.shape[3], cache_dsa_k.shape[3]
    sbw, dsw = n_sb * HEAD_DIM, n_dsa * HEAD_DIM
    qiw = N_IDX_HEADS * IDX_DIM
    cw = w_cq.shape[2]
    Mp, Ms = Bp * Tp, Bs * Ts
    M = Mp + Ms
    prompt = _Group(Bp, Tp, 0, 0)
    sample = _Group(Bs, Ts, Mp, past)

    x = jnp.concatenate([x_prompt.reshape(Mp, D), x_sample.reshape(Ms, D)], axis=0)
    mem_flat = mem_prompt.reshape(Bp * n_mem, D)

    src = dict(zip(("q_sb", "k_sb", "v_sb", "q_d", "k_d", "v_d", "q_i", "k_i", "w_i"),
                   np.cumsum([0, sbw, sbw, sbw, dsw, dsw, dsw, qiw, IDX_DIM])))
    order = ("k_sb", "v_sb", "k_d", "v_d", "q_sb", "q_d", "q_i")
    width = dict(q_sb=sbw, k_sb=sbw, v_sb=sbw, q_d=dsw, k_d=dsw, v_d=dsw, q_i=qiw)
    col = dict(zip(order, np.cumsum([0] + [width[n] for n in order])[:-1]))
    col = {k: int(v) for k, v in col.items()}
    n32 = 2 * sbw + 2 * dsw

    outs = {k: [] for k in ("p_sb_k", "p_sb_v", "p_d_k", "p_d_v", "p_ki", "p_mem_k", "p_mem_v", "p_conv",
                            "s_sb_k", "s_sb_v", "s_d_k", "s_d_v", "s_ki", "s_conv")}

    def split_rows(a):
        return a[:Mp].reshape(Bp, Tp, -1), a[Mp:].reshape(Bs, Ts, -1)

    def last_rows(a, grp, state, n):
        if grp.seq >= n:
            return jnp.stack([a[grp.row0 + (b + 1) * grp.seq - n:grp.row0 + (b + 1) * grp.seq]
                              for b in range(grp.batch)])
        rows = a[grp.row0:grp.row0 + grp.batch * grp.seq].reshape(grp.batch, grp.seq, -1)
        return jnp.concatenate([state, rows], axis=1)[:, -n:]

    w_out16, w_pw1_16, w_pw2_16 = w_out_ab.astype(BF16), w_pw1.astype(BF16), w_pw2.astype(BF16)
    w_cq16, w_mk16, w_mv16, w_co16 = (w.astype(BF16) for w in (w_cq, w_mk, w_mv, w_co))
    w_up16, w_down16 = w_up.astype(BF16), w_down.astype(BF16)
    q_scale = HEAD_DIM ** -0.5 * LOG2E
    col_scale = jnp.concatenate([jnp.full((width[n],), q_scale if n in ("q_sb", "q_d") else 1.0, F32)
                                 for n in order])

    for l in range(depth):
        i = l // 2
        if l % 2 == 0:
            w_in = w_in_ab[i]
            w_main = jnp.concatenate([w_in[:, src[n]:src[n] + width[n]] for n in order], axis=1).astype(BF16)
            tail_w = w_in.shape[1] - int(src["k_i"])
            w_tail = jnp.pad(w_in[:, src["k_i"]:], ((0, 0), (0, HEAD_DIM - tail_w))).astype(BF16)
            proj, kv32 = _in_proj(x, w_main, norm_mix[l], col_scale, n32=n32)
            tail = _matmul(x, w_tail[None], gain=norm_mix[l])

            for name, key, nh in (("sb_k", "k_sb", n_sb), ("sb_v", "v_sb", n_sb),
                                  ("d_k", "k_d", n_dsa), ("d_v", "v_d", n_dsa)):
                ap, as_ = split_rows(kv32[:, col[key]:col[key] + width[key]])
                outs["p_" + name].append(ap.reshape(Bp, Tp, nh, HEAD_DIM))
                outs["s_" + name].append(as_.reshape(Bs, Ts, nh, HEAD_DIM))
            k_i = tail[:, :IDX_DIM]
            kip, kis = split_rows(k_i)
            outs["p_ki"].append(kip)
            outs["s_ki"].append(kis)

            tk_s = ATTN_TILE
            s_pad_s = -(-(past + Ts) // tk_s) * tk_s
            kidx_p = kip.astype(BF16)
            kidx_s = jnp.concatenate([cache_idx_k[i].astype(BF16), kis.astype(BF16),
                                      jnp.zeros((Bs, s_pad_s - past - Ts, IDX_DIM), BF16)], axis=1)

            o_shape = (M, sbw + dsw)
            sb_cols = (col["q_sb"], col["k_sb"], col["v_sb"])
            d_cols = (col["q_i"], col["q_d"], col["k_d"], col["v_d"])
            o = _sb_attention(prompt, proj, sb_cols, cache_sb_k[i], cache_sb_v[i], None, o_shape, 0,
                              hp=SB_HEADS_PER_STEP)
            o = _dsa_attention(prompt, proj, tail, d_cols, kidx_p, cache_dsa_k[i], cache_dsa_v[i], rel_bias,
                               o, o_shape, sbw, hp=DSA_HEADS_PER_STEP)
            o = _sb_attention(sample, proj, sb_cols, cache_sb_k[i], cache_sb_v[i], o, o_shape, 0,
                              hp=SB_HEADS_PER_STEP)
            o = _dsa_attention(sample, proj, tail, d_cols, kidx_s, cache_dsa_k[i], cache_dsa_v[i], rel_bias,
                               o, o_shape, sbw, hp=DSA_HEADS_PER_STEP)
            x, qc = _residual_matmul_then_proj(o, w_out16, i, x, None, norm_cross[l], w_cq16, l)
        else:
            u = _matmul(x, w_pw1_16, layer=i, gain=norm_mix[l], bias=b_pw1[i], glu=True)
            front = CONV_HALO - (CONV_WIDTH - 1)
            state_p = jnp.zeros((Bp, CONV_HALO, D), F32)
            state_s = jnp.concatenate([jnp.zeros((Bs, front, D), F32), state_conv[i]], axis=1)
            y = _conv_norm_swish(prompt, u, state_p, w_dw[i], b_dw[i], g_conv_norm[i], None)
            y = _conv_norm_swish(sample, u, state_s, w_dw[i], b_dw[i], g_conv_norm[i], y)
            outs["p_conv"].append(last_rows(u, prompt, state_p, CONV_WIDTH - 1))
            outs["s_conv"].append(last_rows(u, sample, state_s, CONV_WIDTH - 1))
            x, qc = _residual_matmul_then_proj(y, w_pw2_16, i, x, b_pw2[i], norm_cross[l], w_cq16, l)

        mk = _matmul(mem_flat, w_mk16, layer=l)
        mv = _matmul(mem_flat, w_mv16, layer=l)
        outs["p_mem_k"].append(mk.reshape(Bp, n_mem, cw // HEAD_DIM, HEAD_DIM))
        outs["p_mem_v"].append(mv.reshape(Bp, n_mem, cw // HEAD_DIM, HEAD_DIM))
        oc = _cross_attention(prompt, qc, mk.astype(BF16).reshape(Bp, n_mem, cw),
                              mv.astype(BF16).reshape(Bp, n_mem, cw), None)
        oc = _cross_attention(sample, qc, cache_mem_k[l].astype(BF16).reshape(Bs, n_mem, cw),
                              cache_mem_v[l].astype(BF16).reshape(Bs, n_mem, cw), oc)
        if l < depth - 1:
            x = _cross_out_mlp(x, oc, w_co16, norm_mlp[l], w_up16, w_down16, l)
        else:
            y_p, y_s = _cross_out_mlp(x, oc, w_co16, norm_mlp[l], w_up16, w_down16, l,
                                      final_gain=norm_final, m_prompt=Mp)

    st = lambda k: jnp.stack(outs[k])
    return (y_p.reshape(Bp, Tp, D), y_s.reshape(Bs, Ts, D), st("p_sb_k"), st("p_sb_v"), st("p_d_k"), st("p_d_v"),
            st("p_ki"), st("p_mem_k"), st("p_mem_v"), st("p_conv"), st("s_sb_k"), st("s_sb_v"), st("s_d_k"),
            st("s_d_v"), st("s_ki"), st("s_conv"))
```

```python
import functools
import math
from typing import NamedTuple

import numpy as np
import jax
import jax.numpy as jnp
from jax import lax
from jax.experimental import pallas as pl
from jax.experimental.pallas import tpu as pltpu

F32 = jnp.float32
BF16 = jnp.bfloat16

HEAD_DIM = 128
IDX_DIM = 64
N_IDX_HEADS = 16
CHUNK = 64
TOPK_MAX = 256
N_BUCKETS = 32
MAX_DISTANCE = 128
CONV_WIDTH = 31
CONV_HALO = 32
RMS_EPS = 1e-6
MASKED_LOGIT = -1e30
LOG2E = math.log2(math.e)
_INT32_MIN = -2 ** 31
VMEM_LIMIT_BYTES = 56 * 1024 * 1024
SB_HEADS_PER_STEP = 8
DSA_HEADS_PER_STEP = 4
SB_INTERLEAVE = 4
ATTN_TILE = 256


class _Group(NamedTuple):
    batch: int
    seq: int
    row0: int
    past: int


def _params(*semantics):
    return pltpu.CompilerParams(dimension_semantics=semantics, vmem_limit_bytes=VMEM_LIMIT_BYTES)


def _pick_tile(n, pref):
    if n <= pref:
        return n
    t = pref
    while n % t:
        t //= 2
    return t


def _row_tile(m, pref):
    t = min(m, pref) // 8 * 8
    while m % t:
        t -= 8
    return t


def _chained(prev):
    if prev is None:
        return [], [], None
    return [prev], [pl.BlockSpec(memory_space=pl.ANY)], True


def _rms_rows(x, g):
    ms = jnp.mean(x * x, axis=-1, keepdims=True)
    return x * lax.rsqrt(ms + RMS_EPS) * g


def _mm_kernel(*refs, has_norm, has_bias, has_res, glu):
    it = iter(refs)
    x_ref = next(it)
    g_ref = next(it) if has_norm else None
    w_ref = next(it)
    w2_ref = next(it) if glu else None
    b_ref = next(it) if has_bias else None
    b2_ref = next(it) if (has_bias and glu) else None
    r_ref = next(it) if has_res else None
    o_ref = next(it)
    xn_ref = next(it) if has_norm else None

    if has_norm:
        @pl.when(pl.program_id(1) == 0)
        def _():
            xn_ref[...] = _rms_rows(x_ref[...], g_ref[...]).astype(BF16)
        xb = xn_ref[...]
    else:
        xb = x_ref[...].astype(BF16)

    y = jnp.dot(xb, w_ref[...], preferred_element_type=F32)
    if has_bias:
        y = y + b_ref[...]
    if glu:
        gate = jnp.dot(xb, w2_ref[...], preferred_element_type=F32)
        if has_bias:
            gate = gate + b2_ref[...]
        y = y * (1.0 / (1.0 + jnp.exp(-gate)))
    if has_res:
        y = y + r_ref[...]
    o_ref[...] = y.astype(o_ref.dtype)


def _matmul(x, w, *, layer=0, gain=None, bias=None, residual=None, glu=False, out_dtype=F32, tm=1152, tn=512):
    M, K = x.shape
    n_out = w.shape[2] // 2 if glu else w.shape[2]
    tm = _row_tile(M, tm)
    tn = _pick_tile(n_out, tn)
    nj = n_out // tn
    has_norm, has_bias, has_res = gain is not None, bias is not None, residual is not None

    args = [x]
    specs = [pl.BlockSpec((tm, K), lambda i, j: (i, 0))]
    if has_norm:
        args.append(gain.reshape(1, K))
        specs.append(pl.BlockSpec((1, K), lambda i, j: (0, 0)))
    args.append(w)
    specs.append(pl.BlockSpec((None, K, tn), lambda i, j: (layer, 0, j)))
    if glu:
        args.append(w)
        specs.append(pl.BlockSpec((None, K, tn), lambda i, j: (layer, 0, j + nj)))
    if has_bias:
        b2d = bias.reshape(1, -1)
        args.append(b2d)
        specs.append(pl.BlockSpec((1, tn), lambda i, j: (0, j)))
        if glu:
            args.append(b2d)
            specs.append(pl.BlockSpec((1, tn), lambda i, j: (0, j + nj)))
    if has_res:
        args.append(residual)
        specs.append(pl.BlockSpec((tm, tn), lambda i, j: (i, j)))

    return pl.pallas_call(
        functools.partial(_mm_kernel, has_norm=has_norm, has_bias=has_bias, has_res=has_res, glu=glu),
        grid=(M // tm, nj),
        in_specs=specs,
        out_specs=pl.BlockSpec((tm, tn), lambda i, j: (i, j)),
        out_shape=jax.ShapeDtypeStruct((M, n_out), out_dtype),
        scratch_shapes=[pltpu.VMEM((tm, K), BF16)] if has_norm else [],
        compiler_params=_params("parallel", "arbitrary"),
        name="norm_matmul",
    )(*args)


def _res_proj_kernel(y_ref, w_ref, *refs, has_bias):
    it = iter(refs)
    b_ref = next(it) if has_bias else None
    r_ref, g_ref, wq_ref, o_ref, q_ref, x1_ref = (next(it) for _ in range(6))
    j = pl.program_id(1)
    y = jnp.dot(y_ref[...], w_ref[...], preferred_element_type=F32) + r_ref[...]
    if has_bias:
        y = y + b_ref[...]
    o_ref[...] = y
    x1_ref[j] = y

    @pl.when(j == pl.num_programs(1) - 1)
    def _():
        x1 = jnp.concatenate([x1_ref[t] for t in range(x1_ref.shape[0])], axis=1)
        q_ref[...] = jnp.dot(_rms_rows(x1, g_ref[...]).astype(BF16), wq_ref[...],
                             preferred_element_type=F32).astype(q_ref.dtype)


def _residual_matmul_then_proj(y, w, layer, residual, bias, gain, wq, q_layer, *, tm=1152, tn=512):
    M, K = y.shape
    N, nq = w.shape[2], wq.shape[2]
    tm = _row_tile(M, tm)
    assert N % tn == 0
    nj = N // tn
    has_bias = bias is not None
    args = [y, w]
    specs = [pl.BlockSpec((tm, K), lambda i, j: (i, 0)), pl.BlockSpec((None, K, tn), lambda i, j: (layer, 0, j))]
    if has_bias:
        args.append(bias.reshape(1, N))
        specs.append(pl.BlockSpec((1, tn), lambda i, j: (0, j)))
    args += [residual, gain.reshape(1, N), wq]
    specs += [pl.BlockSpec((tm, tn), lambda i, j: (i, j)), pl.BlockSpec((1, N), lambda i, j: (0, 0)),
              pl.BlockSpec((None, N, nq), lambda i, j: (q_layer, 0, 0))]
    return pl.pallas_call(
        functools.partial(_res_proj_kernel, has_bias=has_bias),
        grid=(M // tm, nj),
        in_specs=specs,
        out_specs=[pl.BlockSpec((tm, tn), lambda i, j: (i, j)), pl.BlockSpec((tm, nq), lambda i, j: (i, 0))],
        out_shape=[jax.ShapeDtypeStruct((M, N), F32), jax.ShapeDtypeStruct((M, nq), BF16)],
        scratch_shapes=[pltpu.VMEM((nj, tm, tn), F32)],
        compiler_params=_params("parallel", "arbitrary"),
        name="residual_matmul_proj",
    )(*args)


def _in_proj_kernel(x_ref, g_ref, w_ref, s_ref, o16_ref, o32_ref, xn_ref, *, n32_tiles):
    j = pl.program_id(1)

    @pl.when(j == 0)
    def _():
        xn_ref[...] = _rms_rows(x_ref[...], g_ref[...]).astype(BF16)

    y = jnp.dot(xn_ref[...], w_ref[...], preferred_element_type=F32)
    o16_ref[...] = (y * s_ref[...]).astype(BF16)

    @pl.when(j < n32_tiles)
    def _():
        o32_ref[...] = y


def _in_proj(x, w, gain, col_scale, *, n32, tm=1152, tn=512):
    M, K = x.shape
    N = w.shape[1]
    tm = _row_tile(M, tm)
    assert N % tn == 0 and n32 % tn == 0
    n32_tiles = n32 // tn
    return pl.pallas_call(
        functools.partial(_in_proj_kernel, n32_tiles=n32_tiles),
        grid=(M // tm, N // tn),
        in_specs=[
            pl.BlockSpec((tm, K), lambda i, j: (i, 0)),
            pl.BlockSpec((1, K), lambda i, j: (0, 0)),
            pl.BlockSpec((K, tn), lambda i, j: (0, j)),
            pl.BlockSpec((1, tn), lambda i, j: (0, j)),
        ],
        out_specs=[
            pl.BlockSpec((tm, tn), lambda i, j: (i, j)),
            pl.BlockSpec((tm, tn), lambda i, j: (i, jnp.minimum(j, n32_tiles - 1))),
        ],
        out_shape=[jax.ShapeDtypeStruct((M, N), BF16), jax.ShapeDtypeStruct((M, n32), F32)],
        scratch_shapes=[pltpu.VMEM((tm, K), BF16)],
        compiler_params=_params("parallel", "arbitrary"),
        name="in_proj",
    )(x, gain.reshape(1, K), w, col_scale.reshape(1, N))


def _mlp_kernel(x_ref, oc_ref, wco_ref, g_ref, wu_ref, wd_ref, *refs, n_prompt_tiles):
    final = n_prompt_tiles is not None
    if final:
        gf_ref, op_ref, os_ref, x2_ref, xn_ref, acc_ref = refs
    else:
        o_ref, x2_ref, xn_ref, acc_ref = refs
    i, f = pl.program_id(0), pl.program_id(1)

    @pl.when(f == 0)
    def _():
        x2 = x_ref[...] + jnp.dot(oc_ref[...], wco_ref[...], preferred_element_type=F32)
        x2_ref[...] = x2
        xn_ref[...] = _rms_rows(x2, g_ref[...]).astype(BF16)
        acc_ref[...] = jnp.zeros_like(acc_ref)

    h = jnp.dot(xn_ref[...], wu_ref[...], preferred_element_type=F32)
    h = jnp.maximum(h, 0.0)
    h = (h * h).astype(BF16)
    acc_ref[...] += jnp.dot(h, wd_ref[...], preferred_element_type=F32)

    last = f == pl.num_programs(1) - 1
    if not final:
        @pl.when(last)
        def _():
            o_ref[...] = x2_ref[...] + acc_ref[...]
    else:
        @pl.when(last & (i < n_prompt_tiles))
        def _():
            op_ref[...] = _rms_rows(x2_ref[...] + acc_ref[...], gf_ref[...])

        @pl.when(last & (i >= n_prompt_tiles))
        def _():
            os_ref[...] = _rms_rows(x2_ref[...] + acc_ref[...], gf_ref[...])


def _cross_out_mlp(x, oc, w_co, gain, w_up, w_down, layer, *, final_gain=None, m_prompt=None, tm=512, tf=512):
    M, D = x.shape
    F = w_up.shape[2]
    cw = oc.shape[1]
    final = final_gain is not None
    tm = _pick_tile(M - m_prompt, tm) if final else _pick_tile(M, tm)
    tf = _pick_tile(F, tf)
    args = [x, oc, w_co, gain.reshape(1, D), w_up, w_down]
    specs = [
        pl.BlockSpec((tm, D), lambda i, f: (i, 0)),
        pl.BlockSpec((tm, cw), lambda i, f: (i, 0)),
        pl.BlockSpec((None, cw, D), lambda i, f: (layer, 0, 0)),
        pl.BlockSpec((1, D), lambda i, f: (0, 0)),
        pl.BlockSpec((None, D, tf), lambda i, f: (layer, 0, f)),
        pl.BlockSpec((None, tf, D), lambda i, f: (layer, f, 0)),
    ]
    if final:
        assert m_prompt % tm == 0 and (M - m_prompt) % tm == 0
        npt = m_prompt // tm
        args.append(final_gain.reshape(1, D))
        specs.append(pl.BlockSpec((1, D), lambda i, f: (0, 0)))
        out_specs = [pl.BlockSpec((tm, D), lambda i, f: (jnp.minimum(i, npt - 1), 0)),
                     pl.BlockSpec((tm, D), lambda i, f: (jnp.maximum(i - npt, 0), 0))]
        out_shape = [jax.ShapeDtypeStruct((m_prompt, D), F32), jax.ShapeDtypeStruct((M - m_prompt, D), F32)]
        semantics = ("arbitrary", "arbitrary")
    else:
        npt = None
        out_specs = pl.BlockSpec((tm, D), lambda i, f: (i, 0))
        out_shape = jax.ShapeDtypeStruct((M, D), F32)
        semantics = ("parallel", "arbitrary")
    return pl.pallas_call(
        functools.partial(_mlp_kernel, n_prompt_tiles=npt),
        grid=(M // tm, F // tf),
        in_specs=specs,
        out_specs=out_specs,
        out_shape=out_shape,
        scratch_shapes=[pltpu.VMEM((tm, D), F32), pltpu.VMEM((tm, D), BF16), pltpu.VMEM((tm, D), F32)],
        compiler_params=_params(*semantics),
        name="cross_out_mlp",
    )(*args)


def _nt_dot(a, b):
    return lax.dot_general(a, b, (((1,), (1,)), ((), ())), preferred_element_type=F32)


def _kv_operands(grp, proj, k_col, v_col, k_cache, v_cache, *, tk, hp):
    wp = hp * HEAD_DIM
    if grp.past == 0:
        blk0 = grp.row0 // grp.seq
        args = [proj, proj]
        specs = [pl.BlockSpec((grp.seq, wp), lambda b, h, i, c=k_col // wp: (blk0 + b, c + h),
                              pipeline_mode=pl.Buffered(1)),
                 pl.BlockSpec((grp.seq, wp), lambda b, h, i, c=v_col // wp: (blk0 + b, c + h),
                              pipeline_mode=pl.Buffered(1))]
        return args, specs

    def new_tile(col):
        rows = lax.slice(proj, (grp.row0, col), (grp.row0 + grp.batch * grp.seq, col + k_cache.shape[2] * HEAD_DIM))
        rows = rows.reshape(grp.batch, grp.seq, -1)
        return jnp.pad(rows, ((0, 0), (0, tk - grp.seq), (0, 0)))

    args = [new_tile(k_col), new_tile(v_col)]
    specs = [pl.BlockSpec((None, tk, wp), lambda b, h, i: (b, 0, h))] * 2
    n_heads = k_cache.shape[2]
    for cache in (k_cache, v_cache):
        args.append(cache.reshape(grp.batch, grp.past * n_heads, HEAD_DIM))
        specs.append(pl.BlockSpec((None, grp.past * n_heads, HEAD_DIM), lambda b, h, i: (b, 0, 0)))
    return args, specs


def _split_kv_refs(refs, has_past):
    if not has_past:
        return refs[0], refs[1], None, None, refs[2:]
    return refs[0], refs[1], refs[2], refs[3], refs[4:]


def _tile_reader(new_ref, past_ref, *, tk, hp, n_heads):
    def read(j, hh, diag, n=1):
        hs = slice(hh * HEAD_DIM, (hh + 1) * HEAD_DIM)
        if past_ref is None:
            return new_ref[pl.ds(pl.multiple_of(j * tk, tk), n * tk), hs]
        if diag:
            return new_ref[:, hs]
        head = pl.program_id(1) * hp + hh
        return past_ref[pl.ds(j * (tk * n_heads) + head, n * tk, stride=n_heads), :].astype(BF16)
    return read


def _sb_kernel(q_ref, *refs, tq, tk, hp, n_heads, past, has_past):
    k_new, v_new, k_past, v_past, rest = _split_kv_refs(refs, has_past)
    u_ref, o_ref, acc_ref, carry_ref = rest[0], rest[-3], rest[-2], rest[-1]
    read_k = _tile_reader(k_new, k_past, tk=tk, hp=hp, n_heads=n_heads)
    read_v = _tile_reader(v_new, v_past, tk=tk, hp=hp, n_heads=n_heads)

    i = pl.program_id(2)
    q_start = past + i * tq
    jd = q_start // tk
    reps = tk // HEAD_DIM

    acc_ref[...] = jnp.zeros_like(acc_ref)
    carry_ref[...] = jnp.zeros_like(carry_ref)

    def step(j, masked):
        if masked:
            t_pos = q_start + lax.broadcasted_iota(jnp.int32, (tq, tk), 0)
            s_pos = j * tk + lax.broadcasted_iota(jnp.int32, (tq, tk), 1)
            causal = t_pos > s_pos
        for h0 in range(0, hp, SB_INTERLEAVE):
            group_step(j, masked, causal if masked else None, range(h0, h0 + SB_INTERLEAVE))

    def group_step(j, masked, causal, group):
        heads = [(hh, slice(hh * HEAD_DIM, (hh + 1) * HEAD_DIM)) for hh in group]
        z2s = [_nt_dot(q_ref[:, hs], read_k(j, hh, masked)) for hh, hs in heads]
        ns, log2_betas, css = [], [], []
        for z2 in z2s:
            neg_abs = lax.bitcast_convert_type(
                lax.bitcast_convert_type(z2, jnp.int32) | jnp.int32(_INT32_MIN), F32)
            n = jnp.maximum(z2, 0.0) + jnp.log(1.0 + jnp.exp2(neg_abs)) * LOG2E
            log2_betas.append(z2 - n)
            if masked:
                n = jnp.where(causal, n, 0.0)
            n16 = n.astype(BF16)
            css.append(jnp.dot(n16, u_ref[...], preferred_element_type=F32))
            ns.append(n16)
        for (hh, hs), n16, log2_beta, cs in zip(heads, ns, log2_betas, css):
            a = jnp.exp2(log2_beta - cs)
            if masked:
                a = jnp.where(causal, a, 0.0)
            carry = carry_ref[:, hs]
            pv = jnp.dot(a.astype(BF16), read_v(j, hh, masked), preferred_element_type=F32)
            acc_ref[:, hs] += jnp.exp2(-carry) * pv
            carry_ref[:, hs] = carry + jnp.broadcast_to(cs[:, 0:1] + n16[:, 0:1].astype(F32), (tq, HEAD_DIM))

    step(jd, True)

    def body(t, c):
        step(jd - 1 - t, False)
        return c

    lax.fori_loop(0, jd, body, 0)
    o_ref[...] = acc_ref[...].astype(o_ref.dtype)


def _suffix_sum_matrix(tk):
    r = np.arange(tk)
    return jnp.asarray((r[:, None] > r[None, :]).astype(np.float32), dtype=BF16)


def _attn_tiles(grp):
    tq = _pick_tile(grp.seq, ATTN_TILE)
    tk = ATTN_TILE if grp.past else tq
    nq = grp.seq // tq
    assert grp.seq % tq == 0 and tk % tq == 0 and grp.past % tk == 0 and grp.row0 % tq == 0
    assert grp.past == 0 or nq == 1
    assert grp.past > 0 or grp.row0 % grp.seq == 0
    return tq, tk, nq


def _sb_attention(grp, proj, cols, k_cache, v_cache, o_prev, o_shape, o_col, *, hp):
    H = k_cache.shape[2]
    tq, tk, nq = _attn_tiles(grp)
    wp = hp * HEAD_DIM
    q_blk0 = grp.row0 // tq
    kv_args, kv_specs = _kv_operands(grp, proj, cols[1], cols[2], k_cache, v_cache, tk=tk, hp=hp)
    prev_args, prev_specs, aliased = _chained(o_prev)
    n_in = 1 + len(kv_args) + 1
    return pl.pallas_call(
        functools.partial(_sb_kernel, tq=tq, tk=tk, hp=hp, n_heads=H, past=grp.past, has_past=grp.past > 0),
        grid=(grp.batch, H // hp, nq),
        in_specs=[pl.BlockSpec((tq, wp), lambda b, h, i: (q_blk0 + b * nq + i, cols[0] // wp + h))]
        + kv_specs
        + [pl.BlockSpec((tk, tk), lambda b, h, i: (0, 0), pipeline_mode=pl.Buffered(1))]
        + prev_specs,
        out_specs=pl.BlockSpec((tq, wp), lambda b, h, i: (q_blk0 + b * nq + i, o_col // wp + h)),
        out_shape=jax.ShapeDtypeStruct(o_shape, BF16),
        input_output_aliases={n_in: 0} if aliased else {},
        scratch_shapes=[pltpu.VMEM((tq, wp), F32), pltpu.VMEM((tq, wp), F32)],
        compiler_params=_params("parallel", "parallel", "arbitrary"),
        name="sb_attention",
    )(proj, *kv_args, _suffix_sum_matrix(tk), *prev_args)


def _sortable_key_inverse(k):
    return k ^ ((k >> 31) & jnp.int32(0x7FFFFFFF))


def _sortable_key(x):
    return _sortable_key_inverse(lax.bitcast_convert_type(x, jnp.int32))


_KEY_NEG_INF = int(np.int32(np.array(-np.inf, np.float32).view(np.int32) ^ 0x7FFFFFFF))
_KEY_LOWEST_FINITE = int(np.int32(np.array(np.finfo(np.float32).min, np.float32).view(np.int32) ^ 0x7FFFFFFF))
COUNT_ROW_GROUP = 128


def _dsa_select_kernel(qi_ref, wi_ref, ki2_ref, mask_ref, keys_ref, wb_ref, colmax_ref, *, tq, tk, nkt, past, topk):
    i = pl.program_id(1)
    q_start = past + i * tq
    jd = q_start // tk
    reps = tk // HEAD_DIM

    w = wi_ref[:, IDX_DIM:IDX_DIM + N_IDX_HEADS] * (IDX_DIM ** -0.5 * N_IDX_HEADS ** -0.5)
    for hh in range(N_IDX_HEADS):
        wb_ref[hh] = jnp.broadcast_to(w[:, hh:hh + 1], (tq, HEAD_DIM))

    def idx_tile(j):
        kt = ki2_ref[j]
        tot = jnp.zeros((tq, tk), F32)
        for p in range(N_IDX_HEADS // 2):
            sc = jnp.dot(qi_ref[:, p * 128:(p + 1) * 128], kt, preferred_element_type=F32)
            w0 = jnp.concatenate([wb_ref[2 * p]] * reps, axis=1)
            w1 = jnp.concatenate([wb_ref[2 * p + 1]] * reps, axis=1)
            tot = tot + w0 * jnp.maximum(sc[:, :tk], 0.0) + w1 * jnp.maximum(sc[:, tk:], 0.0)
        return _sortable_key(tot)

    t_pos = q_start + lax.broadcasted_iota(jnp.int32, (tq, tk), 0)
    s_pos = jd * tk + lax.broadcasted_iota(jnp.int32, (tq, tk), 1)
    visible = (t_pos // CHUNK) >= (s_pos // CHUNK)
    colmax_ref[...] = jnp.where(visible, idx_tile(jd), jnp.int32(_KEY_NEG_INF))
    keys_ref[jd] = colmax_ref[...]

    def idx_body(j, c):
        key = idx_tile(j)
        keys_ref[j] = key
        colmax_ref[...] = jnp.maximum(colmax_ref[...], key)
        return c

    lax.fori_loop(0, jd, idx_body, 0)

    n_vis = jd + 1

    @pl.when(n_vis % 2 == 1)
    def _():
        keys_ref[n_vis] = jnp.full((tq, tk), _INT32_MIN, jnp.int32)

    n_pairs = (n_vis + 1) // 2
    rg = min(COUNT_ROW_GROUP, tq)

    def count_ge(cand):
        parts = []
        for g in range(tq // rg):
            cg = cand[g * rg:(g + 1) * rg]

            def body(t, cnt, g=g, cg=cg):
                for jj in (2 * t, 2 * t + 1):
                    kj = keys_ref[jj, pl.ds(g * rg, rg), :]
                    for r in range(reps):
                        cnt = cnt + jnp.where(kj[:, r * 128:(r + 1) * 128] >= cg, 1.0, 0.0)
                return cnt

            parts.append(lax.fori_loop(0, n_pairs, body, jnp.zeros((rg, HEAD_DIM), F32)))
        return jnp.sum(jnp.concatenate(parts, axis=0), axis=-1, keepdims=True)

    kf = float(topk)
    off = jnp.int32(_INT32_MIN)
    col_f = lax.bitcast_convert_type(_sortable_key_inverse(colmax_ref[...]), F32)
    hi_key = _sortable_key(jnp.max(col_f, axis=-1, keepdims=True))
    lo_key = _sortable_key(jnp.min(col_f, axis=-1, keepdims=True))
    n_bits = jnp.max((32 - lax.clz(hi_key ^ lo_key)).astype(F32)).astype(jnp.int32)
    low_mask = jnp.where(n_bits >= 32, jnp.int32(-1), (jnp.int32(1) << jnp.minimum(n_bits, 31)) - 1)
    res0 = jnp.broadcast_to((lo_key ^ off) & ~low_mask, (tq, HEAD_DIM))
    cnt0 = count_ge(res0 ^ off)

    def unfinished(state):
        bit, _, cnt = state
        return (bit >= 0) & (jnp.min(jnp.where(cnt == kf, 1.0, 0.0)) < 0.5)

    def search_bit(state):
        bit, res, cnt = state
        cand = res | (jnp.int32(1) << bit)
        c = count_ge(cand ^ off)
        take = c >= kf
        return bit - 1, jnp.where(take, cand, res), jnp.where(take, c, cnt)

    _, res, _ = lax.while_loop(unfinished, search_bit, (n_bits - 1, res0, cnt0))
    thr = jnp.maximum(res ^ off, jnp.int32(_KEY_LOWEST_FINITE))
    thr = jnp.concatenate([thr] * reps, axis=1)

    def write_mask(j, c):
        mask_ref[j] = jnp.where(keys_ref[j] >= thr, 0.0, MASKED_LOGIT).astype(mask_ref.dtype)
        return c

    lax.fori_loop(0, jd + 1, write_mask, 0)

    def write_hidden(j, c):
        mask_ref[j] = jnp.full((tq, tk), MASKED_LOGIT, mask_ref.dtype)
        return c

    lax.fori_loop(jd + 1, nkt, write_hidden, 0)


def _dsa_flash_kernel(q_ref, *refs, tq, tk, hp, n_heads, past, has_past):
    k_new, v_new, k_past, v_past, rest = _split_kv_refs(refs, has_past)
    mask_ref, bnear_ref = rest[0], rest[1]
    o_ref, m_ref, l_ref, acc_ref = rest[-4], rest[-3], rest[-2], rest[-1]
    read_k = _tile_reader(k_new, k_past, tk=tk, hp=hp, n_heads=n_heads)
    read_v = _tile_reader(v_new, v_past, tk=tk, hp=hp, n_heads=n_heads)

    i = pl.program_id(2)
    jd = (past + i * tq) // tk
    reps = tk // HEAD_DIM
    heads = [slice(hh * HEAD_DIM, (hh + 1) * HEAD_DIM) for hh in range(hp)]

    m_ref[...] = jnp.full_like(m_ref, MASKED_LOGIT)
    l_ref[...] = jnp.zeros_like(l_ref)
    acc_ref[...] = jnp.zeros_like(acc_ref)

    def step(j, near, n=1):
        diag = near == 0
        maskf = jnp.concatenate([mask_ref[j + t].astype(F32) for t in range(n)], axis=1)
        logits = []
        for hh, hs in enumerate(heads):
            s = _nt_dot(q_ref[:, hs], read_k(j, hh, diag, n)) + maskf
            if near is not None:
                s = s + bnear_ref[hh, near]
            logits.append(s)
        probs, alphas = [], []
        for hs, s in zip(heads, logits):
            m_old = m_ref[:, hs]
            m_new = jnp.maximum(m_old, jnp.max(s, axis=-1, keepdims=True))
            alpha = jnp.exp2(m_old - m_new)
            p = jnp.exp2(s - jnp.concatenate([m_new] * (n * reps), axis=1))
            l_ref[:, hs] = alpha * l_ref[:, hs] + sum(p[:, r * HEAD_DIM:(r + 1) * HEAD_DIM]
                                                      for r in range(n * reps))
            m_ref[:, hs] = m_new
            probs.append(p.astype(BF16))
            alphas.append(alpha)
        for hh, (hs, p, alpha) in enumerate(zip(heads, probs, alphas)):
            pv = jnp.dot(p, read_v(j, hh, diag, n), preferred_element_type=F32)
            acc_ref[:, hs] = alpha * acc_ref[:, hs] + pv

    step(jd, 0)

    @pl.when(jd >= 1)
    def _():
        step(jd - 1, 1)

    n_far = jnp.maximum(jd - 1, 0)

    @pl.when(n_far % 2 == 1)
    def _():
        step(n_far - 1, None)

    def body(t, c):
        step(2 * t, None, 2)
        return c

    lax.fori_loop(0, n_far // 2, body, 0)
    for hs in heads:
        o_ref[:, hs] = (acc_ref[:, hs] / jnp.sum(l_ref[:, hs], axis=-1, keepdims=True)).astype(o_ref.dtype)


def _t5_bucket(rel):
    nb = N_BUCKETS // 2
    max_exact = nb // 2
    side = jnp.where(rel > 0, nb, 0)
    n = jnp.abs(rel)
    nf = jnp.maximum(n, 1).astype(F32)
    large = max_exact + (jnp.log(nf / max_exact) / math.log(MAX_DISTANCE / max_exact) * (nb - max_exact)).astype(jnp.int32)
    large = jnp.minimum(large, nb - 1)
    return side + jnp.where(n < max_exact, n, large)


def _near_bias(rel_bias, *, tq, tk):
    r = jnp.arange(tq, dtype=jnp.int32)[:, None]
    c = jnp.arange(tk, dtype=jnp.int32)[None, :]
    bucket = _t5_bucket(jnp.stack([c - r, c - r - tk]))
    far_bucket = _t5_bucket(jnp.int32(-2 * tk))
    table = (rel_bias - rel_bias[far_bucket][None, :]) * LOG2E
    onehot = (bucket[..., None] == jnp.arange(N_BUCKETS, dtype=jnp.int32)).astype(F32)
    return jnp.einsum("dqkb,bh->hdqk", onehot, table, precision=lax.Precision.HIGHEST)


def _dsa_attention(grp, proj, tail, cols, k_idx_all, k_cache, v_cache, rel_bias, o_prev, o_shape, o_col, *, hp):
    H = k_cache.shape[2]
    tq, tk, nq = _attn_tiles(grp)
    s_pad = k_idx_all.shape[1]
    nkt = s_pad // tk
    assert tk % CHUNK == 0 and tq % CHUNK == 0 and s_pad % tk == 0
    assert tk >= MAX_DISTANCE and H % hp == 0
    topk = min(TOPK_MAX, (grp.past + grp.seq) // 4)
    assert topk <= tk
    q_blk0 = grp.row0 // tq
    qiw = N_IDX_HEADS * IDX_DIM

    kt = jnp.transpose(k_idx_all.reshape(grp.batch, nkt, tk, IDX_DIM), (0, 1, 3, 2))
    z = jnp.zeros_like(kt)
    ki2 = jnp.concatenate([jnp.concatenate([kt, z], axis=3), jnp.concatenate([z, kt], axis=3)], axis=2)
    resident = pl.Buffered(1) if nq > 1 else None

    mask = pl.pallas_call(
        functools.partial(_dsa_select_kernel, tq=tq, tk=tk, nkt=nkt, past=grp.past, topk=topk),
        grid=(grp.batch, nq),
        in_specs=[
            pl.BlockSpec((tq, qiw), lambda b, i: (q_blk0 + b * nq + i, cols[0] // qiw)),
            pl.BlockSpec((tq, HEAD_DIM), lambda b, i: (q_blk0 + b * nq + i, 0)),
            pl.BlockSpec((None, nkt, 2 * IDX_DIM, 2 * tk), lambda b, i: (b, 0, 0, 0), pipeline_mode=resident),
        ],
        out_specs=pl.BlockSpec((None, None, nkt, tq, tk), lambda b, i: (b, i, 0, 0, 0)),
        out_shape=jax.ShapeDtypeStruct((grp.batch, nq, nkt, tq, tk), BF16),
        scratch_shapes=[
            pltpu.VMEM((nkt + 1, tq, tk), jnp.int32),
            pltpu.VMEM((N_IDX_HEADS, tq, HEAD_DIM), F32),
            pltpu.VMEM((tq, tk), jnp.int32),
        ],
        compiler_params=_params("parallel", "parallel"),
        name="dsa_select",
    )(proj, tail, ki2)

    wp = hp * HEAD_DIM
    kv_args, kv_specs = _kv_operands(grp, proj, cols[2], cols[3], k_cache, v_cache, tk=tk, hp=hp)
    prev_args, prev_specs, aliased = _chained(o_prev)
    n_in = 1 + len(kv_args) + 2
    return pl.pallas_call(
        functools.partial(_dsa_flash_kernel, tq=tq, tk=tk, hp=hp, n_heads=H, past=grp.past, has_past=grp.past > 0),
        grid=(grp.batch, H // hp, nq),
        in_specs=[pl.BlockSpec((tq, wp), lambda b, h, i: (q_blk0 + b * nq + i, cols[1] // wp + h))]
        + kv_specs
        + [pl.BlockSpec((None, None, nkt, tq, tk), lambda b, h, i: (b, i, 0, 0, 0)),
           pl.BlockSpec((hp, 2, tq, tk), lambda b, h, i: (h, 0, 0, 0), pipeline_mode=resident)]
        + prev_specs,
        out_specs=pl.BlockSpec((tq, wp), lambda b, h, i: (q_blk0 + b * nq + i, o_col // wp + h)),
        out_shape=jax.ShapeDtypeStruct(o_shape, BF16),
        input_output_aliases={n_in: 0} if aliased else {},
        scratch_shapes=[pltpu.VMEM((tq, wp), F32), pltpu.VMEM((tq, wp), F32), pltpu.VMEM((tq, wp), F32)],
        compiler_params=_params("parallel", "parallel", "arbitrary"),
        name="dsa_flash",
    )(proj, *kv_args, mask, _near_bias(rel_bias, tq=tq, tk=tk), *prev_args)


def _cross_kernel(q_ref, k_ref, v_ref, *rest, n_heads, scale):
    o_ref = rest[-1]
    for hh in range(n_heads):
        sl = slice(hh * HEAD_DIM, (hh + 1) * HEAD_DIM)
        s = _nt_dot(q_ref[:, sl], k_ref[:, sl]) * scale
        p = jnp.exp(s - jnp.max(s, axis=-1, keepdims=True))
        denom = jnp.sum(p, axis=-1, keepdims=True)
        o = jnp.dot(p.astype(BF16), v_ref[:, sl], preferred_element_type=F32)
        o_ref[:, sl] = (o / denom).astype(o_ref.dtype)


def _cross_attention(grp, q, mem_k, mem_v, o_prev, *, tq=512):
    M, W = q.shape
    n_mem = mem_k.shape[1]
    tq = _pick_tile(grp.seq, tq)
    nq = grp.seq // tq
    assert grp.row0 % tq == 0
    blk0 = grp.row0 // tq
    prev_args, prev_specs, aliased = _chained(o_prev)
    return pl.pallas_call(
        functools.partial(_cross_kernel, n_heads=W // HEAD_DIM, scale=HEAD_DIM ** -0.5),
        grid=(grp.batch, nq),
        in_specs=[
            pl.BlockSpec((tq, W), lambda b, i: (blk0 + b * nq + i, 0)),
            pl.BlockSpec((None, n_mem, W), lambda b, i: (b, 0, 0)),
            pl.BlockSpec((None, n_mem, W), lambda b, i: (b, 0, 0)),
        ] + prev_specs,
        out_specs=pl.BlockSpec((tq, W), lambda b, i: (blk0 + b * nq + i, 0)),
        out_shape=jax.ShapeDtypeStruct((M, W), BF16),
        input_output_aliases={3: 0} if aliased else {},
        compiler_params=_params("parallel", "parallel"),
        name="cross_attention",
    )(q, mem_k, mem_v, *prev_args)


CONV_ROW_CHUNK = 128
SUBLANES = 8


def _conv_kernel(a_ref, prev_ref, state_ref, w_ref, bias_ref, g_ref, *rest, tt):
    o_ref, buf_ref, q_ref, y_ref = rest[-4], rest[-3], rest[-2], rest[-1]
    D = a_ref.shape[-1]

    @pl.when(pl.program_id(1) == 0)
    def _():
        buf_ref[0:CONV_HALO, :] = state_ref[...]

    @pl.when(pl.program_id(1) > 0)
    def _():
        buf_ref[0:CONV_HALO, :] = prev_ref[...]

    buf_ref[CONV_HALO:CONV_HALO + tt, :] = a_ref[...]
    buf_ref[CONV_HALO + tt:CONV_HALO + tt + SUBLANES, :] = jnp.zeros((SUBLANES, D), F32)

    first = CONV_HALO - (CONV_WIDTH - 1)
    rc = min(CONV_ROW_CHUNK, tt)
    for c0 in range(0, D, HEAD_DIM):
        cs = slice(c0, c0 + HEAD_DIM)
        for t0 in range(0, tt, rc):
            out = jnp.zeros((rc, HEAD_DIM), F32)
            for r in range(SUBLANES):
                q = None
                for k in range(CONV_WIDTH):
                    if (first + k) % SUBLANES != r:
                        continue
                    base = t0 + first + k - r
                    term = buf_ref[base:base + rc + SUBLANES, cs] * w_ref[k:k + 1, cs]
                    q = term if q is None else q + term
                if r == 0:
                    out = out + q[0:rc]
                else:
                    q_ref[...] = q
                    out = out + q_ref[r:r + rc, :]
            y_ref[t0:t0 + rc, cs] = out + bias_ref[:, cs]
    y = _rms_rows(y_ref[...], g_ref[...])
    o_ref[...] = (y * (1.0 / (1.0 + jnp.exp(-y)))).astype(o_ref.dtype)


def _conv_norm_swish(grp, u, state, w_dw, b_dw, g_norm, o_prev):
    M, D = u.shape
    tt = _pick_tile(grp.seq, 256)
    nt = grp.seq // tt
    assert grp.seq % tt == 0 and tt % CONV_HALO == 0 and grp.row0 % tt == 0
    blk0 = grp.row0 // tt
    hb = tt // CONV_HALO
    prev_args, prev_specs, aliased = _chained(o_prev)
    return pl.pallas_call(
        functools.partial(_conv_kernel, tt=tt),
        grid=(grp.batch, nt),
        in_specs=[
            pl.BlockSpec((tt, D), lambda b, i: (blk0 + b * nt + i, 0)),
            pl.BlockSpec((CONV_HALO, D), lambda b, i: (jnp.maximum((blk0 + b * nt + i) * hb - 1, 0), 0)),
            pl.BlockSpec((None, CONV_HALO, D), lambda b, i: (b, 0, 0)),
            pl.BlockSpec((CONV_WIDTH, D), lambda b, i: (0, 0)),
            pl.BlockSpec((1, D), lambda b, i: (0, 0)),
            pl.BlockSpec((1, D), lambda b, i: (0, 0)),
        ] + prev_specs,
        out_specs=pl.BlockSpec((tt, D), lambda b, i: (blk0 + b * nt + i, 0)),
        out_shape=jax.ShapeDtypeStruct((M, D), BF16),
        input_output_aliases={6: 0} if aliased else {},
        scratch_shapes=[pltpu.VMEM((tt + CONV_HALO + SUBLANES, D), F32),
                        pltpu.VMEM((min(CONV_ROW_CHUNK, tt) + SUBLANES, HEAD_DIM), F32),
                        pltpu.VMEM((tt, D), F32)],
        compiler_params=_params("parallel", "arbitrary"),
        name="conv_norm_swish",
    )(u, u, state, w_dw, b_dw.reshape(1, D), g_norm.reshape(1, D), *prev_args)


def kernel(x_prompt, x_sample, mem_prompt, cache_sb_k, cache_sb_v, cache_dsa_k, cache_dsa_v, cache_idx_k, cache_mem_k, cache_mem_v, state_conv, norm_mix, norm_cross, norm_mlp, norm_final, w_in_ab, w_out_ab, rel_bias, w_pw1, b_pw1, w_dw, b_dw, g_conv_norm, w_pw2, b_pw2, w_cq, w_mk, w_mv, w_co, w_up, w_down):
    Bp, Tp, D = x_prompt.shape
    Bs, Ts, _ = x_sample.shape
    past = cache_sb_k.shape[2]
    depth = norm_mix.shape[0]
    n_mem = mem_prompt.shape[1]
    n_sb, n_dsa = cache_sb_k.shape[3], cache_dsa_k.shape[3]
    sbw, dsw = n_sb * HEAD_DIM, n_dsa * HEAD_DIM
    qiw = N_IDX_HEADS * IDX_DIM
    cw = w_cq.shape[2]
    Mp, Ms = Bp * Tp, Bs * Ts
    M = Mp + Ms
    prompt = _Group(Bp, Tp, 0, 0)
    sample = _Group(Bs, Ts, Mp, past)

    x = jnp.concatenate([x_prompt.reshape(Mp, D), x_sample.reshape(Ms, D)], axis=0)
    mem_flat = mem_prompt.reshape(Bp * n_mem, D)

    src = dict(zip(("q_sb", "k_sb", "v_sb", "q_d", "k_d", "v_d", "q_i", "k_i", "w_i"),
                   np.cumsum([0, sbw, sbw, sbw, dsw, dsw, dsw, qiw, IDX_DIM])))
    order = ("k_sb", "v_sb", "k_d", "v_d", "q_sb", "q_d", "q_i")
    width = dict(q_sb=sbw, k_sb=sbw, v_sb=sbw, q_d=dsw, k_d=dsw, v_d=dsw, q_i=qiw)
    col = dict(zip(order, np.cumsum([0] + [width[n] for n in order])[:-1]))
    col = {k: int(v) for k, v in col.items()}
    n32 = 2 * sbw + 2 * dsw

    outs = {k: [] for k in ("p_sb_k", "p_sb_v", "p_d_k", "p_d_v", "p_ki", "p_mem_k", "p_mem_v", "p_conv",
                            "s_sb_k", "s_sb_v", "s_d_k", "s_d_v", "s_ki", "s_conv")}

    def split_rows(a):
        return a[:Mp].reshape(Bp, Tp, -1), a[Mp:].reshape(Bs, Ts, -1)

    def last_rows(a, grp, state, n):
        if grp.seq >= n:
            return jnp.stack([a[grp.row0 + (b + 1) * grp.seq - n:grp.row0 + (b + 1) * grp.seq]
                              for b in range(grp.batch)])
        rows = a[grp.row0:grp.row0 + grp.batch * grp.seq].reshape(grp.batch, grp.seq, -1)
        return jnp.concatenate([state, rows], axis=1)[:, -n:]

    w_out16, w_pw1_16, w_pw2_16 = w_out_ab.astype(BF16), w_pw1.astype(BF16), w_pw2.astype(BF16)
    w_cq16, w_mk16, w_mv16, w_co16 = (w.astype(BF16) for w in (w_cq, w_mk, w_mv, w_co))
    w_up16, w_down16 = w_up.astype(BF16), w_down.astype(BF16)
    q_scale = HEAD_DIM ** -0.5 * LOG2E
    col_scale = jnp.concatenate([jnp.full((width[n],), q_scale if n in ("q_sb", "q_d") else 1.0, F32)
                                 for n in order])

    for l in range(depth):
        i = l // 2
        if l % 2 == 0:
            w_in = w_in_ab[i]
            w_main = jnp.concatenate([w_in[:, src[n]:src[n] + width[n]] for n in order], axis=1).astype(BF16)
            tail_w = w_in.shape[1] - int(src["k_i"])
            w_tail = jnp.pad(w_in[:, src["k_i"]:], ((0, 0), (0, HEAD_DIM - tail_w))).astype(BF16)
            proj, kv32 = _in_proj(x, w_main, norm_mix[l], col_scale, n32=n32)
            tail = _matmul(x, w_tail[None], gain=norm_mix[l])

            for name, key, nh in (("sb_k", "k_sb", n_sb), ("sb_v", "v_sb", n_sb),
                                  ("d_k", "k_d", n_dsa), ("d_v", "v_d", n_dsa)):
                ap, as_ = split_rows(kv32[:, col[key]:col[key] + width[key]])
                outs["p_" + name].append(ap.reshape(Bp, Tp, nh, HEAD_DIM))
                outs["s_" + name].append(as_.reshape(Bs, Ts, nh, HEAD_DIM))
            k_i = tail[:, :IDX_DIM]
            kip, kis = split_rows(k_i)
            outs["p_ki"].append(kip)
            outs["s_ki"].append(kis)

            tk_s = ATTN_TILE
            s_pad_s = -(-(past + Ts) // tk_s) * tk_s
            kidx_p = kip.astype(BF16)
            kidx_s = jnp.concatenate([cache_idx_k[i].astype(BF16), kis.astype(BF16),
                                      jnp.zeros((Bs, s_pad_s - past - Ts, IDX_DIM), BF16)], axis=1)

            o_shape = (M, sbw + dsw)
            sb_cols = (col["q_sb"], col["k_sb"], col["v_sb"])
            d_cols = (col["q_i"], col["q_d"], col["k_d"], col["v_d"])
            o = _sb_attention(prompt, proj, sb_cols, cache_sb_k[i], cache_sb_v[i], None, o_shape, 0,
                              hp=SB_HEADS_PER_STEP)
            o = _dsa_attention(prompt, proj, tail, d_cols, kidx_p, cache_dsa_k[i], cache_dsa_v[i], rel_bias,
                               o, o_shape, sbw, hp=DSA_HEADS_PER_STEP)
            o = _sb_attention(sample, proj, sb_cols, cache_sb_k[i], cache_sb_v[i], o, o_shape, 0,
                              hp=SB_HEADS_PER_STEP)
            o = _dsa_attention(sample, proj, tail, d_cols, kidx_s, cache_dsa_k[i], cache_dsa_v[i], rel_bias,
                               o, o_shape, sbw, hp=DSA_HEADS_PER_STEP)
            x, qc = _residual_matmul_then_proj(o, w_out16, i, x, None, norm_cross[l], w_cq16, l)
        else:
            u = _matmul(x, w_pw1_16, layer=i, gain=norm_mix[l], bias=b_pw1[i], glu=True)
            front = CONV_HALO - (CONV_WIDTH - 1)
            state_p = jnp.zeros((Bp, CONV_HALO, D), F32)
            state_s = jnp.concatenate([jnp.zeros((Bs, front, D), F32), state_conv[i]], axis=1)
            y = _conv_norm_swish(prompt, u, state_p, w_dw[i], b_dw[i], g_conv_norm[i], None)
            y = _conv_norm_swish(sample, u, state_s, w_dw[i], b_dw[i], g_conv_norm[i], y)
            outs["p_conv"].append(last_rows(u, prompt, state_p, CONV_WIDTH - 1))
            outs["s_conv"].append(last_rows(u, sample, state_s, CONV_WIDTH - 1))
            x, qc = _residual_matmul_then_proj(y, w_pw2_16, i, x, b_pw2[i], norm_cross[l], w_cq16, l)

        mk = _matmul(mem_flat, w_mk16, layer=l)
        mv = _matmul(mem_flat, w_mv16, layer=l)
        outs["p_mem_k"].append(mk.reshape(Bp, n_mem, cw // HEAD_DIM, HEAD_DIM))
        outs["p_mem_v"].append(mv.reshape(Bp, n_mem, cw // HEAD_DIM, HEAD_DIM))
        oc = _cross_attention(prompt, qc, mk.astype(BF16).reshape(Bp, n_mem, cw),
                              mv.astype(BF16).reshape(Bp, n_mem, cw), None)
        oc = _cross_attention(sample, qc, cache_mem_k[l].astype(BF16).reshape(Bs, n_mem, cw),
                              cache_mem_v[l].astype(BF16).reshape(Bs, n_mem, cw), oc)
        if l < depth - 1:
            x = _cross_out_mlp(x, oc, w_co16, norm_mlp[l], w_up16, w_down16, l)
        else:
            y_p, y_s = _cross_out_mlp(x, oc, w_co16, norm_mlp[l], w_up16, w_down16, l,
                                      final_gain=norm_final, m_prompt=Mp)

    st = lambda k: jnp.stack(outs[k])
    return (y_p.reshape(Bp, Tp, D), y_s.reshape(Bs, Ts, D), st("p_sb_k"), st("p_sb_v"), st("p_d_k"), st("p_d_v"),
            st("p_ki"), st("p_mem_k"), st("p_mem_v"), st("p_conv"), st("s_sb_k"), st("s_sb_v"), st("s_d_k"),
            st("s_d_v"), st("s_ki"), st("s_conv"))
```

```python
import functools
import math
from typing import NamedTuple

import numpy as np
import jax
import jax.numpy as jnp
from jax import lax
from jax.experimental import pallas as pl
from jax.experimental.pallas import tpu as pltpu

F32 = jnp.float32
BF16 = jnp.bfloat16

HEAD_DIM = 128
IDX_DIM = 64
N_IDX_HEADS = 16
CHUNK = 64
TOPK_MAX = 256
N_BUCKETS = 32
MAX_DISTANCE = 128
CONV_WIDTH = 31
CONV_HALO = 32
RMS_EPS = 1e-6
MASKED_LOGIT = -1e30
LOG2E = math.log2(math.e)
_INT32_MIN = -2 ** 31
VMEM_LIMIT_BYTES = 56 * 1024 * 1024
SB_HEADS_PER_STEP = 8
DSA_HEADS_PER_STEP = 4
SB_INTERLEAVE = 4
ATTN_TILE = 256


class _Group(NamedTuple):
    batch: int
    seq: int
    row0: int
    past: int


def _params(*semantics):
    return pltpu.CompilerParams(dimension_semantics=semantics, vmem_limit_bytes=VMEM_LIMIT_BYTES)


def _pick_tile(n, pref):
    if n <= pref:
        return n
    t = pref
    while n % t:
        t //= 2
    return t


def _row_tile(m, pref):
    t = min(m, pref) // 8 * 8
    while m % t:
        t -= 8
    return t


def _chained(prev):
    if prev is None:
        return [], [], None
    return [prev], [pl.BlockSpec(memory_space=pl.ANY)], True


def _rms_rows(x, g):
    ms = jnp.mean(x * x, axis=-1, keepdims=True)
    return x * lax.rsqrt(ms + RMS_EPS) * g


def _mm_kernel(*refs, has_norm, has_bias, has_res, glu):
    it = iter(refs)
    x_ref = next(it)
    g_ref = next(it) if has_norm else None
    w_ref = next(it)
    w2_ref = next(it) if glu else None
    b_ref = next(it) if has_bias else None
    b2_ref = next(it) if (has_bias and glu) else None
    r_ref = next(it) if has_res else None
    o_ref = next(it)
    xn_ref = next(it) if has_norm else None

    if has_norm:
        @pl.when(pl.program_id(1) == 0)
        def _():
            xn_ref[...] = _rms_rows(x_ref[...], g_ref[...]).astype(BF16)
        xb = xn_ref[...]
    else:
        xb = x_ref[...].astype(BF16)

    y = jnp.dot(xb, w_ref[...], preferred_element_type=F32)
    if has_bias:
        y = y + b_ref[...]
    if glu:
        gate = jnp.dot(xb, w2_ref[...], preferred_element_type=F32)
        if has_bias:
            gate = gate + b2_ref[...]
        y = y * (1.0 / (1.0 + jnp.exp(-gate)))
    if has_res:
        y = y + r_ref[...]
    o_ref[...] = y.astype(o_ref.dtype)


def _matmul(x, w, *, layer=0, gain=None, bias=None, residual=None, glu=False, out_dtype=F32, tm=1152, tn=512):
    M, K = x.shape
    n_out = w.shape[2] // 2 if glu else w.shape[2]
    tm = _row_tile(M, tm)
    tn = _pick_tile(n_out, tn)
    nj = n_out // tn
    has_norm, has_bias, has_res = gain is not None, bias is not None, residual is not None

    args = [x]
    specs = [pl.BlockSpec((tm, K), lambda i, j: (i, 0))]
    if has_norm:
        args.append(gain.reshape(1, K))
        specs.append(pl.BlockSpec((1, K), lambda i, j: (0, 0)))
    args.append(w)
    specs.append(pl.BlockSpec((None, K, tn), lambda i, j: (layer, 0, j)))
    if glu:
        args.append(w)
        specs.append(pl.BlockSpec((None, K, tn), lambda i, j: (layer, 0, j + nj)))
    if has_bias:
        b2d = bias.reshape(1, -1)
        args.append(b2d)
        specs.append(pl.BlockSpec((1, tn), lambda i, j: (0, j)))
        if glu:
            args.append(b2d)
            specs.append(pl.BlockSpec((1, tn), lambda i, j: (0, j + nj)))
    if has_res:
        args.append(residual)
        specs.append(pl.BlockSpec((tm, tn), lambda i, j: (i, j)))

    return pl.pallas_call(
        functools.partial(_mm_kernel, has_norm=has_norm, has_bias=has_bias, has_res=has_res, glu=glu),
        grid=(M // tm, nj),
        in_specs=specs,
        out_specs=pl.BlockSpec((tm, tn), lambda i, j: (i, j)),
        out_shape=jax.ShapeDtypeStruct((M, n_out), out_dtype),
        scratch_shapes=[pltpu.VMEM((tm, K), BF16)] if has_norm else [],
        compiler_params=_params("parallel", "arbitrary"),
        name="norm_matmul",
    )(*args)


def _res_proj_kernel(y_ref, w_ref, *refs, has_bias):
    it = iter(refs)
    b_ref = next(it) if has_bias else None
    r_ref, g_ref, wq_ref, o_ref, q_ref, x1_ref = (next(it) for _ in range(6))
    j = pl.program_id(1)
    y = jnp.dot(y_ref[...], w_ref[...], preferred_element_type=F32) + r_ref[...]
    if has_bias:
        y = y + b_ref[...]
    o_ref[...] = y
    x1_ref[j] = y

    @pl.when(j == pl.num_programs(1) - 1)
    def _():
        x1 = jnp.concatenate([x1_ref[t] for t in range(x1_ref.shape[0])], axis=1)
        q_ref[...] = jnp.dot(_rms_rows(x1, g_ref[...]).astype(BF16), wq_ref[...],
                             preferred_element_type=F32).astype(q_ref.dtype)


def _residual_matmul_then_proj(y, w, layer, residual, bias, gain, wq, q_layer, *, tm=1152, tn=512):
    M, K = y.shape
    N, nq = w.shape[2], wq.shape[2]
    tm = _row_tile(M, tm)
    assert N % tn == 0
    nj = N // tn
    has_bias = bias is not None
    args = [y, w]
    specs = [pl.BlockSpec((tm, K), lambda i, j: (i, 0)), pl.BlockSpec((None, K, tn), lambda i, j: (layer, 0, j))]
    if has_bias:
        args.append(bias.reshape(1, N))
        specs.append(pl.BlockSpec((1, tn), lambda i, j: (0, j)))
    args += [residual, gain.reshape(1, N), wq]
    specs += [pl.BlockSpec((tm, tn), lambda i, j: (i, j)), pl.BlockSpec((1, N), lambda i, j: (0, 0)),
              pl.BlockSpec((None, N, nq), lambda i, j: (q_layer, 0, 0))]
    return pl.pallas_call(
        functools.partial(_res_proj_kernel, has_bias=has_bias),
        grid=(M // tm, nj),
        in_specs=specs,
        out_specs=[pl.BlockSpec((tm, tn), lambda i, j: (i, j)), pl.BlockSpec((tm, nq), lambda i, j: (i, 0))],
        out_shape=[jax.ShapeDtypeStruct((M, N), F32), jax.ShapeDtypeStruct((M, nq), BF16)],
        scratch_shapes=[pltpu.VMEM((nj, tm, tn), F32)],
        compiler_params=_params("parallel", "arbitrary"),
        name="residual_matmul_proj",
    )(*args)


def _in_proj_kernel(x_ref, g_ref, w_ref, s_ref, wt_ref, o16_ref, o32_ref, ot_ref, xn_ref, *, n32_tiles):
    j = pl.program_id(1)

    @pl.when(j == 0)
    def _():
        xn_ref[...] = _rms_rows(x_ref[...], g_ref[...]).astype(BF16)
        ot_ref[...] = jnp.dot(xn_ref[...], wt_ref[...], preferred_element_type=F32)

    y = jnp.dot(xn_ref[...], w_ref[...], preferred_element_type=F32)
    o16_ref[...] = (y * s_ref[...]).astype(BF16)

    @pl.when(j < n32_tiles)
    def _():
        o32_ref[...] = y


def _in_proj(x, w, gain, col_scale, w_tail, *, n32, tm=1152, tn=512):
    M, K = x.shape
    N = w.shape[1]
    nt = w_tail.shape[1]
    tm = _row_tile(M, tm)
    assert N % tn == 0 and n32 % tn == 0
    n32_tiles = n32 // tn
    return pl.pallas_call(
        functools.partial(_in_proj_kernel, n32_tiles=n32_tiles),
        grid=(M // tm, N // tn),
        in_specs=[
            pl.BlockSpec((tm, K), lambda i, j: (i, 0)),
            pl.BlockSpec((1, K), lambda i, j: (0, 0)),
            pl.BlockSpec((K, tn), lambda i, j: (0, j)),
            pl.BlockSpec((1, tn), lambda i, j: (0, j)),
            pl.BlockSpec((K, nt), lambda i, j: (0, 0)),
        ],
        out_specs=[
            pl.BlockSpec((tm, tn), lambda i, j: (i, j)),
            pl.BlockSpec((tm, tn), lambda i, j: (i, jnp.minimum(j, n32_tiles - 1))),
            pl.BlockSpec((tm, nt), lambda i, j: (i, 0)),
        ],
        out_shape=[jax.ShapeDtypeStruct((M, N), BF16), jax.ShapeDtypeStruct((M, n32), F32),
                   jax.ShapeDtypeStruct((M, nt), F32)],
        scratch_shapes=[pltpu.VMEM((tm, K), BF16)],
        compiler_params=_params("parallel", "arbitrary"),
        name="in_proj",
    )(x, gain.reshape(1, K), w, col_scale.reshape(1, N), w_tail)


def _mlp_kernel(x_ref, oc_ref, wco_ref, g_ref, wu_ref, wd_ref, *refs, n_prompt_tiles):
    final = n_prompt_tiles is not None
    if final:
        gf_ref, op_ref, os_ref, x2_ref, xn_ref, acc_ref = refs
    else:
        o_ref, x2_ref, xn_ref, acc_ref = refs
    i, f = pl.program_id(0), pl.program_id(1)

    @pl.when(f == 0)
    def _():
        x2 = x_ref[...] + jnp.dot(oc_ref[...], wco_ref[...], preferred_element_type=F32)
        x2_ref[...] = x2
        xn_ref[...] = _rms_rows(x2, g_ref[...]).astype(BF16)
        acc_ref[...] = jnp.zeros_like(acc_ref)

    h = jnp.dot(xn_ref[...], wu_ref[...], preferred_element_type=F32)
    h = jnp.maximum(h, 0.0)
    h = (h * h).astype(BF16)
    acc_ref[...] += jnp.dot(h, wd_ref[...], preferred_element_type=F32)

    last = f == pl.num_programs(1) - 1
    if not final:
        @pl.when(last)
        def _():
            o_ref[...] = x2_ref[...] + acc_ref[...]
    else:
        @pl.when(last & (i < n_prompt_tiles))
        def _():
            op_ref[...] = _rms_rows(x2_ref[...] + acc_ref[...], gf_ref[...])

        @pl.when(last & (i >= n_prompt_tiles))
        def _():
            os_ref[...] = _rms_rows(x2_ref[...] + acc_ref[...], gf_ref[...])


def _cross_out_mlp(x, oc, w_co, gain, w_up, w_down, layer, *, final_gain=None, m_prompt=None, tm=512, tf=512):
    M, D = x.shape
    F = w_up.shape[2]
    cw = oc.shape[1]
    final = final_gain is not None
    tm = _pick_tile(M - m_prompt, tm) if final else _pick_tile(M, tm)
    tf = _pick_tile(F, tf)
    args = [x, oc, w_co, gain.reshape(1, D), w_up, w_down]
    specs = [
        pl.BlockSpec((tm, D), lambda i, f: (i, 0)),
        pl.BlockSpec((tm, cw), lambda i, f: (i, 0)),
        pl.BlockSpec((None, cw, D), lambda i, f: (layer, 0, 0)),
        pl.BlockSpec((1, D), lambda i, f: (0, 0)),
        pl.BlockSpec((None, D, tf), lambda i, f: (layer, 0, f)),
        pl.BlockSpec((None, tf, D), lambda i, f: (layer, f, 0)),
    ]
    if final:
        assert m_prompt % tm == 0 and (M - m_prompt) % tm == 0
        npt = m_prompt // tm
        args.append(final_gain.reshape(1, D))
        specs.append(pl.BlockSpec((1, D), lambda i, f: (0, 0)))
        out_specs = [pl.BlockSpec((tm, D), lambda i, f: (jnp.minimum(i, npt - 1), 0)),
                     pl.BlockSpec((tm, D), lambda i, f: (jnp.maximum(i - npt, 0), 0))]
        out_shape = [jax.ShapeDtypeStruct((m_prompt, D), F32), jax.ShapeDtypeStruct((M - m_prompt, D), F32)]
        semantics = ("arbitrary", "arbitrary")
    else:
        npt = None
        out_specs = pl.BlockSpec((tm, D), lambda i, f: (i, 0))
        out_shape = jax.ShapeDtypeStruct((M, D), F32)
        semantics = ("parallel", "arbitrary")
    return pl.pallas_call(
        functools.partial(_mlp_kernel, n_prompt_tiles=npt),
        grid=(M // tm, F // tf),
        in_specs=specs,
        out_specs=out_specs,
        out_shape=out_shape,
        scratch_shapes=[pltpu.VMEM((tm, D), F32), pltpu.VMEM((tm, D), BF16), pltpu.VMEM((tm, D), F32)],
        compiler_params=_params(*semantics),
        name="cross_out_mlp",
    )(*args)


def _nt_dot(a, b):
    return lax.dot_general(a, b, (((1,), (1,)), ((), ())), preferred_element_type=F32)


def _kv_operands(grp, proj, k_col, v_col, k_cache, v_cache, *, tk, hp):
    wp = hp * HEAD_DIM
    if grp.past == 0:
        blk0 = grp.row0 // grp.seq
        args = [proj, proj]
        specs = [pl.BlockSpec((grp.seq, wp), lambda b, h, i, c=k_col // wp: (blk0 + b, c + h),
                              pipeline_mode=pl.Buffered(1)),
                 pl.BlockSpec((grp.seq, wp), lambda b, h, i, c=v_col // wp: (blk0 + b, c + h),
                              pipeline_mode=pl.Buffered(1))]
        return args, specs

    def new_tile(col):
        rows = lax.slice(proj, (grp.row0, col), (grp.row0 + grp.batch * grp.seq, col + k_cache.shape[2] * HEAD_DIM))
        rows = rows.reshape(grp.batch, grp.seq, -1)
        return jnp.pad(rows, ((0, 0), (0, tk - grp.seq), (0, 0)))

    args = [new_tile(k_col), new_tile(v_col)]
    specs = [pl.BlockSpec((None, tk, wp), lambda b, h, i: (b, 0, h))] * 2
    n_heads = k_cache.shape[2]
    for cache in (k_cache, v_cache):
        args.append(cache.reshape(grp.batch, grp.past * n_heads, HEAD_DIM))
        specs.append(pl.BlockSpec((None, grp.past * n_heads, HEAD_DIM), lambda b, h, i: (b, 0, 0)))
    return args, specs


def _split_kv_refs(refs, has_past):
    if not has_past:
        return refs[0], refs[1], None, None, refs[2:]
    return refs[0], refs[1], refs[2], refs[3], refs[4:]


def _tile_reader(new_ref, past_ref, *, tk, hp, n_heads):
    def read(j, hh, diag, n=1):
        hs = slice(hh * HEAD_DIM, (hh + 1) * HEAD_DIM)
        if past_ref is None:
            return new_ref[pl.ds(pl.multiple_of(j * tk, tk), n * tk), hs]
        if diag:
            return new_ref[:, hs]
        head = pl.program_id(1) * hp + hh
        return past_ref[pl.ds(j * (tk * n_heads) + head, n * tk, stride=n_heads), :].astype(BF16)
    return read


def _sb_kernel(q_ref, *refs, tq, tk, hp, n_heads, past, has_past):
    k_new, v_new, k_past, v_past, rest = _split_kv_refs(refs, has_past)
    u_ref, o_ref, acc_ref, carry_ref = rest[0], rest[-3], rest[-2], rest[-1]
    read_k = _tile_reader(k_new, k_past, tk=tk, hp=hp, n_heads=n_heads)
    read_v = _tile_reader(v_new, v_past, tk=tk, hp=hp, n_heads=n_heads)

    i = pl.program_id(2)
    q_start = past + i * tq
    jd = q_start // tk
    reps = tk // HEAD_DIM

    acc_ref[...] = jnp.zeros_like(acc_ref)
    carry_ref[...] = jnp.zeros_like(carry_ref)

    def step(j, masked):
        if masked:
            t_pos = q_start + lax.broadcasted_iota(jnp.int32, (tq, tk), 0)
            s_pos = j * tk + lax.broadcasted_iota(jnp.int32, (tq, tk), 1)
            causal = t_pos > s_pos
        for h0 in range(0, hp, SB_INTERLEAVE):
            group_step(j, masked, causal if masked else None, range(h0, h0 + SB_INTERLEAVE))

    def group_step(j, masked, causal, group):
        heads = [(hh, slice(hh * HEAD_DIM, (hh + 1) * HEAD_DIM)) for hh in group]
        z2s = [_nt_dot(q_ref[:, hs], read_k(j, hh, masked)) for hh, hs in heads]
        ns, log2_betas, css = [], [], []
        for z2 in z2s:
            neg_abs = lax.bitcast_convert_type(
                lax.bitcast_convert_type(z2, jnp.int32) | jnp.int32(_INT32_MIN), F32)
            n = jnp.maximum(z2, 0.0) + jnp.log(1.0 + jnp.exp2(neg_abs)) * LOG2E
            log2_betas.append(z2 - n)
            if masked:
                n = jnp.where(causal, n, 0.0)
            n16 = n.astype(BF16)
            css.append(jnp.dot(n16, u_ref[...], preferred_element_type=F32))
            ns.append(n16)
        for (hh, hs), n16, log2_beta, cs in zip(heads, ns, log2_betas, css):
            a = jnp.exp2(log2_beta - cs)
            if masked:
                a = jnp.where(causal, a, 0.0)
            carry = carry_ref[:, hs]
            pv = jnp.dot(a.astype(BF16), read_v(j, hh, masked), preferred_element_type=F32)
            acc_ref[:, hs] += jnp.exp2(-carry) * pv
            carry_ref[:, hs] = carry + jnp.broadcast_to(cs[:, 0:1] + n16[:, 0:1].astype(F32), (tq, HEAD_DIM))

    step(jd, True)

    def body(t, c):
        step(jd - 1 - t, False)
        return c

    lax.fori_loop(0, jd, body, 0)
    o_ref[...] = acc_ref[...].astype(o_ref.dtype)


def _suffix_sum_matrix(tk):
    r = np.arange(tk)
    return jnp.asarray((r[:, None] > r[None, :]).astype(np.float32), dtype=BF16)


def _attn_tiles(grp):
    tq = _pick_tile(grp.seq, ATTN_TILE)
    tk = ATTN_TILE if grp.past else tq
    nq = grp.seq // tq
    assert grp.seq % tq == 0 and tk % tq == 0 and grp.past % tk == 0 and grp.row0 % tq == 0
    assert grp.past == 0 or nq == 1
    assert grp.past > 0 or grp.row0 % grp.seq == 0
    return tq, tk, nq


def _sb_attention(grp, proj, cols, k_cache, v_cache, o_prev, o_shape, o_col, *, hp):
    H = k_cache.shape[2]
    tq, tk, nq = _attn_tiles(grp)
    wp = hp * HEAD_DIM
    q_blk0 = grp.row0 // tq
    kv_args, kv_specs = _kv_operands(grp, proj, cols[1], cols[2], k_cache, v_cache, tk=tk, hp=hp)
    prev_args, prev_specs, aliased = _chained(o_prev)
    n_in = 1 + len(kv_args) + 1
    return pl.pallas_call(
        functools.partial(_sb_kernel, tq=tq, tk=tk, hp=hp, n_heads=H, past=grp.past, has_past=grp.past > 0),
        grid=(grp.batch, H // hp, nq),
        in_specs=[pl.BlockSpec((tq, wp), lambda b, h, i: (q_blk0 + b * nq + i, cols[0] // wp + h))]
        + kv_specs
        + [pl.BlockSpec((tk, tk), lambda b, h, i: (0, 0), pipeline_mode=pl.Buffered(1))]
        + prev_specs,
        out_specs=pl.BlockSpec((tq, wp), lambda b, h, i: (q_blk0 + b * nq + i, o_col // wp + h)),
        out_shape=jax.ShapeDtypeStruct(o_shape, BF16),
        input_output_aliases={n_in: 0} if aliased else {},
        scratch_shapes=[pltpu.VMEM((tq, wp), F32), pltpu.VMEM((tq, wp), F32)],
        compiler_params=_params("parallel", "parallel", "arbitrary"),
        name="sb_attention",
    )(proj, *kv_args, _suffix_sum_matrix(tk), *prev_args)


def _sortable_key(x):
    b = lax.bitcast_convert_type(x, jnp.int32)
    return b ^ ((b >> 31) & jnp.int32(0x7FFFFFFF))


_KEY_NEG_INF = int(np.int32(np.array(-np.inf, np.float32).view(np.int32) ^ 0x7FFFFFFF))
_KEY_LOWEST_FINITE = int(np.int32(np.array(np.finfo(np.float32).min, np.float32).view(np.int32) ^ 0x7FFFFFFF))
COUNT_ROW_GROUP = 128


def _dsa_select_kernel(qi_ref, wi_ref, ki2_ref, mask_ref, keys_ref, wb_ref, *, tq, tk, nkt, past, topk):
    i = pl.program_id(1)
    q_start = past + i * tq
    jd = q_start // tk
    reps = tk // HEAD_DIM

    w = wi_ref[:, IDX_DIM:IDX_DIM + N_IDX_HEADS] * (IDX_DIM ** -0.5 * N_IDX_HEADS ** -0.5)
    for hh in range(N_IDX_HEADS):
        wb_ref[hh] = jnp.broadcast_to(w[:, hh:hh + 1], (tq, HEAD_DIM))

    def idx_tile(j):
        kt = ki2_ref[j]
        tot = jnp.zeros((tq, tk), F32)
        for p in range(N_IDX_HEADS // 2):
            sc = jnp.dot(qi_ref[:, p * 128:(p + 1) * 128], kt, preferred_element_type=F32)
            w0 = jnp.concatenate([wb_ref[2 * p]] * reps, axis=1)
            w1 = jnp.concatenate([wb_ref[2 * p + 1]] * reps, axis=1)
            tot = tot + w0 * jnp.maximum(sc[:, :tk], 0.0) + w1 * jnp.maximum(sc[:, tk:], 0.0)
        return _sortable_key(tot)

    def idx_body(j, c):
        keys_ref[j] = idx_tile(j)
        return c

    lax.fori_loop(0, jd, idx_body, 0)
    t_pos = q_start + lax.broadcasted_iota(jnp.int32, (tq, tk), 0)
    s_pos = jd * tk + lax.broadcasted_iota(jnp.int32, (tq, tk), 1)
    visible = (t_pos // CHUNK) >= (s_pos // CHUNK)
    keys_ref[jd] = jnp.where(visible, idx_tile(jd), jnp.int32(_KEY_NEG_INF))

    n_vis = jd + 1

    @pl.when(n_vis % 2 == 1)
    def _():
        keys_ref[n_vis] = jnp.full((tq, tk), _INT32_MIN, jnp.int32)

    n_pairs = (n_vis + 1) // 2
    rg = min(COUNT_ROW_GROUP, tq)

    def count_ge(cand):
        parts = []
        for g in range(tq // rg):
            cg = cand[g * rg:(g + 1) * rg]

            def body(t, cnt, g=g, cg=cg):
                for jj in (2 * t, 2 * t + 1):
                    kj = keys_ref[jj, pl.ds(g * rg, rg), :]
                    for r in range(reps):
                        cnt = cnt + jnp.where(kj[:, r * 128:(r + 1) * 128] >= cg, 1.0, 0.0)
                return cnt

            parts.append(lax.fori_loop(0, n_pairs, body, jnp.zeros((rg, HEAD_DIM), F32)))
        return jnp.sum(jnp.concatenate(parts, axis=0), axis=-1, keepdims=True)

    kf = float(topk)
    zero = jnp.zeros((tq, HEAD_DIM), jnp.int32)
    c = count_ge(zero)
    res = jnp.where(c >= kf, zero, jnp.int32(_INT32_MIN))
    cnt = jnp.where(c >= kf, c, float(tk) * (2 * n_pairs).astype(F32))
    for bit in range(30, -1, -1):
        cand = res + jnp.int32(1 << bit)
        c = count_ge(cand)
        res = jnp.where(c >= kf, cand, res)
        cnt = jnp.where(c >= kf, c, cnt)

    tied_cut = jnp.max(jnp.where((cnt != kf) & (res[:, 0:1] > jnp.int32(_KEY_NEG_INF)), 1.0, 0.0)) > 0.0

    @pl.when(tied_cut)
    def _():
        col = lax.broadcasted_iota(jnp.int32, (rg, tk), 1)

        def count_tied(before, strict_above):
            parts = []
            for g in range(tq // rg):
                rows = slice(g * rg, (g + 1) * rg)
                res_g = jnp.concatenate([res[rows]] * reps, axis=1)
                bef_g = None if strict_above else jnp.concatenate([before[rows]] * reps, axis=1)

                def body(j, acc, g=g, res_g=res_g, bef_g=bef_g):
                    kj = keys_ref[j, pl.ds(g * rg, rg), :]
                    if strict_above:
                        hit = jnp.where(kj > res_g, 1.0, 0.0)
                    else:
                        hit = jnp.where(kj == res_g, jnp.where(j * tk + col < bef_g, 1.0, 0.0), 0.0)
                    return acc + sum(hit[:, r * 128:(r + 1) * 128] for r in range(reps))

                parts.append(lax.fori_loop(0, n_vis, body, jnp.zeros((rg, HEAD_DIM), F32)))
            return jnp.sum(jnp.concatenate(parts, axis=0), axis=-1, keepdims=True)

        need = kf - count_tied(None, True)
        r_keep = jnp.zeros((tq, HEAD_DIM), jnp.int32)
        for bit in range((nkt * tk).bit_length() - 1, -1, -1):
            cand = r_keep | jnp.int32(1 << bit)
            r_keep = jnp.where(count_tied(cand, False) < need, cand, r_keep)

        def drop_late_ties(j, c):
            for g in range(tq // rg):
                rows = slice(g * rg, (g + 1) * rg)
                res_g = jnp.concatenate([res[rows]] * reps, axis=1)
                keep_g = jnp.concatenate([r_keep[rows]] * reps, axis=1)
                kj = keys_ref[j, pl.ds(g * rg, rg), :]
                keys_ref[j, pl.ds(g * rg, rg), :] = jnp.where(
                    kj == res_g, jnp.where(j * tk + col > keep_g, res_g - 1, kj), kj)
            return c

        lax.fori_loop(0, n_vis, drop_late_ties, 0)

    thr = jnp.maximum(res, jnp.int32(_KEY_LOWEST_FINITE))
    thr = jnp.concatenate([thr] * reps, axis=1)

    def write_mask(j, c):
        mask_ref[j] = jnp.where(keys_ref[j] >= thr, 0.0, MASKED_LOGIT).astype(mask_ref.dtype)
        return c

    lax.fori_loop(0, jd + 1, write_mask, 0)

    def write_hidden(j, c):
        mask_ref[j] = jnp.full((tq, tk), MASKED_LOGIT, mask_ref.dtype)
        return c

    lax.fori_loop(jd + 1, nkt, write_hidden, 0)


def _dsa_flash_kernel(q_ref, *refs, tq, tk, hp, n_heads, past, has_past):
    k_new, v_new, k_past, v_past, rest = _split_kv_refs(refs, has_past)
    mask_ref, bnear_ref = rest[0], rest[1]
    o_ref, m_ref, l_ref, acc_ref = rest[-4], rest[-3], rest[-2], rest[-1]
    read_k = _tile_reader(k_new, k_past, tk=tk, hp=hp, n_heads=n_heads)
    read_v = _tile_reader(v_new, v_past, tk=tk, hp=hp, n_heads=n_heads)

    i = pl.program_id(2)
    jd = (past + i * tq) // tk
    reps = tk // HEAD_DIM
    heads = [slice(hh * HEAD_DIM, (hh + 1) * HEAD_DIM) for hh in range(hp)]

    m_ref[...] = jnp.full_like(m_ref, MASKED_LOGIT)
    l_ref[...] = jnp.zeros_like(l_ref)
    acc_ref[...] = jnp.zeros_like(acc_ref)

    def step(j, near, n=1):
        diag = near == 0
        maskf = jnp.concatenate([mask_ref[j + t].astype(F32) for t in range(n)], axis=1)
        logits = []
        for hh, hs in enumerate(heads):
            s = _nt_dot(q_ref[:, hs], read_k(j, hh, diag, n)) + maskf
            if near is not None:
                s = s + bnear_ref[hh, near]
            logits.append(s)
        probs, alphas = [], []
        for hs, s in zip(heads, logits):
            m_old = m_ref[:, hs]
            m_new = jnp.maximum(m_old, jnp.max(s, axis=-1, keepdims=True))
            alpha = jnp.exp2(m_old - m_new)
            p = jnp.exp2(s - jnp.concatenate([m_new] * (n * reps), axis=1))
            l_ref[:, hs] = alpha * l_ref[:, hs] + sum(p[:, r * HEAD_DIM:(r + 1) * HEAD_DIM]
                                                      for r in range(n * reps))
            m_ref[:, hs] = m_new
            probs.append(p.astype(BF16))
            alphas.append(alpha)
        for hh, (hs, p, alpha) in enumerate(zip(heads, probs, alphas)):
            pv = jnp.dot(p, read_v(j, hh, diag, n), preferred_element_type=F32)
            acc_ref[:, hs] = alpha * acc_ref[:, hs] + pv

    step(jd, 0)

    @pl.when(jd >= 1)
    def _():
        step(jd - 1, 1)

    n_far = jnp.maximum(jd - 1, 0)

    @pl.when(n_far % 2 == 1)
    def _():
        step(n_far - 1, None)

    def body(t, c):
        step(2 * t, None, 2)
        return c

    lax.fori_loop(0, n_far // 2, body, 0)
    for hs in heads:
        o_ref[:, hs] = (acc_ref[:, hs] / jnp.sum(l_ref[:, hs], axis=-1, keepdims=True)).astype(o_ref.dtype)


def _t5_bucket(rel):
    nb = N_BUCKETS // 2
    max_exact = nb // 2
    side = jnp.where(rel > 0, nb, 0)
    n = jnp.abs(rel)
    nf = jnp.maximum(n, 1).astype(F32)
    large = max_exact + (jnp.log(nf / max_exact) / math.log(MAX_DISTANCE / max_exact) * (nb - max_exact)).astype(jnp.int32)
    large = jnp.minimum(large, nb - 1)
    return side + jnp.where(n < max_exact, n, large)


def _near_bias(rel_bias, *, tq, tk):
    r = jnp.arange(tq, dtype=jnp.int32)[:, None]
    c = jnp.arange(tk, dtype=jnp.int32)[None, :]
    bucket = _t5_bucket(jnp.stack([c - r, c - r - tk]))
    far_bucket = _t5_bucket(jnp.int32(-2 * tk))
    table = (rel_bias - rel_bias[far_bucket][None, :]) * LOG2E
    onehot = (bucket[..., None] == jnp.arange(N_BUCKETS, dtype=jnp.int32)).astype(F32)
    return jnp.einsum("dqkb,bh->hdqk", onehot, table, precision=lax.Precision.HIGHEST)


def _dsa_attention(grp, proj, tail, cols, k_idx_all, k_cache, v_cache, rel_bias, o_prev, o_shape, o_col, *, hp):
    H = k_cache.shape[2]
    tq, tk, nq = _attn_tiles(grp)
    s_pad = k_idx_all.shape[1]
    nkt = s_pad // tk
    assert tk % CHUNK == 0 and tq % CHUNK == 0 and s_pad % tk == 0
    assert tk >= MAX_DISTANCE and H % hp == 0
    topk = min(TOPK_MAX, (grp.past + grp.seq) // 4)
    q_blk0 = grp.row0 // tq
    qiw = N_IDX_HEADS * IDX_DIM

    kt = jnp.transpose(k_idx_all.reshape(grp.batch, nkt, tk, IDX_DIM), (0, 1, 3, 2))
    z = jnp.zeros_like(kt)
    ki2 = jnp.concatenate([jnp.concatenate([kt, z], axis=3), jnp.concatenate([z, kt], axis=3)], axis=2)
    resident = pl.Buffered(1) if nq > 1 else None

    mask = pl.pallas_call(
        functools.partial(_dsa_select_kernel, tq=tq, tk=tk, nkt=nkt, past=grp.past, topk=topk),
        grid=(grp.batch, nq),
        in_specs=[
            pl.BlockSpec((tq, qiw), lambda b, i: (q_blk0 + b * nq + i, cols[0] // qiw)),
            pl.BlockSpec((tq, HEAD_DIM), lambda b, i: (q_blk0 + b * nq + i, 0)),
            pl.BlockSpec((None, nkt, 2 * IDX_DIM, 2 * tk), lambda b, i: (b, 0, 0, 0), pipeline_mode=resident),
        ],
        out_specs=pl.BlockSpec((None, None, nkt, tq, tk), lambda b, i: (b, i, 0, 0, 0)),
        out_shape=jax.ShapeDtypeStruct((grp.batch, nq, nkt, tq, tk), BF16),
        scratch_shapes=[
            pltpu.VMEM((nkt + 1, tq, tk), jnp.int32),
            pltpu.VMEM((N_IDX_HEADS, tq, HEAD_DIM), F32),
        ],
        compiler_params=_params("parallel", "parallel"),
        name="dsa_select",
    )(proj, tail, ki2)

    wp = hp * HEAD_DIM
    kv_args, kv_specs = _kv_operands(grp, proj, cols[2], cols[3], k_cache, v_cache, tk=tk, hp=hp)
    prev_args, prev_specs, aliased = _chained(o_prev)
    n_in = 1 + len(kv_args) + 2
    return pl.pallas_call(
        functools.partial(_dsa_flash_kernel, tq=tq, tk=tk, hp=hp, n_heads=H, past=grp.past, has_past=grp.past > 0),
        grid=(grp.batch, H // hp, nq),
        in_specs=[pl.BlockSpec((tq, wp), lambda b, h, i: (q_blk0 + b * nq + i, cols[1] // wp + h))]
        + kv_specs
        + [pl.BlockSpec((None, None, nkt, tq, tk), lambda b, h, i: (b, i, 0, 0, 0)),
           pl.BlockSpec((hp, 2, tq, tk), lambda b, h, i: (h, 0, 0, 0), pipeline_mode=resident)]
        + prev_specs,
        out_specs=pl.BlockSpec((tq, wp), lambda b, h, i: (q_blk0 + b * nq + i, o_col // wp + h)),
        out_shape=jax.ShapeDtypeStruct(o_shape, BF16),
        input_output_aliases={n_in: 0} if aliased else {},
        scratch_shapes=[pltpu.VMEM((tq, wp), F32), pltpu.VMEM((tq, wp), F32), pltpu.VMEM((tq, wp), F32)],
        compiler_params=_params("parallel", "parallel", "arbitrary"),
        name="dsa_flash",
    )(proj, *kv_args, mask, _near_bias(rel_bias, tq=tq, tk=tk), *prev_args)


def _cross_kernel(q_ref, k_ref, v_ref, *rest, n_heads, scale):
    o_ref = rest[-1]
    for hh in range(n_heads):
        sl = slice(hh * HEAD_DIM, (hh + 1) * HEAD_DIM)
        s = _nt_dot(q_ref[:, sl], k_ref[:, sl]) * scale
        p = jnp.exp(s - jnp.max(s, axis=-1, keepdims=True))
        denom = jnp.sum(p, axis=-1, keepdims=True)
        o = jnp.dot(p.astype(BF16), v_ref[:, sl], preferred_element_type=F32)
        o_ref[:, sl] = (o / denom).astype(o_ref.dtype)


def _cross_attention(grp, q, mem_k, mem_v, o_prev, *, tq=512):
    M, W = q.shape
    n_mem = mem_k.shape[1]
    tq = _pick_tile(grp.seq, tq)
    nq = grp.seq // tq
    assert grp.row0 % tq == 0
    blk0 = grp.row0 // tq
    prev_args, prev_specs, aliased = _chained(o_prev)
    return pl.pallas_call(
        functools.partial(_cross_kernel, n_heads=W // HEAD_DIM, scale=HEAD_DIM ** -0.5),
        grid=(grp.batch, nq),
        in_specs=[
            pl.BlockSpec((tq, W), lambda b, i: (blk0 + b * nq + i, 0)),
            pl.BlockSpec((None, n_mem, W), lambda b, i: (b, 0, 0)),
            pl.BlockSpec((None, n_mem, W), lambda b, i: (b, 0, 0)),
        ] + prev_specs,
        out_specs=pl.BlockSpec((tq, W), lambda b, i: (blk0 + b * nq + i, 0)),
        out_shape=jax.ShapeDtypeStruct((M, W), BF16),
        input_output_aliases={3: 0} if aliased else {},
        compiler_params=_params("parallel", "parallel"),
        name="cross_attention",
    )(q, mem_k, mem_v, *prev_args)


CONV_ROW_CHUNK = 128
SUBLANES = 8


def _conv_kernel(a_ref, prev_ref, state_ref, w_ref, bias_ref, g_ref, *rest, tt):
    o_ref, buf_ref, q_ref, y_ref = rest[-4], rest[-3], rest[-2], rest[-1]
    D = a_ref.shape[-1]

    @pl.when(pl.program_id(1) == 0)
    def _():
        buf_ref[0:CONV_HALO, :] = state_ref[...]

    @pl.when(pl.program_id(1) > 0)
    def _():
        buf_ref[0:CONV_HALO, :] = prev_ref[...]

    buf_ref[CONV_HALO:CONV_HALO + tt, :] = a_ref[...]
    buf_ref[CONV_HALO + tt:CONV_HALO + tt + SUBLANES, :] = jnp.zeros((SUBLANES, D), F32)

    first = CONV_HALO - (CONV_WIDTH - 1)
    rc = min(CONV_ROW_CHUNK, tt)
    for c0 in range(0, D, HEAD_DIM):
        cs = slice(c0, c0 + HEAD_DIM)
        for t0 in range(0, tt, rc):
            out = jnp.zeros((rc, HEAD_DIM), F32)
            for r in range(SUBLANES):
                q = None
                for k in range(CONV_WIDTH):
                    if (first + k) % SUBLANES != r:
                        continue
                    base = t0 + first + k - r
                    term = buf_ref[base:base + rc + SUBLANES, cs] * w_ref[k:k + 1, cs]
                    q = term if q is None else q + term
                if r == 0:
                    out = out + q[0:rc]
                else:
                    q_ref[...] = q
                    out = out + q_ref[r:r + rc, :]
            y_ref[t0:t0 + rc, cs] = out + bias_ref[:, cs]
    y = _rms_rows(y_ref[...], g_ref[...])
    o_ref[...] = (y * (1.0 / (1.0 + jnp.exp(-y)))).astype(o_ref.dtype)


def _conv_norm_swish(grp, u, state, w_dw, b_dw, g_norm, o_prev):
    M, D = u.shape
    tt = _pick_tile(grp.seq, 256)
    nt = grp.seq // tt
    assert grp.seq % tt == 0 and tt % CONV_HALO == 0 and grp.row0 % tt == 0
    blk0 = grp.row0 // tt
    hb = tt // CONV_HALO
    prev_args, prev_specs, aliased = _chained(o_prev)
    return pl.pallas_call(
        functools.partial(_conv_kernel, tt=tt),
        grid=(grp.batch, nt),
        in_specs=[
            pl.BlockSpec((tt, D), lambda b, i: (blk0 + b * nt + i, 0)),
            pl.BlockSpec((CONV_HALO, D), lambda b, i: (jnp.maximum((blk0 + b * nt + i) * hb - 1, 0), 0)),
            pl.BlockSpec((None, CONV_HALO, D), lambda b, i: (b, 0, 0)),
            pl.BlockSpec((CONV_WIDTH, D), lambda b, i: (0, 0)),
            pl.BlockSpec((1, D), lambda b, i: (0, 0)),
            pl.BlockSpec((1, D), lambda b, i: (0, 0)),
        ] + prev_specs,
        out_specs=pl.BlockSpec((tt, D), lambda b, i: (blk0 + b * nt + i, 0)),
        out_shape=jax.ShapeDtypeStruct((M, D), BF16),
        input_output_aliases={6: 0} if aliased else {},
        scratch_shapes=[pltpu.VMEM((tt + CONV_HALO + SUBLANES, D), F32),
                        pltpu.VMEM((min(CONV_ROW_CHUNK, tt) + SUBLANES, HEAD_DIM), F32),
                        pltpu.VMEM((tt, D), F32)],
        compiler_params=_params("parallel", "arbitrary"),
        name="conv_norm_swish",
    )(u, u, state, w_dw, b_dw.reshape(1, D), g_norm.reshape(1, D), *prev_args)


def kernel(x_prompt, x_sample, mem_prompt, cache_sb_k, cache_sb_v, cache_dsa_k, cache_dsa_v, cache_idx_k, cache_mem_k, cache_mem_v, state_conv, norm_mix, norm_cross, norm_mlp, norm_final, w_in_ab, w_out_ab, rel_bias, w_pw1, b_pw1, w_dw, b_dw, g_conv_norm, w_pw2, b_pw2, w_cq, w_mk, w_mv, w_co, w_up, w_down):
    Bp, Tp, D = x_prompt.shape
    Bs, Ts, _ = x_sample.shape
    past = cache_sb_k.shape[2]
    depth = norm_mix.shape[0]
    n_mem = mem_prompt.shape[1]
    n_sb, n_dsa = cache_sb_k.shape[3], cache_dsa_k.shape[3]
    sbw, dsw = n_sb * HEAD_DIM, n_dsa * HEAD_DIM
    qiw = N_IDX_HEADS * IDX_DIM
    cw = w_cq.shape[2]
    Mp, Ms = Bp * Tp, Bs * Ts
    M = Mp + Ms
    prompt = _Group(Bp, Tp, 0, 0)
    sample = _Group(Bs, Ts, Mp, past)

    x = jnp.concatenate([x_prompt.reshape(Mp, D), x_sample.reshape(Ms, D)], axis=0)
    mem_flat = mem_prompt.reshape(Bp * n_mem, D)

    src = dict(zip(("q_sb", "k_sb", "v_sb", "q_d", "k_d", "v_d", "q_i", "k_i", "w_i"),
                   np.cumsum([0, sbw, sbw, sbw, dsw, dsw, dsw, qiw, IDX_DIM])))
    order = ("k_sb", "v_sb", "k_d", "v_d", "q_sb", "q_d", "q_i")
    width = dict(q_sb=sbw, k_sb=sbw, v_sb=sbw, q_d=dsw, k_d=dsw, v_d=dsw, q_i=qiw)
    col = dict(zip(order, np.cumsum([0] + [width[n] for n in order])[:-1]))
    col = {k: int(v) for k, v in col.items()}
    n32 = 2 * sbw + 2 * dsw

    outs = {k: [] for k in ("p_sb_k", "p_sb_v", "p_d_k", "p_d_v", "p_ki", "p_mem_k", "p_mem_v", "p_conv",
                            "s_sb_k", "s_sb_v", "s_d_k", "s_d_v", "s_ki", "s_conv")}

    def split_rows(a):
        return a[:Mp].reshape(Bp, Tp, -1), a[Mp:].reshape(Bs, Ts, -1)

    def last_rows(a, grp, state, n):
        if grp.seq >= n:
            return jnp.stack([a[grp.row0 + (b + 1) * grp.seq - n:grp.row0 + (b + 1) * grp.seq]
                              for b in range(grp.batch)])
        rows = a[grp.row0:grp.row0 + grp.batch * grp.seq].reshape(grp.batch, grp.seq, -1)
        return jnp.concatenate([state, rows], axis=1)[:, -n:]

    w_out16, w_pw1_16, w_pw2_16 = w_out_ab.astype(BF16), w_pw1.astype(BF16), w_pw2.astype(BF16)
    w_cq16, w_mk16, w_mv16, w_co16 = (w.astype(BF16) for w in (w_cq, w_mk, w_mv, w_co))
    w_up16, w_down16 = w_up.astype(BF16), w_down.astype(BF16)
    q_scale = HEAD_DIM ** -0.5 * LOG2E
    col_scale = jnp.concatenate([jnp.full((width[n],), q_scale if n in ("q_sb", "q_d") else 1.0, F32)
                                 for n in order])

    for l in range(depth):
        i = l // 2
        if l % 2 == 0:
            w_in = w_in_ab[i]
            w_main = jnp.concatenate([w_in[:, src[n]:src[n] + width[n]] for n in order], axis=1).astype(BF16)
            tail_w = w_in.shape[1] - int(src["k_i"])
            w_tail = jnp.pad(w_in[:, src["k_i"]:], ((0, 0), (0, HEAD_DIM - tail_w))).astype(BF16)
            proj, kv32, tail = _in_proj(x, w_main, norm_mix[l], col_scale, w_tail, n32=n32)

            for name, key, nh in (("sb_k", "k_sb", n_sb), ("sb_v", "v_sb", n_sb),
                                  ("d_k", "k_d", n_dsa), ("d_v", "v_d", n_dsa)):
                ap, as_ = split_rows(kv32[:, col[key]:col[key] + width[key]])
                outs["p_" + name].append(ap.reshape(Bp, Tp, nh, HEAD_DIM))
                outs["s_" + name].append(as_.reshape(Bs, Ts, nh, HEAD_DIM))
            k_i = tail[:, :IDX_DIM]
            kip, kis = split_rows(k_i)
            outs["p_ki"].append(kip)
            outs["s_ki"].append(kis)

            tk_s = ATTN_TILE
            s_pad_s = -(-(past + Ts) // tk_s) * tk_s
            kidx_p = kip.astype(BF16)
            kidx_s = jnp.concatenate([cache_idx_k[i].astype(BF16), kis.astype(BF16),
                                      jnp.zeros((Bs, s_pad_s - past - Ts, IDX_DIM), BF16)], axis=1)

            o_shape = (M, sbw + dsw)
            sb_cols = (col["q_sb"], col["k_sb"], col["v_sb"])
            d_cols = (col["q_i"], col["q_d"], col["k_d"], col["v_d"])
            o = _sb_attention(prompt, proj, sb_cols, cache_sb_k[i], cache_sb_v[i], None, o_shape, 0,
                              hp=SB_HEADS_PER_STEP)
            o = _dsa_attention(prompt, proj, tail, d_cols, kidx_p, cache_dsa_k[i], cache_dsa_v[i], rel_bias,
                               o, o_shape, sbw, hp=DSA_HEADS_PER_STEP)
            o = _sb_attention(sample, proj, sb_cols, cache_sb_k[i], cache_sb_v[i], o, o_shape, 0,
                              hp=SB_HEADS_PER_STEP)
            o = _dsa_attention(sample, proj, tail, d_cols, kidx_s, cache_dsa_k[i], cache_dsa_v[i], rel_bias,
                               o, o_shape, sbw, hp=n_dsa)
            x, qc = _residual_matmul_then_proj(o, w_out16, i, x, None, norm_cross[l], w_cq16, l)
        else:
            u = _matmul(x, w_pw1_16, layer=i, gain=norm_mix[l], bias=b_pw1[i], glu=True)
            front = CONV_HALO - (CONV_WIDTH - 1)
            state_p = jnp.zeros((Bp, CONV_HALO, D), F32)
            state_s = jnp.concatenate([jnp.zeros((Bs, front, D), F32), state_conv[i]], axis=1)
            y = _conv_norm_swish(prompt, u, state_p, w_dw[i], b_dw[i], g_conv_norm[i], None)
            y = _conv_norm_swish(sample, u, state_s, w_dw[i], b_dw[i], g_conv_norm[i], y)
            outs["p_conv"].append(last_rows(u, prompt, state_p, CONV_WIDTH - 1))
            outs["s_conv"].append(last_rows(u, sample, state_s, CONV_WIDTH - 1))
            x, qc = _residual_matmul_then_proj(y, w_pw2_16, i, x, b_pw2[i], norm_cross[l], w_cq16, l)

        mk = _matmul(mem_flat, w_mk16, layer=l)
        mv = _matmul(mem_flat, w_mv16, layer=l)
        outs["p_mem_k"].append(mk.reshape(Bp, n_mem, cw // HEAD_DIM, HEAD_DIM))
        outs["p_mem_v"].append(mv.reshape(Bp, n_mem, cw // HEAD_DIM, HEAD_DIM))
        oc = _cross_attention(prompt, qc, mk.astype(BF16).reshape(Bp, n_mem, cw),
                              mv.astype(BF16).reshape(Bp, n_mem, cw), None)
        oc = _cross_attention(sample, qc, cache_mem_k[l].astype(BF16).reshape(Bs, n_mem, cw),
                              cache_mem_v[l].astype(BF16).reshape(Bs, n_mem, cw), oc)
        if l < depth - 1:
            x = _cross_out_mlp(x, oc, w_co16, norm_mlp[l], w_up16, w_down16, l)
        else:
            y_p, y_s = _cross_out_mlp(x, oc, w_co16, norm_mlp[l], w_up16, w_down16, l,
                                      final_gain=norm_final, m_prompt=Mp)

    st = lambda k: jnp.stack(outs[k])
    return (y_p.reshape(Bp, Tp, D), y_s.reshape(Bs, Ts, D), st("p_sb_k"), st("p_sb_v"), st("p_d_k"), st("p_d_v"),
            st("p_ki"), st("p_mem_k"), st("p_mem_v"), st("p_conv"), st("s_sb_k"), st("s_sb_v"), st("s_d_k"),
            st("s_d_v"), st("s_ki"), st("s_conv"))
```

```python
import functools
import math
from typing import NamedTuple

import numpy as np
import jax
import jax.numpy as jnp
from jax import lax
from jax.experimental import pallas as pl
from jax.experimental.pallas import tpu as pltpu

F32 = jnp.float32
BF16 = jnp.bfloat16

HEAD_DIM = 128
IDX_DIM = 64
N_IDX_HEADS = 16
CHUNK = 64
TOPK_MAX = 256
N_BUCKETS = 32
MAX_DISTANCE = 128
CONV_WIDTH = 31
CONV_HALO = 32
RMS_EPS = 1e-6
MASKED_LOGIT = -1e30
LOG2E = math.log2(math.e)
_INT32_MIN = -2 ** 31
VMEM_LIMIT_BYTES = 56 * 1024 * 1024
SB_HEADS_PER_STEP = 8
DSA_HEADS_PER_STEP = 4
SB_INTERLEAVE = 4
ATTN_TILE = 256
FAR_TILES_PER_TRIP = 4


class _Group(NamedTuple):
    batch: int
    seq: int
    row0: int
    past: int


def _params(*semantics):
    return pltpu.CompilerParams(dimension_semantics=semantics, vmem_limit_bytes=VMEM_LIMIT_BYTES)


def _pick_tile(n, pref):
    if n <= pref:
        return n
    t = pref
    while n % t:
        t //= 2
    return t


def _row_tile(m, pref):
    t = min(m, pref) // 8 * 8
    while m % t:
        t -= 8
    return t


def _chained(prev):
    if prev is None:
        return [], [], None
    return [prev], [pl.BlockSpec(memory_space=pl.ANY)], True


def _rms_rows(x, g):
    ms = jnp.mean(x * x, axis=-1, keepdims=True)
    return x * lax.rsqrt(ms + RMS_EPS) * g


def _mm_kernel(*refs, has_norm, has_bias, has_res, glu):
    it = iter(refs)
    x_ref = next(it)
    g_ref = next(it) if has_norm else None
    w_ref = next(it)
    w2_ref = next(it) if glu else None
    b_ref = next(it) if has_bias else None
    b2_ref = next(it) if (has_bias and glu) else None
    r_ref = next(it) if has_res else None
    o_ref = next(it)
    xn_ref = next(it) if has_norm else None

    if has_norm:
        @pl.when(pl.program_id(1) == 0)
        def _():
            xn_ref[...] = _rms_rows(x_ref[...], g_ref[...]).astype(BF16)
        xb = xn_ref[...]
    else:
        xb = x_ref[...].astype(BF16)

    y = jnp.dot(xb, w_ref[...], preferred_element_type=F32)
    if has_bias:
        y = y + b_ref[...]
    if glu:
        gate = jnp.dot(xb, w2_ref[...], preferred_element_type=F32)
        if has_bias:
            gate = gate + b2_ref[...]
        y = y * (1.0 / (1.0 + jnp.exp(-gate)))
    if has_res:
        y = y + r_ref[...]
    o_ref[...] = y.astype(o_ref.dtype)


def _matmul(x, w, *, layer=0, gain=None, bias=None, residual=None, glu=False, out_dtype=F32, tm=1152, tn=512):
    M, K = x.shape
    n_out = w.shape[2] // 2 if glu else w.shape[2]
    tm = _row_tile(M, tm)
    tn = _pick_tile(n_out, tn)
    nj = n_out // tn
    has_norm, has_bias, has_res = gain is not None, bias is not None, residual is not None

    args = [x]
    specs = [pl.BlockSpec((tm, K), lambda i, j: (i, 0))]
    if has_norm:
        args.append(gain.reshape(1, K))
        specs.append(pl.BlockSpec((1, K), lambda i, j: (0, 0)))
    args.append(w)
    specs.append(pl.BlockSpec((None, K, tn), lambda i, j: (layer, 0, j)))
    if glu:
        args.append(w)
        specs.append(pl.BlockSpec((None, K, tn), lambda i, j: (layer, 0, j + nj)))
    if has_bias:
        b2d = bias.reshape(1, -1)
        args.append(b2d)
        specs.append(pl.BlockSpec((1, tn), lambda i, j: (0, j)))
        if glu:
            args.append(b2d)
            specs.append(pl.BlockSpec((1, tn), lambda i, j: (0, j + nj)))
    if has_res:
        args.append(residual)
        specs.append(pl.BlockSpec((tm, tn), lambda i, j: (i, j)))

    return pl.pallas_call(
        functools.partial(_mm_kernel, has_norm=has_norm, has_bias=has_bias, has_res=has_res, glu=glu),
        grid=(M // tm, nj),
        in_specs=specs,
        out_specs=pl.BlockSpec((tm, tn), lambda i, j: (i, j)),
        out_shape=jax.ShapeDtypeStruct((M, n_out), out_dtype),
        scratch_shapes=[pltpu.VMEM((tm, K), BF16)] if has_norm else [],
        compiler_params=_params("parallel", "arbitrary"),
        name="norm_matmul",
    )(*args)


def _res_proj_kernel(y_ref, w_ref, *refs, has_bias):
    it = iter(refs)
    b_ref = next(it) if has_bias else None
    r_ref, g_ref, wq_ref, o_ref, q_ref, x1_ref = (next(it) for _ in range(6))
    j = pl.program_id(1)
    y = jnp.dot(y_ref[...], w_ref[...], preferred_element_type=F32) + r_ref[...]
    if has_bias:
        y = y + b_ref[...]
    o_ref[...] = y
    x1_ref[j] = y

    @pl.when(j == pl.num_programs(1) - 1)
    def _():
        x1 = jnp.concatenate([x1_ref[t] for t in range(x1_ref.shape[0])], axis=1)
        q_ref[...] = jnp.dot(_rms_rows(x1, g_ref[...]).astype(BF16), wq_ref[...],
                             preferred_element_type=F32).astype(q_ref.dtype)


def _residual_matmul_then_proj(y, w, layer, residual, bias, gain, wq, q_layer, *, tm=1152, tn=512):
    M, K = y.shape
    N, nq = w.shape[2], wq.shape[2]
    tm = _row_tile(M, tm)
    assert N % tn == 0
    nj = N // tn
    has_bias = bias is not None
    args = [y, w]
    specs = [pl.BlockSpec((tm, K), lambda i, j: (i, 0)), pl.BlockSpec((None, K, tn), lambda i, j: (layer, 0, j))]
    if has_bias:
        args.append(bias.reshape(1, N))
        specs.append(pl.BlockSpec((1, tn), lambda i, j: (0, j)))
    args += [residual, gain.reshape(1, N), wq]
    specs += [pl.BlockSpec((tm, tn), lambda i, j: (i, j)), pl.BlockSpec((1, N), lambda i, j: (0, 0)),
              pl.BlockSpec((None, N, nq), lambda i, j: (q_layer, 0, 0))]
    return pl.pallas_call(
        functools.partial(_res_proj_kernel, has_bias=has_bias),
        grid=(M // tm, nj),
        in_specs=specs,
        out_specs=[pl.BlockSpec((tm, tn), lambda i, j: (i, j)), pl.BlockSpec((tm, nq), lambda i, j: (i, 0))],
        out_shape=[jax.ShapeDtypeStruct((M, N), F32), jax.ShapeDtypeStruct((M, nq), BF16)],
        scratch_shapes=[pltpu.VMEM((nj, tm, tn), F32)],
        compiler_params=_params("parallel", "arbitrary"),
        name="residual_matmul_proj",
    )(*args)


def _in_proj_kernel(x_ref, g_ref, w_ref, s_ref, wt_ref, o16_ref, o32_ref, ot_ref, xn_ref, *, n32_tiles):
    j = pl.program_id(1)

    @pl.when(j == 0)
    def _():
        xn_ref[...] = _rms_rows(x_ref[...], g_ref[...]).astype(BF16)
        ot_ref[...] = jnp.dot(xn_ref[...], wt_ref[...], preferred_element_type=F32)

    y = jnp.dot(xn_ref[...], w_ref[...], preferred_element_type=F32)
    o16_ref[...] = (y * s_ref[...]).astype(BF16)

    @pl.when(j < n32_tiles)
    def _():
        o32_ref[...] = y


def _in_proj(x, w, gain, col_scale, w_tail, *, n32, tm=1152, tn=512):
    M, K = x.shape
    N = w.shape[1]
    nt = w_tail.shape[1]
    tm = _row_tile(M, tm)
    assert N % tn == 0 and n32 % tn == 0
    n32_tiles = n32 // tn
    return pl.pallas_call(
        functools.partial(_in_proj_kernel, n32_tiles=n32_tiles),
        grid=(M // tm, N // tn),
        in_specs=[
            pl.BlockSpec((tm, K), lambda i, j: (i, 0)),
            pl.BlockSpec((1, K), lambda i, j: (0, 0)),
            pl.BlockSpec((K, tn), lambda i, j: (0, j)),
            pl.BlockSpec((1, tn), lambda i, j: (0, j)),
            pl.BlockSpec((K, nt), lambda i, j: (0, 0)),
        ],
        out_specs=[
            pl.BlockSpec((tm, tn), lambda i, j: (i, j)),
            pl.BlockSpec((tm, tn), lambda i, j: (i, jnp.minimum(j, n32_tiles - 1))),
            pl.BlockSpec((tm, nt), lambda i, j: (i, 0)),
        ],
        out_shape=[jax.ShapeDtypeStruct((M, N), BF16), jax.ShapeDtypeStruct((M, n32), F32),
                   jax.ShapeDtypeStruct((M, nt), F32)],
        scratch_shapes=[pltpu.VMEM((tm, K), BF16)],
        compiler_params=_params("parallel", "arbitrary"),
        name="in_proj",
    )(x, gain.reshape(1, K), w, col_scale.reshape(1, N), w_tail)


def _mlp_kernel(x_ref, oc_ref, wco_ref, g_ref, wu_ref, wd_ref, *refs, n_prompt_tiles):
    final = n_prompt_tiles is not None
    if final:
        gf_ref, op_ref, os_ref, x2_ref, xn_ref, acc_ref = refs
    else:
        o_ref, x2_ref, xn_ref, acc_ref = refs
    i, f = pl.program_id(0), pl.program_id(1)

    @pl.when(f == 0)
    def _():
        x2 = x_ref[...] + jnp.dot(oc_ref[...], wco_ref[...], preferred_element_type=F32)
        x2_ref[...] = x2
        xn_ref[...] = _rms_rows(x2, g_ref[...]).astype(BF16)
        acc_ref[...] = jnp.zeros_like(acc_ref)

    h = jnp.dot(xn_ref[...], wu_ref[...], preferred_element_type=F32)
    h = jnp.maximum(h, 0.0)
    h = (h * h).astype(BF16)
    acc_ref[...] += jnp.dot(h, wd_ref[...], preferred_element_type=F32)

    last = f == pl.num_programs(1) - 1
    if not final:
        @pl.when(last)
        def _():
            o_ref[...] = x2_ref[...] + acc_ref[...]
    else:
        @pl.when(last & (i < n_prompt_tiles))
        def _():
            op_ref[...] = _rms_rows(x2_ref[...] + acc_ref[...], gf_ref[...])

        @pl.when(last & (i >= n_prompt_tiles))
        def _():
            os_ref[...] = _rms_rows(x2_ref[...] + acc_ref[...], gf_ref[...])


def _cross_out_mlp(x, oc, w_co, gain, w_up, w_down, layer, *, final_gain=None, m_prompt=None, tm=512, tf=512):
    M, D = x.shape
    F = w_up.shape[2]
    cw = oc.shape[1]
    final = final_gain is not None
    tm = _pick_tile(M - m_prompt, tm) if final else _pick_tile(M, tm)
    tf = _pick_tile(F, tf)
    args = [x, oc, w_co, gain.reshape(1, D), w_up, w_down]
    specs = [
        pl.BlockSpec((tm, D), lambda i, f: (i, 0)),
        pl.BlockSpec((tm, cw), lambda i, f: (i, 0)),
        pl.BlockSpec((None, cw, D), lambda i, f: (layer, 0, 0)),
        pl.BlockSpec((1, D), lambda i, f: (0, 0)),
        pl.BlockSpec((None, D, tf), lambda i, f: (layer, 0, f)),
        pl.BlockSpec((None, tf, D), lambda i, f: (layer, f, 0)),
    ]
    if final:
        assert m_prompt % tm == 0 and (M - m_prompt) % tm == 0
        npt = m_prompt // tm
        args.append(final_gain.reshape(1, D))
        specs.append(pl.BlockSpec((1, D), lambda i, f: (0, 0)))
        out_specs = [pl.BlockSpec((tm, D), lambda i, f: (jnp.minimum(i, npt - 1), 0)),
                     pl.BlockSpec((tm, D), lambda i, f: (jnp.maximum(i - npt, 0), 0))]
        out_shape = [jax.ShapeDtypeStruct((m_prompt, D), F32), jax.ShapeDtypeStruct((M - m_prompt, D), F32)]
        semantics = ("arbitrary", "arbitrary")
    else:
        npt = None
        out_specs = pl.BlockSpec((tm, D), lambda i, f: (i, 0))
        out_shape = jax.ShapeDtypeStruct((M, D), F32)
        semantics = ("parallel", "arbitrary")
    return pl.pallas_call(
        functools.partial(_mlp_kernel, n_prompt_tiles=npt),
        grid=(M // tm, F // tf),
        in_specs=specs,
        out_specs=out_specs,
        out_shape=out_shape,
        scratch_shapes=[pltpu.VMEM((tm, D), F32), pltpu.VMEM((tm, D), BF16), pltpu.VMEM((tm, D), F32)],
        compiler_params=_params(*semantics),
        name="cross_out_mlp",
    )(*args)


def _nt_dot(a, b):
    return lax.dot_general(a, b, (((1,), (1,)), ((), ())), preferred_element_type=F32)


def _kv_operands(grp, proj, k_col, v_col, k_cache, v_cache, *, tk, hp):
    wp = hp * HEAD_DIM
    if grp.past == 0:
        blk0 = grp.row0 // grp.seq
        args = [proj, proj]
        specs = [pl.BlockSpec((grp.seq, wp), lambda b, h, i, c=k_col // wp: (blk0 + b, c + h),
                              pipeline_mode=pl.Buffered(1)),
                 pl.BlockSpec((grp.seq, wp), lambda b, h, i, c=v_col // wp: (blk0 + b, c + h),
                              pipeline_mode=pl.Buffered(1))]
        return args, specs

    def new_tile(col):
        rows = lax.slice(proj, (grp.row0, col), (grp.row0 + grp.batch * grp.seq, col + k_cache.shape[2] * HEAD_DIM))
        rows = rows.reshape(grp.batch, grp.seq, -1)
        return jnp.pad(rows, ((0, 0), (0, tk - grp.seq), (0, 0)))

    args = [new_tile(k_col), new_tile(v_col)]
    specs = [pl.BlockSpec((None, tk, wp), lambda b, h, i: (b, 0, h))] * 2
    n_heads = k_cache.shape[2]
    for cache in (k_cache, v_cache):
        args.append(cache.reshape(grp.batch, grp.past * n_heads, HEAD_DIM))
        specs.append(pl.BlockSpec((None, grp.past * n_heads, HEAD_DIM), lambda b, h, i: (b, 0, 0)))
    return args, specs


def _split_kv_refs(refs, has_past):
    if not has_past:
        return refs[0], refs[1], None, None, refs[2:]
    return refs[0], refs[1], refs[2], refs[3], refs[4:]


def _tile_reader(new_ref, past_ref, *, tk, hp, n_heads):
    def read(j, hh, diag, n=1):
        hs = slice(hh * HEAD_DIM, (hh + 1) * HEAD_DIM)
        if past_ref is None:
            return new_ref[pl.ds(pl.multiple_of(j * tk, tk), n * tk), hs]
        if diag:
            return new_ref[:, hs]
        head = pl.program_id(1) * hp + hh
        return past_ref[pl.ds(j * (tk * n_heads) + head, n * tk, stride=n_heads), :].astype(BF16)
    return read


def _sb_kernel(q_ref, *refs, tq, tk, hp, n_heads, past, has_past):
    k_new, v_new, k_past, v_past, rest = _split_kv_refs(refs, has_past)
    u_ref, o_ref, acc_ref, carry_ref = rest[0], rest[-3], rest[-2], rest[-1]
    read_k = _tile_reader(k_new, k_past, tk=tk, hp=hp, n_heads=n_heads)
    read_v = _tile_reader(v_new, v_past, tk=tk, hp=hp, n_heads=n_heads)

    i = pl.program_id(2)
    q_start = past + i * tq
    jd = q_start // tk
    reps = tk // HEAD_DIM

    acc_ref[...] = jnp.zeros_like(acc_ref)
    carry_ref[...] = jnp.zeros_like(carry_ref)

    def step(j, masked):
        if masked:
            t_pos = q_start + lax.broadcasted_iota(jnp.int32, (tq, tk), 0)
            s_pos = j * tk + lax.broadcasted_iota(jnp.int32, (tq, tk), 1)
            causal = t_pos > s_pos
        for h0 in range(0, hp, SB_INTERLEAVE):
            group_step(j, masked, causal if masked else None, range(h0, h0 + SB_INTERLEAVE))

    def group_step(j, masked, causal, group):
        heads = [(hh, slice(hh * HEAD_DIM, (hh + 1) * HEAD_DIM)) for hh in group]
        z2s = [_nt_dot(q_ref[:, hs], read_k(j, hh, masked)) for hh, hs in heads]
        ns, log2_betas, css = [], [], []
        for z2 in z2s:
            neg_abs = lax.bitcast_convert_type(
                lax.bitcast_convert_type(z2, jnp.int32) | jnp.int32(_INT32_MIN), F32)
            n = jnp.maximum(z2, 0.0) + jnp.log(1.0 + jnp.exp2(neg_abs)) * LOG2E
            log2_betas.append(z2 - n)
            if masked:
                n = jnp.where(causal, n, 0.0)
            n16 = n.astype(BF16)
            css.append(jnp.dot(n16, u_ref[...], preferred_element_type=F32))
            ns.append(n16)
        for (hh, hs), n16, log2_beta, cs in zip(heads, ns, log2_betas, css):
            a = jnp.exp2(log2_beta - cs)
            if masked:
                a = jnp.where(causal, a, 0.0)
            carry = carry_ref[:, hs]
            pv = jnp.dot(a.astype(BF16), read_v(j, hh, masked), preferred_element_type=F32)
            acc_ref[:, hs] += jnp.exp2(-carry) * pv
            carry_ref[:, hs] = carry + jnp.broadcast_to(cs[:, 0:1] + n16[:, 0:1].astype(F32), (tq, HEAD_DIM))

    step(jd, True)

    def body(t, c):
        step(jd - 1 - t, False)
        return c

    lax.fori_loop(0, jd, body, 0)
    o_ref[...] = acc_ref[...].astype(o_ref.dtype)


def _suffix_sum_matrix(tk):
    r = np.arange(tk)
    return jnp.asarray((r[:, None] > r[None, :]).astype(np.float32), dtype=BF16)


def _attn_tiles(grp):
    tq = _pick_tile(grp.seq, ATTN_TILE)
    tk = ATTN_TILE if grp.past else tq
    nq = grp.seq // tq
    assert grp.seq % tq == 0 and tk % tq == 0 and grp.past % tk == 0 and grp.row0 % tq == 0
    assert grp.past == 0 or nq == 1
    assert grp.past > 0 or grp.row0 % grp.seq == 0
    return tq, tk, nq


def _sb_attention(grp, proj, cols, k_cache, v_cache, o_prev, o_shape, o_col, *, hp):
    H = k_cache.shape[2]
    tq, tk, nq = _attn_tiles(grp)
    wp = hp * HEAD_DIM
    q_blk0 = grp.row0 // tq
    kv_args, kv_specs = _kv_operands(grp, proj, cols[1], cols[2], k_cache, v_cache, tk=tk, hp=hp)
    prev_args, prev_specs, aliased = _chained(o_prev)
    n_in = 1 + len(kv_args) + 1
    return pl.pallas_call(
        functools.partial(_sb_kernel, tq=tq, tk=tk, hp=hp, n_heads=H, past=grp.past, has_past=grp.past > 0),
        grid=(grp.batch, H // hp, nq),
        in_specs=[pl.BlockSpec((tq, wp), lambda b, h, i: (q_blk0 + b * nq + i, cols[0] // wp + h))]
        + kv_specs
        + [pl.BlockSpec((tk, tk), lambda b, h, i: (0, 0), pipeline_mode=pl.Buffered(1))]
        + prev_specs,
        out_specs=pl.BlockSpec((tq, wp), lambda b, h, i: (q_blk0 + b * nq + i, o_col // wp + h)),
        out_shape=jax.ShapeDtypeStruct(o_shape, BF16),
        input_output_aliases={n_in: 0} if aliased else {},
        scratch_shapes=[pltpu.VMEM((tq, wp), F32), pltpu.VMEM((tq, wp), F32)],
        compiler_params=_params("parallel", "parallel", "arbitrary"),
        name="sb_attention",
    )(proj, *kv_args, _suffix_sum_matrix(tk), *prev_args)


def _sortable_key(x):
    b = lax.bitcast_convert_type(x, jnp.int32)
    return b ^ ((b >> 31) & jnp.int32(0x7FFFFFFF))


_KEY_NEG_INF = int(np.int32(np.array(-np.inf, np.float32).view(np.int32) ^ 0x7FFFFFFF))
_KEY_LOWEST_FINITE = int(np.int32(np.array(np.finfo(np.float32).min, np.float32).view(np.int32) ^ 0x7FFFFFFF))
COUNT_ROW_GROUP = 128


def _dsa_select_kernel(qi_ref, wi_ref, ki2_ref, mask_ref, keys_ref, wb_ref, *, tq, tk, nkt, past, topk):
    i = pl.program_id(1)
    q_start = past + i * tq
    jd = q_start // tk
    reps = tk // HEAD_DIM

    w = wi_ref[:, IDX_DIM:IDX_DIM + N_IDX_HEADS] * (IDX_DIM ** -0.5 * N_IDX_HEADS ** -0.5)
    for hh in range(N_IDX_HEADS):
        wb_ref[hh] = jnp.broadcast_to(w[:, hh:hh + 1], (tq, HEAD_DIM))

    def idx_tile(j):
        kt = ki2_ref[j]
        tot = jnp.zeros((tq, tk), F32)
        for p in range(N_IDX_HEADS // 2):
            sc = jnp.dot(qi_ref[:, p * 2 * IDX_DIM:(p + 1) * 2 * IDX_DIM], kt, preferred_element_type=F32)
            w0 = jnp.concatenate([wb_ref[2 * p]] * reps, axis=1)
            w1 = jnp.concatenate([wb_ref[2 * p + 1]] * reps, axis=1)
            tot = tot + w0 * jnp.maximum(sc[:, :tk], 0.0) + w1 * jnp.maximum(sc[:, tk:], 0.0)
        return _sortable_key(tot)

    def idx_body(j, c):
        keys_ref[j] = idx_tile(j)
        return c

    lax.fori_loop(0, jd, idx_body, 0)
    t_pos = q_start + lax.broadcasted_iota(jnp.int32, (tq, tk), 0)
    s_pos = jd * tk + lax.broadcasted_iota(jnp.int32, (tq, tk), 1)
    visible = (t_pos // CHUNK) >= (s_pos // CHUNK)
    keys_ref[jd] = jnp.where(visible, idx_tile(jd), jnp.int32(_KEY_NEG_INF))

    n_vis = jd + 1

    @pl.when(n_vis % 2 == 1)
    def _():
        keys_ref[n_vis] = jnp.full((tq, tk), _INT32_MIN, jnp.int32)

    n_pairs = (n_vis + 1) // 2
    rg = min(COUNT_ROW_GROUP, tq)

    def count_ge(cand):
        parts = []
        for g in range(tq // rg):
            cg = cand[g * rg:(g + 1) * rg]

            def body(t, cnt, g=g, cg=cg):
                for jj in (2 * t, 2 * t + 1):
                    kj = keys_ref[jj, pl.ds(g * rg, rg), :]
                    for r in range(reps):
                        cnt = cnt + jnp.where(kj[:, r * HEAD_DIM:(r + 1) * HEAD_DIM] >= cg, 1.0, 0.0)
                return cnt

            parts.append(lax.fori_loop(0, n_pairs, body, jnp.zeros((rg, HEAD_DIM), F32)))
        return jnp.sum(jnp.concatenate(parts, axis=0), axis=-1, keepdims=True)

    kf = float(topk)
    zero = jnp.zeros((tq, HEAD_DIM), jnp.int32)
    c = count_ge(zero)
    res = jnp.where(c >= kf, zero, jnp.int32(_INT32_MIN))
    cnt = jnp.where(c >= kf, c, float(tk) * (2 * n_pairs).astype(F32))
    for bit in range(30, -1, -1):
        cand = res + jnp.int32(1 << bit)
        c = count_ge(cand)
        res = jnp.where(c >= kf, cand, res)
        cnt = jnp.where(c >= kf, c, cnt)

    tied_cut = jnp.max(jnp.where((cnt != kf) & (res[:, 0:1] > jnp.int32(_KEY_NEG_INF)), 1.0, 0.0)) > 0.0

    @pl.when(tied_cut)
    def _():
        col = lax.broadcasted_iota(jnp.int32, (rg, tk), 1)

        def count_tied(before, strict_above):
            parts = []
            for g in range(tq // rg):
                rows = slice(g * rg, (g + 1) * rg)
                res_g = jnp.concatenate([res[rows]] * reps, axis=1)
                bef_g = None if strict_above else jnp.concatenate([before[rows]] * reps, axis=1)

                def body(j, acc, g=g, res_g=res_g, bef_g=bef_g):
                    kj = keys_ref[j, pl.ds(g * rg, rg), :]
                    if strict_above:
                        hit = jnp.where(kj > res_g, 1.0, 0.0)
                    else:
                        hit = jnp.where(kj == res_g, jnp.where(j * tk + col < bef_g, 1.0, 0.0), 0.0)
                    return acc + sum(hit[:, r * HEAD_DIM:(r + 1) * HEAD_DIM] for r in range(reps))

                parts.append(lax.fori_loop(0, n_vis, body, jnp.zeros((rg, HEAD_DIM), F32)))
            return jnp.sum(jnp.concatenate(parts, axis=0), axis=-1, keepdims=True)

        need = kf - count_tied(None, True)
        r_keep = jnp.zeros((tq, HEAD_DIM), jnp.int32)
        for bit in range((nkt * tk).bit_length() - 1, -1, -1):
            cand = r_keep | jnp.int32(1 << bit)
            r_keep = jnp.where(count_tied(cand, False) < need, cand, r_keep)

        def drop_late_ties(j, c):
            for g in range(tq // rg):
                rows = slice(g * rg, (g + 1) * rg)
                res_g = jnp.concatenate([res[rows]] * reps, axis=1)
                keep_g = jnp.concatenate([r_keep[rows]] * reps, axis=1)
                kj = keys_ref[j, pl.ds(g * rg, rg), :]
                keys_ref[j, pl.ds(g * rg, rg), :] = jnp.where(
                    kj == res_g, jnp.where(j * tk + col > keep_g, res_g - 1, kj), kj)
            return c

        lax.fori_loop(0, n_vis, drop_late_ties, 0)

    thr = jnp.maximum(res, jnp.int32(_KEY_LOWEST_FINITE))
    thr = jnp.concatenate([thr] * reps, axis=1)

    def write_mask(j, c):
        mask_ref[j] = jnp.where(keys_ref[j] >= thr, 0.0, MASKED_LOGIT).astype(mask_ref.dtype)
        return c

    lax.fori_loop(0, jd + 1, write_mask, 0)

    def write_hidden(j, c):
        mask_ref[j] = jnp.full((tq, tk), MASKED_LOGIT, mask_ref.dtype)
        return c

    lax.fori_loop(jd + 1, nkt, write_hidden, 0)


def _dsa_flash_kernel(q_ref, *refs, tq, tk, hp, n_heads, past, has_past):
    k_new, v_new, k_past, v_past, rest = _split_kv_refs(refs, has_past)
    mask_ref, bnear_ref = rest[0], rest[1]
    o_ref, m_ref, l_ref, acc_ref = rest[-4], rest[-3], rest[-2], rest[-1]
    read_k = _tile_reader(k_new, k_past, tk=tk, hp=hp, n_heads=n_heads)
    read_v = _tile_reader(v_new, v_past, tk=tk, hp=hp, n_heads=n_heads)

    i = pl.program_id(2)
    jd = (past + i * tq) // tk
    reps = tk // HEAD_DIM
    heads = [slice(hh * HEAD_DIM, (hh + 1) * HEAD_DIM) for hh in range(hp)]

    m_ref[...] = jnp.full_like(m_ref, MASKED_LOGIT)
    l_ref[...] = jnp.zeros_like(l_ref)
    acc_ref[...] = jnp.zeros_like(acc_ref)

    def step(j, near, n=1):
        diag = near == 0
        maskf = jnp.concatenate([mask_ref[j + t].astype(F32) for t in range(n)], axis=1)
        logits = []
        for hh, hs in enumerate(heads):
            s = _nt_dot(q_ref[:, hs], read_k(j, hh, diag, n)) + maskf
            if near is not None:
                s = s + bnear_ref[hh, near]
            logits.append(s)
        probs, alphas = [], []
        for hs, s in zip(heads, logits):
            m_old = m_ref[:, hs]
            m_new = jnp.maximum(m_old, jnp.max(s, axis=-1, keepdims=True))
            alpha = jnp.exp2(m_old - m_new)
            p = jnp.exp2(s - jnp.concatenate([m_new] * (n * reps), axis=1))
            l_ref[:, hs] = alpha * l_ref[:, hs] + sum(p[:, r * HEAD_DIM:(r + 1) * HEAD_DIM]
                                                      for r in range(n * reps))
            m_ref[:, hs] = m_new
            probs.append(p.astype(BF16))
            alphas.append(alpha)
        for hh, (hs, p, alpha) in enumerate(zip(heads, probs, alphas)):
            pv = jnp.dot(p, read_v(j, hh, diag, n), preferred_element_type=F32)
            acc_ref[:, hs] = alpha * acc_ref[:, hs] + pv

    step(jd, 0)

    @pl.when(jd >= 1)
    def _():
        step(jd - 1, 1)

    n_far = jnp.maximum(jd - 1, 0)
    n_trips = n_far // FAR_TILES_PER_TRIP

    def single(j, c):
        step(j, None)
        return c

    lax.fori_loop(n_trips * FAR_TILES_PER_TRIP, n_far, single, 0)

    def body(t, c):
        step(FAR_TILES_PER_TRIP * t, None, FAR_TILES_PER_TRIP)
        return c

    lax.fori_loop(0, n_trips, body, 0)
    for hs in heads:
        o_ref[:, hs] = (acc_ref[:, hs] / jnp.sum(l_ref[:, hs], axis=-1, keepdims=True)).astype(o_ref.dtype)


def _t5_bucket(rel):
    nb = N_BUCKETS // 2
    max_exact = nb // 2
    side = jnp.where(rel > 0, nb, 0)
    n = jnp.abs(rel)
    nf = jnp.maximum(n, 1).astype(F32)
    large = max_exact + (jnp.log(nf / max_exact) / math.log(MAX_DISTANCE / max_exact) * (nb - max_exact)).astype(jnp.int32)
    large = jnp.minimum(large, nb - 1)
    return side + jnp.where(n < max_exact, n, large)


def _near_bias(rel_bias, *, tq, tk):
    r = jnp.arange(tq, dtype=jnp.int32)[:, None]
    c = jnp.arange(tk, dtype=jnp.int32)[None, :]
    bucket = _t5_bucket(jnp.stack([c - r, c - r - tk]))
    far_bucket = _t5_bucket(jnp.int32(-2 * tk))
    table = (rel_bias - rel_bias[far_bucket][None, :]) * LOG2E
    onehot = (bucket[..., None] == jnp.arange(N_BUCKETS, dtype=jnp.int32)).astype(F32)
    return jnp.einsum("dqkb,bh->hdqk", onehot, table, precision=lax.Precision.HIGHEST)


def _dsa_attention(grp, proj, tail, cols, k_idx_all, k_cache, v_cache, rel_bias, o_prev, o_shape, o_col, *, hp):
    H = k_cache.shape[2]
    tq, tk, nq = _attn_tiles(grp)
    s_pad = k_idx_all.shape[1]
    nkt = s_pad // tk
    assert tk % CHUNK == 0 and tq % CHUNK == 0 and s_pad % tk == 0
    assert tk >= MAX_DISTANCE and H % hp == 0
    topk = min(TOPK_MAX, (grp.past + grp.seq) // 4)
    q_blk0 = grp.row0 // tq
    qiw = N_IDX_HEADS * IDX_DIM

    kt = jnp.transpose(k_idx_all.reshape(grp.batch, nkt, tk, IDX_DIM), (0, 1, 3, 2))
    z = jnp.zeros_like(kt)
    ki2 = jnp.concatenate([jnp.concatenate([kt, z], axis=3), jnp.concatenate([z, kt], axis=3)], axis=2)
    resident = pl.Buffered(1) if nq > 1 else None

    mask = pl.pallas_call(
        functools.partial(_dsa_select_kernel, tq=tq, tk=tk, nkt=nkt, past=grp.past, topk=topk),
        grid=(grp.batch, nq),
        in_specs=[
            pl.BlockSpec((tq, qiw), lambda b, i: (q_blk0 + b * nq + i, cols[0] // qiw)),
            pl.BlockSpec((tq, HEAD_DIM), lambda b, i: (q_blk0 + b * nq + i, 0)),
            pl.BlockSpec((None, nkt, 2 * IDX_DIM, 2 * tk), lambda b, i: (b, 0, 0, 0), pipeline_mode=resident),
        ],
        out_specs=pl.BlockSpec((None, None, nkt, tq, tk), lambda b, i: (b, i, 0, 0, 0)),
        out_shape=jax.ShapeDtypeStruct((grp.batch, nq, nkt, tq, tk), BF16),
        scratch_shapes=[
            pltpu.VMEM((nkt + 1, tq, tk), jnp.int32),
            pltpu.VMEM((N_IDX_HEADS, tq, HEAD_DIM), F32),
        ],
        compiler_params=_params("parallel", "parallel"),
        name="dsa_select",
    )(proj, tail, ki2)

    wp = hp * HEAD_DIM
    kv_args, kv_specs = _kv_operands(grp, proj, cols[2], cols[3], k_cache, v_cache, tk=tk, hp=hp)
    prev_args, prev_specs, aliased = _chained(o_prev)
    n_in = 1 + len(kv_args) + 2
    return pl.pallas_call(
        functools.partial(_dsa_flash_kernel, tq=tq, tk=tk, hp=hp, n_heads=H, past=grp.past, has_past=grp.past > 0),
        grid=(grp.batch, H // hp, nq),
        in_specs=[pl.BlockSpec((tq, wp), lambda b, h, i: (q_blk0 + b * nq + i, cols[1] // wp + h))]
        + kv_specs
        + [pl.BlockSpec((None, None, nkt, tq, tk), lambda b, h, i: (b, i, 0, 0, 0)),
           pl.BlockSpec((hp, 2, tq, tk), lambda b, h, i: (h, 0, 0, 0), pipeline_mode=resident)]
        + prev_specs,
        out_specs=pl.BlockSpec((tq, wp), lambda b, h, i: (q_blk0 + b * nq + i, o_col // wp + h)),
        out_shape=jax.ShapeDtypeStruct(o_shape, BF16),
        input_output_aliases={n_in: 0} if aliased else {},
        scratch_shapes=[pltpu.VMEM((tq, wp), F32), pltpu.VMEM((tq, wp), F32), pltpu.VMEM((tq, wp), F32)],
        compiler_params=_params("parallel", "parallel", "arbitrary"),
        name="dsa_flash",
    )(proj, *kv_args, mask, _near_bias(rel_bias, tq=tq, tk=tk), *prev_args)


def _cross_kernel(q_ref, k_ref, v_ref, *rest, n_heads, scale):
    o_ref = rest[-1]
    for hh in range(n_heads):
        sl = slice(hh * HEAD_DIM, (hh + 1) * HEAD_DIM)
        s = _nt_dot(q_ref[:, sl], k_ref[:, sl]) * scale
        p = jnp.exp(s - jnp.max(s, axis=-1, keepdims=True))
        denom = jnp.sum(p, axis=-1, keepdims=True)
        o = jnp.dot(p.astype(BF16), v_ref[:, sl], preferred_element_type=F32)
        o_ref[:, sl] = (o / denom).astype(o_ref.dtype)


def _cross_attention(grp, q, mem_k, mem_v, o_prev, *, tq=512):
    M, W = q.shape
    n_mem = mem_k.shape[1]
    tq = _pick_tile(grp.seq, tq)
    nq = grp.seq // tq
    assert grp.row0 % tq == 0
    blk0 = grp.row0 // tq
    prev_args, prev_specs, aliased = _chained(o_prev)
    return pl.pallas_call(
        functools.partial(_cross_kernel, n_heads=W // HEAD_DIM, scale=HEAD_DIM ** -0.5),
        grid=(grp.batch, nq),
        in_specs=[
            pl.BlockSpec((tq, W), lambda b, i: (blk0 + b * nq + i, 0)),
            pl.BlockSpec((None, n_mem, W), lambda b, i: (b, 0, 0)),
            pl.BlockSpec((None, n_mem, W), lambda b, i: (b, 0, 0)),
        ] + prev_specs,
        out_specs=pl.BlockSpec((tq, W), lambda b, i: (blk0 + b * nq + i, 0)),
        out_shape=jax.ShapeDtypeStruct((M, W), BF16),
        input_output_aliases={3: 0} if aliased else {},
        compiler_params=_params("parallel", "parallel"),
        name="cross_attention",
    )(q, mem_k, mem_v, *prev_args)


CONV_ROW_CHUNK = 128
SUBLANES = 8


def _conv_kernel(a_ref, prev_ref, state_ref, w_ref, bias_ref, g_ref, *rest, tt):
    o_ref, buf_ref, q_ref, y_ref = rest[-4], rest[-3], rest[-2], rest[-1]
    D = a_ref.shape[-1]

    @pl.when(pl.program_id(1) == 0)
    def _():
        buf_ref[0:CONV_HALO, :] = state_ref[...]

    @pl.when(pl.program_id(1) > 0)
    def _():
        buf_ref[0:CONV_HALO, :] = prev_ref[...]

    buf_ref[CONV_HALO:CONV_HALO + tt, :] = a_ref[...]
    buf_ref[CONV_HALO + tt:CONV_HALO + tt + SUBLANES, :] = jnp.zeros((SUBLANES, D), F32)

    first = CONV_HALO - (CONV_WIDTH - 1)
    rc = min(CONV_ROW_CHUNK, tt)
    for c0 in range(0, D, HEAD_DIM):
        cs = slice(c0, c0 + HEAD_DIM)
        for t0 in range(0, tt, rc):
            out = jnp.zeros((rc, HEAD_DIM), F32)
            for r in range(SUBLANES):
                q = None
                for k in range(CONV_WIDTH):
                    if (first + k) % SUBLANES != r:
                        continue
                    base = t0 + first + k - r
                    term = buf_ref[base:base + rc + SUBLANES, cs] * w_ref[k:k + 1, cs]
                    q = term if q is None else q + term
                if r == 0:
                    out = out + q[0:rc]
                else:
                    q_ref[...] = q
                    out = out + q_ref[r:r + rc, :]
            y_ref[t0:t0 + rc, cs] = out + bias_ref[:, cs]
    y = _rms_rows(y_ref[...], g_ref[...])
    o_ref[...] = (y * (1.0 / (1.0 + jnp.exp(-y)))).astype(o_ref.dtype)


def _conv_norm_swish(grp, u, state, w_dw, b_dw, g_norm, o_prev):
    M, D = u.shape
    tt = _pick_tile(grp.seq, 256)
    nt = grp.seq // tt
    assert grp.seq % tt == 0 and tt % CONV_HALO == 0 and grp.row0 % tt == 0
    blk0 = grp.row0 // tt
    hb = tt // CONV_HALO
    prev_args, prev_specs, aliased = _chained(o_prev)
    return pl.pallas_call(
        functools.partial(_conv_kernel, tt=tt),
        grid=(grp.batch, nt),
        in_specs=[
            pl.BlockSpec((tt, D), lambda b, i: (blk0 + b * nt + i, 0)),
            pl.BlockSpec((CONV_HALO, D), lambda b, i: (jnp.maximum((blk0 + b * nt + i) * hb - 1, 0), 0)),
            pl.BlockSpec((None, CONV_HALO, D), lambda b, i: (b, 0, 0)),
            pl.BlockSpec((CONV_WIDTH, D), lambda b, i: (0, 0)),
            pl.BlockSpec((1, D), lambda b, i: (0, 0)),
            pl.BlockSpec((1, D), lambda b, i: (0, 0)),
        ] + prev_specs,
        out_specs=pl.BlockSpec((tt, D), lambda b, i: (blk0 + b * nt + i, 0)),
        out_shape=jax.ShapeDtypeStruct((M, D), BF16),
        input_output_aliases={6: 0} if aliased else {},
        scratch_shapes=[pltpu.VMEM((tt + CONV_HALO + SUBLANES, D), F32),
                        pltpu.VMEM((min(CONV_ROW_CHUNK, tt) + SUBLANES, HEAD_DIM), F32),
                        pltpu.VMEM((tt, D), F32)],
        compiler_params=_params("parallel", "arbitrary"),
        name="conv_norm_swish",
    )(u, u, state, w_dw, b_dw.reshape(1, D), g_norm.reshape(1, D), *prev_args)


def kernel(x_prompt, x_sample, mem_prompt, cache_sb_k, cache_sb_v, cache_dsa_k, cache_dsa_v, cache_idx_k, cache_mem_k, cache_mem_v, state_conv, norm_mix, norm_cross, norm_mlp, norm_final, w_in_ab, w_out_ab, rel_bias, w_pw1, b_pw1, w_dw, b_dw, g_conv_norm, w_pw2, b_pw2, w_cq, w_mk, w_mv, w_co, w_up, w_down):
    Bp, Tp, D = x_prompt.shape
    Bs, Ts, _ = x_sample.shape
    past = cache_sb_k.shape[2]
    depth = norm_mix.shape[0]
    n_mem = mem_prompt.shape[1]
    n_sb, n_dsa = cache_sb_k.shape[3], cache_dsa_k.shape[3]
    sbw, dsw = n_sb * HEAD_DIM, n_dsa * HEAD_DIM
    qiw = N_IDX_HEADS * IDX_DIM
    cw = w_cq.shape[2]
    Mp, Ms = Bp * Tp, Bs * Ts
    M = Mp + Ms
    prompt = _Group(Bp, Tp, 0, 0)
    sample = _Group(Bs, Ts, Mp, past)

    x = jnp.concatenate([x_prompt.reshape(Mp, D), x_sample.reshape(Ms, D)], axis=0)
    mem_flat = mem_prompt.reshape(Bp * n_mem, D)

    src = dict(zip(("q_sb", "k_sb", "v_sb", "q_d", "k_d", "v_d", "q_i", "k_i", "w_i"),
                   np.cumsum([0, sbw, sbw, sbw, dsw, dsw, dsw, qiw, IDX_DIM])))
    order = ("k_sb", "v_sb", "k_d", "v_d", "q_sb", "q_d", "q_i")
    width = dict(q_sb=sbw, k_sb=sbw, v_sb=sbw, q_d=dsw, k_d=dsw, v_d=dsw, q_i=qiw)
    col = dict(zip(order, np.cumsum([0] + [width[n] for n in order])[:-1]))
    col = {k: int(v) for k, v in col.items()}
    n32 = 2 * sbw + 2 * dsw

    outs = {k: [] for k in ("p_sb_k", "p_sb_v", "p_d_k", "p_d_v", "p_ki", "p_mem_k", "p_mem_v", "p_conv",
                            "s_sb_k", "s_sb_v", "s_d_k", "s_d_v", "s_ki", "s_conv")}

    def split_rows(a):
        return a[:Mp].reshape(Bp, Tp, -1), a[Mp:].reshape(Bs, Ts, -1)

    def last_rows(a, grp, state, n):
        if grp.seq >= n:
            return jnp.stack([a[grp.row0 + (b + 1) * grp.seq - n:grp.row0 + (b + 1) * grp.seq]
                              for b in range(grp.batch)])
        rows = a[grp.row0:grp.row0 + grp.batch * grp.seq].reshape(grp.batch, grp.seq, -1)
        return jnp.concatenate([state, rows], axis=1)[:, -n:]

    w_out16, w_pw1_16, w_pw2_16 = w_out_ab.astype(BF16), w_pw1.astype(BF16), w_pw2.astype(BF16)
    w_cq16, w_mk16, w_mv16, w_co16 = (w.astype(BF16) for w in (w_cq, w_mk, w_mv, w_co))
    w_up16, w_down16 = w_up.astype(BF16), w_down.astype(BF16)
    q_scale = HEAD_DIM ** -0.5 * LOG2E
    col_scale = jnp.concatenate([jnp.full((width[n],), q_scale if n in ("q_sb", "q_d") else 1.0, F32)
                                 for n in order])

    for l in range(depth):
        i = l // 2
        if l % 2 == 0:
            w_in = w_in_ab[i]
            w_main = jnp.concatenate([w_in[:, src[n]:src[n] + width[n]] for n in order], axis=1).astype(BF16)
            tail_w = w_in.shape[1] - int(src["k_i"])
            w_tail = jnp.pad(w_in[:, src["k_i"]:], ((0, 0), (0, HEAD_DIM - tail_w))).astype(BF16)
            proj, kv32, tail = _in_proj(x, w_main, norm_mix[l], col_scale, w_tail, n32=n32)

            for name, key, nh in (("sb_k", "k_sb", n_sb), ("sb_v", "v_sb", n_sb),
                                  ("d_k", "k_d", n_dsa), ("d_v", "v_d", n_dsa)):
                ap, as_ = split_rows(kv32[:, col[key]:col[key] + width[key]])
                outs["p_" + name].append(ap.reshape(Bp, Tp, nh, HEAD_DIM))
                outs["s_" + name].append(as_.reshape(Bs, Ts, nh, HEAD_DIM))
            k_i = tail[:, :IDX_DIM]
            kip, kis = split_rows(k_i)
            outs["p_ki"].append(kip)
            outs["s_ki"].append(kis)

            tk_s = ATTN_TILE
            s_pad_s = -(-(past + Ts) // tk_s) * tk_s
            kidx_p = kip.astype(BF16)
            kidx_s = jnp.concatenate([cache_idx_k[i].astype(BF16), kis.astype(BF16),
                                      jnp.zeros((Bs, s_pad_s - past - Ts, IDX_DIM), BF16)], axis=1)

            o_shape = (M, sbw + dsw)
            sb_cols = (col["q_sb"], col["k_sb"], col["v_sb"])
            d_cols = (col["q_i"], col["q_d"], col["k_d"], col["v_d"])
            o = _sb_attention(prompt, proj, sb_cols, cache_sb_k[i], cache_sb_v[i], None, o_shape, 0,
                              hp=SB_HEADS_PER_STEP)
            o = _dsa_attention(prompt, proj, tail, d_cols, kidx_p, cache_dsa_k[i], cache_dsa_v[i], rel_bias,
                               o, o_shape, sbw, hp=DSA_HEADS_PER_STEP)
            o = _sb_attention(sample, proj, sb_cols, cache_sb_k[i], cache_sb_v[i], o, o_shape, 0,
                              hp=SB_HEADS_PER_STEP)
            o = _dsa_attention(sample, proj, tail, d_cols, kidx_s, cache_dsa_k[i], cache_dsa_v[i], rel_bias,
                               o, o_shape, sbw, hp=n_dsa)
            x, qc = _residual_matmul_then_proj(o, w_out16, i, x, None, norm_cross[l], w_cq16, l)
        else:
            u = _matmul(x, w_pw1_16, layer=i, gain=norm_mix[l], bias=b_pw1[i], glu=True)
            front = CONV_HALO - (CONV_WIDTH - 1)
            state_p = jnp.zeros((Bp, CONV_HALO, D), F32)
            state_s = jnp.concatenate([jnp.zeros((Bs, front, D), F32), state_conv[i]], axis=1)
            y = _conv_norm_swish(prompt, u, state_p, w_dw[i], b_dw[i], g_conv_norm[i], None)
            y = _conv_norm_swish(sample, u, state_s, w_dw[i], b_dw[i], g_conv_norm[i], y)
            outs["p_conv"].append(last_rows(u, prompt, state_p, CONV_WIDTH - 1))
            outs["s_conv"].append(last_rows(u, sample, state_s, CONV_WIDTH - 1))
            x, qc = _residual_matmul_then_proj(y, w_pw2_16, i, x, b_pw2[i], norm_cross[l], w_cq16, l)

        mk = _matmul(mem_flat, w_mk16, layer=l)
        mv = _matmul(mem_flat, w_mv16, layer=l)
        outs["p_mem_k"].append(mk.reshape(Bp, n_mem, cw // HEAD_DIM, HEAD_DIM))
        outs["p_mem_v"].append(mv.reshape(Bp, n_mem, cw // HEAD_DIM, HEAD_DIM))
        oc = _cross_attention(prompt, qc, mk.astype(BF16).reshape(Bp, n_mem, cw),
                              mv.astype(BF16).reshape(Bp, n_mem, cw), None)
        oc = _cross_attention(sample, qc, cache_mem_k[l].astype(BF16).reshape(Bs, n_mem, cw),
                              cache_mem_v[l].astype(BF16).reshape(Bs, n_mem, cw), oc)
        if l < depth - 1:
            x = _cross_out_mlp(x, oc, w_co16, norm_mlp[l], w_up16, w_down16, l)
        else:
            y_p, y_s = _cross_out_mlp(x, oc, w_co16, norm_mlp[l], w_up16, w_down16, l,
                                      final_gain=norm_final, m_prompt=Mp)

    st = lambda k: jnp.stack(outs[k])
    return (y_p.reshape(Bp, Tp, D), y_s.reshape(Bs, Ts, D), st("p_sb_k"), st("p_sb_v"), st("p_d_k"), st("p_d_v"),
            st("p_ki"), st("p_mem_k"), st("p_mem_v"), st("p_conv"), st("s_sb_k"), st("s_sb_v"), st("s_d_k"),
            st("s_d_v"), st("s_ki"), st("s_conv"))
```

```python
import functools
import math
from typing import NamedTuple

import numpy as np
import jax
import jax.numpy as jnp
from jax import lax
from jax.experimental import pallas as pl
from jax.experimental.pallas import tpu as pltpu

F32 = jnp.float32
BF16 = jnp.bfloat16

HEAD_DIM = 128
IDX_DIM = 64
N_IDX_HEADS = 16
CHUNK = 64
TOPK_MAX = 256
N_BUCKETS = 32
MAX_DISTANCE = 128
CONV_WIDTH = 31
CONV_HALO = 32
RMS_EPS = 1e-6
MASKED_LOGIT = -1e30
LOG2E = math.log2(math.e)
_INT32_MIN = -2 ** 31
VMEM_LIMIT_BYTES = 56 * 1024 * 1024
SB_HEADS_PER_STEP = 8
DSA_HEADS_PER_STEP = 4
SB_INTERLEAVE = 4
ATTN_TILE = 256
SELECT_ROWS_PER_STEP = 512
FAR_TILES_PER_TRIP = 4


class _Group(NamedTuple):
    batch: int
    seq: int
    row0: int
    past: int


def _params(*semantics):
    return pltpu.CompilerParams(dimension_semantics=semantics, vmem_limit_bytes=VMEM_LIMIT_BYTES)


def _pick_tile(n, pref):
    if n <= pref:
        return n
    t = pref
    while n % t:
        t //= 2
    return t


def _row_tile(m, pref):
    t = min(m, pref) // 8 * 8
    while m % t:
        t -= 8
    return t


def _chained(prev):
    if prev is None:
        return [], [], None
    return [prev], [pl.BlockSpec(memory_space=pl.ANY)], True


def _rms_rows(x, g):
    ms = jnp.mean(x * x, axis=-1, keepdims=True)
    return x * lax.rsqrt(ms + RMS_EPS) * g


def _mm_kernel(*refs, has_norm, has_bias, has_res, glu):
    it = iter(refs)
    x_ref = next(it)
    g_ref = next(it) if has_norm else None
    w_ref = next(it)
    w2_ref = next(it) if glu else None
    b_ref = next(it) if has_bias else None
    b2_ref = next(it) if (has_bias and glu) else None
    r_ref = next(it) if has_res else None
    o_ref = next(it)
    xn_ref = next(it) if has_norm else None

    if has_norm:
        @pl.when(pl.program_id(1) == 0)
        def _():
            xn_ref[...] = _rms_rows(x_ref[...], g_ref[...]).astype(BF16)
        xb = xn_ref[...]
    else:
        xb = x_ref[...].astype(BF16)

    y = jnp.dot(xb, w_ref[...], preferred_element_type=F32)
    if has_bias:
        y = y + b_ref[...]
    if glu:
        gate = jnp.dot(xb, w2_ref[...], preferred_element_type=F32)
        if has_bias:
            gate = gate + b2_ref[...]
        y = y * (1.0 / (1.0 + jnp.exp(-gate)))
    if has_res:
        y = y + r_ref[...]
    o_ref[...] = y.astype(o_ref.dtype)


def _matmul(x, w, *, layer=0, gain=None, bias=None, residual=None, glu=False, out_dtype=F32, tm=1152, tn=512):
    M, K = x.shape
    n_out = w.shape[2] // 2 if glu else w.shape[2]
    tm = _row_tile(M, tm)
    tn = _pick_tile(n_out, tn)
    nj = n_out // tn
    has_norm, has_bias, has_res = gain is not None, bias is not None, residual is not None

    args = [x]
    specs = [pl.BlockSpec((tm, K), lambda i, j: (i, 0))]
    if has_norm:
        args.append(gain.reshape(1, K))
        specs.append(pl.BlockSpec((1, K), lambda i, j: (0, 0)))
    args.append(w)
    specs.append(pl.BlockSpec((None, K, tn), lambda i, j: (layer, 0, j)))
    if glu:
        args.append(w)
        specs.append(pl.BlockSpec((None, K, tn), lambda i, j: (layer, 0, j + nj)))
    if has_bias:
        b2d = bias.reshape(1, -1)
        args.append(b2d)
        specs.append(pl.BlockSpec((1, tn), lambda i, j: (0, j)))
        if glu:
            args.append(b2d)
            specs.append(pl.BlockSpec((1, tn), lambda i, j: (0, j + nj)))
    if has_res:
        args.append(residual)
        specs.append(pl.BlockSpec((tm, tn), lambda i, j: (i, j)))

    return pl.pallas_call(
        functools.partial(_mm_kernel, has_norm=has_norm, has_bias=has_bias, has_res=has_res, glu=glu),
        grid=(M // tm, nj),
        in_specs=specs,
        out_specs=pl.BlockSpec((tm, tn), lambda i, j: (i, j)),
        out_shape=jax.ShapeDtypeStruct((M, n_out), out_dtype),
        scratch_shapes=[pltpu.VMEM((tm, K), BF16)] if has_norm else [],
        compiler_params=_params("parallel", "arbitrary"),
        name="norm_matmul",
    )(*args)


def _res_proj_kernel(y_ref, w_ref, *refs, has_bias):
    it = iter(refs)
    b_ref = next(it) if has_bias else None
    r_ref, g_ref, wq_ref, o_ref, q_ref, x1_ref = (next(it) for _ in range(6))
    j = pl.program_id(1)
    y = jnp.dot(y_ref[...], w_ref[...], preferred_element_type=F32) + r_ref[...]
    if has_bias:
        y = y + b_ref[...]
    o_ref[...] = y
    x1_ref[j] = y

    @pl.when(j == pl.num_programs(1) - 1)
    def _():
        x1 = jnp.concatenate([x1_ref[t] for t in range(x1_ref.shape[0])], axis=1)
        q_ref[...] = jnp.dot(_rms_rows(x1, g_ref[...]).astype(BF16), wq_ref[...],
                             preferred_element_type=F32).astype(q_ref.dtype)


def _residual_matmul_then_proj(y, w, layer, residual, bias, gain, wq, q_layer, *, tm=1152, tn=512):
    M, K = y.shape
    N, nq = w.shape[2], wq.shape[2]
    tm = _row_tile(M, tm)
    assert N % tn == 0
    nj = N // tn
    has_bias = bias is not None
    args = [y, w]
    specs = [pl.BlockSpec((tm, K), lambda i, j: (i, 0)), pl.BlockSpec((None, K, tn), lambda i, j: (layer, 0, j))]
    if has_bias:
        args.append(bias.reshape(1, N))
        specs.append(pl.BlockSpec((1, tn), lambda i, j: (0, j)))
    args += [residual, gain.reshape(1, N), wq]
    specs += [pl.BlockSpec((tm, tn), lambda i, j: (i, j)), pl.BlockSpec((1, N), lambda i, j: (0, 0)),
              pl.BlockSpec((None, N, nq), lambda i, j: (q_layer, 0, 0))]
    return pl.pallas_call(
        functools.partial(_res_proj_kernel, has_bias=has_bias),
        grid=(M // tm, nj),
        in_specs=specs,
        out_specs=[pl.BlockSpec((tm, tn), lambda i, j: (i, j)), pl.BlockSpec((tm, nq), lambda i, j: (i, 0))],
        out_shape=[jax.ShapeDtypeStruct((M, N), F32), jax.ShapeDtypeStruct((M, nq), BF16)],
        scratch_shapes=[pltpu.VMEM((nj, tm, tn), F32)],
        compiler_params=_params("parallel", "arbitrary"),
        name="residual_matmul_proj",
    )(*args)


def _in_proj_kernel(x_ref, g_ref, w_ref, s_ref, wt_ref, o16_ref, o32_ref, ot_ref, xn_ref, *, n32_tiles):
    j = pl.program_id(1)

    @pl.when(j == 0)
    def _():
        xn_ref[...] = _rms_rows(x_ref[...], g_ref[...]).astype(BF16)
        ot_ref[...] = jnp.dot(xn_ref[...], wt_ref[...], preferred_element_type=F32)

    y = jnp.dot(xn_ref[...], w_ref[...], preferred_element_type=F32)
    o16_ref[...] = (y * s_ref[...]).astype(BF16)

    @pl.when(j < n32_tiles)
    def _():
        o32_ref[...] = y


def _in_proj(x, w, gain, col_scale, w_tail, *, n32, tm=1152, tn=512):
    M, K = x.shape
    N = w.shape[1]
    nt = w_tail.shape[1]
    tm = _row_tile(M, tm)
    assert N % tn == 0 and n32 % tn == 0
    n32_tiles = n32 // tn
    return pl.pallas_call(
        functools.partial(_in_proj_kernel, n32_tiles=n32_tiles),
        grid=(M // tm, N // tn),
        in_specs=[
            pl.BlockSpec((tm, K), lambda i, j: (i, 0)),
            pl.BlockSpec((1, K), lambda i, j: (0, 0)),
            pl.BlockSpec((K, tn), lambda i, j: (0, j)),
            pl.BlockSpec((1, tn), lambda i, j: (0, j)),
            pl.BlockSpec((K, nt), lambda i, j: (0, 0)),
        ],
        out_specs=[
            pl.BlockSpec((tm, tn), lambda i, j: (i, j)),
            pl.BlockSpec((tm, tn), lambda i, j: (i, jnp.minimum(j, n32_tiles - 1))),
            pl.BlockSpec((tm, nt), lambda i, j: (i, 0)),
        ],
        out_shape=[jax.ShapeDtypeStruct((M, N), BF16), jax.ShapeDtypeStruct((M, n32), F32),
                   jax.ShapeDtypeStruct((M, nt), F32)],
        scratch_shapes=[pltpu.VMEM((tm, K), BF16)],
        compiler_params=_params("parallel", "arbitrary"),
        name="in_proj",
    )(x, gain.reshape(1, K), w, col_scale.reshape(1, N), w_tail)


def _mlp_kernel(x_ref, oc_ref, wco_ref, g_ref, wu_ref, wd_ref, *refs, n_prompt_tiles):
    final = n_prompt_tiles is not None
    if final:
        gf_ref, op_ref, os_ref, x2_ref, xn_ref, acc_ref = refs
    else:
        o_ref, x2_ref, xn_ref, acc_ref = refs
    i, f = pl.program_id(0), pl.program_id(1)

    @pl.when(f == 0)
    def _():
        x2 = x_ref[...] + jnp.dot(oc_ref[...], wco_ref[...], preferred_element_type=F32)
        x2_ref[...] = x2
        xn_ref[...] = _rms_rows(x2, g_ref[...]).astype(BF16)
        acc_ref[...] = jnp.zeros_like(acc_ref)

    h = jnp.dot(xn_ref[...], wu_ref[...], preferred_element_type=F32)
    h = jnp.maximum(h, 0.0)
    h = (h * h).astype(BF16)
    acc_ref[...] += jnp.dot(h, wd_ref[...], preferred_element_type=F32)

    last = f == pl.num_programs(1) - 1
    if not final:
        @pl.when(last)
        def _():
            o_ref[...] = x2_ref[...] + acc_ref[...]
    else:
        @pl.when(last & (i < n_prompt_tiles))
        def _():
            op_ref[...] = _rms_rows(x2_ref[...] + acc_ref[...], gf_ref[...])

        @pl.when(last & (i >= n_prompt_tiles))
        def _():
            os_ref[...] = _rms_rows(x2_ref[...] + acc_ref[...], gf_ref[...])


def _cross_out_mlp(x, oc, w_co, gain, w_up, w_down, layer, *, final_gain=None, m_prompt=None, tm=512, tf=512):
    M, D = x.shape
    F = w_up.shape[2]
    cw = oc.shape[1]
    final = final_gain is not None
    tm = _pick_tile(M - m_prompt, tm) if final else _pick_tile(M, tm)
    tf = _pick_tile(F, tf)
    args = [x, oc, w_co, gain.reshape(1, D), w_up, w_down]
    specs = [
        pl.BlockSpec((tm, D), lambda i, f: (i, 0)),
        pl.BlockSpec((tm, cw), lambda i, f: (i, 0)),
        pl.BlockSpec((None, cw, D), lambda i, f: (layer, 0, 0)),
        pl.BlockSpec((1, D), lambda i, f: (0, 0)),
        pl.BlockSpec((None, D, tf), lambda i, f: (layer, 0, f)),
        pl.BlockSpec((None, tf, D), lambda i, f: (layer, f, 0)),
    ]
    if final:
        assert m_prompt % tm == 0 and (M - m_prompt) % tm == 0
        npt = m_prompt // tm
        args.append(final_gain.reshape(1, D))
        specs.append(pl.BlockSpec((1, D), lambda i, f: (0, 0)))
        out_specs = [pl.BlockSpec((tm, D), lambda i, f: (jnp.minimum(i, npt - 1), 0)),
                     pl.BlockSpec((tm, D), lambda i, f: (jnp.maximum(i - npt, 0), 0))]
        out_shape = [jax.ShapeDtypeStruct((m_prompt, D), F32), jax.ShapeDtypeStruct((M - m_prompt, D), F32)]
        semantics = ("arbitrary", "arbitrary")
    else:
        npt = None
        out_specs = pl.BlockSpec((tm, D), lambda i, f: (i, 0))
        out_shape = jax.ShapeDtypeStruct((M, D), F32)
        semantics = ("parallel", "arbitrary")
    return pl.pallas_call(
        functools.partial(_mlp_kernel, n_prompt_tiles=npt),
        grid=(M // tm, F // tf),
        in_specs=specs,
        out_specs=out_specs,
        out_shape=out_shape,
        scratch_shapes=[pltpu.VMEM((tm, D), F32), pltpu.VMEM((tm, D), BF16), pltpu.VMEM((tm, D), F32)],
        compiler_params=_params(*semantics),
        name="cross_out_mlp",
    )(*args)


def _nt_dot(a, b):
    return lax.dot_general(a, b, (((1,), (1,)), ((), ())), preferred_element_type=F32)


def _kv_operands(grp, proj, k_col, v_col, k_cache, v_cache, *, tk, hp):
    wp = hp * HEAD_DIM
    if grp.past == 0:
        blk0 = grp.row0 // grp.seq
        args = [proj, proj]
        specs = [pl.BlockSpec((grp.seq, wp), lambda b, h, i, c=k_col // wp: (blk0 + b, c + h),
                              pipeline_mode=pl.Buffered(1)),
                 pl.BlockSpec((grp.seq, wp), lambda b, h, i, c=v_col // wp: (blk0 + b, c + h),
                              pipeline_mode=pl.Buffered(1))]
        return args, specs

    def new_tile(col):
        rows = lax.slice(proj, (grp.row0, col), (grp.row0 + grp.batch * grp.seq, col + k_cache.shape[2] * HEAD_DIM))
        rows = rows.reshape(grp.batch, grp.seq, -1)
        return jnp.pad(rows, ((0, 0), (0, tk - grp.seq), (0, 0)))

    args = [new_tile(k_col), new_tile(v_col)]
    specs = [pl.BlockSpec((None, tk, wp), lambda b, h, i: (b, 0, h))] * 2
    n_heads = k_cache.shape[2]
    for cache in (k_cache, v_cache):
        args.append(cache.reshape(grp.batch, grp.past * n_heads, HEAD_DIM))
        specs.append(pl.BlockSpec((None, grp.past * n_heads, HEAD_DIM), lambda b, h, i: (b, 0, 0)))
    return args, specs


def _split_kv_refs(refs, has_past):
    if not has_past:
        return refs[0], refs[1], None, None, refs[2:]
    return refs[0], refs[1], refs[2], refs[3], refs[4:]


def _tile_reader(new_ref, past_ref, *, tk, hp, n_heads):
    def read(j, hh, diag, n=1):
        hs = slice(hh * HEAD_DIM, (hh + 1) * HEAD_DIM)
        if past_ref is None:
            return new_ref[pl.ds(pl.multiple_of(j * tk, tk), n * tk), hs]
        if diag:
            return new_ref[:, hs]
        head = pl.program_id(1) * hp + hh
        return past_ref[pl.ds(j * (tk * n_heads) + head, n * tk, stride=n_heads), :].astype(BF16)
    return read


def _sb_kernel(q_ref, *refs, tq, tk, hp, n_heads, past, has_past):
    k_new, v_new, k_past, v_past, rest = _split_kv_refs(refs, has_past)
    u_ref, o_ref, acc_ref, carry_ref = rest[0], rest[-3], rest[-2], rest[-1]
    read_k = _tile_reader(k_new, k_past, tk=tk, hp=hp, n_heads=n_heads)
    read_v = _tile_reader(v_new, v_past, tk=tk, hp=hp, n_heads=n_heads)

    i = pl.program_id(2)
    q_start = past + i * tq
    jd = q_start // tk
    reps = tk // HEAD_DIM

    acc_ref[...] = jnp.zeros_like(acc_ref)
    carry_ref[...] = jnp.zeros_like(carry_ref)

    def step(j, masked):
        if masked:
            t_pos = q_start + lax.broadcasted_iota(jnp.int32, (tq, tk), 0)
            s_pos = j * tk + lax.broadcasted_iota(jnp.int32, (tq, tk), 1)
            causal = t_pos > s_pos
        for h0 in range(0, hp, SB_INTERLEAVE):
            group_step(j, masked, causal if masked else None, range(h0, h0 + SB_INTERLEAVE))

    def group_step(j, masked, causal, group):
        heads = [(hh, slice(hh * HEAD_DIM, (hh + 1) * HEAD_DIM)) for hh in group]
        z2s = [_nt_dot(q_ref[:, hs], read_k(j, hh, masked)) for hh, hs in heads]
        ns, log2_betas, css = [], [], []
        for z2 in z2s:
            neg_abs = lax.bitcast_convert_type(
                lax.bitcast_convert_type(z2, jnp.int32) | jnp.int32(_INT32_MIN), F32)
            n = jnp.maximum(z2, 0.0) + jnp.log(1.0 + jnp.exp2(neg_abs)) * LOG2E
            log2_betas.append(z2 - n)
            if masked:
                n = jnp.where(causal, n, 0.0)
            n16 = n.astype(BF16)
            css.append(jnp.dot(n16, u_ref[...], preferred_element_type=F32))
            ns.append(n16)
        for (hh, hs), n16, log2_beta, cs in zip(heads, ns, log2_betas, css):
            a = jnp.exp2(log2_beta - cs)
            if masked:
                a = jnp.where(causal, a, 0.0)
            carry = carry_ref[:, hs]
            pv = jnp.dot(a.astype(BF16), read_v(j, hh, masked), preferred_element_type=F32)
            acc_ref[:, hs] += jnp.exp2(-carry) * pv
            carry_ref[:, hs] = carry + jnp.broadcast_to(cs[:, 0:1] + n16[:, 0:1].astype(F32), (tq, HEAD_DIM))

    step(jd, True)

    def body(t, c):
        step(jd - 1 - t, False)
        return c

    lax.fori_loop(0, jd, body, 0)
    o_ref[...] = acc_ref[...].astype(o_ref.dtype)


def _suffix_sum_matrix(tk):
    r = np.arange(tk)
    return jnp.asarray((r[:, None] > r[None, :]).astype(np.float32), dtype=BF16)


def _attn_tiles(grp):
    tq = _pick_tile(grp.seq, ATTN_TILE)
    tk = ATTN_TILE if grp.past else tq
    nq = grp.seq // tq
    assert grp.seq % tq == 0 and tk % tq == 0 and grp.past % tk == 0 and grp.row0 % tq == 0
    assert grp.past == 0 or nq == 1
    assert grp.past > 0 or grp.row0 % grp.seq == 0
    return tq, tk, nq


def _sb_attention(grp, proj, cols, k_cache, v_cache, o_prev, o_shape, o_col, *, hp):
    H = k_cache.shape[2]
    tq, tk, nq = _attn_tiles(grp)
    wp = hp * HEAD_DIM
    q_blk0 = grp.row0 // tq
    kv_args, kv_specs = _kv_operands(grp, proj, cols[1], cols[2], k_cache, v_cache, tk=tk, hp=hp)
    prev_args, prev_specs, aliased = _chained(o_prev)
    n_in = 1 + len(kv_args) + 1
    return pl.pallas_call(
        functools.partial(_sb_kernel, tq=tq, tk=tk, hp=hp, n_heads=H, past=grp.past, has_past=grp.past > 0),
        grid=(grp.batch, H // hp, nq),
        in_specs=[pl.BlockSpec((tq, wp), lambda b, h, i: (q_blk0 + b * nq + i, cols[0] // wp + h))]
        + kv_specs
        + [pl.BlockSpec((tk, tk), lambda b, h, i: (0, 0), pipeline_mode=pl.Buffered(1))]
        + prev_specs,
        out_specs=pl.BlockSpec((tq, wp), lambda b, h, i: (q_blk0 + b * nq + i, o_col // wp + h)),
        out_shape=jax.ShapeDtypeStruct(o_shape, BF16),
        input_output_aliases={n_in: 0} if aliased else {},
        scratch_shapes=[pltpu.VMEM((tq, wp), F32), pltpu.VMEM((tq, wp), F32)],
        compiler_params=_params("parallel", "parallel", "arbitrary"),
        name="sb_attention",
    )(proj, *kv_args, _suffix_sum_matrix(tk), *prev_args)


def _sortable_key(x):
    b = lax.bitcast_convert_type(x, jnp.int32)
    return b ^ ((b >> 31) & jnp.int32(0x7FFFFFFF))


_KEY_NEG_INF = int(np.int32(np.array(-np.inf, np.float32).view(np.int32) ^ 0x7FFFFFFF))
_KEY_LOWEST_FINITE = int(np.int32(np.array(np.finfo(np.float32).min, np.float32).view(np.int32) ^ 0x7FFFFFFF))
COUNT_ROW_GROUP = 128


def _dsa_select_kernel(qi_ref, wi_ref, ki2_ref, mask_ref, keys_ref, wb_ref,
                       *, tq, tk, nkt, nsub, by_batch, past, topk):
    i = pl.program_id(1)
    rows = nsub * tq
    first_start = past + i * tq * (1 if by_batch else nsub)
    jd_min = first_start // tk
    n_cut = 1 if by_batch else nsub
    jd = jd_min + n_cut - 1
    reps = tk // HEAD_DIM

    w = wi_ref[:, IDX_DIM:IDX_DIM + N_IDX_HEADS] * (IDX_DIM ** -0.5 * N_IDX_HEADS ** -0.5)
    for hh in range(N_IDX_HEADS):
        wb_ref[hh] = jnp.broadcast_to(w[:, hh:hh + 1], (rows, HEAD_DIM))

    def idx_rows(j, r0, nr, kt):
        tot = jnp.zeros((nr, tk), F32)
        for p in range(N_IDX_HEADS // 2):
            sc = jnp.dot(qi_ref[r0:r0 + nr, p * 2 * IDX_DIM:(p + 1) * 2 * IDX_DIM], kt,
                         preferred_element_type=F32)
            w0 = jnp.concatenate([wb_ref[2 * p, r0:r0 + nr, :]] * reps, axis=1)
            w1 = jnp.concatenate([wb_ref[2 * p + 1, r0:r0 + nr, :]] * reps, axis=1)
            tot = tot + w0 * jnp.maximum(sc[:, :tk], 0.0) + w1 * jnp.maximum(sc[:, tk:], 0.0)
        return _sortable_key(tot)

    def idx_tile(j):
        if by_batch:
            return jnp.concatenate([idx_rows(j, s * tq, tq, ki2_ref[s, j]) for s in range(nsub)], axis=0)
        return idx_rows(j, 0, rows, ki2_ref[0, j])

    def idx_body(j, c):
        keys_ref[j] = idx_tile(j)
        return c

    lax.fori_loop(0, jd_min, idx_body, 0)
    row = lax.broadcasted_iota(jnp.int32, (rows, tk), 0)
    t_pos = first_start + (row % tq if by_batch else row)
    for d in range(n_cut):
        s_pos = (jd_min + d) * tk + lax.broadcasted_iota(jnp.int32, (rows, tk), 1)
        visible = (t_pos // CHUNK) >= (s_pos // CHUNK)
        keys_ref[jd_min + d] = jnp.where(visible, idx_tile(jd_min + d), jnp.int32(_KEY_NEG_INF))

    n_vis = jd + 1

    @pl.when(n_vis % 2 == 1)
    def _():
        keys_ref[n_vis] = jnp.full((rows, tk), _INT32_MIN, jnp.int32)

    n_pairs = (n_vis + 1) // 2
    rg = min(COUNT_ROW_GROUP, rows)

    def count_ge(cand):
        parts = []
        for g in range(rows // rg):
            cg = cand[g * rg:(g + 1) * rg]

            def body(t, cnt, g=g, cg=cg):
                for jj in (2 * t, 2 * t + 1):
                    kj = keys_ref[jj, pl.ds(g * rg, rg), :]
                    for r in range(reps):
                        cnt = cnt + jnp.where(kj[:, r * HEAD_DIM:(r + 1) * HEAD_DIM] >= cg, 1.0, 0.0)
                return cnt

            parts.append(lax.fori_loop(0, n_pairs, body, jnp.zeros((rg, HEAD_DIM), F32)))
        return jnp.sum(jnp.concatenate(parts, axis=0), axis=-1, keepdims=True)

    kf = float(topk)
    zero = jnp.zeros((rows, HEAD_DIM), jnp.int32)
    c = count_ge(zero)
    res = jnp.where(c >= kf, zero, jnp.int32(_INT32_MIN))
    cnt = jnp.where(c >= kf, c, float(tk) * (2 * n_pairs).astype(F32))
    for bit in range(30, -1, -1):
        cand = res + jnp.int32(1 << bit)
        c = count_ge(cand)
        res = jnp.where(c >= kf, cand, res)
        cnt = jnp.where(c >= kf, c, cnt)

    tied_cut = jnp.max(jnp.where((cnt != kf) & (res[:, 0:1] > jnp.int32(_KEY_NEG_INF)), 1.0, 0.0)) > 0.0

    @pl.when(tied_cut)
    def _():
        col = lax.broadcasted_iota(jnp.int32, (rg, tk), 1)

        def count_tied(before, strict_above):
            parts = []
            for g in range(rows // rg):
                grp_rows = slice(g * rg, (g + 1) * rg)
                res_g = jnp.concatenate([res[grp_rows]] * reps, axis=1)
                bef_g = None if strict_above else jnp.concatenate([before[grp_rows]] * reps, axis=1)

                def body(j, acc, g=g, res_g=res_g, bef_g=bef_g):
                    kj = keys_ref[j, pl.ds(g * rg, rg), :]
                    if strict_above:
                        hit = jnp.where(kj > res_g, 1.0, 0.0)
                    else:
                        hit = jnp.where(kj == res_g, jnp.where(j * tk + col < bef_g, 1.0, 0.0), 0.0)
                    return acc + sum(hit[:, r * HEAD_DIM:(r + 1) * HEAD_DIM] for r in range(reps))

                parts.append(lax.fori_loop(0, n_vis, body, jnp.zeros((rg, HEAD_DIM), F32)))
            return jnp.sum(jnp.concatenate(parts, axis=0), axis=-1, keepdims=True)

        need = kf - count_tied(None, True)
        r_keep = jnp.zeros((rows, HEAD_DIM), jnp.int32)
        for bit in range((nkt * tk).bit_length() - 1, -1, -1):
            cand = r_keep | jnp.int32(1 << bit)
            r_keep = jnp.where(count_tied(cand, False) < need, cand, r_keep)

        def drop_late_ties(j, c):
            for g in range(rows // rg):
                grp_rows = slice(g * rg, (g + 1) * rg)
                res_g = jnp.concatenate([res[grp_rows]] * reps, axis=1)
                keep_g = jnp.concatenate([r_keep[grp_rows]] * reps, axis=1)
                kj = keys_ref[j, pl.ds(g * rg, rg), :]
                keys_ref[j, pl.ds(g * rg, rg), :] = jnp.where(
                    kj == res_g, jnp.where(j * tk + col > keep_g, res_g - 1, kj), kj)
            return c

        lax.fori_loop(0, n_vis, drop_late_ties, 0)

    thr = jnp.maximum(res, jnp.int32(_KEY_LOWEST_FINITE))
    thr = jnp.concatenate([thr] * reps, axis=1)

    def write_mask(j, c):
        tile = jnp.where(keys_ref[j] >= thr, 0.0, MASKED_LOGIT).astype(mask_ref.dtype)
        for s_ in range(nsub):
            mask_ref[s_, j] = tile[s_ * tq:(s_ + 1) * tq]
        return c

    lax.fori_loop(0, jd + 1, write_mask, 0)

    def write_hidden(j, c):
        for s_ in range(nsub):
            mask_ref[s_, j] = jnp.full((tq, tk), MASKED_LOGIT, mask_ref.dtype)
        return c

    lax.fori_loop(jd + 1, nkt, write_hidden, 0)


def _dsa_flash_kernel(q_ref, *refs, tq, tk, hp, n_heads, past, has_past):
    k_new, v_new, k_past, v_past, rest = _split_kv_refs(refs, has_past)
    mask_ref, bnear_ref = rest[0], rest[1]
    o_ref, m_ref, l_ref, acc_ref = rest[-4], rest[-3], rest[-2], rest[-1]
    read_k = _tile_reader(k_new, k_past, tk=tk, hp=hp, n_heads=n_heads)
    read_v = _tile_reader(v_new, v_past, tk=tk, hp=hp, n_heads=n_heads)

    i = pl.program_id(2)
    jd = (past + i * tq) // tk
    reps = tk // HEAD_DIM
    heads = [slice(hh * HEAD_DIM, (hh + 1) * HEAD_DIM) for hh in range(hp)]

    m_ref[...] = jnp.full_like(m_ref, MASKED_LOGIT)
    l_ref[...] = jnp.zeros_like(l_ref)
    acc_ref[...] = jnp.zeros_like(acc_ref)

    def step(j, near, n=1):
        diag = near == 0
        maskf = jnp.concatenate([mask_ref[j + t].astype(F32) for t in range(n)], axis=1)
        logits = []
        for hh, hs in enumerate(heads):
            s = _nt_dot(q_ref[:, hs], read_k(j, hh, diag, n)) + maskf
            if near is not None:
                s = s + bnear_ref[hh, near]
            logits.append(s)
        probs, alphas = [], []
        for hs, s in zip(heads, logits):
            m_old = m_ref[:, hs]
            m_new = jnp.maximum(m_old, jnp.max(s, axis=-1, keepdims=True))
            alpha = jnp.exp2(m_old - m_new)
            p = jnp.exp2(s - jnp.concatenate([m_new] * (n * reps), axis=1))
            l_ref[:, hs] = alpha * l_ref[:, hs] + sum(p[:, r * HEAD_DIM:(r + 1) * HEAD_DIM]
                                                      for r in range(n * reps))
            m_ref[:, hs] = m_new
            probs.append(p.astype(BF16))
            alphas.append(alpha)
        for hh, (hs, p, alpha) in enumerate(zip(heads, probs, alphas)):
            pv = jnp.dot(p, read_v(j, hh, diag, n), preferred_element_type=F32)
            acc_ref[:, hs] = alpha * acc_ref[:, hs] + pv

    step(jd, 0)

    @pl.when(jd >= 1)
    def _():
        step(jd - 1, 1)

    n_far = jnp.maximum(jd - 1, 0)
    n_trips = n_far // FAR_TILES_PER_TRIP

    def single(j, c):
        step(j, None)
        return c

    lax.fori_loop(n_trips * FAR_TILES_PER_TRIP, n_far, single, 0)

    def body(t, c):
        step(FAR_TILES_PER_TRIP * t, None, FAR_TILES_PER_TRIP)
        return c

    lax.fori_loop(0, n_trips, body, 0)
    for hs in heads:
        o_ref[:, hs] = (acc_ref[:, hs] / jnp.sum(l_ref[:, hs], axis=-1, keepdims=True)).astype(o_ref.dtype)


def _t5_bucket(rel):
    nb = N_BUCKETS // 2
    max_exact = nb // 2
    side = jnp.where(rel > 0, nb, 0)
    n = jnp.abs(rel)
    nf = jnp.maximum(n, 1).astype(F32)
    large = max_exact + (jnp.log(nf / max_exact) / math.log(MAX_DISTANCE / max_exact) * (nb - max_exact)).astype(jnp.int32)
    large = jnp.minimum(large, nb - 1)
    return side + jnp.where(n < max_exact, n, large)


def _near_bias(rel_bias, *, tq, tk):
    r = jnp.arange(tq, dtype=jnp.int32)[:, None]
    c = jnp.arange(tk, dtype=jnp.int32)[None, :]
    bucket = _t5_bucket(jnp.stack([c - r, c - r - tk]))
    far_bucket = _t5_bucket(jnp.int32(-2 * tk))
    table = (rel_bias - rel_bias[far_bucket][None, :]) * LOG2E
    onehot = (bucket[..., None] == jnp.arange(N_BUCKETS, dtype=jnp.int32)).astype(F32)
    return jnp.einsum("dqkb,bh->hdqk", onehot, table, precision=lax.Precision.HIGHEST)


def _dsa_attention(grp, proj, tail, cols, k_idx_all, k_cache, v_cache, rel_bias, o_prev, o_shape, o_col, *, hp):
    H = k_cache.shape[2]
    tq, tk, nq = _attn_tiles(grp)
    s_pad = k_idx_all.shape[1]
    nkt = s_pad // tk
    assert tk % CHUNK == 0 and tq % CHUNK == 0 and s_pad % tk == 0
    assert tk >= MAX_DISTANCE and H % hp == 0
    topk = min(TOPK_MAX, (grp.past + grp.seq) // 4)
    q_blk0 = grp.row0 // tq
    qiw = N_IDX_HEADS * IDX_DIM

    kt = jnp.transpose(k_idx_all.reshape(grp.batch, nkt, tk, IDX_DIM), (0, 1, 3, 2))
    z = jnp.zeros_like(kt)
    ki2 = jnp.concatenate([jnp.concatenate([kt, z], axis=3), jnp.concatenate([z, kt], axis=3)], axis=2)
    resident = pl.Buffered(1) if nq > 1 else None

    by_batch = nq == 1
    nsub = max(SELECT_ROWS_PER_STEP // tq, 1)
    while (grp.batch if by_batch else nq) % nsub or (grp.row0 // tq) % nsub or (not by_batch and tq != tk):
        nsub //= 2
    rows = nsub * tq
    sel_blk0 = grp.row0 // rows
    if by_batch:
        grid = (grp.batch // nsub, 1)
        row_map = lambda b, i: sel_blk0 + b
        ki2_spec = pl.BlockSpec((nsub, nkt, 2 * IDX_DIM, 2 * tk), lambda b, i: (b, 0, 0, 0))
        mask_spec = pl.BlockSpec((nsub, None, nkt, tq, tk), lambda b, i: (b, 0, 0, 0, 0))
    else:
        grid = (grp.batch, nq // nsub)
        row_map = lambda b, i: sel_blk0 + b * (nq // nsub) + i
        ki2_spec = pl.BlockSpec((1, nkt, 2 * IDX_DIM, 2 * tk), lambda b, i: (b, 0, 0, 0), pipeline_mode=resident)
        mask_spec = pl.BlockSpec((None, nsub, nkt, tq, tk), lambda b, i: (b, i, 0, 0, 0))

    mask = pl.pallas_call(
        functools.partial(_dsa_select_kernel, tq=tq, tk=tk, nkt=nkt, nsub=nsub, by_batch=by_batch,
                          past=grp.past, topk=topk),
        grid=grid,
        in_specs=[
            pl.BlockSpec((rows, qiw), lambda b, i: (row_map(b, i), cols[0] // qiw)),
            pl.BlockSpec((rows, HEAD_DIM), lambda b, i: (row_map(b, i), 0)),
            ki2_spec,
        ],
        out_specs=mask_spec,
        out_shape=jax.ShapeDtypeStruct((grp.batch, nq, nkt, tq, tk), BF16),
        scratch_shapes=[
            pltpu.VMEM((nkt + 1, rows, tk), jnp.int32),
            pltpu.VMEM((N_IDX_HEADS, rows, HEAD_DIM), F32),
        ],
        compiler_params=_params("parallel", "parallel"),
        name="dsa_select",
    )(proj, tail, ki2)

    wp = hp * HEAD_DIM
    kv_args, kv_specs = _kv_operands(grp, proj, cols[2], cols[3], k_cache, v_cache, tk=tk, hp=hp)
    prev_args, prev_specs, aliased = _chained(o_prev)
    n_in = 1 + len(kv_args) + 2
    return pl.pallas_call(
        functools.partial(_dsa_flash_kernel, tq=tq, tk=tk, hp=hp, n_heads=H, past=grp.past, has_past=grp.past > 0),
        grid=(grp.batch, H // hp, nq),
        in_specs=[pl.BlockSpec((tq, wp), lambda b, h, i: (q_blk0 + b * nq + i, cols[1] // wp + h))]
        + kv_specs
        + [pl.BlockSpec((None, None, nkt, tq, tk), lambda b, h, i: (b, i, 0, 0, 0)),
           pl.BlockSpec((hp, 2, tq, tk), lambda b, h, i: (h, 0, 0, 0), pipeline_mode=resident)]
        + prev_specs,
        out_specs=pl.BlockSpec((tq, wp), lambda b, h, i: (q_blk0 + b * nq + i, o_col // wp + h)),
        out_shape=jax.ShapeDtypeStruct(o_shape, BF16),
        input_output_aliases={n_in: 0} if aliased else {},
        scratch_shapes=[pltpu.VMEM((tq, wp), F32), pltpu.VMEM((tq, wp), F32), pltpu.VMEM((tq, wp), F32)],
        compiler_params=_params("parallel", "parallel", "arbitrary"),
        name="dsa_flash",
    )(proj, *kv_args, mask, _near_bias(rel_bias, tq=tq, tk=tk), *prev_args)


def _cross_kernel(q_ref, k_ref, v_ref, *rest, n_heads, scale):
    o_ref = rest[-1]
    for hh in range(n_heads):
        sl = slice(hh * HEAD_DIM, (hh + 1) * HEAD_DIM)
        s = _nt_dot(q_ref[:, sl], k_ref[:, sl]) * scale
        p = jnp.exp(s - jnp.max(s, axis=-1, keepdims=True))
        denom = jnp.sum(p, axis=-1, keepdims=True)
        o = jnp.dot(p.astype(BF16), v_ref[:, sl], preferred_element_type=F32)
        o_ref[:, sl] = (o / denom).astype(o_ref.dtype)


def _cross_attention(grp, q, mem_k, mem_v, o_prev, *, tq=512):
    M, W = q.shape
    n_mem = mem_k.shape[1]
    tq = _pick_tile(grp.seq, tq)
    nq = grp.seq // tq
    assert grp.row0 % tq == 0
    blk0 = grp.row0 // tq
    prev_args, prev_specs, aliased = _chained(o_prev)
    return pl.pallas_call(
        functools.partial(_cross_kernel, n_heads=W // HEAD_DIM, scale=HEAD_DIM ** -0.5),
        grid=(grp.batch, nq),
        in_specs=[
            pl.BlockSpec((tq, W), lambda b, i: (blk0 + b * nq + i, 0)),
            pl.BlockSpec((None, n_mem, W), lambda b, i: (b, 0, 0)),
            pl.BlockSpec((None, n_mem, W), lambda b, i: (b, 0, 0)),
        ] + prev_specs,
        out_specs=pl.BlockSpec((tq, W), lambda b, i: (blk0 + b * nq + i, 0)),
        out_shape=jax.ShapeDtypeStruct((M, W), BF16),
        input_output_aliases={3: 0} if aliased else {},
        compiler_params=_params("parallel", "parallel"),
        name="cross_attention",
    )(q, mem_k, mem_v, *prev_args)


CONV_ROW_CHUNK = 128
SUBLANES = 8


def _conv_kernel(a_ref, prev_ref, state_ref, w_ref, bias_ref, g_ref, *rest, tt):
    o_ref, buf_ref, q_ref, y_ref = rest[-4], rest[-3], rest[-2], rest[-1]
    D = a_ref.shape[-1]

    @pl.when(pl.program_id(1) == 0)
    def _():
        buf_ref[0:CONV_HALO, :] = state_ref[...]

    @pl.when(pl.program_id(1) > 0)
    def _():
        buf_ref[0:CONV_HALO, :] = prev_ref[...]

    buf_ref[CONV_HALO:CONV_HALO + tt, :] = a_ref[...]
    buf_ref[CONV_HALO + tt:CONV_HALO + tt + SUBLANES, :] = jnp.zeros((SUBLANES, D), F32)

    first = CONV_HALO - (CONV_WIDTH - 1)
    rc = min(CONV_ROW_CHUNK, tt)
    for c0 in range(0, D, HEAD_DIM):
        cs = slice(c0, c0 + HEAD_DIM)
        for t0 in range(0, tt, rc):
            out = jnp.zeros((rc, HEAD_DIM), F32)
            for r in range(SUBLANES):
                q = None
                for k in range(CONV_WIDTH):
                    if (first + k) % SUBLANES != r:
                        continue
                    base = t0 + first + k - r
                    term = buf_ref[base:base + rc + SUBLANES, cs] * w_ref[k:k + 1, cs]
                    q = term if q is None else q + term
                if r == 0:
                    out = out + q[0:rc]
                else:
                    q_ref[...] = q
                    out = out + q_ref[r:r + rc, :]
            y_ref[t0:t0 + rc, cs] = out + bias_ref[:, cs]
    y = _rms_rows(y_ref[...], g_ref[...])
    o_ref[...] = (y * (1.0 / (1.0 + jnp.exp(-y)))).astype(o_ref.dtype)


def _conv_norm_swish(grp, u, state, w_dw, b_dw, g_norm, o_prev):
    M, D = u.shape
    tt = _pick_tile(grp.seq, 256)
    nt = grp.seq // tt
    assert grp.seq % tt == 0 and tt % CONV_HALO == 0 and grp.row0 % tt == 0
    blk0 = grp.row0 // tt
    hb = tt // CONV_HALO
    prev_args, prev_specs, aliased = _chained(o_prev)
    return pl.pallas_call(
        functools.partial(_conv_kernel, tt=tt),
        grid=(grp.batch, nt),
        in_specs=[
            pl.BlockSpec((tt, D), lambda b, i: (blk0 + b * nt + i, 0)),
            pl.BlockSpec((CONV_HALO, D), lambda b, i: (jnp.maximum((blk0 + b * nt + i) * hb - 1, 0), 0)),
            pl.BlockSpec((None, CONV_HALO, D), lambda b, i: (b, 0, 0)),
            pl.BlockSpec((CONV_WIDTH, D), lambda b, i: (0, 0)),
            pl.BlockSpec((1, D), lambda b, i: (0, 0)),
            pl.BlockSpec((1, D), lambda b, i: (0, 0)),
        ] + prev_specs,
        out_specs=pl.BlockSpec((tt, D), lambda b, i: (blk0 + b * nt + i, 0)),
        out_shape=jax.ShapeDtypeStruct((M, D), BF16),
        input_output_aliases={6: 0} if aliased else {},
        scratch_shapes=[pltpu.VMEM((tt + CONV_HALO + SUBLANES, D), F32),
                        pltpu.VMEM((min(CONV_ROW_CHUNK, tt) + SUBLANES, HEAD_DIM), F32),
                        pltpu.VMEM((tt, D), F32)],
        compiler_params=_params("parallel", "arbitrary"),
        name="conv_norm_swish",
    )(u, u, state, w_dw, b_dw.reshape(1, D), g_norm.reshape(1, D), *prev_args)


def kernel(x_prompt, x_sample, mem_prompt, cache_sb_k, cache_sb_v, cache_dsa_k, cache_dsa_v, cache_idx_k, cache_mem_k, cache_mem_v, state_conv, norm_mix, norm_cross, norm_mlp, norm_final, w_in_ab, w_out_ab, rel_bias, w_pw1, b_pw1, w_dw, b_dw, g_conv_norm, w_pw2, b_pw2, w_cq, w_mk, w_mv, w_co, w_up, w_down):
    Bp, Tp, D = x_prompt.shape
    Bs, Ts, _ = x_sample.shape
    past = cache_sb_k.shape[2]
    depth = norm_mix.shape[0]
    n_mem = mem_prompt.shape[1]
    n_sb, n_dsa = cache_sb_k.shape[3], cache_dsa_k.shape[3]
    sbw, dsw = n_sb * HEAD_DIM, n_dsa * HEAD_DIM
    qiw = N_IDX_HEADS * IDX_DIM
    cw = w_cq.shape[2]
    Mp, Ms = Bp * Tp, Bs * Ts
    M = Mp + Ms
    prompt = _Group(Bp, Tp, 0, 0)
    sample = _Group(Bs, Ts, Mp, past)

    x = jnp.concatenate([x_prompt.reshape(Mp, D), x_sample.reshape(Ms, D)], axis=0)
    mem_flat = mem_prompt.reshape(Bp * n_mem, D)

    src = dict(zip(("q_sb", "k_sb", "v_sb", "q_d", "k_d", "v_d", "q_i", "k_i", "w_i"),
                   np.cumsum([0, sbw, sbw, sbw, dsw, dsw, dsw, qiw, IDX_DIM])))
    order = ("k_sb", "v_sb", "k_d", "v_d", "q_sb", "q_d", "q_i")
    width = dict(q_sb=sbw, k_sb=sbw, v_sb=sbw, q_d=dsw, k_d=dsw, v_d=dsw, q_i=qiw)
    col = dict(zip(order, np.cumsum([0] + [width[n] for n in order])[:-1]))
    col = {k: int(v) for k, v in col.items()}
    n32 = 2 * sbw + 2 * dsw

    outs = {k: [] for k in ("p_sb_k", "p_sb_v", "p_d_k", "p_d_v", "p_ki", "p_mem_k", "p_mem_v", "p_conv",
                            "s_sb_k", "s_sb_v", "s_d_k", "s_d_v", "s_ki", "s_conv")}

    def split_rows(a):
        return a[:Mp].reshape(Bp, Tp, -1), a[Mp:].reshape(Bs, Ts, -1)

    def last_rows(a, grp, state, n):
        if grp.seq >= n:
            return jnp.stack([a[grp.row0 + (b + 1) * grp.seq - n:grp.row0 + (b + 1) * grp.seq]
                              for b in range(grp.batch)])
        rows = a[grp.row0:grp.row0 + grp.batch * grp.seq].reshape(grp.batch, grp.seq, -1)
        return jnp.concatenate([state, rows], axis=1)[:, -n:]

    w_out16, w_pw1_16, w_pw2_16 = w_out_ab.astype(BF16), w_pw1.astype(BF16), w_pw2.astype(BF16)
    w_cq16, w_mk16, w_mv16, w_co16 = (w.astype(BF16) for w in (w_cq, w_mk, w_mv, w_co))
    w_up16, w_down16 = w_up.astype(BF16), w_down.astype(BF16)
    q_scale = HEAD_DIM ** -0.5 * LOG2E
    col_scale = jnp.concatenate([jnp.full((width[n],), q_scale if n in ("q_sb", "q_d") else 1.0, F32)
                                 for n in order])

    for l in range(depth):
        i = l // 2
        if l % 2 == 0:
            w_in = w_in_ab[i]
            w_main = jnp.concatenate([w_in[:, src[n]:src[n] + width[n]] for n in order], axis=1).astype(BF16)
            tail_w = w_in.shape[1] - int(src["k_i"])
            w_tail = jnp.pad(w_in[:, src["k_i"]:], ((0, 0), (0, HEAD_DIM - tail_w))).astype(BF16)
            proj, kv32, tail = _in_proj(x, w_main, norm_mix[l], col_scale, w_tail, n32=n32)

            for name, key, nh in (("sb_k", "k_sb", n_sb), ("sb_v", "v_sb", n_sb),
                                  ("d_k", "k_d", n_dsa), ("d_v", "v_d", n_dsa)):
                ap, as_ = split_rows(kv32[:, col[key]:col[key] + width[key]])
                outs["p_" + name].append(ap.reshape(Bp, Tp, nh, HEAD_DIM))
                outs["s_" + name].append(as_.reshape(Bs, Ts, nh, HEAD_DIM))
            k_i = tail[:, :IDX_DIM]
            kip, kis = split_rows(k_i)
            outs["p_ki"].append(kip)
            outs["s_ki"].append(kis)

            tk_s = ATTN_TILE
            s_pad_s = -(-(past + Ts) // tk_s) * tk_s
            kidx_p = kip.astype(BF16)
            kidx_s = jnp.concatenate([cache_idx_k[i].astype(BF16), kis.astype(BF16),
                                      jnp.zeros((Bs, s_pad_s - past - Ts, IDX_DIM), BF16)], axis=1)

            o_shape = (M, sbw + dsw)
            sb_cols = (col["q_sb"], col["k_sb"], col["v_sb"])
            d_cols = (col["q_i"], col["q_d"], col["k_d"], col["v_d"])
            o = _sb_attention(prompt, proj, sb_cols, cache_sb_k[i], cache_sb_v[i], None, o_shape, 0,
                              hp=SB_HEADS_PER_STEP)
            o = _dsa_attention(prompt, proj, tail, d_cols, kidx_p, cache_dsa_k[i], cache_dsa_v[i], rel_bias,
                               o, o_shape, sbw, hp=DSA_HEADS_PER_STEP)
            o = _sb_attention(sample, proj, sb_cols, cache_sb_k[i], cache_sb_v[i], o, o_shape, 0,
                              hp=SB_HEADS_PER_STEP)
            o = _dsa_attention(sample, proj, tail, d_cols, kidx_s, cache_dsa_k[i], cache_dsa_v[i], rel_bias,
                               o, o_shape, sbw, hp=n_dsa)
            x, qc = _residual_matmul_then_proj(o, w_out16, i, x, None, norm_cross[l], w_cq16, l)
        else:
            u = _matmul(x, w_pw1_16, layer=i, gain=norm_mix[l], bias=b_pw1[i], glu=True)
            front = CONV_HALO - (CONV_WIDTH - 1)
            state_p = jnp.zeros((Bp, CONV_HALO, D), F32)
            state_s = jnp.concatenate([jnp.zeros((Bs, front, D), F32), state_conv[i]], axis=1)
            y = _conv_norm_swish(prompt, u, state_p, w_dw[i], b_dw[i], g_conv_norm[i], None)
            y = _conv_norm_swish(sample, u, state_s, w_dw[i], b_dw[i], g_conv_norm[i], y)
            outs["p_conv"].append(last_rows(u, prompt, state_p, CONV_WIDTH - 1))
            outs["s_conv"].append(last_rows(u, sample, state_s, CONV_WIDTH - 1))
            x, qc = _residual_matmul_then_proj(y, w_pw2_16, i, x, b_pw2[i], norm_cross[l], w_cq16, l)

        mk = _matmul(mem_flat, w_mk16, layer=l)
        mv = _matmul(mem_flat, w_mv16, layer=l)
        outs["p_mem_k"].append(mk.reshape(Bp, n_mem, cw // HEAD_DIM, HEAD_DIM))
        outs["p_mem_v"].append(mv.reshape(Bp, n_mem, cw // HEAD_DIM, HEAD_DIM))
        oc = _cross_attention(prompt, qc, mk.astype(BF16).reshape(Bp, n_mem, cw),
                              mv.astype(BF16).reshape(Bp, n_mem, cw), None)
        oc = _cross_attention(sample, qc, cache_mem_k[l].astype(BF16).reshape(Bs, n_mem, cw),
                              cache_mem_v[l].astype(BF16).reshape(Bs, n_mem, cw), oc)
        if l < depth - 1:
            x = _cross_out_mlp(x, oc, w_co16, norm_mlp[l], w_up16, w_down16, l)
        else:
            y_p, y_s = _cross_out_mlp(x, oc, w_co16, norm_mlp[l], w_up16, w_down16, l,
                                      final_gain=norm_final, m_prompt=Mp)

    st = lambda k: jnp.stack(outs[k])
    return (y_p.reshape(Bp, Tp, D), y_s.reshape(Bs, Ts, D), st("p_sb_k"), st("p_sb_v"), st("p_d_k"), st("p_d_v"),
            st("p_ki"), st("p_mem_k"), st("p_mem_v"), st("p_conv"), st("s_sb_k"), st("s_sb_v"), st("s_d_k"),
            st("s_d_v"), st("s_ki"), st("s_conv"))
```

```python
import functools
import math
from typing import NamedTuple

import numpy as np
import jax
import jax.numpy as jnp
from jax import lax
from jax.experimental import pallas as pl
from jax.experimental.pallas import tpu as pltpu

F32 = jnp.float32
BF16 = jnp.bfloat16

HEAD_DIM = 128
IDX_DIM = 64
N_IDX_HEADS = 16
CHUNK = 64
TOPK_MAX = 256
N_BUCKETS = 32
MAX_DISTANCE = 128
CONV_WIDTH = 31
CONV_HALO = 32
RMS_EPS = 1e-6
MASKED_LOGIT = -1e30
LOG2E = math.log2(math.e)
_INT32_MIN = -2 ** 31
VMEM_LIMIT_BYTES = 56 * 1024 * 1024
SB_HEADS_PER_STEP = 8
DSA_HEADS_PER_STEP = 4
SB_DEAD_BITS = 160.0
SB_INTERLEAVE = 4
ATTN_TILE = 256
SELECT_ROWS_PER_STEP = 512
FAR_TILES_PER_TRIP = 4


class _Group(NamedTuple):
    batch: int
    seq: int
    row0: int
    past: int


def _params(*semantics):
    return pltpu.CompilerParams(dimension_semantics=semantics, vmem_limit_bytes=VMEM_LIMIT_BYTES)


def _pick_tile(n, pref):
    if n <= pref:
        return n
    t = pref
    while n % t:
        t //= 2
    return t


def _row_tile(m, pref):
    t = min(m, pref) // 8 * 8
    while m % t:
        t -= 8
    return t


def _chained(prev):
    if prev is None:
        return [], [], None
    return [prev], [pl.BlockSpec(memory_space=pl.ANY)], True


def _rms_rows(x, g):
    ms = jnp.mean(x * x, axis=-1, keepdims=True)
    return x * lax.rsqrt(ms + RMS_EPS) * g


def _mm_kernel(*refs, has_norm, has_bias, has_res, glu):
    it = iter(refs)
    x_ref = next(it)
    g_ref = next(it) if has_norm else None
    w_ref = next(it)
    w2_ref = next(it) if glu else None
    b_ref = next(it) if has_bias else None
    b2_ref = next(it) if (has_bias and glu) else None
    r_ref = next(it) if has_res else None
    o_ref = next(it)
    xn_ref = next(it) if has_norm else None

    if has_norm:
        @pl.when(pl.program_id(1) == 0)
        def _():
            xn_ref[...] = _rms_rows(x_ref[...], g_ref[...]).astype(BF16)
        xb = xn_ref[...]
    else:
        xb = x_ref[...].astype(BF16)

    y = jnp.dot(xb, w_ref[...], preferred_element_type=F32)
    if has_bias:
        y = y + b_ref[...]
    if glu:
        gate = jnp.dot(xb, w2_ref[...], preferred_element_type=F32)
        if has_bias:
            gate = gate + b2_ref[...]
        y = y * (1.0 / (1.0 + jnp.exp(-gate)))
    if has_res:
        y = y + r_ref[...]
    o_ref[...] = y.astype(o_ref.dtype)


def _matmul(x, w, *, layer=0, gain=None, bias=None, residual=None, glu=False, out_dtype=F32, tm=1152, tn=512):
    M, K = x.shape
    n_out = w.shape[2] // 2 if glu else w.shape[2]
    tm = _row_tile(M, tm)
    tn = _pick_tile(n_out, tn)
    nj = n_out // tn
    has_norm, has_bias, has_res = gain is not None, bias is not None, residual is not None

    args = [x]
    specs = [pl.BlockSpec((tm, K), lambda i, j: (i, 0))]
    if has_norm:
        args.append(gain.reshape(1, K))
        specs.append(pl.BlockSpec((1, K), lambda i, j: (0, 0)))
    args.append(w)
    specs.append(pl.BlockSpec((None, K, tn), lambda i, j: (layer, 0, j)))
    if glu:
        args.append(w)
        specs.append(pl.BlockSpec((None, K, tn), lambda i, j: (layer, 0, j + nj)))
    if has_bias:
        b2d = bias.reshape(1, -1)
        args.append(b2d)
        specs.append(pl.BlockSpec((1, tn), lambda i, j: (0, j)))
        if glu:
            args.append(b2d)
            specs.append(pl.BlockSpec((1, tn), lambda i, j: (0, j + nj)))
    if has_res:
        args.append(residual)
        specs.append(pl.BlockSpec((tm, tn), lambda i, j: (i, j)))

    return pl.pallas_call(
        functools.partial(_mm_kernel, has_norm=has_norm, has_bias=has_bias, has_res=has_res, glu=glu),
        grid=(M // tm, nj),
        in_specs=specs,
        out_specs=pl.BlockSpec((tm, tn), lambda i, j: (i, j)),
        out_shape=jax.ShapeDtypeStruct((M, n_out), out_dtype),
        scratch_shapes=[pltpu.VMEM((tm, K), BF16)] if has_norm else [],
        compiler_params=_params("parallel", "arbitrary"),
        name="norm_matmul",
    )(*args)


def _res_proj_kernel(y_ref, w_ref, *refs, has_bias):
    it = iter(refs)
    b_ref = next(it) if has_bias else None
    r_ref, g_ref, wq_ref, o_ref, q_ref, x1_ref = (next(it) for _ in range(6))
    j = pl.program_id(1)
    y = jnp.dot(y_ref[...], w_ref[...], preferred_element_type=F32) + r_ref[...]
    if has_bias:
        y = y + b_ref[...]
    o_ref[...] = y
    x1_ref[j] = y

    @pl.when(j == pl.num_programs(1) - 1)
    def _():
        x1 = jnp.concatenate([x1_ref[t] for t in range(x1_ref.shape[0])], axis=1)
        q_ref[...] = jnp.dot(_rms_rows(x1, g_ref[...]).astype(BF16), wq_ref[...],
                             preferred_element_type=F32).astype(q_ref.dtype)


def _residual_matmul_then_proj(y, w, layer, residual, bias, gain, wq, q_layer, *, tm=1152, tn=512):
    M, K = y.shape
    N, nq = w.shape[2], wq.shape[2]
    tm = _row_tile(M, tm)
    assert N % tn == 0
    nj = N // tn
    has_bias = bias is not None
    args = [y, w]
    specs = [pl.BlockSpec((tm, K), lambda i, j: (i, 0)), pl.BlockSpec((None, K, tn), lambda i, j: (layer, 0, j))]
    if has_bias:
        args.append(bias.reshape(1, N))
        specs.append(pl.BlockSpec((1, tn), lambda i, j: (0, j)))
    args += [residual, gain.reshape(1, N), wq]
    specs += [pl.BlockSpec((tm, tn), lambda i, j: (i, j)), pl.BlockSpec((1, N), lambda i, j: (0, 0)),
              pl.BlockSpec((None, N, nq), lambda i, j: (q_layer, 0, 0))]
    return pl.pallas_call(
        functools.partial(_res_proj_kernel, has_bias=has_bias),
        grid=(M // tm, nj),
        in_specs=specs,
        out_specs=[pl.BlockSpec((tm, tn), lambda i, j: (i, j)), pl.BlockSpec((tm, nq), lambda i, j: (i, 0))],
        out_shape=[jax.ShapeDtypeStruct((M, N), F32), jax.ShapeDtypeStruct((M, nq), BF16)],
        scratch_shapes=[pltpu.VMEM((nj, tm, tn), F32)],
        compiler_params=_params("parallel", "arbitrary"),
        name="residual_matmul_proj",
    )(*args)


def _in_proj_kernel(x_ref, g_ref, w_ref, s_ref, wt_ref, o16_ref, o32_ref, ot_ref, xn_ref, *, n32_tiles):
    j = pl.program_id(1)

    @pl.when(j == 0)
    def _():
        xn_ref[...] = _rms_rows(x_ref[...], g_ref[...]).astype(BF16)
        ot_ref[...] = jnp.dot(xn_ref[...], wt_ref[...], preferred_element_type=F32)

    y = jnp.dot(xn_ref[...], w_ref[...], preferred_element_type=F32)
    o16_ref[...] = (y * s_ref[...]).astype(BF16)

    @pl.when(j < n32_tiles)
    def _():
        o32_ref[...] = y


def _in_proj(x, w, gain, col_scale, w_tail, *, n32, tm=1152, tn=512):
    M, K = x.shape
    N = w.shape[1]
    nt = w_tail.shape[1]
    tm = _row_tile(M, tm)
    assert N % tn == 0 and n32 % tn == 0
    n32_tiles = n32 // tn
    return pl.pallas_call(
        functools.partial(_in_proj_kernel, n32_tiles=n32_tiles),
        grid=(M // tm, N // tn),
        in_specs=[
            pl.BlockSpec((tm, K), lambda i, j: (i, 0)),
            pl.BlockSpec((1, K), lambda i, j: (0, 0)),
            pl.BlockSpec((K, tn), lambda i, j: (0, j)),
            pl.BlockSpec((1, tn), lambda i, j: (0, j)),
            pl.BlockSpec((K, nt), lambda i, j: (0, 0)),
        ],
        out_specs=[
            pl.BlockSpec((tm, tn), lambda i, j: (i, j)),
            pl.BlockSpec((tm, tn), lambda i, j: (i, jnp.minimum(j, n32_tiles - 1))),
            pl.BlockSpec((tm, nt), lambda i, j: (i, 0)),
        ],
        out_shape=[jax.ShapeDtypeStruct((M, N), BF16), jax.ShapeDtypeStruct((M, n32), F32),
                   jax.ShapeDtypeStruct((M, nt), F32)],
        scratch_shapes=[pltpu.VMEM((tm, K), BF16)],
        compiler_params=_params("parallel", "arbitrary"),
        name="in_proj",
    )(x, gain.reshape(1, K), w, col_scale.reshape(1, N), w_tail)


def _mlp_kernel(x_ref, oc_ref, wco_ref, g_ref, wu_ref, wd_ref, *refs, n_prompt_tiles):
    final = n_prompt_tiles is not None
    if final:
        gf_ref, op_ref, os_ref, x2_ref, xn_ref, acc_ref = refs
    else:
        o_ref, x2_ref, xn_ref, acc_ref = refs
    i, f = pl.program_id(0), pl.program_id(1)

    @pl.when(f == 0)
    def _():
        x2 = x_ref[...] + jnp.dot(oc_ref[...], wco_ref[...], preferred_element_type=F32)
        x2_ref[...] = x2
        xn_ref[...] = _rms_rows(x2, g_ref[...]).astype(BF16)
        acc_ref[...] = jnp.zeros_like(acc_ref)

    h = jnp.dot(xn_ref[...], wu_ref[...], preferred_element_type=F32)
    h = jnp.maximum(h, 0.0)
    h = (h * h).astype(BF16)
    acc_ref[...] += jnp.dot(h, wd_ref[...], preferred_element_type=F32)

    last = f == pl.num_programs(1) - 1
    if not final:
        @pl.when(last)
        def _():
            o_ref[...] = x2_ref[...] + acc_ref[...]
    else:
        @pl.when(last & (i < n_prompt_tiles))
        def _():
            op_ref[...] = _rms_rows(x2_ref[...] + acc_ref[...], gf_ref[...])

        @pl.when(last & (i >= n_prompt_tiles))
        def _():
            os_ref[...] = _rms_rows(x2_ref[...] + acc_ref[...], gf_ref[...])


def _cross_out_mlp(x, oc, w_co, gain, w_up, w_down, layer, *, final_gain=None, m_prompt=None, tm=512, tf=512):
    M, D = x.shape
    F = w_up.shape[2]
    cw = oc.shape[1]
    final = final_gain is not None
    tm = _pick_tile(M - m_prompt, tm) if final else _pick_tile(M, tm)
    tf = _pick_tile(F, tf)
    args = [x, oc, w_co, gain.reshape(1, D), w_up, w_down]
    specs = [
        pl.BlockSpec((tm, D), lambda i, f: (i, 0)),
        pl.BlockSpec((tm, cw), lambda i, f: (i, 0)),
        pl.BlockSpec((None, cw, D), lambda i, f: (layer, 0, 0)),
        pl.BlockSpec((1, D), lambda i, f: (0, 0)),
        pl.BlockSpec((None, D, tf), lambda i, f: (layer, 0, f)),
        pl.BlockSpec((None, tf, D), lambda i, f: (layer, f, 0)),
    ]
    if final:
        assert m_prompt % tm == 0 and (M - m_prompt) % tm == 0
        npt = m_prompt // tm
        args.append(final_gain.reshape(1, D))
        specs.append(pl.BlockSpec((1, D), lambda i, f: (0, 0)))
        out_specs = [pl.BlockSpec((tm, D), lambda i, f: (jnp.minimum(i, npt - 1), 0)),
                     pl.BlockSpec((tm, D), lambda i, f: (jnp.maximum(i - npt, 0), 0))]
        out_shape = [jax.ShapeDtypeStruct((m_prompt, D), F32), jax.ShapeDtypeStruct((M - m_prompt, D), F32)]
        semantics = ("arbitrary", "arbitrary")
    else:
        npt = None
        out_specs = pl.BlockSpec((tm, D), lambda i, f: (i, 0))
        out_shape = jax.ShapeDtypeStruct((M, D), F32)
        semantics = ("parallel", "arbitrary")
    return pl.pallas_call(
        functools.partial(_mlp_kernel, n_prompt_tiles=npt),
        grid=(M // tm, F // tf),
        in_specs=specs,
        out_specs=out_specs,
        out_shape=out_shape,
        scratch_shapes=[pltpu.VMEM((tm, D), F32), pltpu.VMEM((tm, D), BF16), pltpu.VMEM((tm, D), F32)],
        compiler_params=_params(*semantics),
        name="cross_out_mlp",
    )(*args)


def _nt_dot(a, b):
    return lax.dot_general(a, b, (((1,), (1,)), ((), ())), preferred_element_type=F32)


def _kv_operands(grp, proj, k_col, v_col, k_cache, v_cache, *, tk, hp):
    wp = hp * HEAD_DIM
    if grp.past == 0:
        blk0 = grp.row0 // grp.seq
        args = [proj, proj]
        specs = [pl.BlockSpec((grp.seq, wp), lambda b, h, i, c=k_col // wp: (blk0 + b, c + h),
                              pipeline_mode=pl.Buffered(1)),
                 pl.BlockSpec((grp.seq, wp), lambda b, h, i, c=v_col // wp: (blk0 + b, c + h),
                              pipeline_mode=pl.Buffered(1))]
        return args, specs

    def new_tile(col):
        rows = lax.slice(proj, (grp.row0, col), (grp.row0 + grp.batch * grp.seq, col + k_cache.shape[2] * HEAD_DIM))
        rows = rows.reshape(grp.batch, grp.seq, -1)
        return jnp.pad(rows, ((0, 0), (0, tk - grp.seq), (0, 0)))

    args = [new_tile(k_col), new_tile(v_col)]
    specs = [pl.BlockSpec((None, tk, wp), lambda b, h, i: (b, 0, h))] * 2
    n_heads = k_cache.shape[2]
    for cache in (k_cache, v_cache):
        args.append(cache.reshape(grp.batch, grp.past * n_heads, HEAD_DIM))
        specs.append(pl.BlockSpec((None, grp.past * n_heads, HEAD_DIM), lambda b, h, i: (b, 0, 0)))
    return args, specs


def _split_kv_refs(refs, has_past):
    if not has_past:
        return refs[0], refs[1], None, None, refs[2:]
    return refs[0], refs[1], refs[2], refs[3], refs[4:]


def _tile_reader(new_ref, past_ref, *, tk, hp, n_heads):
    def read(j, hh, diag, n=1):
        hs = slice(hh * HEAD_DIM, (hh + 1) * HEAD_DIM)
        if past_ref is None:
            return new_ref[pl.ds(pl.multiple_of(j * tk, tk), n * tk), hs]
        if diag:
            return new_ref[:, hs]
        head = pl.program_id(1) * hp + hh
        return past_ref[pl.ds(j * (tk * n_heads) + head, n * tk, stride=n_heads), :].astype(BF16)
    return read


def _sb_kernel(q_ref, *refs, tq, tk, hp, n_heads, past, has_past):
    k_new, v_new, k_past, v_past, rest = _split_kv_refs(refs, has_past)
    u_ref, o_ref, acc_ref, carry_ref = rest[0], rest[-3], rest[-2], rest[-1]
    read_k = _tile_reader(k_new, k_past, tk=tk, hp=hp, n_heads=n_heads)
    read_v = _tile_reader(v_new, v_past, tk=tk, hp=hp, n_heads=n_heads)

    i = pl.program_id(2)
    q_start = past + i * tq
    jd = q_start // tk
    reps = tk // HEAD_DIM

    acc_ref[...] = jnp.zeros_like(acc_ref)
    carry_ref[...] = jnp.zeros_like(carry_ref)

    def step(j, masked):
        if masked:
            t_pos = q_start + lax.broadcasted_iota(jnp.int32, (tq, tk), 0)
            s_pos = j * tk + lax.broadcasted_iota(jnp.int32, (tq, tk), 1)
            causal = t_pos > s_pos
        for h0 in range(0, hp, SB_INTERLEAVE):
            group_step(j, masked, causal if masked else None, range(h0, h0 + SB_INTERLEAVE))

    def group_step(j, masked, causal, group):
        heads = [(hh, slice(hh * HEAD_DIM, (hh + 1) * HEAD_DIM)) for hh in group]
        z2s = [_nt_dot(q_ref[:, hs], read_k(j, hh, masked)) for hh, hs in heads]
        ns, log2_betas, css = [], [], []
        for z2 in z2s:
            neg_abs = lax.bitcast_convert_type(
                lax.bitcast_convert_type(z2, jnp.int32) | jnp.int32(_INT32_MIN), F32)
            n = jnp.maximum(z2, 0.0) + jnp.log(1.0 + jnp.exp2(neg_abs)) * LOG2E
            log2_betas.append(z2 - n)
            if masked:
                n = jnp.where(causal, n, 0.0)
            n16 = n.astype(BF16)
            css.append(jnp.dot(n16, u_ref[...], preferred_element_type=F32))
            ns.append(n16)
        for (hh, hs), n16, log2_beta, cs in zip(heads, ns, log2_betas, css):
            a = jnp.exp2(log2_beta - cs)
            if masked:
                a = jnp.where(causal, a, 0.0)
            carry = carry_ref[:, hs]
            pv = jnp.dot(a.astype(BF16), read_v(j, hh, masked), preferred_element_type=F32)
            acc_ref[:, hs] += jnp.exp2(-carry) * pv
            carry_ref[:, hs] = carry + jnp.broadcast_to(cs[:, 0:1] + n16[:, 0:1].astype(F32), (tq, HEAD_DIM))

    step(jd, True)

    def live(state):
        t, min_carry = state
        return (t < jd) & (min_carry < SB_DEAD_BITS)

    def body(state):
        t, _ = state
        step(jd - 1 - t, False)
        return t + 1, jnp.min(carry_ref[...])

    lax.while_loop(live, body, (jnp.int32(0), jnp.min(carry_ref[...])))
    o_ref[...] = acc_ref[...].astype(o_ref.dtype)


def _suffix_sum_matrix(tk):
    r = np.arange(tk)
    return jnp.asarray((r[:, None] > r[None, :]).astype(np.float32), dtype=BF16)


def _attn_tiles(grp):
    tq = _pick_tile(grp.seq, ATTN_TILE)
    tk = ATTN_TILE if grp.past else tq
    nq = grp.seq // tq
    assert grp.seq % tq == 0 and tk % tq == 0 and grp.past % tk == 0 and grp.row0 % tq == 0
    assert grp.past == 0 or nq == 1
    assert grp.past > 0 or grp.row0 % grp.seq == 0
    return tq, tk, nq


def _sb_attention(grp, proj, cols, k_cache, v_cache, o_prev, o_shape, o_col, *, hp):
    H = k_cache.shape[2]
    tq, tk, nq = _attn_tiles(grp)
    wp = hp * HEAD_DIM
    q_blk0 = grp.row0 // tq
    kv_args, kv_specs = _kv_operands(grp, proj, cols[1], cols[2], k_cache, v_cache, tk=tk, hp=hp)
    prev_args, prev_specs, aliased = _chained(o_prev)
    n_in = 1 + len(kv_args) + 1
    return pl.pallas_call(
        functools.partial(_sb_kernel, tq=tq, tk=tk, hp=hp, n_heads=H, past=grp.past, has_past=grp.past > 0),
        grid=(grp.batch, H // hp, nq),
        in_specs=[pl.BlockSpec((tq, wp), lambda b, h, i: (q_blk0 + b * nq + i, cols[0] // wp + h))]
        + kv_specs
        + [pl.BlockSpec((tk, tk), lambda b, h, i: (0, 0), pipeline_mode=pl.Buffered(1))]
        + prev_specs,
        out_specs=pl.BlockSpec((tq, wp), lambda b, h, i: (q_blk0 + b * nq + i, o_col // wp + h)),
        out_shape=jax.ShapeDtypeStruct(o_shape, BF16),
        input_output_aliases={n_in: 0} if aliased else {},
        scratch_shapes=[pltpu.VMEM((tq, wp), F32), pltpu.VMEM((tq, wp), F32)],
        compiler_params=_params("parallel", "parallel", "arbitrary"),
        name="sb_attention",
    )(proj, *kv_args, _suffix_sum_matrix(tk), *prev_args)


def _sortable_key(x):
    b = lax.bitcast_convert_type(x, jnp.int32)
    return b ^ ((b >> 31) & jnp.int32(0x7FFFFFFF))


_KEY_NEG_INF = int(np.int32(np.array(-np.inf, np.float32).view(np.int32) ^ 0x7FFFFFFF))
_KEY_LOWEST_FINITE = int(np.int32(np.array(np.finfo(np.float32).min, np.float32).view(np.int32) ^ 0x7FFFFFFF))
COUNT_ROW_GROUP = 128


def _dsa_select_kernel(qi_ref, wi_ref, ki2_ref, mask_ref, keys_ref, wb_ref,
                       *, tq, tk, nkt, nsub, by_batch, past, topk):
    i = pl.program_id(1)
    rows = nsub * tq
    first_start = past + i * tq * (1 if by_batch else nsub)
    jd_min = first_start // tk
    n_cut = 1 if by_batch else nsub
    jd = jd_min + n_cut - 1
    reps = tk // HEAD_DIM

    w = wi_ref[:, IDX_DIM:IDX_DIM + N_IDX_HEADS] * (IDX_DIM ** -0.5 * N_IDX_HEADS ** -0.5)
    for hh in range(N_IDX_HEADS):
        wb_ref[hh] = jnp.broadcast_to(w[:, hh:hh + 1], (rows, HEAD_DIM))

    def idx_rows(j, r0, nr, kt):
        tot = jnp.zeros((nr, tk), F32)
        for p in range(N_IDX_HEADS // 2):
            sc = jnp.dot(qi_ref[r0:r0 + nr, p * 2 * IDX_DIM:(p + 1) * 2 * IDX_DIM], kt,
                         preferred_element_type=F32)
            w0 = jnp.concatenate([wb_ref[2 * p, r0:r0 + nr, :]] * reps, axis=1)
            w1 = jnp.concatenate([wb_ref[2 * p + 1, r0:r0 + nr, :]] * reps, axis=1)
            tot = tot + w0 * jnp.maximum(sc[:, :tk], 0.0) + w1 * jnp.maximum(sc[:, tk:], 0.0)
        return _sortable_key(tot)

    def idx_tile(j):
        if by_batch:
            return jnp.concatenate([idx_rows(j, s * tq, tq, ki2_ref[s, j]) for s in range(nsub)], axis=0)
        return idx_rows(j, 0, rows, ki2_ref[0, j])

    def idx_body(j, c):
        keys_ref[j] = idx_tile(j)
        return c

    lax.fori_loop(0, jd_min, idx_body, 0)
    row = lax.broadcasted_iota(jnp.int32, (rows, tk), 0)
    t_pos = first_start + (row % tq if by_batch else row)
    for d in range(n_cut):
        s_pos = (jd_min + d) * tk + lax.broadcasted_iota(jnp.int32, (rows, tk), 1)
        visible = (t_pos // CHUNK) >= (s_pos // CHUNK)
        keys_ref[jd_min + d] = jnp.where(visible, idx_tile(jd_min + d), jnp.int32(_KEY_NEG_INF))

    n_vis = jd + 1

    @pl.when(n_vis % 2 == 1)
    def _():
        keys_ref[n_vis] = jnp.full((rows, tk), _INT32_MIN, jnp.int32)

    n_pairs = (n_vis + 1) // 2
    rg = min(COUNT_ROW_GROUP, rows)

    def count_ge(cand):
        parts = []
        for g in range(rows // rg):
            cg = cand[g * rg:(g + 1) * rg]

            def body(t, cnt, g=g, cg=cg):
                for jj in (2 * t, 2 * t + 1):
                    kj = keys_ref[jj, pl.ds(g * rg, rg), :]
                    for r in range(reps):
                        cnt = cnt + jnp.where(kj[:, r * HEAD_DIM:(r + 1) * HEAD_DIM] >= cg, 1.0, 0.0)
                return cnt

            parts.append(lax.fori_loop(0, n_pairs, body, jnp.zeros((rg, HEAD_DIM), F32)))
        return jnp.sum(jnp.concatenate(parts, axis=0), axis=-1, keepdims=True)

    kf = float(topk)
    zero = jnp.zeros((rows, HEAD_DIM), jnp.int32)
    c = count_ge(zero)
    res = jnp.where(c >= kf, zero, jnp.int32(_INT32_MIN))
    cnt = jnp.where(c >= kf, c, float(tk) * (2 * n_pairs).astype(F32))
    for bit in range(30, -1, -1):
        cand = res + jnp.int32(1 << bit)
        c = count_ge(cand)
        res = jnp.where(c >= kf, cand, res)
        cnt = jnp.where(c >= kf, c, cnt)

    tied_cut = jnp.max(jnp.where((cnt != kf) & (res[:, 0:1] > jnp.int32(_KEY_NEG_INF)), 1.0, 0.0)) > 0.0

    @pl.when(tied_cut)
    def _():
        col = lax.broadcasted_iota(jnp.int32, (rg, tk), 1)

        def count_tied(before, strict_above):
            parts = []
            for g in range(rows // rg):
                grp_rows = slice(g * rg, (g + 1) * rg)
                res_g = jnp.concatenate([res[grp_rows]] * reps, axis=1)
                bef_g = None if strict_above else jnp.concatenate([before[grp_rows]] * reps, axis=1)

                def body(j, acc, g=g, res_g=res_g, bef_g=bef_g):
                    kj = keys_ref[j, pl.ds(g * rg, rg), :]
                    if strict_above:
                        hit = jnp.where(kj > res_g, 1.0, 0.0)
                    else:
                        hit = jnp.where(kj == res_g, jnp.where(j * tk + col < bef_g, 1.0, 0.0), 0.0)
                    return acc + sum(hit[:, r * HEAD_DIM:(r + 1) * HEAD_DIM] for r in range(reps))

                parts.append(lax.fori_loop(0, n_vis, body, jnp.zeros((rg, HEAD_DIM), F32)))
            return jnp.sum(jnp.concatenate(parts, axis=0), axis=-1, keepdims=True)

        need = kf - count_tied(None, True)
        r_keep = jnp.zeros((rows, HEAD_DIM), jnp.int32)
        for bit in range((nkt * tk).bit_length() - 1, -1, -1):
            cand = r_keep | jnp.int32(1 << bit)
            r_keep = jnp.where(count_tied(cand, False) < need, cand, r_keep)

        def drop_late_ties(j, c):
            for g in range(rows // rg):
                grp_rows = slice(g * rg, (g + 1) * rg)
                res_g = jnp.concatenate([res[grp_rows]] * reps, axis=1)
                keep_g = jnp.concatenate([r_keep[grp_rows]] * reps, axis=1)
                kj = keys_ref[j, pl.ds(g * rg, rg), :]
                keys_ref[j, pl.ds(g * rg, rg), :] = jnp.where(
                    kj == res_g, jnp.where(j * tk + col > keep_g, res_g - 1, kj), kj)
            return c

        lax.fori_loop(0, n_vis, drop_late_ties, 0)

    thr = jnp.maximum(res, jnp.int32(_KEY_LOWEST_FINITE))
    thr = jnp.concatenate([thr] * reps, axis=1)

    def write_mask(j, c):
        tile = jnp.where(keys_ref[j] >= thr, 0.0, MASKED_LOGIT).astype(mask_ref.dtype)
        for s_ in range(nsub):
            mask_ref[s_, j] = tile[s_ * tq:(s_ + 1) * tq]
        return c

    lax.fori_loop(0, jd + 1, write_mask, 0)

    def write_hidden(j, c):
        for s_ in range(nsub):
            mask_ref[s_, j] = jnp.full((tq, tk), MASKED_LOGIT, mask_ref.dtype)
        return c

    lax.fori_loop(jd + 1, nkt, write_hidden, 0)


def _dsa_flash_kernel(q_ref, *refs, tq, tk, hp, n_heads, past, has_past):
    k_new, v_new, k_past, v_past, rest = _split_kv_refs(refs, has_past)
    mask_ref, bnear_ref = rest[0], rest[1]
    o_ref, m_ref, l_ref, acc_ref = rest[-4], rest[-3], rest[-2], rest[-1]
    read_k = _tile_reader(k_new, k_past, tk=tk, hp=hp, n_heads=n_heads)
    read_v = _tile_reader(v_new, v_past, tk=tk, hp=hp, n_heads=n_heads)

    i = pl.program_id(2)
    jd = (past + i * tq) // tk
    reps = tk // HEAD_DIM
    heads = [slice(hh * HEAD_DIM, (hh + 1) * HEAD_DIM) for hh in range(hp)]

    m_ref[...] = jnp.full_like(m_ref, MASKED_LOGIT)
    l_ref[...] = jnp.zeros_like(l_ref)
    acc_ref[...] = jnp.zeros_like(acc_ref)

    def step(j, near, n=1):
        diag = near == 0
        maskf = jnp.concatenate([mask_ref[j + t].astype(F32) for t in range(n)], axis=1)
        logits = []
        for hh, hs in enumerate(heads):
            s = _nt_dot(q_ref[:, hs], read_k(j, hh, diag, n)) + maskf
            if near is not None:
                s = s + bnear_ref[hh, near]
            logits.append(s)
        probs, alphas = [], []
        for hs, s in zip(heads, logits):
            m_old = m_ref[:, hs]
            m_new = jnp.maximum(m_old, jnp.max(s, axis=-1, keepdims=True))
            alpha = jnp.exp2(m_old - m_new)
            p = jnp.exp2(s - jnp.concatenate([m_new] * (n * reps), axis=1))
            l_ref[:, hs] = alpha * l_ref[:, hs] + sum(p[:, r * HEAD_DIM:(r + 1) * HEAD_DIM]
                                                      for r in range(n * reps))
            m_ref[:, hs] = m_new
            probs.append(p.astype(BF16))
            alphas.append(alpha)
        for hh, (hs, p, alpha) in enumerate(zip(heads, probs, alphas)):
            pv = jnp.dot(p, read_v(j, hh, diag, n), preferred_element_type=F32)
            acc_ref[:, hs] = alpha * acc_ref[:, hs] + pv

    step(jd, 0)

    @pl.when(jd >= 1)
    def _():
        step(jd - 1, 1)

    n_far = jnp.maximum(jd - 1, 0)
    n_trips = n_far // FAR_TILES_PER_TRIP

    def single(j, c):
        step(j, None)
        return c

    lax.fori_loop(n_trips * FAR_TILES_PER_TRIP, n_far, single, 0)

    def body(t, c):
        step(FAR_TILES_PER_TRIP * t, None, FAR_TILES_PER_TRIP)
        return c

    lax.fori_loop(0, n_trips, body, 0)
    for hs in heads:
        o_ref[:, hs] = (acc_ref[:, hs] / jnp.sum(l_ref[:, hs], axis=-1, keepdims=True)).astype(o_ref.dtype)


def _t5_bucket(rel):
    nb = N_BUCKETS // 2
    max_exact = nb // 2
    side = jnp.where(rel > 0, nb, 0)
    n = jnp.abs(rel)
    nf = jnp.maximum(n, 1).astype(F32)
    large = max_exact + (jnp.log(nf / max_exact) / math.log(MAX_DISTANCE / max_exact) * (nb - max_exact)).astype(jnp.int32)
    large = jnp.minimum(large, nb - 1)
    return side + jnp.where(n < max_exact, n, large)


def _near_bias(rel_bias, *, tq, tk):
    r = jnp.arange(tq, dtype=jnp.int32)[:, None]
    c = jnp.arange(tk, dtype=jnp.int32)[None, :]
    bucket = _t5_bucket(jnp.stack([c - r, c - r - tk]))
    far_bucket = _t5_bucket(jnp.int32(-2 * tk))
    table = (rel_bias - rel_bias[far_bucket][None, :]) * LOG2E
    onehot = (bucket[..., None] == jnp.arange(N_BUCKETS, dtype=jnp.int32)).astype(F32)
    return jnp.einsum("dqkb,bh->hdqk", onehot, table, precision=lax.Precision.HIGHEST)


def _dsa_attention(grp, proj, tail, cols, k_idx_all, k_cache, v_cache, rel_bias, o_prev, o_shape, o_col, *, hp):
    H = k_cache.shape[2]
    tq, tk, nq = _attn_tiles(grp)
    s_pad = k_idx_all.shape[1]
    nkt = s_pad // tk
    assert tk % CHUNK == 0 and tq % CHUNK == 0 and s_pad % tk == 0
    assert tk >= MAX_DISTANCE and H % hp == 0
    topk = min(TOPK_MAX, (grp.past + grp.seq) // 4)
    q_blk0 = grp.row0 // tq
    qiw = N_IDX_HEADS * IDX_DIM

    kt = jnp.transpose(k_idx_all.reshape(grp.batch, nkt, tk, IDX_DIM), (0, 1, 3, 2))
    z = jnp.zeros_like(kt)
    ki2 = jnp.concatenate([jnp.concatenate([kt, z], axis=3), jnp.concatenate([z, kt], axis=3)], axis=2)
    resident = pl.Buffered(1) if nq > 1 else None

    by_batch = nq == 1
    nsub = max(SELECT_ROWS_PER_STEP // tq, 1)
    while (grp.batch if by_batch else nq) % nsub or (grp.row0 // tq) % nsub or (not by_batch and tq != tk):
        nsub //= 2
    rows = nsub * tq
    sel_blk0 = grp.row0 // rows
    if by_batch:
        grid = (grp.batch // nsub, 1)
        row_map = lambda b, i: sel_blk0 + b
        ki2_spec = pl.BlockSpec((nsub, nkt, 2 * IDX_DIM, 2 * tk), lambda b, i: (b, 0, 0, 0))
        mask_spec = pl.BlockSpec((nsub, None, nkt, tq, tk), lambda b, i: (b, 0, 0, 0, 0))
    else:
        grid = (grp.batch, nq // nsub)
        row_map = lambda b, i: sel_blk0 + b * (nq // nsub) + i
        ki2_spec = pl.BlockSpec((1, nkt, 2 * IDX_DIM, 2 * tk), lambda b, i: (b, 0, 0, 0), pipeline_mode=resident)
        mask_spec = pl.BlockSpec((None, nsub, nkt, tq, tk), lambda b, i: (b, i, 0, 0, 0))

    mask = pl.pallas_call(
        functools.partial(_dsa_select_kernel, tq=tq, tk=tk, nkt=nkt, nsub=nsub, by_batch=by_batch,
                          past=grp.past, topk=topk),
        grid=grid,
        in_specs=[
            pl.BlockSpec((rows, qiw), lambda b, i: (row_map(b, i), cols[0] // qiw)),
            pl.BlockSpec((rows, HEAD_DIM), lambda b, i: (row_map(b, i), 0)),
            ki2_spec,
        ],
        out_specs=mask_spec,
        out_shape=jax.ShapeDtypeStruct((grp.batch, nq, nkt, tq, tk), BF16),
        scratch_shapes=[
            pltpu.VMEM((nkt + 1, rows, tk), jnp.int32),
            pltpu.VMEM((N_IDX_HEADS, rows, HEAD_DIM), F32),
        ],
        compiler_params=_params("parallel", "parallel"),
        name="dsa_select",
    )(proj, tail, ki2)

    wp = hp * HEAD_DIM
    kv_args, kv_specs = _kv_operands(grp, proj, cols[2], cols[3], k_cache, v_cache, tk=tk, hp=hp)
    prev_args, prev_specs, aliased = _chained(o_prev)
    n_in = 1 + len(kv_args) + 2
    return pl.pallas_call(
        functools.partial(_dsa_flash_kernel, tq=tq, tk=tk, hp=hp, n_heads=H, past=grp.past, has_past=grp.past > 0),
        grid=(grp.batch, H // hp, nq),
        in_specs=[pl.BlockSpec((tq, wp), lambda b, h, i: (q_blk0 + b * nq + i, cols[1] // wp + h))]
        + kv_specs
        + [pl.BlockSpec((None, None, nkt, tq, tk), lambda b, h, i: (b, i, 0, 0, 0)),
           pl.BlockSpec((hp, 2, tq, tk), lambda b, h, i: (h, 0, 0, 0), pipeline_mode=resident)]
        + prev_specs,
        out_specs=pl.BlockSpec((tq, wp), lambda b, h, i: (q_blk0 + b * nq + i, o_col // wp + h)),
        out_shape=jax.ShapeDtypeStruct(o_shape, BF16),
        input_output_aliases={n_in: 0} if aliased else {},
        scratch_shapes=[pltpu.VMEM((tq, wp), F32), pltpu.VMEM((tq, wp), F32), pltpu.VMEM((tq, wp), F32)],
        compiler_params=_params("parallel", "parallel", "arbitrary"),
        name="dsa_flash",
    )(proj, *kv_args, mask, _near_bias(rel_bias, tq=tq, tk=tk), *prev_args)


def _cross_kernel(q_ref, k_ref, v_ref, *rest, n_heads, scale):
    o_ref = rest[-1]
    for hh in range(n_heads):
        sl = slice(hh * HEAD_DIM, (hh + 1) * HEAD_DIM)
        s = _nt_dot(q_ref[:, sl], k_ref[:, sl]) * scale
        p = jnp.exp(s - jnp.max(s, axis=-1, keepdims=True))
        denom = jnp.sum(p, axis=-1, keepdims=True)
        o = jnp.dot(p.astype(BF16), v_ref[:, sl], preferred_element_type=F32)
        o_ref[:, sl] = (o / denom).astype(o_ref.dtype)


def _cross_attention(grp, q, mem_k, mem_v, o_prev, *, tq=512):
    M, W = q.shape
    n_mem = mem_k.shape[1]
    tq = _pick_tile(grp.seq, tq)
    nq = grp.seq // tq
    assert grp.row0 % tq == 0
    blk0 = grp.row0 // tq
    prev_args, prev_specs, aliased = _chained(o_prev)
    return pl.pallas_call(
        functools.partial(_cross_kernel, n_heads=W // HEAD_DIM, scale=HEAD_DIM ** -0.5),
        grid=(grp.batch, nq),
        in_specs=[
            pl.BlockSpec((tq, W), lambda b, i: (blk0 + b * nq + i, 0)),
            pl.BlockSpec((None, n_mem, W), lambda b, i: (b, 0, 0)),
            pl.BlockSpec((None, n_mem, W), lambda b, i: (b, 0, 0)),
        ] + prev_specs,
        out_specs=pl.BlockSpec((tq, W), lambda b, i: (blk0 + b * nq + i, 0)),
        out_shape=jax.ShapeDtypeStruct((M, W), BF16),
        input_output_aliases={3: 0} if aliased else {},
        compiler_params=_params("parallel", "parallel"),
        name="cross_attention",
    )(q, mem_k, mem_v, *prev_args)


CONV_ROW_CHUNK = 128
SUBLANES = 8


def _conv_kernel(a_ref, prev_ref, state_ref, w_ref, bias_ref, g_ref, *rest, tt):
    o_ref, buf_ref, q_ref, y_ref = rest[-4], rest[-3], rest[-2], rest[-1]
    D = a_ref.shape[-1]

    @pl.when(pl.program_id(1) == 0)
    def _():
        buf_ref[0:CONV_HALO, :] = state_ref[...]

    @pl.when(pl.program_id(1) > 0)
    def _():
        buf_ref[0:CONV_HALO, :] = prev_ref[...]

    buf_ref[CONV_HALO:CONV_HALO + tt, :] = a_ref[...]
    buf_ref[CONV_HALO + tt:CONV_HALO + tt + SUBLANES, :] = jnp.zeros((SUBLANES, D), F32)

    first = CONV_HALO - (CONV_WIDTH - 1)
    rc = min(CONV_ROW_CHUNK, tt)
    for c0 in range(0, D, HEAD_DIM):
        cs = slice(c0, c0 + HEAD_DIM)
        for t0 in range(0, tt, rc):
            out = jnp.zeros((rc, HEAD_DIM), F32)
            for r in range(SUBLANES):
                q = None
                for k in range(CONV_WIDTH):
                    if (first + k) % SUBLANES != r:
                        continue
                    base = t0 + first + k - r
                    term = buf_ref[base:base + rc + SUBLANES, cs] * w_ref[k:k + 1, cs]
                    q = term if q is None else q + term
                if r == 0:
                    out = out + q[0:rc]
                else:
                    q_ref[...] = q
                    out = out + q_ref[r:r + rc, :]
            y_ref[t0:t0 + rc, cs] = out + bias_ref[:, cs]
    y = _rms_rows(y_ref[...], g_ref[...])
    o_ref[...] = (y * (1.0 / (1.0 + jnp.exp(-y)))).astype(o_ref.dtype)


def _conv_norm_swish(grp, u, state, w_dw, b_dw, g_norm, o_prev):
    M, D = u.shape
    tt = _pick_tile(grp.seq, 256)
    nt = grp.seq // tt
    assert grp.seq % tt == 0 and tt % CONV_HALO == 0 and grp.row0 % tt == 0
    blk0 = grp.row0 // tt
    hb = tt // CONV_HALO
    prev_args, prev_specs, aliased = _chained(o_prev)
    return pl.pallas_call(
        functools.partial(_conv_kernel, tt=tt),
        grid=(grp.batch, nt),
        in_specs=[
            pl.BlockSpec((tt, D), lambda b, i: (blk0 + b * nt + i, 0)),
            pl.BlockSpec((CONV_HALO, D), lambda b, i: (jnp.maximum((blk0 + b * nt + i) * hb - 1, 0), 0)),
            pl.BlockSpec((None, CONV_HALO, D), lambda b, i: (b, 0, 0)),
            pl.BlockSpec((CONV_WIDTH, D), lambda b, i: (0, 0)),
            pl.BlockSpec((1, D), lambda b, i: (0, 0)),
            pl.BlockSpec((1, D), lambda b, i: (0, 0)),
        ] + prev_specs,
        out_specs=pl.BlockSpec((tt, D), lambda b, i: (blk0 + b * nt + i, 0)),
        out_shape=jax.ShapeDtypeStruct((M, D), BF16),
        input_output_aliases={6: 0} if aliased else {},
        scratch_shapes=[pltpu.VMEM((tt + CONV_HALO + SUBLANES, D), F32),
                        pltpu.VMEM((min(CONV_ROW_CHUNK, tt) + SUBLANES, HEAD_DIM), F32),
                        pltpu.VMEM((tt, D), F32)],
        compiler_params=_params("parallel", "arbitrary"),
        name="conv_norm_swish",
    )(u, u, state, w_dw, b_dw.reshape(1, D), g_norm.reshape(1, D), *prev_args)


def kernel(x_prompt, x_sample, mem_prompt, cache_sb_k, cache_sb_v, cache_dsa_k, cache_dsa_v, cache_idx_k, cache_mem_k, cache_mem_v, state_conv, norm_mix, norm_cross, norm_mlp, norm_final, w_in_ab, w_out_ab, rel_bias, w_pw1, b_pw1, w_dw, b_dw, g_conv_norm, w_pw2, b_pw2, w_cq, w_mk, w_mv, w_co, w_up, w_down):
    Bp, Tp, D = x_prompt.shape
    Bs, Ts, _ = x_sample.shape
    past = cache_sb_k.shape[2]
    depth = norm_mix.shape[0]
    n_mem = mem_prompt.shape[1]
    n_sb, n_dsa = cache_sb_k.shape[3], cache_dsa_k.shape[3]
    sbw, dsw = n_sb * HEAD_DIM, n_dsa * HEAD_DIM
    qiw = N_IDX_HEADS * IDX_DIM
    cw = w_cq.shape[2]
    Mp, Ms = Bp * Tp, Bs * Ts
    M = Mp + Ms
    prompt = _Group(Bp, Tp, 0, 0)
    sample = _Group(Bs, Ts, Mp, past)

    x = jnp.concatenate([x_prompt.reshape(Mp, D), x_sample.reshape(Ms, D)], axis=0)
    mem_flat = mem_prompt.reshape(Bp * n_mem, D)

    src = dict(zip(("q_sb", "k_sb", "v_sb", "q_d", "k_d", "v_d", "q_i", "k_i", "w_i"),
                   np.cumsum([0, sbw, sbw, sbw, dsw, dsw, dsw, qiw, IDX_DIM])))
    order = ("k_sb", "v_sb", "k_d", "v_d", "q_sb", "q_d", "q_i")
    width = dict(q_sb=sbw, k_sb=sbw, v_sb=sbw, q_d=dsw, k_d=dsw, v_d=dsw, q_i=qiw)
    col = dict(zip(order, np.cumsum([0] + [width[n] for n in order])[:-1]))
    col = {k: int(v) for k, v in col.items()}
    n32 = 2 * sbw + 2 * dsw

    outs = {k: [] for k in ("p_sb_k", "p_sb_v", "p_d_k", "p_d_v", "p_ki", "p_mem_k", "p_mem_v", "p_conv",
                            "s_sb_k", "s_sb_v", "s_d_k", "s_d_v", "s_ki", "s_conv")}

    def split_rows(a):
        return a[:Mp].reshape(Bp, Tp, -1), a[Mp:].reshape(Bs, Ts, -1)

    def last_rows(a, grp, state, n):
        if grp.seq >= n:
            return jnp.stack([a[grp.row0 + (b + 1) * grp.seq - n:grp.row0 + (b + 1) * grp.seq]
                              for b in range(grp.batch)])
        rows = a[grp.row0:grp.row0 + grp.batch * grp.seq].reshape(grp.batch, grp.seq, -1)
        return jnp.concatenate([state, rows], axis=1)[:, -n:]

    w_out16, w_pw1_16, w_pw2_16 = w_out_ab.astype(BF16), w_pw1.astype(BF16), w_pw2.astype(BF16)
    w_cq16, w_mk16, w_mv16, w_co16 = (w.astype(BF16) for w in (w_cq, w_mk, w_mv, w_co))
    w_up16, w_down16 = w_up.astype(BF16), w_down.astype(BF16)
    q_scale = HEAD_DIM ** -0.5 * LOG2E
    col_scale = jnp.concatenate([jnp.full((width[n],), q_scale if n in ("q_sb", "q_d") else 1.0, F32)
                                 for n in order])

    for l in range(depth):
        i = l // 2
        if l % 2 == 0:
            w_in = w_in_ab[i]
            w_main = jnp.concatenate([w_in[:, src[n]:src[n] + width[n]] for n in order], axis=1).astype(BF16)
            tail_w = w_in.shape[1] - int(src["k_i"])
            w_tail = jnp.pad(w_in[:, src["k_i"]:], ((0, 0), (0, HEAD_DIM - tail_w))).astype(BF16)
            proj, kv32, tail = _in_proj(x, w_main, norm_mix[l], col_scale, w_tail, n32=n32)

            for name, key, nh in (("sb_k", "k_sb", n_sb), ("sb_v", "v_sb", n_sb),
                                  ("d_k", "k_d", n_dsa), ("d_v", "v_d", n_dsa)):
                ap, as_ = split_rows(kv32[:, col[key]:col[key] + width[key]])
                outs["p_" + name].append(ap.reshape(Bp, Tp, nh, HEAD_DIM))
                outs["s_" + name].append(as_.reshape(Bs, Ts, nh, HEAD_DIM))
            k_i = tail[:, :IDX_DIM]
            kip, kis = split_rows(k_i)
            outs["p_ki"].append(kip)
            outs["s_ki"].append(kis)

            tk_s = ATTN_TILE
            s_pad_s = -(-(past + Ts) // tk_s) * tk_s
            kidx_p = kip.astype(BF16)
            kidx_s = jnp.concatenate([cache_idx_k[i].astype(BF16), kis.astype(BF16),
                                      jnp.zeros((Bs, s_pad_s - past - Ts, IDX_DIM), BF16)], axis=1)

            o_shape = (M, sbw + dsw)
            sb_cols = (col["q_sb"], col["k_sb"], col["v_sb"])
            d_cols = (col["q_i"], col["q_d"], col["k_d"], col["v_d"])
            o = _sb_attention(prompt, proj, sb_cols, cache_sb_k[i], cache_sb_v[i], None, o_shape, 0,
                              hp=SB_HEADS_PER_STEP)
            o = _dsa_attention(prompt, proj, tail, d_cols, kidx_p, cache_dsa_k[i], cache_dsa_v[i], rel_bias,
                               o, o_shape, sbw, hp=DSA_HEADS_PER_STEP)
            o = _sb_attention(sample, proj, sb_cols, cache_sb_k[i], cache_sb_v[i], o, o_shape, 0,
                              hp=SB_HEADS_PER_STEP)
            o = _dsa_attention(sample, proj, tail, d_cols, kidx_s, cache_dsa_k[i], cache_dsa_v[i], rel_bias,
                               o, o_shape, sbw, hp=n_dsa)
            x, qc = _residual_matmul_then_proj(o, w_out16, i, x, None, norm_cross[l], w_cq16, l)
        else:
            u = _matmul(x, w_pw1_16, layer=i, gain=norm_mix[l], bias=b_pw1[i], glu=True)
            front = CONV_HALO - (CONV_WIDTH - 1)
            state_p = jnp.zeros((Bp, CONV_HALO, D), F32)
            state_s = jnp.concatenate([jnp.zeros((Bs, front, D), F32), state_conv[i]], axis=1)
            y = _conv_norm_swish(prompt, u, state_p, w_dw[i], b_dw[i], g_conv_norm[i], None)
            y = _conv_norm_swish(sample, u, state_s, w_dw[i], b_dw[i], g_conv_norm[i], y)
            outs["p_conv"].append(last_rows(u, prompt, state_p, CONV_WIDTH - 1))
            outs["s_conv"].append(last_rows(u, sample, state_s, CONV_WIDTH - 1))
            x, qc = _residual_matmul_then_proj(y, w_pw2_16, i, x, b_pw2[i], norm_cross[l], w_cq16, l)

        mk = _matmul(mem_flat, w_mk16, layer=l)
        mv = _matmul(mem_flat, w_mv16, layer=l)
        outs["p_mem_k"].append(mk.reshape(Bp, n_mem, cw // HEAD_DIM, HEAD_DIM))
        outs["p_mem_v"].append(mv.reshape(Bp, n_mem, cw // HEAD_DIM, HEAD_DIM))
        oc = _cross_attention(prompt, qc, mk.astype(BF16).reshape(Bp, n_mem, cw),
                              mv.astype(BF16).reshape(Bp, n_mem, cw), None)
        oc = _cross_attention(sample, qc, cache_mem_k[l].astype(BF16).reshape(Bs, n_mem, cw),
                              cache_mem_v[l].astype(BF16).reshape(Bs, n_mem, cw), oc)
        if l < depth - 1:
            x = _cross_out_mlp(x, oc, w_co16, norm_mlp[l], w_up16, w_down16, l)
        else:
            y_p, y_s = _cross_out_mlp(x, oc, w_co16, norm_mlp[l], w_up16, w_down16, l,
                                      final_gain=norm_final, m_prompt=Mp)

    st = lambda k: jnp.stack(outs[k])
    return (y_p.reshape(Bp, Tp, D), y_s.reshape(Bs, Ts, D), st("p_sb_k"), st("p_sb_v"), st("p_d_k"), st("p_d_v"),
            st("p_ki"), st("p_mem_k"), st("p_mem_v"), st("p_conv"), st("s_sb_k"), st("s_sb_v"), st("s_d_k"),
            st("s_d_v"), st("s_ki"), st("s_conv"))
```

```python
import functools
import math
from typing import NamedTuple

import numpy as np
import jax
import jax.numpy as jnp
from jax import lax
from jax.experimental import pallas as pl
from jax.experimental.pallas import tpu as pltpu

F32 = jnp.float32
BF16 = jnp.bfloat16

HEAD_DIM = 128
IDX_DIM = 64
N_IDX_HEADS = 16
CHUNK = 64
TOPK_MAX = 256
N_BUCKETS = 32
MAX_DISTANCE = 128
CONV_WIDTH = 31
CONV_HALO = 32
RMS_EPS = 1e-6
MASKED_LOGIT = -1e30
LOG2E = math.log2(math.e)
_INT32_MIN = -2 ** 31
VMEM_LIMIT_BYTES = 56 * 1024 * 1024
SB_HEADS_PER_STEP = 8
DSA_HEADS_PER_STEP = 4
SB_DEAD_BITS = 160.0
SB_INTERLEAVE = 4
ATTN_TILE = 256
SELECT_ROWS_PER_STEP = 512
FAR_TILES_PER_TRIP = 4


class _Group(NamedTuple):
    batch: int
    seq: int
    row0: int
    past: int


def _params(*semantics):
    return pltpu.CompilerParams(dimension_semantics=semantics, vmem_limit_bytes=VMEM_LIMIT_BYTES)


def _pick_tile(n, pref):
    if n <= pref:
        return n
    t = pref
    while n % t:
        t //= 2
    return t


def _row_tile(m, pref):
    t = min(m, pref) // 8 * 8
    while m % t:
        t -= 8
    return t


def _chained(prev):
    if prev is None:
        return [], [], None
    return [prev], [pl.BlockSpec(memory_space=pl.ANY)], True


def _rms_rows(x, g):
    ms = jnp.mean(x * x, axis=-1, keepdims=True)
    return x * lax.rsqrt(ms + RMS_EPS) * g


def _mm_kernel(*refs, has_norm, has_bias, has_res, glu):
    it = iter(refs)
    x_ref = next(it)
    g_ref = next(it) if has_norm else None
    w_ref = next(it)
    w2_ref = next(it) if glu else None
    b_ref = next(it) if has_bias else None
    b2_ref = next(it) if (has_bias and glu) else None
    r_ref = next(it) if has_res else None
    o_ref = next(it)
    xn_ref = next(it) if has_norm else None

    if has_norm:
        @pl.when(pl.program_id(1) == 0)
        def _():
            xn_ref[...] = _rms_rows(x_ref[...], g_ref[...]).astype(BF16)
        xb = xn_ref[...]
    else:
        xb = x_ref[...].astype(BF16)

    y = jnp.dot(xb, w_ref[...].astype(BF16), preferred_element_type=F32)
    if has_bias:
        y = y + b_ref[...]
    if glu:
        gate = jnp.dot(xb, w2_ref[...].astype(BF16), preferred_element_type=F32)
        if has_bias:
            gate = gate + b2_ref[...]
        y = y * (1.0 / (1.0 + jnp.exp(-gate)))
    if has_res:
        y = y + r_ref[...]
    o_ref[...] = y.astype(o_ref.dtype)


def _matmul(x, w, *, layer=0, gain=None, bias=None, residual=None, glu=False, out_dtype=F32, tm=1152, tn=512):
    M, K = x.shape
    n_out = w.shape[2] // 2 if glu else w.shape[2]
    tm = _row_tile(M, tm)
    tn = _pick_tile(n_out, tn)
    nj = n_out // tn
    has_norm, has_bias, has_res = gain is not None, bias is not None, residual is not None

    args = [x]
    specs = [pl.BlockSpec((tm, K), lambda i, j: (i, 0))]
    if has_norm:
        args.append(gain.reshape(1, K))
        specs.append(pl.BlockSpec((1, K), lambda i, j: (0, 0)))
    args.append(w)
    specs.append(pl.BlockSpec((None, K, tn), lambda i, j: (layer, 0, j)))
    if glu:
        args.append(w)
        specs.append(pl.BlockSpec((None, K, tn), lambda i, j: (layer, 0, j + nj)))
    if has_bias:
        b2d = bias.reshape(1, -1)
        args.append(b2d)
        specs.append(pl.BlockSpec((1, tn), lambda i, j: (0, j)))
        if glu:
            args.append(b2d)
            specs.append(pl.BlockSpec((1, tn), lambda i, j: (0, j + nj)))
    if has_res:
        args.append(residual)
        specs.append(pl.BlockSpec((tm, tn), lambda i, j: (i, j)))

    return pl.pallas_call(
        functools.partial(_mm_kernel, has_norm=has_norm, has_bias=has_bias, has_res=has_res, glu=glu),
        grid=(M // tm, nj),
        in_specs=specs,
        out_specs=pl.BlockSpec((tm, tn), lambda i, j: (i, j)),
        out_shape=jax.ShapeDtypeStruct((M, n_out), out_dtype),
        scratch_shapes=[pltpu.VMEM((tm, K), BF16)] if has_norm else [],
        compiler_params=_params("parallel", "arbitrary"),
        name="norm_matmul",
    )(*args)


def _res_proj_kernel(y_ref, w_ref, *refs, has_bias):
    it = iter(refs)
    b_ref = next(it) if has_bias else None
    r_ref, g_ref, wq_ref, o_ref, q_ref, x1_ref = (next(it) for _ in range(6))
    j = pl.program_id(1)
    y = jnp.dot(y_ref[...], w_ref[...].astype(BF16), preferred_element_type=F32) + r_ref[...]
    if has_bias:
        y = y + b_ref[...]
    o_ref[...] = y
    x1_ref[j] = y

    @pl.when(j == pl.num_programs(1) - 1)
    def _():
        x1 = jnp.concatenate([x1_ref[t] for t in range(x1_ref.shape[0])], axis=1)
        q_ref[...] = jnp.dot(_rms_rows(x1, g_ref[...]).astype(BF16), wq_ref[...],
                             preferred_element_type=F32).astype(q_ref.dtype)


def _residual_matmul_then_proj(y, w, layer, residual, bias, gain, wq, q_layer, *, tm=1152, tn=512):
    M, K = y.shape
    N, nq = w.shape[2], wq.shape[2]
    tm = _row_tile(M, tm)
    assert N % tn == 0
    nj = N // tn
    has_bias = bias is not None
    args = [y, w]
    specs = [pl.BlockSpec((tm, K), lambda i, j: (i, 0)), pl.BlockSpec((None, K, tn), lambda i, j: (layer, 0, j))]
    if has_bias:
        args.append(bias.reshape(1, N))
        specs.append(pl.BlockSpec((1, tn), lambda i, j: (0, j)))
    args += [residual, gain.reshape(1, N), wq]
    specs += [pl.BlockSpec((tm, tn), lambda i, j: (i, j)), pl.BlockSpec((1, N), lambda i, j: (0, 0)),
              pl.BlockSpec((None, N, nq), lambda i, j: (q_layer, 0, 0))]
    return pl.pallas_call(
        functools.partial(_res_proj_kernel, has_bias=has_bias),
        grid=(M // tm, nj),
        in_specs=specs,
        out_specs=[pl.BlockSpec((tm, tn), lambda i, j: (i, j)), pl.BlockSpec((tm, nq), lambda i, j: (i, 0))],
        out_shape=[jax.ShapeDtypeStruct((M, N), F32), jax.ShapeDtypeStruct((M, nq), BF16)],
        scratch_shapes=[pltpu.VMEM((nj, tm, tn), F32)],
        compiler_params=_params("parallel", "arbitrary"),
        name="residual_matmul_proj",
    )(*args)


def _in_proj_kernel(x_ref, g_ref, w_ref, s_ref, wt_ref, o16_ref, o32_ref, ot_ref, xn_ref, *, n32_tiles):
    j = pl.program_id(1)

    @pl.when(j == 0)
    def _():
        xn_ref[...] = _rms_rows(x_ref[...], g_ref[...]).astype(BF16)
        ot_ref[...] = jnp.dot(xn_ref[...], wt_ref[...], preferred_element_type=F32)

    y = jnp.dot(xn_ref[...], w_ref[...].astype(BF16), preferred_element_type=F32)
    o16_ref[...] = (y * s_ref[...]).astype(BF16)

    @pl.when(j < n32_tiles)
    def _():
        o32_ref[...] = y


def _in_proj(x, w, layer, src_tiles, gain, col_scale, w_tail, *, n32, tm=1152, tn=512):
    M, K = x.shape
    N = len(src_tiles) * tn
    nt = w_tail.shape[1]

    def src_tile(j):
        return sum(jnp.where(j == t, src, 0) for t, src in enumerate(src_tiles))
    tm = _row_tile(M, tm)
    assert N % tn == 0 and n32 % tn == 0
    n32_tiles = n32 // tn
    return pl.pallas_call(
        functools.partial(_in_proj_kernel, n32_tiles=n32_tiles),
        grid=(M // tm, N // tn),
        in_specs=[
            pl.BlockSpec((tm, K), lambda i, j: (i, 0)),
            pl.BlockSpec((1, K), lambda i, j: (0, 0)),
            pl.BlockSpec((None, K, tn), lambda i, j: (layer, 0, src_tile(j))),
            pl.BlockSpec((1, tn), lambda i, j: (0, j)),
            pl.BlockSpec((K, nt), lambda i, j: (0, 0)),
        ],
        out_specs=[
            pl.BlockSpec((tm, tn), lambda i, j: (i, j)),
            pl.BlockSpec((tm, tn), lambda i, j: (i, jnp.minimum(j, n32_tiles - 1))),
            pl.BlockSpec((tm, nt), lambda i, j: (i, 0)),
        ],
        out_shape=[jax.ShapeDtypeStruct((M, N), BF16), jax.ShapeDtypeStruct((M, n32), F32),
                   jax.ShapeDtypeStruct((M, nt), F32)],
        scratch_shapes=[pltpu.VMEM((tm, K), BF16)],
        compiler_params=_params("parallel", "arbitrary"),
        name="in_proj",
    )(x, gain.reshape(1, K), w, col_scale.reshape(1, N), w_tail)


def _mlp_kernel(x_ref, oc_ref, wco_ref, g_ref, wu_ref, wd_ref, *refs, n_prompt_tiles):
    final = n_prompt_tiles is not None
    if final:
        gf_ref, op_ref, os_ref, x2_ref, xn_ref, acc_ref = refs
    else:
        o_ref, x2_ref, xn_ref, acc_ref = refs
    i, f = pl.program_id(0), pl.program_id(1)

    @pl.when(f == 0)
    def _():
        x2 = x_ref[...] + jnp.dot(oc_ref[...], wco_ref[...], preferred_element_type=F32)
        x2_ref[...] = x2
        xn_ref[...] = _rms_rows(x2, g_ref[...]).astype(BF16)
        acc_ref[...] = jnp.zeros_like(acc_ref)

    h = jnp.dot(xn_ref[...], wu_ref[...], preferred_element_type=F32)
    h = jnp.maximum(h, 0.0)
    h = (h * h).astype(BF16)
    acc_ref[...] += jnp.dot(h, wd_ref[...], preferred_element_type=F32)

    last = f == pl.num_programs(1) - 1
    if not final:
        @pl.when(last)
        def _():
            o_ref[...] = x2_ref[...] + acc_ref[...]
    else:
        @pl.when(last & (i < n_prompt_tiles))
        def _():
            op_ref[...] = _rms_rows(x2_ref[...] + acc_ref[...], gf_ref[...])

        @pl.when(last & (i >= n_prompt_tiles))
        def _():
            os_ref[...] = _rms_rows(x2_ref[...] + acc_ref[...], gf_ref[...])


def _cross_out_mlp(x, oc, w_co, gain, w_up, w_down, layer, *, final_gain=None, m_prompt=None, tm=512, tf=512):
    M, D = x.shape
    F = w_up.shape[2]
    cw = oc.shape[1]
    final = final_gain is not None
    tm = _pick_tile(M - m_prompt, tm) if final else _pick_tile(M, tm)
    tf = _pick_tile(F, tf)
    args = [x, oc, w_co, gain.reshape(1, D), w_up, w_down]
    specs = [
        pl.BlockSpec((tm, D), lambda i, f: (i, 0)),
        pl.BlockSpec((tm, cw), lambda i, f: (i, 0)),
        pl.BlockSpec((None, cw, D), lambda i, f: (layer, 0, 0)),
        pl.BlockSpec((1, D), lambda i, f: (0, 0)),
        pl.BlockSpec((None, D, tf), lambda i, f: (layer, 0, f)),
        pl.BlockSpec((None, tf, D), lambda i, f: (layer, f, 0)),
    ]
    if final:
        assert m_prompt % tm == 0 and (M - m_prompt) % tm == 0
        npt = m_prompt // tm
        args.append(final_gain.reshape(1, D))
        specs.append(pl.BlockSpec((1, D), lambda i, f: (0, 0)))
        out_specs = [pl.BlockSpec((tm, D), lambda i, f: (jnp.minimum(i, npt - 1), 0)),
                     pl.BlockSpec((tm, D), lambda i, f: (jnp.maximum(i - npt, 0), 0))]
        out_shape = [jax.ShapeDtypeStruct((m_prompt, D), F32), jax.ShapeDtypeStruct((M - m_prompt, D), F32)]
        semantics = ("arbitrary", "arbitrary")
    else:
        npt = None
        out_specs = pl.BlockSpec((tm, D), lambda i, f: (i, 0))
        out_shape = jax.ShapeDtypeStruct((M, D), F32)
        semantics = ("parallel", "arbitrary")
    return pl.pallas_call(
        functools.partial(_mlp_kernel, n_prompt_tiles=npt),
        grid=(M // tm, F // tf),
        in_specs=specs,
        out_specs=out_specs,
        out_shape=out_shape,
        scratch_shapes=[pltpu.VMEM((tm, D), F32), pltpu.VMEM((tm, D), BF16), pltpu.VMEM((tm, D), F32)],
        compiler_params=_params(*semantics),
        name="cross_out_mlp",
    )(*args)


def _nt_dot(a, b):
    return lax.dot_general(a, b, (((1,), (1,)), ((), ())), preferred_element_type=F32)


def _kv_operands(grp, proj, k_col, v_col, k_cache, v_cache, *, tk, hp):
    wp = hp * HEAD_DIM
    if grp.past == 0:
        blk0 = grp.row0 // grp.seq
        args = [proj, proj]
        specs = [pl.BlockSpec((grp.seq, wp), lambda b, h, i, c=k_col // wp: (blk0 + b, c + h),
                              pipeline_mode=pl.Buffered(1)),
                 pl.BlockSpec((grp.seq, wp), lambda b, h, i, c=v_col // wp: (blk0 + b, c + h),
                              pipeline_mode=pl.Buffered(1))]
        return args, specs

    def new_tile(col):
        rows = lax.slice(proj, (grp.row0, col), (grp.row0 + grp.batch * grp.seq, col + k_cache.shape[2] * HEAD_DIM))
        rows = rows.reshape(grp.batch, grp.seq, -1)
        return jnp.pad(rows, ((0, 0), (0, tk - grp.seq), (0, 0)))

    args = [new_tile(k_col), new_tile(v_col)]
    specs = [pl.BlockSpec((None, tk, wp), lambda b, h, i: (b, 0, h))] * 2
    n_heads = k_cache.shape[2]
    for cache in (k_cache, v_cache):
        args.append(cache.reshape(grp.batch, grp.past * n_heads, HEAD_DIM))
        specs.append(pl.BlockSpec((None, grp.past * n_heads, HEAD_DIM), lambda b, h, i: (b, 0, 0)))
    return args, specs


def _split_kv_refs(refs, has_past):
    if not has_past:
        return refs[0], refs[1], None, None, refs[2:]
    return refs[0], refs[1], refs[2], refs[3], refs[4:]


def _tile_reader(new_ref, past_ref, *, tk, hp, n_heads):
    def read(j, hh, diag, n=1):
        hs = slice(hh * HEAD_DIM, (hh + 1) * HEAD_DIM)
        if past_ref is None:
            return new_ref[pl.ds(pl.multiple_of(j * tk, tk), n * tk), hs]
        if diag:
            return new_ref[:, hs]
        head = pl.program_id(1) * hp + hh
        return past_ref[pl.ds(j * (tk * n_heads) + head, n * tk, stride=n_heads), :].astype(BF16)
    return read


def _sb_kernel(q_ref, *refs, tq, tk, hp, n_heads, past, has_past):
    k_new, v_new, k_past, v_past, rest = _split_kv_refs(refs, has_past)
    u_ref, o_ref, acc_ref, carry_ref = rest[0], rest[-3], rest[-2], rest[-1]
    read_k = _tile_reader(k_new, k_past, tk=tk, hp=hp, n_heads=n_heads)
    read_v = _tile_reader(v_new, v_past, tk=tk, hp=hp, n_heads=n_heads)

    i = pl.program_id(2)
    q_start = past + i * tq
    jd = q_start // tk
    reps = tk // HEAD_DIM

    acc_ref[...] = jnp.zeros_like(acc_ref)
    carry_ref[...] = jnp.zeros_like(carry_ref)

    def step(j, masked):
        if masked:
            t_pos = q_start + lax.broadcasted_iota(jnp.int32, (tq, tk), 0)
            s_pos = j * tk + lax.broadcasted_iota(jnp.int32, (tq, tk), 1)
            causal = t_pos > s_pos
        for h0 in range(0, hp, SB_INTERLEAVE):
            group_step(j, masked, causal if masked else None, range(h0, h0 + SB_INTERLEAVE))

    def group_step(j, masked, causal, group):
        heads = [(hh, slice(hh * HEAD_DIM, (hh + 1) * HEAD_DIM)) for hh in group]
        z2s = [_nt_dot(q_ref[:, hs], read_k(j, hh, masked)) for hh, hs in heads]
        ns, log2_betas, css = [], [], []
        for z2 in z2s:
            neg_abs = lax.bitcast_convert_type(
                lax.bitcast_convert_type(z2, jnp.int32) | jnp.int32(_INT32_MIN), F32)
            n = jnp.maximum(z2, 0.0) + jnp.log(1.0 + jnp.exp2(neg_abs)) * LOG2E
            log2_betas.append(z2 - n)
            if masked:
                n = jnp.where(causal, n, 0.0)
            n16 = n.astype(BF16)
            css.append(jnp.dot(n16, u_ref[...], preferred_element_type=F32))
            ns.append(n16)
        for (hh, hs), n16, log2_beta, cs in zip(heads, ns, log2_betas, css):
            a = jnp.exp2(log2_beta - cs)
            if masked:
                a = jnp.where(causal, a, 0.0)
            carry = carry_ref[:, hs]
            pv = jnp.dot(a.astype(BF16), read_v(j, hh, masked), preferred_element_type=F32)
            acc_ref[:, hs] += jnp.exp2(-carry) * pv
            carry_ref[:, hs] = carry + jnp.broadcast_to(cs[:, 0:1] + n16[:, 0:1].astype(F32), (tq, HEAD_DIM))

    step(jd, True)

    def live(state):
        t, min_carry = state
        return (t < jd) & (min_carry < SB_DEAD_BITS)

    def body(state):
        t, _ = state
        step(jd - 1 - t, False)
        return t + 1, jnp.min(carry_ref[...])

    lax.while_loop(live, body, (jnp.int32(0), jnp.min(carry_ref[...])))
    o_ref[...] = acc_ref[...].astype(o_ref.dtype)


def _suffix_sum_matrix(tk):
    r = np.arange(tk)
    return jnp.asarray((r[:, None] > r[None, :]).astype(np.float32), dtype=BF16)


def _attn_tiles(grp):
    tq = _pick_tile(grp.seq, ATTN_TILE)
    tk = ATTN_TILE if grp.past else tq
    nq = grp.seq // tq
    assert grp.seq % tq == 0 and tk % tq == 0 and grp.past % tk == 0 and grp.row0 % tq == 0
    assert grp.past == 0 or nq == 1
    assert grp.past > 0 or grp.row0 % grp.seq == 0
    return tq, tk, nq


def _sb_attention(grp, proj, cols, k_cache, v_cache, o_prev, o_shape, o_col, *, hp):
    H = k_cache.shape[2]
    tq, tk, nq = _attn_tiles(grp)
    wp = hp * HEAD_DIM
    q_blk0 = grp.row0 // tq
    kv_args, kv_specs = _kv_operands(grp, proj, cols[1], cols[2], k_cache, v_cache, tk=tk, hp=hp)
    prev_args, prev_specs, aliased = _chained(o_prev)
    n_in = 1 + len(kv_args) + 1
    return pl.pallas_call(
        functools.partial(_sb_kernel, tq=tq, tk=tk, hp=hp, n_heads=H, past=grp.past, has_past=grp.past > 0),
        grid=(grp.batch, H // hp, nq),
        in_specs=[pl.BlockSpec((tq, wp), lambda b, h, i: (q_blk0 + b * nq + i, cols[0] // wp + h))]
        + kv_specs
        + [pl.BlockSpec((tk, tk), lambda b, h, i: (0, 0), pipeline_mode=pl.Buffered(1))]
        + prev_specs,
        out_specs=pl.BlockSpec((tq, wp), lambda b, h, i: (q_blk0 + b * nq + i, o_col // wp + h)),
        out_shape=jax.ShapeDtypeStruct(o_shape, BF16),
        input_output_aliases={n_in: 0} if aliased else {},
        scratch_shapes=[pltpu.VMEM((tq, wp), F32), pltpu.VMEM((tq, wp), F32)],
        compiler_params=_params("parallel", "parallel", "arbitrary"),
        name="sb_attention",
    )(proj, *kv_args, _suffix_sum_matrix(tk), *prev_args)


def _sortable_key(x):
    b = lax.bitcast_convert_type(x, jnp.int32)
    return b ^ ((b >> 31) & jnp.int32(0x7FFFFFFF))


_KEY_NEG_INF = int(np.int32(np.array(-np.inf, np.float32).view(np.int32) ^ 0x7FFFFFFF))
_KEY_LOWEST_FINITE = int(np.int32(np.array(np.finfo(np.float32).min, np.float32).view(np.int32) ^ 0x7FFFFFFF))
COUNT_ROW_GROUP = 128


def _dsa_select_kernel(qi_ref, wi_ref, ki2_ref, mask_ref, keys_ref, wb_ref,
                       *, tq, tk, nkt, nsub, by_batch, past, topk):
    i = pl.program_id(1)
    rows = nsub * tq
    first_start = past + i * tq * (1 if by_batch else nsub)
    jd_min = first_start // tk
    n_cut = 1 if by_batch else nsub
    jd = jd_min + n_cut - 1
    reps = tk // HEAD_DIM

    w = wi_ref[:, IDX_DIM:IDX_DIM + N_IDX_HEADS] * (IDX_DIM ** -0.5 * N_IDX_HEADS ** -0.5)
    for hh in range(N_IDX_HEADS):
        wb_ref[hh] = jnp.broadcast_to(w[:, hh:hh + 1], (rows, HEAD_DIM))

    def idx_rows(j, r0, nr, kt):
        tot = jnp.zeros((nr, tk), F32)
        for p in range(N_IDX_HEADS // 2):
            sc = jnp.dot(qi_ref[r0:r0 + nr, p * 2 * IDX_DIM:(p + 1) * 2 * IDX_DIM], kt,
                         preferred_element_type=F32)
            w0 = jnp.concatenate([wb_ref[2 * p, r0:r0 + nr, :]] * reps, axis=1)
            w1 = jnp.concatenate([wb_ref[2 * p + 1, r0:r0 + nr, :]] * reps, axis=1)
            tot = tot + w0 * jnp.maximum(sc[:, :tk], 0.0) + w1 * jnp.maximum(sc[:, tk:], 0.0)
        return _sortable_key(tot)

    def idx_tile(j):
        if by_batch:
            return jnp.concatenate([idx_rows(j, s * tq, tq, ki2_ref[s, j]) for s in range(nsub)], axis=0)
        return idx_rows(j, 0, rows, ki2_ref[0, j])

    def idx_body(j, c):
        keys_ref[j] = idx_tile(j)
        return c

    lax.fori_loop(0, jd_min, idx_body, 0)
    row = lax.broadcasted_iota(jnp.int32, (rows, tk), 0)
    t_pos = first_start + (row % tq if by_batch else row)
    for d in range(n_cut):
        s_pos = (jd_min + d) * tk + lax.broadcasted_iota(jnp.int32, (rows, tk), 1)
        visible = (t_pos // CHUNK) >= (s_pos // CHUNK)
        keys_ref[jd_min + d] = jnp.where(visible, idx_tile(jd_min + d), jnp.int32(_KEY_NEG_INF))

    n_vis = jd + 1

    @pl.when(n_vis % 2 == 1)
    def _():
        keys_ref[n_vis] = jnp.full((rows, tk), _INT32_MIN, jnp.int32)

    n_pairs = (n_vis + 1) // 2
    rg = min(COUNT_ROW_GROUP, rows)

    def count_ge(cand):
        parts = []
        for g in range(rows // rg):
            cg = cand[g * rg:(g + 1) * rg]

            def body(t, cnt, g=g, cg=cg):
                for jj in (2 * t, 2 * t + 1):
                    kj = keys_ref[jj, pl.ds(g * rg, rg), :]
                    for r in range(reps):
                        cnt = cnt + jnp.where(kj[:, r * HEAD_DIM:(r + 1) * HEAD_DIM] >= cg, 1.0, 0.0)
                return cnt

            parts.append(lax.fori_loop(0, n_pairs, body, jnp.zeros((rg, HEAD_DIM), F32)))
        return jnp.sum(jnp.concatenate(parts, axis=0), axis=-1, keepdims=True)

    kf = float(topk)
    zero = jnp.zeros((rows, HEAD_DIM), jnp.int32)
    c = count_ge(zero)
    res = jnp.where(c >= kf, zero, jnp.int32(_INT32_MIN))
    cnt = jnp.where(c >= kf, c, float(tk) * (2 * n_pairs).astype(F32))
    for bit in range(30, -1, -1):
        cand = res + jnp.int32(1 << bit)
        c = count_ge(cand)
        res = jnp.where(c >= kf, cand, res)
        cnt = jnp.where(c >= kf, c, cnt)

    tied_cut = jnp.max(jnp.where((cnt != kf) & (res[:, 0:1] > jnp.int32(_KEY_NEG_INF)), 1.0, 0.0)) > 0.0

    @pl.when(tied_cut)
    def _():
        col = lax.broadcasted_iota(jnp.int32, (rg, tk), 1)

        def count_tied(before, strict_above):
            parts = []
            for g in range(rows // rg):
                grp_rows = slice(g * rg, (g + 1) * rg)
                res_g = jnp.concatenate([res[grp_rows]] * reps, axis=1)
                bef_g = None if strict_above else jnp.concatenate([before[grp_rows]] * reps, axis=1)

                def body(j, acc, g=g, res_g=res_g, bef_g=bef_g):
                    kj = keys_ref[j, pl.ds(g * rg, rg), :]
                    if strict_above:
                        hit = jnp.where(kj > res_g, 1.0, 0.0)
                    else:
                        hit = jnp.where(kj == res_g, jnp.where(j * tk + col < bef_g, 1.0, 0.0), 0.0)
                    return acc + sum(hit[:, r * HEAD_DIM:(r + 1) * HEAD_DIM] for r in range(reps))

                parts.append(lax.fori_loop(0, n_vis, body, jnp.zeros((rg, HEAD_DIM), F32)))
            return jnp.sum(jnp.concatenate(parts, axis=0), axis=-1, keepdims=True)

        need = kf - count_tied(None, True)
        r_keep = jnp.zeros((rows, HEAD_DIM), jnp.int32)
        for bit in range((nkt * tk).bit_length() - 1, -1, -1):
            cand = r_keep | jnp.int32(1 << bit)
            r_keep = jnp.where(count_tied(cand, False) < need, cand, r_keep)

        def drop_late_ties(j, c):
            for g in range(rows // rg):
                grp_rows = slice(g * rg, (g + 1) * rg)
                res_g = jnp.concatenate([res[grp_rows]] * reps, axis=1)
                keep_g = jnp.concatenate([r_keep[grp_rows]] * reps, axis=1)
                kj = keys_ref[j, pl.ds(g * rg, rg), :]
                keys_ref[j, pl.ds(g * rg, rg), :] = jnp.where(
                    kj == res_g, jnp.where(j * tk + col > keep_g, res_g - 1, kj), kj)
            return c

        lax.fori_loop(0, n_vis, drop_late_ties, 0)

    thr = jnp.maximum(res, jnp.int32(_KEY_LOWEST_FINITE))
    thr = jnp.concatenate([thr] * reps, axis=1)

    def write_mask(j, c):
        tile = jnp.where(keys_ref[j] >= thr, 0.0, MASKED_LOGIT).astype(mask_ref.dtype)
        for s_ in range(nsub):
            mask_ref[s_, j] = tile[s_ * tq:(s_ + 1) * tq]
        return c

    lax.fori_loop(0, jd + 1, write_mask, 0)

    def write_hidden(j, c):
        for s_ in range(nsub):
            mask_ref[s_, j] = jnp.full((tq, tk), MASKED_LOGIT, mask_ref.dtype)
        return c

    lax.fori_loop(jd + 1, nkt, write_hidden, 0)


def _dsa_flash_kernel(q_ref, *refs, tq, tk, hp, n_heads, past, has_past):
    k_new, v_new, k_past, v_past, rest = _split_kv_refs(refs, has_past)
    mask_ref, bnear_ref = rest[0], rest[1]
    o_ref, m_ref, l_ref, acc_ref = rest[-4], rest[-3], rest[-2], rest[-1]
    read_k = _tile_reader(k_new, k_past, tk=tk, hp=hp, n_heads=n_heads)
    read_v = _tile_reader(v_new, v_past, tk=tk, hp=hp, n_heads=n_heads)

    i = pl.program_id(2)
    jd = (past + i * tq) // tk
    reps = tk // HEAD_DIM
    heads = [slice(hh * HEAD_DIM, (hh + 1) * HEAD_DIM) for hh in range(hp)]

    m_ref[...] = jnp.full_like(m_ref, MASKED_LOGIT)
    l_ref[...] = jnp.zeros_like(l_ref)
    acc_ref[...] = jnp.zeros_like(acc_ref)

    def step(j, near, n=1):
        diag = near == 0
        maskf = jnp.concatenate([mask_ref[j + t].astype(F32) for t in range(n)], axis=1)
        logits = []
        for hh, hs in enumerate(heads):
            s = _nt_dot(q_ref[:, hs], read_k(j, hh, diag, n)) + maskf
            if near is not None:
                s = s + bnear_ref[hh, near]
            logits.append(s)
        probs, alphas = [], []
        for hs, s in zip(heads, logits):
            m_old = m_ref[:, hs]
            m_new = jnp.maximum(m_old, jnp.max(s, axis=-1, keepdims=True))
            alpha = jnp.exp2(m_old - m_new)
            p = jnp.exp2(s - jnp.concatenate([m_new] * (n * reps), axis=1))
            l_ref[:, hs] = alpha * l_ref[:, hs] + sum(p[:, r * HEAD_DIM:(r + 1) * HEAD_DIM]
                                                      for r in range(n * reps))
            m_ref[:, hs] = m_new
            probs.append(p.astype(BF16))
            alphas.append(alpha)
        for hh, (hs, p, alpha) in enumerate(zip(heads, probs, alphas)):
            pv = jnp.dot(p, read_v(j, hh, diag, n), preferred_element_type=F32)
            acc_ref[:, hs] = alpha * acc_ref[:, hs] + pv

    step(jd, 0)

    @pl.when(jd >= 1)
    def _():
        step(jd - 1, 1)

    n_far = jnp.maximum(jd - 1, 0)
    n_trips = n_far // FAR_TILES_PER_TRIP

    def single(j, c):
        step(j, None)
        return c

    lax.fori_loop(n_trips * FAR_TILES_PER_TRIP, n_far, single, 0)

    def body(t, c):
        step(FAR_TILES_PER_TRIP * t, None, FAR_TILES_PER_TRIP)
        return c

    lax.fori_loop(0, n_trips, body, 0)
    for hs in heads:
        o_ref[:, hs] = (acc_ref[:, hs] / jnp.sum(l_ref[:, hs], axis=-1, keepdims=True)).astype(o_ref.dtype)


def _t5_bucket(rel):
    nb = N_BUCKETS // 2
    max_exact = nb // 2
    side = jnp.where(rel > 0, nb, 0)
    n = jnp.abs(rel)
    nf = jnp.maximum(n, 1).astype(F32)
    large = max_exact + (jnp.log(nf / max_exact) / math.log(MAX_DISTANCE / max_exact) * (nb - max_exact)).astype(jnp.int32)
    large = jnp.minimum(large, nb - 1)
    return side + jnp.where(n < max_exact, n, large)


def _near_bias(rel_bias, *, tq, tk):
    r = jnp.arange(tq, dtype=jnp.int32)[:, None]
    c = jnp.arange(tk, dtype=jnp.int32)[None, :]
    bucket = _t5_bucket(jnp.stack([c - r, c - r - tk]))
    far_bucket = _t5_bucket(jnp.int32(-2 * tk))
    table = (rel_bias - rel_bias[far_bucket][None, :]) * LOG2E
    onehot = (bucket[..., None] == jnp.arange(N_BUCKETS, dtype=jnp.int32)).astype(F32)
    return jnp.einsum("dqkb,bh->hdqk", onehot, table, precision=lax.Precision.HIGHEST)


def _dsa_attention(grp, proj, tail, cols, k_idx_all, k_cache, v_cache, rel_bias, o_prev, o_shape, o_col, *, hp):
    H = k_cache.shape[2]
    tq, tk, nq = _attn_tiles(grp)
    s_pad = k_idx_all.shape[1]
    nkt = s_pad // tk
    assert tk % CHUNK == 0 and tq % CHUNK == 0 and s_pad % tk == 0
    assert tk >= MAX_DISTANCE and H % hp == 0
    topk = min(TOPK_MAX, (grp.past + grp.seq) // 4)
    q_blk0 = grp.row0 // tq
    qiw = N_IDX_HEADS * IDX_DIM

    kt = jnp.transpose(k_idx_all.reshape(grp.batch, nkt, tk, IDX_DIM), (0, 1, 3, 2))
    z = jnp.zeros_like(kt)
    ki2 = jnp.concatenate([jnp.concatenate([kt, z], axis=3), jnp.concatenate([z, kt], axis=3)], axis=2)
    resident = pl.Buffered(1) if nq > 1 else None

    by_batch = nq == 1
    nsub = max(SELECT_ROWS_PER_STEP // tq, 1)
    while (grp.batch if by_batch else nq) % nsub or (grp.row0 // tq) % nsub or (not by_batch and tq != tk):
        nsub //= 2
    rows = nsub * tq
    sel_blk0 = grp.row0 // rows
    if by_batch:
        grid = (grp.batch // nsub, 1)
        row_map = lambda b, i: sel_blk0 + b
        ki2_spec = pl.BlockSpec((nsub, nkt, 2 * IDX_DIM, 2 * tk), lambda b, i: (b, 0, 0, 0))
        mask_spec = pl.BlockSpec((nsub, None, nkt, tq, tk), lambda b, i: (b, 0, 0, 0, 0))
    else:
        grid = (grp.batch, nq // nsub)
        row_map = lambda b, i: sel_blk0 + b * (nq // nsub) + i
        ki2_spec = pl.BlockSpec((1, nkt, 2 * IDX_DIM, 2 * tk), lambda b, i: (b, 0, 0, 0), pipeline_mode=resident)
        mask_spec = pl.BlockSpec((None, nsub, nkt, tq, tk), lambda b, i: (b, i, 0, 0, 0))

    mask = pl.pallas_call(
        functools.partial(_dsa_select_kernel, tq=tq, tk=tk, nkt=nkt, nsub=nsub, by_batch=by_batch,
                          past=grp.past, topk=topk),
        grid=grid,
        in_specs=[
            pl.BlockSpec((rows, qiw), lambda b, i: (row_map(b, i), cols[0] // qiw)),
            pl.BlockSpec((rows, HEAD_DIM), lambda b, i: (row_map(b, i), 0)),
            ki2_spec,
        ],
        out_specs=mask_spec,
        out_shape=jax.ShapeDtypeStruct((grp.batch, nq, nkt, tq, tk), BF16),
        scratch_shapes=[
            pltpu.VMEM((nkt + 1, rows, tk), jnp.int32),
            pltpu.VMEM((N_IDX_HEADS, rows, HEAD_DIM), F32),
        ],
        compiler_params=_params("parallel", "parallel"),
        name="dsa_select",
    )(proj, tail, ki2)

    wp = hp * HEAD_DIM
    kv_args, kv_specs = _kv_operands(grp, proj, cols[2], cols[3], k_cache, v_cache, tk=tk, hp=hp)
    prev_args, prev_specs, aliased = _chained(o_prev)
    n_in = 1 + len(kv_args) + 2
    return pl.pallas_call(
        functools.partial(_dsa_flash_kernel, tq=tq, tk=tk, hp=hp, n_heads=H, past=grp.past, has_past=grp.past > 0),
        grid=(grp.batch, H // hp, nq),
        in_specs=[pl.BlockSpec((tq, wp), lambda b, h, i: (q_blk0 + b * nq + i, cols[1] // wp + h))]
        + kv_specs
        + [pl.BlockSpec((None, None, nkt, tq, tk), lambda b, h, i: (b, i, 0, 0, 0)),
           pl.BlockSpec((hp, 2, tq, tk), lambda b, h, i: (h, 0, 0, 0), pipeline_mode=resident)]
        + prev_specs,
        out_specs=pl.BlockSpec((tq, wp), lambda b, h, i: (q_blk0 + b * nq + i, o_col // wp + h)),
        out_shape=jax.ShapeDtypeStruct(o_shape, BF16),
        input_output_aliases={n_in: 0} if aliased else {},
        scratch_shapes=[pltpu.VMEM((tq, wp), F32), pltpu.VMEM((tq, wp), F32), pltpu.VMEM((tq, wp), F32)],
        compiler_params=_params("parallel", "parallel", "arbitrary"),
        name="dsa_flash",
    )(proj, *kv_args, mask, _near_bias(rel_bias, tq=tq, tk=tk), *prev_args)


def _cross_kernel(q_ref, k_ref, v_ref, *rest, n_heads, scale):
    o_ref = rest[-1]
    for hh in range(n_heads):
        sl = slice(hh * HEAD_DIM, (hh + 1) * HEAD_DIM)
        s = _nt_dot(q_ref[:, sl], k_ref[:, sl]) * scale
        p = jnp.exp(s - jnp.max(s, axis=-1, keepdims=True))
        denom = jnp.sum(p, axis=-1, keepdims=True)
        o = jnp.dot(p.astype(BF16), v_ref[:, sl], preferred_element_type=F32)
        o_ref[:, sl] = (o / denom).astype(o_ref.dtype)


def _cross_attention(grp, q, mem_k, mem_v, o_prev, *, tq=512):
    M, W = q.shape
    n_mem = mem_k.shape[1]
    tq = _pick_tile(grp.seq, tq)
    nq = grp.seq // tq
    assert grp.row0 % tq == 0
    blk0 = grp.row0 // tq
    prev_args, prev_specs, aliased = _chained(o_prev)
    return pl.pallas_call(
        functools.partial(_cross_kernel, n_heads=W // HEAD_DIM, scale=HEAD_DIM ** -0.5),
        grid=(grp.batch, nq),
        in_specs=[
            pl.BlockSpec((tq, W), lambda b, i: (blk0 + b * nq + i, 0)),
            pl.BlockSpec((None, n_mem, W), lambda b, i: (b, 0, 0)),
            pl.BlockSpec((None, n_mem, W), lambda b, i: (b, 0, 0)),
        ] + prev_specs,
        out_specs=pl.BlockSpec((tq, W), lambda b, i: (blk0 + b * nq + i, 0)),
        out_shape=jax.ShapeDtypeStruct((M, W), BF16),
        input_output_aliases={3: 0} if aliased else {},
        compiler_params=_params("parallel", "parallel"),
        name="cross_attention",
    )(q, mem_k, mem_v, *prev_args)


CONV_ROW_CHUNK = 128
SUBLANES = 8


def _conv_kernel(a_ref, prev_ref, state_ref, w_ref, bias_ref, g_ref, *rest, tt):
    o_ref, buf_ref, q_ref, y_ref = rest[-4], rest[-3], rest[-2], rest[-1]
    D = a_ref.shape[-1]

    @pl.when(pl.program_id(1) == 0)
    def _():
        buf_ref[0:CONV_HALO, :] = state_ref[...]

    @pl.when(pl.program_id(1) > 0)
    def _():
        buf_ref[0:CONV_HALO, :] = prev_ref[...]

    buf_ref[CONV_HALO:CONV_HALO + tt, :] = a_ref[...]
    buf_ref[CONV_HALO + tt:CONV_HALO + tt + SUBLANES, :] = jnp.zeros((SUBLANES, D), F32)

    first = CONV_HALO - (CONV_WIDTH - 1)
    rc = min(CONV_ROW_CHUNK, tt)
    for c0 in range(0, D, HEAD_DIM):
        cs = slice(c0, c0 + HEAD_DIM)
        for t0 in range(0, tt, rc):
            out = jnp.zeros((rc, HEAD_DIM), F32)
            for r in range(SUBLANES):
                q = None
                for k in range(CONV_WIDTH):
                    if (first + k) % SUBLANES != r:
                        continue
                    base = t0 + first + k - r
                    term = buf_ref[base:base + rc + SUBLANES, cs] * w_ref[k:k + 1, cs]
                    q = term if q is None else q + term
                if r == 0:
                    out = out + q[0:rc]
                else:
                    q_ref[...] = q
                    out = out + q_ref[r:r + rc, :]
            y_ref[t0:t0 + rc, cs] = out + bias_ref[:, cs]
    y = _rms_rows(y_ref[...], g_ref[...])
    o_ref[...] = (y * (1.0 / (1.0 + jnp.exp(-y)))).astype(o_ref.dtype)


def _conv_norm_swish(grp, u, state, w_dw, b_dw, g_norm, o_prev):
    M, D = u.shape
    tt = _pick_tile(grp.seq, 256)
    nt = grp.seq // tt
    assert grp.seq % tt == 0 and tt % CONV_HALO == 0 and grp.row0 % tt == 0
    blk0 = grp.row0 // tt
    hb = tt // CONV_HALO
    prev_args, prev_specs, aliased = _chained(o_prev)
    return pl.pallas_call(
        functools.partial(_conv_kernel, tt=tt),
        grid=(grp.batch, nt),
        in_specs=[
            pl.BlockSpec((tt, D), lambda b, i: (blk0 + b * nt + i, 0)),
            pl.BlockSpec((CONV_HALO, D), lambda b, i: (jnp.maximum((blk0 + b * nt + i) * hb - 1, 0), 0)),
            pl.BlockSpec((None, CONV_HALO, D), lambda b, i: (b, 0, 0)),
            pl.BlockSpec((CONV_WIDTH, D), lambda b, i: (0, 0)),
            pl.BlockSpec((1, D), lambda b, i: (0, 0)),
            pl.BlockSpec((1, D), lambda b, i: (0, 0)),
        ] + prev_specs,
        out_specs=pl.BlockSpec((tt, D), lambda b, i: (blk0 + b * nt + i, 0)),
        out_shape=jax.ShapeDtypeStruct((M, D), BF16),
        input_output_aliases={6: 0} if aliased else {},
        scratch_shapes=[pltpu.VMEM((tt + CONV_HALO + SUBLANES, D), F32),
                        pltpu.VMEM((min(CONV_ROW_CHUNK, tt) + SUBLANES, HEAD_DIM), F32),
                        pltpu.VMEM((tt, D), F32)],
        compiler_params=_params("parallel", "arbitrary"),
        name="conv_norm_swish",
    )(u, u, state, w_dw, b_dw.reshape(1, D), g_norm.reshape(1, D), *prev_args)


def kernel(x_prompt, x_sample, mem_prompt, cache_sb_k, cache_sb_v, cache_dsa_k, cache_dsa_v, cache_idx_k, cache_mem_k, cache_mem_v, state_conv, norm_mix, norm_cross, norm_mlp, norm_final, w_in_ab, w_out_ab, rel_bias, w_pw1, b_pw1, w_dw, b_dw, g_conv_norm, w_pw2, b_pw2, w_cq, w_mk, w_mv, w_co, w_up, w_down):
    Bp, Tp, D = x_prompt.shape
    Bs, Ts, _ = x_sample.shape
    past = cache_sb_k.shape[2]
    depth = norm_mix.shape[0]
    n_mem = mem_prompt.shape[1]
    n_sb, n_dsa = cache_sb_k.shape[3], cache_dsa_k.shape[3]
    sbw, dsw = n_sb * HEAD_DIM, n_dsa * HEAD_DIM
    qiw = N_IDX_HEADS * IDX_DIM
    cw = w_cq.shape[2]
    Mp, Ms = Bp * Tp, Bs * Ts
    M = Mp + Ms
    prompt = _Group(Bp, Tp, 0, 0)
    sample = _Group(Bs, Ts, Mp, past)

    x = jnp.concatenate([x_prompt.reshape(Mp, D), x_sample.reshape(Ms, D)], axis=0)
    mem_flat = mem_prompt.reshape(Bp * n_mem, D)

    src = dict(zip(("q_sb", "k_sb", "v_sb", "q_d", "k_d", "v_d", "q_i", "k_i", "w_i"),
                   np.cumsum([0, sbw, sbw, sbw, dsw, dsw, dsw, qiw, IDX_DIM])))
    order = ("k_sb", "v_sb", "k_d", "v_d", "q_sb", "q_d", "q_i")
    width = dict(q_sb=sbw, k_sb=sbw, v_sb=sbw, q_d=dsw, k_d=dsw, v_d=dsw, q_i=qiw)
    col = dict(zip(order, np.cumsum([0] + [width[n] for n in order])[:-1]))
    col = {k: int(v) for k, v in col.items()}
    n32 = 2 * sbw + 2 * dsw

    outs = {k: [] for k in ("p_sb_k", "p_sb_v", "p_d_k", "p_d_v", "p_ki", "p_mem_k", "p_mem_v", "p_conv",
                            "s_sb_k", "s_sb_v", "s_d_k", "s_d_v", "s_ki", "s_conv")}

    def split_rows(a):
        return a[:Mp].reshape(Bp, Tp, -1), a[Mp:].reshape(Bs, Ts, -1)

    def last_rows(a, grp, state, n):
        if grp.seq >= n:
            return jnp.stack([a[grp.row0 + (b + 1) * grp.seq - n:grp.row0 + (b + 1) * grp.seq]
                              for b in range(grp.batch)])
        rows = a[grp.row0:grp.row0 + grp.batch * grp.seq].reshape(grp.batch, grp.seq, -1)
        return jnp.concatenate([state, rows], axis=1)[:, -n:]

    w_cq16, w_mk16, w_mv16, w_co16 = (w.astype(BF16) for w in (w_cq, w_mk, w_mv, w_co))
    w_up16, w_down16 = w_up.astype(BF16), w_down.astype(BF16)
    q_scale = HEAD_DIM ** -0.5 * LOG2E
    col_scale = jnp.concatenate([jnp.full((width[n],), q_scale if n in ("q_sb", "q_d") else 1.0, F32)
                                 for n in order])

    for l in range(depth):
        i = l // 2
        if l % 2 == 0:
            w_in = w_in_ab[i]
            tail_w = w_in.shape[1] - int(src["k_i"])
            w_tail = jnp.pad(w_in[:, src["k_i"]:], ((0, 0), (0, HEAD_DIM - tail_w))).astype(BF16)
            tn = 512
            src_tiles = [(int(src[n]) + c) // tn for n in order for c in range(0, width[n], tn)]
            proj, kv32, tail = _in_proj(x, w_in_ab, i, src_tiles, norm_mix[l], col_scale, w_tail,
                                        n32=n32, tn=tn)

            for name, key, nh in (("sb_k", "k_sb", n_sb), ("sb_v", "v_sb", n_sb),
                                  ("d_k", "k_d", n_dsa), ("d_v", "v_d", n_dsa)):
                ap, as_ = split_rows(kv32[:, col[key]:col[key] + width[key]])
                outs["p_" + name].append(ap.reshape(Bp, Tp, nh, HEAD_DIM))
                outs["s_" + name].append(as_.reshape(Bs, Ts, nh, HEAD_DIM))
            k_i = tail[:, :IDX_DIM]
            kip, kis = split_rows(k_i)
            outs["p_ki"].append(kip)
            outs["s_ki"].append(kis)

            tk_s = ATTN_TILE
            s_pad_s = -(-(past + Ts) // tk_s) * tk_s
            kidx_p = kip.astype(BF16)
            kidx_s = jnp.concatenate([cache_idx_k[i].astype(BF16), kis.astype(BF16),
                                      jnp.zeros((Bs, s_pad_s - past - Ts, IDX_DIM), BF16)], axis=1)

            o_shape = (M, sbw + dsw)
            sb_cols = (col["q_sb"], col["k_sb"], col["v_sb"])
            d_cols = (col["q_i"], col["q_d"], col["k_d"], col["v_d"])
            o = _sb_attention(prompt, proj, sb_cols, cache_sb_k[i], cache_sb_v[i], None, o_shape, 0,
                              hp=SB_HEADS_PER_STEP)
            o = _dsa_attention(prompt, proj, tail, d_cols, kidx_p, cache_dsa_k[i], cache_dsa_v[i], rel_bias,
                               o, o_shape, sbw, hp=DSA_HEADS_PER_STEP)
            o = _sb_attention(sample, proj, sb_cols, cache_sb_k[i], cache_sb_v[i], o, o_shape, 0,
                              hp=SB_HEADS_PER_STEP)
            o = _dsa_attention(sample, proj, tail, d_cols, kidx_s, cache_dsa_k[i], cache_dsa_v[i], rel_bias,
                               o, o_shape, sbw, hp=n_dsa)
            x, qc = _residual_matmul_then_proj(o, w_out_ab, i, x, None, norm_cross[l], w_cq16, l)
        else:
            u = _matmul(x, w_pw1, layer=i, gain=norm_mix[l], bias=b_pw1[i], glu=True)
            front = CONV_HALO - (CONV_WIDTH - 1)
            state_p = jnp.zeros((Bp, CONV_HALO, D), F32)
            state_s = jnp.concatenate([jnp.zeros((Bs, front, D), F32), state_conv[i]], axis=1)
            y = _conv_norm_swish(prompt, u, state_p, w_dw[i], b_dw[i], g_conv_norm[i], None)
            y = _conv_norm_swish(sample, u, state_s, w_dw[i], b_dw[i], g_conv_norm[i], y)
            outs["p_conv"].append(last_rows(u, prompt, state_p, CONV_WIDTH - 1))
            outs["s_conv"].append(last_rows(u, sample, state_s, CONV_WIDTH - 1))
            x, qc = _residual_matmul_then_proj(y, w_pw2, i, x, b_pw2[i], norm_cross[l], w_cq16, l)

        mk = _matmul(mem_flat, w_mk16, layer=l)
        mv = _matmul(mem_flat, w_mv16, layer=l)
        outs["p_mem_k"].append(mk.reshape(Bp, n_mem, cw // HEAD_DIM, HEAD_DIM))
        outs["p_mem_v"].append(mv.reshape(Bp, n_mem, cw // HEAD_DIM, HEAD_DIM))
        oc = _cross_attention(prompt, qc, mk.astype(BF16).reshape(Bp, n_mem, cw),
                              mv.astype(BF16).reshape(Bp, n_mem, cw), None)
        oc = _cross_attention(sample, qc, cache_mem_k[l].astype(BF16).reshape(Bs, n_mem, cw),
                              cache_mem_v[l].astype(BF16).reshape(Bs, n_mem, cw), oc)
        if l < depth - 1:
            x = _cross_out_mlp(x, oc, w_co16, norm_mlp[l], w_up16, w_down16, l)
        else:
            y_p, y_s = _cross_out_mlp(x, oc, w_co16, norm_mlp[l], w_up16, w_down16, l,
                                      final_gain=norm_final, m_prompt=Mp)

    st = lambda k: jnp.stack(outs[k])
    return (y_p.reshape(Bp, Tp, D), y_s.reshape(Bs, Ts, D), st("p_sb_k"), st("p_sb_v"), st("p_d_k"), st("p_d_v"),
            st("p_ki"), st("p_mem_k"), st("p_mem_v"), st("p_conv"), st("s_sb_k"), st("s_sb_v"), st("s_d_k"),
            st("s_d_v"), st("s_ki"), st("s_conv"))
```

```python
import functools
import math
from typing import NamedTuple

import numpy as np
import jax
import jax.numpy as jnp
from jax import lax
from jax.experimental import pallas as pl
from jax.experimental.pallas import tpu as pltpu

F32 = jnp.float32
BF16 = jnp.bfloat16

HEAD_DIM = 128
IDX_DIM = 64
N_IDX_HEADS = 16
CHUNK = 64
TOPK_MAX = 256
N_BUCKETS = 32
MAX_DISTANCE = 128
CONV_WIDTH = 31
CONV_HALO = 32
RMS_EPS = 1e-6
MASKED_LOGIT = -1e30
LOG2E = math.log2(math.e)
_INT32_MIN = -2 ** 31
VMEM_LIMIT_BYTES = 56 * 1024 * 1024
SB_HEADS_PER_STEP = 8
DSA_HEADS_PER_STEP = 4
SB_DEAD_BITS = 160.0
SB_INTERLEAVE = 4
ATTN_TILE = 256
SELECT_ROWS_PER_STEP = 512
FAR_TILES_PER_TRIP = 4


class _Group(NamedTuple):
    batch: int
    seq: int
    row0: int
    past: int


def _params(*semantics):
    return pltpu.CompilerParams(dimension_semantics=semantics, vmem_limit_bytes=VMEM_LIMIT_BYTES)


def _pick_tile(n, pref):
    if n <= pref:
        return n
    t = pref
    while n % t:
        t //= 2
    return t


def _row_tile(m, pref):
    t = min(m, pref) // 8 * 8
    while m % t:
        t -= 8
    return t


def _chained(prev):
    if prev is None:
        return [], [], None
    return [prev], [pl.BlockSpec(memory_space=pl.ANY)], True


def _rms_rows(x, g):
    ms = jnp.mean(x * x, axis=-1, keepdims=True)
    return x * lax.rsqrt(ms + RMS_EPS) * g


def _mm_kernel(*refs, has_norm, has_bias, has_res, glu):
    it = iter(refs)
    x_ref = next(it)
    g_ref = next(it) if has_norm else None
    w_ref = next(it)
    w2_ref = next(it) if glu else None
    b_ref = next(it) if has_bias else None
    b2_ref = next(it) if (has_bias and glu) else None
    r_ref = next(it) if has_res else None
    o_ref = next(it)
    xn_ref = next(it) if has_norm else None

    if has_norm:
        @pl.when(pl.program_id(1) == 0)
        def _():
            xn_ref[...] = _rms_rows(x_ref[...], g_ref[...]).astype(BF16)
        xb = xn_ref[...]
    else:
        xb = x_ref[...].astype(BF16)

    y = jnp.dot(xb, w_ref[...], preferred_element_type=F32)
    if has_bias:
        y = y + b_ref[...]
    if glu:
        gate = jnp.dot(xb, w2_ref[...], preferred_element_type=F32)
        if has_bias:
            gate = gate + b2_ref[...]
        y = y * (1.0 / (1.0 + jnp.exp(-gate)))
    if has_res:
        y = y + r_ref[...]
    o_ref[...] = y.astype(o_ref.dtype)


def _matmul(x, w, *, layer=0, gain=None, bias=None, residual=None, glu=False, out_dtype=F32, tm=1152, tn=512):
    M, K = x.shape
    n_out = w.shape[2] // 2 if glu else w.shape[2]
    tm = _row_tile(M, tm)
    tn = _pick_tile(n_out, tn)
    nj = n_out // tn
    has_norm, has_bias, has_res = gain is not None, bias is not None, residual is not None

    args = [x]
    specs = [pl.BlockSpec((tm, K), lambda i, j: (i, 0))]
    if has_norm:
        args.append(gain.reshape(1, K))
        specs.append(pl.BlockSpec((1, K), lambda i, j: (0, 0)))
    args.append(w)
    specs.append(pl.BlockSpec((None, K, tn), lambda i, j: (layer, 0, j)))
    if glu:
        args.append(w)
        specs.append(pl.BlockSpec((None, K, tn), lambda i, j: (layer, 0, j + nj)))
    if has_bias:
        b2d = bias.reshape(1, -1)
        args.append(b2d)
        specs.append(pl.BlockSpec((1, tn), lambda i, j: (0, j)))
        if glu:
            args.append(b2d)
            specs.append(pl.BlockSpec((1, tn), lambda i, j: (0, j + nj)))
    if has_res:
        args.append(residual)
        specs.append(pl.BlockSpec((tm, tn), lambda i, j: (i, j)))

    return pl.pallas_call(
        functools.partial(_mm_kernel, has_norm=has_norm, has_bias=has_bias, has_res=has_res, glu=glu),
        grid=(M // tm, nj),
        in_specs=specs,
        out_specs=pl.BlockSpec((tm, tn), lambda i, j: (i, j)),
        out_shape=jax.ShapeDtypeStruct((M, n_out), out_dtype),
        scratch_shapes=[pltpu.VMEM((tm, K), BF16)] if has_norm else [],
        compiler_params=_params("parallel", "arbitrary"),
        name="norm_matmul",
    )(*args)


def _res_proj_kernel(y_ref, w_ref, *refs, has_bias):
    it = iter(refs)
    b_ref = next(it) if has_bias else None
    r_ref, g_ref, wq_ref, o_ref, q_ref, x1_ref = (next(it) for _ in range(6))
    j = pl.program_id(1)
    y = jnp.dot(y_ref[...], w_ref[...], preferred_element_type=F32) + r_ref[...]
    if has_bias:
        y = y + b_ref[...]
    o_ref[...] = y
    x1_ref[j] = y

    @pl.when(j == pl.num_programs(1) - 1)
    def _():
        x1 = jnp.concatenate([x1_ref[t] for t in range(x1_ref.shape[0])], axis=1)
        q_ref[...] = jnp.dot(_rms_rows(x1, g_ref[...]).astype(BF16), wq_ref[...],
                             preferred_element_type=F32).astype(q_ref.dtype)


def _residual_matmul_then_proj(y, w, layer, residual, bias, gain, wq, q_layer, *, tm=1152, tn=512):
    M, K = y.shape
    N, nq = w.shape[2], wq.shape[2]
    tm = _row_tile(M, tm)
    assert N % tn == 0
    nj = N // tn
    has_bias = bias is not None
    args = [y, w]
    specs = [pl.BlockSpec((tm, K), lambda i, j: (i, 0)), pl.BlockSpec((None, K, tn), lambda i, j: (layer, 0, j))]
    if has_bias:
        args.append(bias.reshape(1, N))
        specs.append(pl.BlockSpec((1, tn), lambda i, j: (0, j)))
    args += [residual, gain.reshape(1, N), wq]
    specs += [pl.BlockSpec((tm, tn), lambda i, j: (i, j)), pl.BlockSpec((1, N), lambda i, j: (0, 0)),
              pl.BlockSpec((None, N, nq), lambda i, j: (q_layer, 0, 0))]
    return pl.pallas_call(
        functools.partial(_res_proj_kernel, has_bias=has_bias),
        grid=(M // tm, nj),
        in_specs=specs,
        out_specs=[pl.BlockSpec((tm, tn), lambda i, j: (i, j)), pl.BlockSpec((tm, nq), lambda i, j: (i, 0))],
        out_shape=[jax.ShapeDtypeStruct((M, N), F32), jax.ShapeDtypeStruct((M, nq), BF16)],
        scratch_shapes=[pltpu.VMEM((nj, tm, tn), F32)],
        compiler_params=_params("parallel", "arbitrary"),
        name="residual_matmul_proj",
    )(*args)


def _in_proj_kernel(x_ref, g_ref, w_ref, s_ref, wt_ref, o16_ref, o32_ref, ot_ref, xn_ref, *, n32_tiles):
    j = pl.program_id(1)

    @pl.when(j == 0)
    def _():
        xn_ref[...] = _rms_rows(x_ref[...], g_ref[...]).astype(BF16)
        ot_ref[...] = jnp.dot(xn_ref[...], wt_ref[...], preferred_element_type=F32)

    y = jnp.dot(xn_ref[...], w_ref[...], preferred_element_type=F32)
    o16_ref[...] = (y * s_ref[...]).astype(BF16)

    @pl.when(j < n32_tiles)
    def _():
        o32_ref[...] = y


def _in_proj(x, w, gain, col_scale, w_tail, *, n32, tm=1152, tn=512):
    M, K = x.shape
    N = w.shape[1]
    nt = w_tail.shape[1]
    tm = _row_tile(M, tm)
    assert N % tn == 0 and n32 % tn == 0
    n32_tiles = n32 // tn
    return pl.pallas_call(
        functools.partial(_in_proj_kernel, n32_tiles=n32_tiles),
        grid=(M // tm, N // tn),
        in_specs=[
            pl.BlockSpec((tm, K), lambda i, j: (i, 0)),
            pl.BlockSpec((1, K), lambda i, j: (0, 0)),
            pl.BlockSpec((K, tn), lambda i, j: (0, j)),
            pl.BlockSpec((1, tn), lambda i, j: (0, j)),
            pl.BlockSpec((K, nt), lambda i, j: (0, 0)),
        ],
        out_specs=[
            pl.BlockSpec((tm, tn), lambda i, j: (i, j)),
            pl.BlockSpec((tm, tn), lambda i, j: (i, jnp.minimum(j, n32_tiles - 1))),
            pl.BlockSpec((tm, nt), lambda i, j: (i, 0)),
        ],
        out_shape=[jax.ShapeDtypeStruct((M, N), BF16), jax.ShapeDtypeStruct((M, n32), F32),
                   jax.ShapeDtypeStruct((M, nt), F32)],
        scratch_shapes=[pltpu.VMEM((tm, K), BF16)],
        compiler_params=_params("parallel", "arbitrary"),
        name="in_proj",
    )(x, gain.reshape(1, K), w, col_scale.reshape(1, N), w_tail)


def _mlp_kernel(x_ref, oc_ref, wco_ref, g_ref, wu_ref, wd_ref, *refs, n_prompt_tiles):
    final = n_prompt_tiles is not None
    if final:
        gf_ref, op_ref, os_ref, x2_ref, xn_ref, acc_ref = refs
    else:
        o_ref, x2_ref, xn_ref, acc_ref = refs
    i, f = pl.program_id(0), pl.program_id(1)

    @pl.when(f == 0)
    def _():
        x2 = x_ref[...] + jnp.dot(oc_ref[...], wco_ref[...], preferred_element_type=F32)
        x2_ref[...] = x2
        xn_ref[...] = _rms_rows(x2, g_ref[...]).astype(BF16)
        acc_ref[...] = jnp.zeros_like(acc_ref)

    h = jnp.dot(xn_ref[...], wu_ref[...], preferred_element_type=F32)
    h = jnp.maximum(h, 0.0)
    h = (h * h).astype(BF16)
    acc_ref[...] += jnp.dot(h, wd_ref[...], preferred_element_type=F32)

    last = f == pl.num_programs(1) - 1
    if not final:
        @pl.when(last)
        def _():
            o_ref[...] = x2_ref[...] + acc_ref[...]
    else:
        @pl.when(last & (i < n_prompt_tiles))
        def _():
            op_ref[...] = _rms_rows(x2_ref[...] + acc_ref[...], gf_ref[...])

        @pl.when(last & (i >= n_prompt_tiles))
        def _():
            os_ref[...] = _rms_rows(x2_ref[...] + acc_ref[...], gf_ref[...])


def _cross_out_mlp(x, oc, w_co, gain, w_up, w_down, layer, *, final_gain=None, m_prompt=None, tm=512, tf=512):
    M, D = x.shape
    F = w_up.shape[2]
    cw = oc.shape[1]
    final = final_gain is not None
    tm = _pick_tile(M - m_prompt, tm) if final else _pick_tile(M, tm)
    tf = _pick_tile(F, tf)
    args = [x, oc, w_co, gain.reshape(1, D), w_up, w_down]
    specs = [
        pl.BlockSpec((tm, D), lambda i, f: (i, 0)),
        pl.BlockSpec((tm, cw), lambda i, f: (i, 0)),
        pl.BlockSpec((None, cw, D), lambda i, f: (layer, 0, 0)),
        pl.BlockSpec((1, D), lambda i, f: (0, 0)),
        pl.BlockSpec((None, D, tf), lambda i, f: (layer, 0, f)),
        pl.BlockSpec((None, tf, D), lambda i, f: (layer, f, 0)),
    ]
    if final:
        assert m_prompt % tm == 0 and (M - m_prompt) % tm == 0
        npt = m_prompt // tm
        args.append(final_gain.reshape(1, D))
        specs.append(pl.BlockSpec((1, D), lambda i, f: (0, 0)))
        out_specs = [pl.BlockSpec((tm, D), lambda i, f: (jnp.minimum(i, npt - 1), 0)),
                     pl.BlockSpec((tm, D), lambda i, f: (jnp.maximum(i - npt, 0), 0))]
        out_shape = [jax.ShapeDtypeStruct((m_prompt, D), F32), jax.ShapeDtypeStruct((M - m_prompt, D), F32)]
        semantics = ("arbitrary", "arbitrary")
    else:
        npt = None
        out_specs = pl.BlockSpec((tm, D), lambda i, f: (i, 0))
        out_shape = jax.ShapeDtypeStruct((M, D), F32)
        semantics = ("parallel", "arbitrary")
    return pl.pallas_call(
        functools.partial(_mlp_kernel, n_prompt_tiles=npt),
        grid=(M // tm, F // tf),
        in_specs=specs,
        out_specs=out_specs,
        out_shape=out_shape,
        scratch_shapes=[pltpu.VMEM((tm, D), F32), pltpu.VMEM((tm, D), BF16), pltpu.VMEM((tm, D), F32)],
        compiler_params=_params(*semantics),
        name="cross_out_mlp",
    )(*args)


def _nt_dot(a, b):
    return lax.dot_general(a, b, (((1,), (1,)), ((), ())), preferred_element_type=F32)


def _kv_operands(grp, proj, k_col, v_col, k_cache, v_cache, *, tk, hp):
    wp = hp * HEAD_DIM
    if grp.past == 0:
        blk0 = grp.row0 // grp.seq
        args = [proj, proj]
        specs = [pl.BlockSpec((grp.seq, wp), lambda b, h, i, c=k_col // wp: (blk0 + b, c + h),
                              pipeline_mode=pl.Buffered(1)),
                 pl.BlockSpec((grp.seq, wp), lambda b, h, i, c=v_col // wp: (blk0 + b, c + h),
                              pipeline_mode=pl.Buffered(1))]
        return args, specs

    def new_tile(col):
        rows = lax.slice(proj, (grp.row0, col), (grp.row0 + grp.batch * grp.seq, col + k_cache.shape[2] * HEAD_DIM))
        rows = rows.reshape(grp.batch, grp.seq, -1)
        return jnp.pad(rows, ((0, 0), (0, tk - grp.seq), (0, 0)))

    args = [new_tile(k_col), new_tile(v_col)]
    specs = [pl.BlockSpec((None, tk, wp), lambda b, h, i: (b, 0, h))] * 2
    n_heads = k_cache.shape[2]
    for cache in (k_cache, v_cache):
        args.append(cache.reshape(grp.batch, grp.past * n_heads, HEAD_DIM))
        specs.append(pl.BlockSpec((None, grp.past * n_heads, HEAD_DIM), lambda b, h, i: (b, 0, 0)))
    return args, specs


def _split_kv_refs(refs, has_past):
    if not has_past:
        return refs[0], refs[1], None, None, refs[2:]
    return refs[0], refs[1], refs[2], refs[3], refs[4:]


def _tile_reader(new_ref, past_ref, *, tk, hp, n_heads):
    def read(j, hh, diag, n=1):
        hs = slice(hh * HEAD_DIM, (hh + 1) * HEAD_DIM)
        if past_ref is None:
            return new_ref[pl.ds(pl.multiple_of(j * tk, tk), n * tk), hs]
        if diag:
            return new_ref[:, hs]
        head = pl.program_id(1) * hp + hh
        return past_ref[pl.ds(j * (tk * n_heads) + head, n * tk, stride=n_heads), :].astype(BF16)
    return read


def _sb_kernel(q_ref, *refs, tq, tk, hp, n_heads, past, has_past):
    k_new, v_new, k_past, v_past, rest = _split_kv_refs(refs, has_past)
    u_ref, o_ref, acc_ref, carry_ref = rest[0], rest[-3], rest[-2], rest[-1]
    read_k = _tile_reader(k_new, k_past, tk=tk, hp=hp, n_heads=n_heads)
    read_v = _tile_reader(v_new, v_past, tk=tk, hp=hp, n_heads=n_heads)

    i = pl.program_id(2)
    q_start = past + i * tq
    jd = q_start // tk
    reps = tk // HEAD_DIM

    acc_ref[...] = jnp.zeros_like(acc_ref)
    carry_ref[...] = jnp.zeros_like(carry_ref)

    def step(j, masked):
        if masked:
            t_pos = q_start + lax.broadcasted_iota(jnp.int32, (tq, tk), 0)
            s_pos = j * tk + lax.broadcasted_iota(jnp.int32, (tq, tk), 1)
            causal = t_pos > s_pos
        for h0 in range(0, hp, SB_INTERLEAVE):
            group_step(j, masked, causal if masked else None, range(h0, h0 + SB_INTERLEAVE))

    def group_step(j, masked, causal, group):
        heads = [(hh, slice(hh * HEAD_DIM, (hh + 1) * HEAD_DIM)) for hh in group]
        z2s = [_nt_dot(q_ref[:, hs], read_k(j, hh, masked)) for hh, hs in heads]
        ns, log2_betas, css = [], [], []
        for z2 in z2s:
            neg_abs = lax.bitcast_convert_type(
                lax.bitcast_convert_type(z2, jnp.int32) | jnp.int32(_INT32_MIN), F32)
            n = jnp.maximum(z2, 0.0) + jnp.log(1.0 + jnp.exp2(neg_abs)) * LOG2E
            log2_betas.append(z2 - n)
            if masked:
                n = jnp.where(causal, n, 0.0)
            n16 = n.astype(BF16)
            css.append(jnp.dot(n16, u_ref[...], preferred_element_type=F32))
            ns.append(n16)
        for (hh, hs), n16, log2_beta, cs in zip(heads, ns, log2_betas, css):
            a = jnp.exp2(log2_beta - cs)
            if masked:
                a = jnp.where(causal, a, 0.0)
            carry = carry_ref[:, hs]
            pv = jnp.dot(a.astype(BF16), read_v(j, hh, masked), preferred_element_type=F32)
            acc_ref[:, hs] += jnp.exp2(-carry) * pv
            carry_ref[:, hs] = carry + jnp.broadcast_to(cs[:, 0:1] + n16[:, 0:1].astype(F32), (tq, HEAD_DIM))

    step(jd, True)

    def live(state):
        t, min_carry = state
        return (t < jd) & (min_carry < SB_DEAD_BITS)

    def body(state):
        t, _ = state
        step(jd - 1 - t, False)
        return t + 1, jnp.min(carry_ref[...])

    lax.while_loop(live, body, (jnp.int32(0), jnp.min(carry_ref[...])))
    o_ref[...] = acc_ref[...].astype(o_ref.dtype)


def _suffix_sum_matrix(tk):
    r = np.arange(tk)
    return jnp.asarray((r[:, None] > r[None, :]).astype(np.float32), dtype=BF16)


def _attn_tiles(grp):
    tq = _pick_tile(grp.seq, ATTN_TILE)
    tk = ATTN_TILE if grp.past else tq
    nq = grp.seq // tq
    assert grp.seq % tq == 0 and tk % tq == 0 and grp.past % tk == 0 and grp.row0 % tq == 0
    assert grp.past == 0 or nq == 1
    assert grp.past > 0 or grp.row0 % grp.seq == 0
    return tq, tk, nq


def _sb_attention(grp, proj, cols, k_cache, v_cache, o_prev, o_shape, o_col, *, hp):
    H = k_cache.shape[2]
    tq, tk, nq = _attn_tiles(grp)
    wp = hp * HEAD_DIM
    q_blk0 = grp.row0 // tq
    kv_args, kv_specs = _kv_operands(grp, proj, cols[1], cols[2], k_cache, v_cache, tk=tk, hp=hp)
    prev_args, prev_specs, aliased = _chained(o_prev)
    n_in = 1 + len(kv_args) + 1
    return pl.pallas_call(
        functools.partial(_sb_kernel, tq=tq, tk=tk, hp=hp, n_heads=H, past=grp.past, has_past=grp.past > 0),
        grid=(grp.batch, H // hp, nq),
        in_specs=[pl.BlockSpec((tq, wp), lambda b, h, i: (q_blk0 + b * nq + i, cols[0] // wp + h))]
        + kv_specs
        + [pl.BlockSpec((tk, tk), lambda b, h, i: (0, 0), pipeline_mode=pl.Buffered(1))]
        + prev_specs,
        out_specs=pl.BlockSpec((tq, wp), lambda b, h, i: (q_blk0 + b * nq + i, o_col // wp + h)),
        out_shape=jax.ShapeDtypeStruct(o_shape, BF16),
        input_output_aliases={n_in: 0} if aliased else {},
        scratch_shapes=[pltpu.VMEM((tq, wp), F32), pltpu.VMEM((tq, wp), F32)],
        compiler_params=_params("parallel", "parallel", "arbitrary"),
        name="sb_attention",
    )(proj, *kv_args, _suffix_sum_matrix(tk), *prev_args)


def _sortable_key_inverse(k):
    return k ^ ((k >> 31) & jnp.int32(0x7FFFFFFF))


def _sortable_key(x):
    return _sortable_key_inverse(lax.bitcast_convert_type(x, jnp.int32))


_KEY_NEG_INF = int(np.int32(np.array(-np.inf, np.float32).view(np.int32) ^ 0x7FFFFFFF))
_KEY_LOWEST_FINITE = int(np.int32(np.array(np.finfo(np.float32).min, np.float32).view(np.int32) ^ 0x7FFFFFFF))
COUNT_ROW_GROUP = 128


def _dsa_select_kernel(qi_ref, wi_ref, ki2_ref, mask_ref, keys_ref, wb_ref, colmax_ref, res_ref, cnt_ref,
                       *, tq, tk, nkt, nsub, by_batch, past, topk):
    i = pl.program_id(1)
    rows = nsub * tq
    first_start = past + i * tq * (1 if by_batch else nsub)
    jd_min = first_start // tk
    n_cut = 1 if by_batch else nsub
    jd = jd_min + n_cut - 1
    reps = tk // HEAD_DIM

    w = wi_ref[:, IDX_DIM:IDX_DIM + N_IDX_HEADS] * (IDX_DIM ** -0.5 * N_IDX_HEADS ** -0.5)
    for hh in range(N_IDX_HEADS):
        wb_ref[hh] = jnp.broadcast_to(w[:, hh:hh + 1], (rows, HEAD_DIM))

    def idx_rows(j, r0, nr, kt):
        tot = jnp.zeros((nr, tk), F32)
        for p in range(N_IDX_HEADS // 2):
            sc = jnp.dot(qi_ref[r0:r0 + nr, p * 2 * IDX_DIM:(p + 1) * 2 * IDX_DIM], kt,
                         preferred_element_type=F32)
            w0 = jnp.concatenate([wb_ref[2 * p, r0:r0 + nr, :]] * reps, axis=1)
            w1 = jnp.concatenate([wb_ref[2 * p + 1, r0:r0 + nr, :]] * reps, axis=1)
            tot = tot + w0 * jnp.maximum(sc[:, :tk], 0.0) + w1 * jnp.maximum(sc[:, tk:], 0.0)
        return _sortable_key(tot)

    def idx_tile(j):
        if by_batch:
            return jnp.concatenate([idx_rows(j, s * tq, tq, ki2_ref[s, j]) for s in range(nsub)], axis=0)
        return idx_rows(j, 0, rows, ki2_ref[0, j])

    row = lax.broadcasted_iota(jnp.int32, (rows, tk), 0)
    t_pos = first_start + (row % tq if by_batch else row)
    for d in range(n_cut):
        s_pos = (jd_min + d) * tk + lax.broadcasted_iota(jnp.int32, (rows, tk), 1)
        visible = (t_pos // CHUNK) >= (s_pos // CHUNK)
        key = jnp.where(visible, idx_tile(jd_min + d), jnp.int32(_KEY_NEG_INF))
        keys_ref[jd_min + d] = key
        colmax_ref[...] = key if d == 0 else jnp.maximum(colmax_ref[...], key)

    def idx_body(j, c):
        key = idx_tile(j)
        keys_ref[j] = key
        colmax_ref[...] = jnp.maximum(colmax_ref[...], key)
        return c

    lax.fori_loop(0, jd_min, idx_body, 0)

    n_vis = jd + 1

    @pl.when(n_vis % 2 == 1)
    def _():
        keys_ref[n_vis] = jnp.full((rows, tk), _INT32_MIN, jnp.int32)

    n_pairs = (n_vis + 1) // 2
    rg = min(COUNT_ROW_GROUP, rows)

    def count_ge(cand):
        parts = []
        for g in range(rows // rg):
            cg = cand[g * rg:(g + 1) * rg]

            def body(t, cnt, g=g, cg=cg):
                for jj in (2 * t, 2 * t + 1):
                    kj = keys_ref[jj, pl.ds(g * rg, rg), :]
                    for r in range(reps):
                        cnt = cnt + jnp.where(kj[:, r * HEAD_DIM:(r + 1) * HEAD_DIM] >= cg, 1.0, 0.0)
                return cnt

            parts.append(lax.fori_loop(0, n_pairs, body, jnp.zeros((rg, HEAD_DIM), F32)))
        return jnp.sum(jnp.concatenate(parts, axis=0), axis=-1, keepdims=True)

    kf = float(topk)
    off = jnp.int32(_INT32_MIN)
    col_f = lax.bitcast_convert_type(_sortable_key_inverse(colmax_ref[...]), F32)
    hi_key = _sortable_key(jnp.max(col_f, axis=-1, keepdims=True))
    lo_key = _sortable_key(jnp.min(col_f, axis=-1, keepdims=True))
    n_bits = jnp.max((32 - lax.clz(hi_key ^ lo_key)).astype(F32)).astype(jnp.int32)
    low_mask = jnp.where(n_bits >= 32, jnp.int32(-1), (jnp.int32(1) << jnp.minimum(n_bits, 31)) - 1)
    res_ref[...] = jnp.broadcast_to((lo_key ^ off) & ~low_mask, (rows, HEAD_DIM))
    cnt_ref[...] = jnp.broadcast_to(count_ge(res_ref[...] ^ off), (rows, HEAD_DIM))
    for bit in range(31, -1, -1):
        @pl.when(bit < n_bits)
        def _(bit=bit):
            cand = res_ref[...] | jnp.int32(_INT32_MIN if bit == 31 else 1 << bit)
            c = count_ge(cand ^ off)
            res_ref[...] = jnp.where(c >= kf, cand, res_ref[...])
            cnt_ref[...] = jnp.where(c >= kf, c, cnt_ref[...])
    res = res_ref[...] ^ off
    cnt = cnt_ref[:, 0:1]

    tied_cut = jnp.max(jnp.where((cnt != kf) & (res[:, 0:1] > jnp.int32(_KEY_NEG_INF)), 1.0, 0.0)) > 0.0

    @pl.when(tied_cut)
    def _():
        col = lax.broadcasted_iota(jnp.int32, (rg, tk), 1)

        def count_tied(before, strict_above):
            parts = []
            for g in range(rows // rg):
                grp_rows = slice(g * rg, (g + 1) * rg)
                res_g = jnp.concatenate([res[grp_rows]] * reps, axis=1)
                bef_g = None if strict_above else jnp.concatenate([before[grp_rows]] * reps, axis=1)

                def body(j, acc, g=g, res_g=res_g, bef_g=bef_g):
                    kj = keys_ref[j, pl.ds(g * rg, rg), :]
                    if strict_above:
                        hit = jnp.where(kj > res_g, 1.0, 0.0)
                    else:
                        hit = jnp.where(kj == res_g, jnp.where(j * tk + col < bef_g, 1.0, 0.0), 0.0)
                    return acc + sum(hit[:, r * HEAD_DIM:(r + 1) * HEAD_DIM] for r in range(reps))

                parts.append(lax.fori_loop(0, n_vis, body, jnp.zeros((rg, HEAD_DIM), F32)))
            return jnp.sum(jnp.concatenate(parts, axis=0), axis=-1, keepdims=True)

        need = kf - count_tied(None, True)
        r_keep = jnp.zeros((rows, HEAD_DIM), jnp.int32)
        for bit in range((nkt * tk).bit_length() - 1, -1, -1):
            cand = r_keep | jnp.int32(1 << bit)
            r_keep = jnp.where(count_tied(cand, False) < need, cand, r_keep)

        def drop_late_ties(j, c):
            for g in range(rows // rg):
                grp_rows = slice(g * rg, (g + 1) * rg)
                res_g = jnp.concatenate([res[grp_rows]] * reps, axis=1)
                keep_g = jnp.concatenate([r_keep[grp_rows]] * reps, axis=1)
                kj = keys_ref[j, pl.ds(g * rg, rg), :]
                keys_ref[j, pl.ds(g * rg, rg), :] = jnp.where(
                    kj == res_g, jnp.where(j * tk + col > keep_g, res_g - 1, kj), kj)
            return c

        lax.fori_loop(0, n_vis, drop_late_ties, 0)

    thr = jnp.maximum(res, jnp.int32(_KEY_LOWEST_FINITE))
    thr = jnp.concatenate([thr] * reps, axis=1)

    def write_mask(j, c):
        tile = jnp.where(keys_ref[j] >= thr, 0.0, MASKED_LOGIT).astype(mask_ref.dtype)
        for s_ in range(nsub):
            mask_ref[s_, j] = tile[s_ * tq:(s_ + 1) * tq]
        return c

    lax.fori_loop(0, jd + 1, write_mask, 0)

    def write_hidden(j, c):
        for s_ in range(nsub):
            mask_ref[s_, j] = jnp.full((tq, tk), MASKED_LOGIT, mask_ref.dtype)
        return c

    lax.fori_loop(jd + 1, nkt, write_hidden, 0)


def _dsa_flash_kernel(q_ref, *refs, tq, tk, hp, n_heads, past, has_past):
    k_new, v_new, k_past, v_past, rest = _split_kv_refs(refs, has_past)
    mask_ref, bnear_ref = rest[0], rest[1]
    o_ref, m_ref, l_ref, acc_ref = rest[-4], rest[-3], rest[-2], rest[-1]
    read_k = _tile_reader(k_new, k_past, tk=tk, hp=hp, n_heads=n_heads)
    read_v = _tile_reader(v_new, v_past, tk=tk, hp=hp, n_heads=n_heads)

    i = pl.program_id(2)
    jd = (past + i * tq) // tk
    reps = tk // HEAD_DIM
    heads = [slice(hh * HEAD_DIM, (hh + 1) * HEAD_DIM) for hh in range(hp)]

    m_ref[...] = jnp.full_like(m_ref, MASKED_LOGIT)
    l_ref[...] = jnp.zeros_like(l_ref)
    acc_ref[...] = jnp.zeros_like(acc_ref)

    def step(j, near, n=1):
        diag = near == 0
        maskf = jnp.concatenate([mask_ref[j + t].astype(F32) for t in range(n)], axis=1)
        logits = []
        for hh, hs in enumerate(heads):
            s = _nt_dot(q_ref[:, hs], read_k(j, hh, diag, n)) + maskf
            if near is not None:
                s = s + bnear_ref[hh, near]
            logits.append(s)
        probs, alphas = [], []
        for hs, s in zip(heads, logits):
            m_old = m_ref[:, hs]
            m_new = jnp.maximum(m_old, jnp.max(s, axis=-1, keepdims=True))
            alpha = jnp.exp2(m_old - m_new)
            p = jnp.exp2(s - jnp.concatenate([m_new] * (n * reps), axis=1))
            l_ref[:, hs] = alpha * l_ref[:, hs] + sum(p[:, r * HEAD_DIM:(r + 1) * HEAD_DIM]
                                                      for r in range(n * reps))
            m_ref[:, hs] = m_new
            probs.append(p.astype(BF16))
            alphas.append(alpha)
        for hh, (hs, p, alpha) in enumerate(zip(heads, probs, alphas)):
            pv = jnp.dot(p, read_v(j, hh, diag, n), preferred_element_type=F32)
            acc_ref[:, hs] = alpha * acc_ref[:, hs] + pv

    step(jd, 0)

    @pl.when(jd >= 1)
    def _():
        step(jd - 1, 1)

    n_far = jnp.maximum(jd - 1, 0)
    n_trips = n_far // FAR_TILES_PER_TRIP

    def single(j, c):
        step(j, None)
        return c

    lax.fori_loop(n_trips * FAR_TILES_PER_TRIP, n_far, single, 0)

    def body(t, c):
        step(FAR_TILES_PER_TRIP * t, None, FAR_TILES_PER_TRIP)
        return c

    lax.fori_loop(0, n_trips, body, 0)
    for hs in heads:
        o_ref[:, hs] = (acc_ref[:, hs] / jnp.sum(l_ref[:, hs], axis=-1, keepdims=True)).astype(o_ref.dtype)


def _t5_bucket(rel):
    nb = N_BUCKETS // 2
    max_exact = nb // 2
    side = jnp.where(rel > 0, nb, 0)
    n = jnp.abs(rel)
    nf = jnp.maximum(n, 1).astype(F32)
    large = max_exact + (jnp.log(nf / max_exact) / math.log(MAX_DISTANCE / max_exact) * (nb - max_exact)).astype(jnp.int32)
    large = jnp.minimum(large, nb - 1)
    return side + jnp.where(n < max_exact, n, large)


def _near_bias(rel_bias, *, tq, tk):
    r = jnp.arange(tq, dtype=jnp.int32)[:, None]
    c = jnp.arange(tk, dtype=jnp.int32)[None, :]
    bucket = _t5_bucket(jnp.stack([c - r, c - r - tk]))
    far_bucket = _t5_bucket(jnp.int32(-2 * tk))
    table = (rel_bias - rel_bias[far_bucket][None, :]) * LOG2E
    onehot = (bucket[..., None] == jnp.arange(N_BUCKETS, dtype=jnp.int32)).astype(F32)
    return jnp.einsum("dqkb,bh->hdqk", onehot, table, precision=lax.Precision.HIGHEST)


def _dsa_attention(grp, proj, tail, cols, k_idx_all, k_cache, v_cache, rel_bias, o_prev, o_shape, o_col, *, hp):
    H = k_cache.shape[2]
    tq, tk, nq = _attn_tiles(grp)
    s_pad = k_idx_all.shape[1]
    nkt = s_pad // tk
    assert tk % CHUNK == 0 and tq % CHUNK == 0 and s_pad % tk == 0
    assert tk >= MAX_DISTANCE and H % hp == 0
    topk = min(TOPK_MAX, (grp.past + grp.seq) // 4)
    assert topk <= tk
    q_blk0 = grp.row0 // tq
    qiw = N_IDX_HEADS * IDX_DIM

    kt = jnp.transpose(k_idx_all.reshape(grp.batch, nkt, tk, IDX_DIM), (0, 1, 3, 2))
    z = jnp.zeros_like(kt)
    ki2 = jnp.concatenate([jnp.concatenate([kt, z], axis=3), jnp.concatenate([z, kt], axis=3)], axis=2)
    resident = pl.Buffered(1) if nq > 1 else None

    by_batch = nq == 1
    nsub = max(SELECT_ROWS_PER_STEP // tq, 1)
    while (grp.batch if by_batch else nq) % nsub or (grp.row0 // tq) % nsub or (not by_batch and tq != tk):
        nsub //= 2
    rows = nsub * tq
    sel_blk0 = grp.row0 // rows
    if by_batch:
        grid = (grp.batch // nsub, 1)
        row_map = lambda b, i: sel_blk0 + b
        ki2_spec = pl.BlockSpec((nsub, nkt, 2 * IDX_DIM, 2 * tk), lambda b, i: (b, 0, 0, 0))
        mask_spec = pl.BlockSpec((nsub, None, nkt, tq, tk), lambda b, i: (b, 0, 0, 0, 0))
    else:
        grid = (grp.batch, nq // nsub)
        row_map = lambda b, i: sel_blk0 + b * (nq // nsub) + i
        ki2_spec = pl.BlockSpec((1, nkt, 2 * IDX_DIM, 2 * tk), lambda b, i: (b, 0, 0, 0), pipeline_mode=resident)
        mask_spec = pl.BlockSpec((None, nsub, nkt, tq, tk), lambda b, i: (b, i, 0, 0, 0))

    mask = pl.pallas_call(
        functools.partial(_dsa_select_kernel, tq=tq, tk=tk, nkt=nkt, nsub=nsub, by_batch=by_batch,
                          past=grp.past, topk=topk),
        grid=grid,
        in_specs=[
            pl.BlockSpec((rows, qiw), lambda b, i: (row_map(b, i), cols[0] // qiw)),
            pl.BlockSpec((rows, HEAD_DIM), lambda b, i: (row_map(b, i), 0)),
            ki2_spec,
        ],
        out_specs=mask_spec,
        out_shape=jax.ShapeDtypeStruct((grp.batch, nq, nkt, tq, tk), BF16),
        scratch_shapes=[
            pltpu.VMEM((nkt + 1, rows, tk), jnp.int32),
            pltpu.VMEM((N_IDX_HEADS, rows, HEAD_DIM), F32),
            pltpu.VMEM((rows, tk), jnp.int32),
            pltpu.VMEM((rows, HEAD_DIM), jnp.int32),
            pltpu.VMEM((rows, HEAD_DIM), F32),
        ],
        compiler_params=_params("parallel", "parallel"),
        name="dsa_select",
    )(proj, tail, ki2)

    wp = hp * HEAD_DIM
    kv_args, kv_specs = _kv_operands(grp, proj, cols[2], cols[3], k_cache, v_cache, tk=tk, hp=hp)
    prev_args, prev_specs, aliased = _chained(o_prev)
    n_in = 1 + len(kv_args) + 2
    return pl.pallas_call(
        functools.partial(_dsa_flash_kernel, tq=tq, tk=tk, hp=hp, n_heads=H, past=grp.past, has_past=grp.past > 0),
        grid=(grp.batch, H // hp, nq),
        in_specs=[pl.BlockSpec((tq, wp), lambda b, h, i: (q_blk0 + b * nq + i, cols[1] // wp + h))]
        + kv_specs
        + [pl.BlockSpec((None, None, nkt, tq, tk), lambda b, h, i: (b, i, 0, 0, 0)),
           pl.BlockSpec((hp, 2, tq, tk), lambda b, h, i: (h, 0, 0, 0), pipeline_mode=resident)]
        + prev_specs,
        out_specs=pl.BlockSpec((tq, wp), lambda b, h, i: (q_blk0 + b * nq + i, o_col // wp + h)),
        out_shape=jax.ShapeDtypeStruct(o_shape, BF16),
        input_output_aliases={n_in: 0} if aliased else {},
        scratch_shapes=[pltpu.VMEM((tq, wp), F32), pltpu.VMEM((tq, wp), F32), pltpu.VMEM((tq, wp), F32)],
        compiler_params=_params("parallel", "parallel", "arbitrary"),
        name="dsa_flash",
    )(proj, *kv_args, mask, _near_bias(rel_bias, tq=tq, tk=tk), *prev_args)


def _cross_kernel(q_ref, k_ref, v_ref, *rest, n_heads, scale):
    o_ref = rest[-1]
    for hh in range(n_heads):
        sl = slice(hh * HEAD_DIM, (hh + 1) * HEAD_DIM)
        s = _nt_dot(q_ref[:, sl], k_ref[:, sl]) * scale
        p = jnp.exp(s - jnp.max(s, axis=-1, keepdims=True))
        denom = jnp.sum(p, axis=-1, keepdims=True)
        o = jnp.dot(p.astype(BF16), v_ref[:, sl], preferred_element_type=F32)
        o_ref[:, sl] = (o / denom).astype(o_ref.dtype)


def _cross_attention(grp, q, mem_k, mem_v, o_prev, *, tq=512):
    M, W = q.shape
    n_mem = mem_k.shape[1]
    tq = _pick_tile(grp.seq, tq)
    nq = grp.seq // tq
    assert grp.row0 % tq == 0
    blk0 = grp.row0 // tq
    prev_args, prev_specs, aliased = _chained(o_prev)
    return pl.pallas_call(
        functools.partial(_cross_kernel, n_heads=W // HEAD_DIM, scale=HEAD_DIM ** -0.5),
        grid=(grp.batch, nq),
        in_specs=[
            pl.BlockSpec((tq, W), lambda b, i: (blk0 + b * nq + i, 0)),
            pl.BlockSpec((None, n_mem, W), lambda b, i: (b, 0, 0)),
            pl.BlockSpec((None, n_mem, W), lambda b, i: (b, 0, 0)),
        ] + prev_specs,
        out_specs=pl.BlockSpec((tq, W), lambda b, i: (blk0 + b * nq + i, 0)),
        out_shape=jax.ShapeDtypeStruct((M, W), BF16),
        input_output_aliases={3: 0} if aliased else {},
        compiler_params=_params("parallel", "parallel"),
        name="cross_attention",
    )(q, mem_k, mem_v, *prev_args)


CONV_ROW_CHUNK = 128
SUBLANES = 8


def _conv_kernel(a_ref, prev_ref, state_ref, w_ref, bias_ref, g_ref, *rest, tt):
    o_ref, buf_ref, q_ref, y_ref = rest[-4], rest[-3], rest[-2], rest[-1]
    D = a_ref.shape[-1]

    @pl.when(pl.program_id(1) == 0)
    def _():
        buf_ref[0:CONV_HALO, :] = state_ref[...]

    @pl.when(pl.program_id(1) > 0)
    def _():
        buf_ref[0:CONV_HALO, :] = prev_ref[...]

    buf_ref[CONV_HALO:CONV_HALO + tt, :] = a_ref[...]
    buf_ref[CONV_HALO + tt:CONV_HALO + tt + SUBLANES, :] = jnp.zeros((SUBLANES, D), F32)

    first = CONV_HALO - (CONV_WIDTH - 1)
    rc = min(CONV_ROW_CHUNK, tt)
    for c0 in range(0, D, HEAD_DIM):
        cs = slice(c0, c0 + HEAD_DIM)
        for t0 in range(0, tt, rc):
            out = jnp.zeros((rc, HEAD_DIM), F32)
            for r in range(SUBLANES):
                q = None
                for k in range(CONV_WIDTH):
                    if (first + k) % SUBLANES != r:
                        continue
                    base = t0 + first + k - r
                    term = buf_ref[base:base + rc + SUBLANES, cs] * w_ref[k:k + 1, cs]
                    q = term if q is None else q + term
                if r == 0:
                    out = out + q[0:rc]
                else:
                    q_ref[...] = q
                    out = out + q_ref[r:r + rc, :]
            y_ref[t0:t0 + rc, cs] = out + bias_ref[:, cs]
    y = _rms_rows(y_ref[...], g_ref[...])
    o_ref[...] = (y * (1.0 / (1.0 + jnp.exp(-y)))).astype(o_ref.dtype)


def _conv_norm_swish(grp, u, state, w_dw, b_dw, g_norm, o_prev):
    M, D = u.shape
    tt = _pick_tile(grp.seq, 256)
    nt = grp.seq // tt
    assert grp.seq % tt == 0 and tt % CONV_HALO == 0 and grp.row0 % tt == 0
    blk0 = grp.row0 // tt
    hb = tt // CONV_HALO
    prev_args, prev_specs, aliased = _chained(o_prev)
    return pl.pallas_call(
        functools.partial(_conv_kernel, tt=tt),
        grid=(grp.batch, nt),
        in_specs=[
            pl.BlockSpec((tt, D), lambda b, i: (blk0 + b * nt + i, 0)),
            pl.BlockSpec((CONV_HALO, D), lambda b, i: (jnp.maximum((blk0 + b * nt + i) * hb - 1, 0), 0)),
            pl.BlockSpec((None, CONV_HALO, D), lambda b, i: (b, 0, 0)),
            pl.BlockSpec((CONV_WIDTH, D), lambda b, i: (0, 0)),
            pl.BlockSpec((1, D), lambda b, i: (0, 0)),
            pl.BlockSpec((1, D), lambda b, i: (0, 0)),
        ] + prev_specs,
        out_specs=pl.BlockSpec((tt, D), lambda b, i: (blk0 + b * nt + i, 0)),
        out_shape=jax.ShapeDtypeStruct((M, D), BF16),
        input_output_aliases={6: 0} if aliased else {},
        scratch_shapes=[pltpu.VMEM((tt + CONV_HALO + SUBLANES, D), F32),
                        pltpu.VMEM((min(CONV_ROW_CHUNK, tt) + SUBLANES, HEAD_DIM), F32),
                        pltpu.VMEM((tt, D), F32)],
        compiler_params=_params("parallel", "arbitrary"),
        name="conv_norm_swish",
    )(u, u, state, w_dw, b_dw.reshape(1, D), g_norm.reshape(1, D), *prev_args)


def kernel(x_prompt, x_sample, mem_prompt, cache_sb_k, cache_sb_v, cache_dsa_k, cache_dsa_v, cache_idx_k, cache_mem_k, cache_mem_v, state_conv, norm_mix, norm_cross, norm_mlp, norm_final, w_in_ab, w_out_ab, rel_bias, w_pw1, b_pw1, w_dw, b_dw, g_conv_norm, w_pw2, b_pw2, w_cq, w_mk, w_mv, w_co, w_up, w_down):
    Bp, Tp, D = x_prompt.shape
    Bs, Ts, _ = x_sample.shape
    past = cache_sb_k.shape[2]
    depth = norm_mix.shape[0]
    n_mem = mem_prompt.shape[1]
    n_sb, n_dsa = cache_sb_k.shape[3], cache_dsa_k.shape[3]
    sbw, dsw = n_sb * HEAD_DIM, n_dsa * HEAD_DIM
    qiw = N_IDX_HEADS * IDX_DIM
    cw = w_cq.shape[2]
    Mp, Ms = Bp * Tp, Bs * Ts
    M = Mp + Ms
    prompt = _Group(Bp, Tp, 0, 0)
    sample = _Group(Bs, Ts, Mp, past)

    x = jnp.concatenate([x_prompt.reshape(Mp, D), x_sample.reshape(Ms, D)], axis=0)
    mem_flat = mem_prompt.reshape(Bp * n_mem, D)

    src = dict(zip(("q_sb", "k_sb", "v_sb", "q_d", "k_d", "v_d", "q_i", "k_i", "w_i"),
                   np.cumsum([0, sbw, sbw, sbw, dsw, dsw, dsw, qiw, IDX_DIM])))
    order = ("k_sb", "v_sb", "k_d", "v_d", "q_sb", "q_d", "q_i")
    width = dict(q_sb=sbw, k_sb=sbw, v_sb=sbw, q_d=dsw, k_d=dsw, v_d=dsw, q_i=qiw)
    col = dict(zip(order, np.cumsum([0] + [width[n] for n in order])[:-1]))
    col = {k: int(v) for k, v in col.items()}
    n32 = 2 * sbw + 2 * dsw

    outs = {k: [] for k in ("p_sb_k", "p_sb_v", "p_d_k", "p_d_v", "p_ki", "p_mem_k", "p_mem_v", "p_conv",
                            "s_sb_k", "s_sb_v", "s_d_k", "s_d_v", "s_ki", "s_conv")}

    def split_rows(a):
        return a[:Mp].reshape(Bp, Tp, -1), a[Mp:].reshape(Bs, Ts, -1)

    def last_rows(a, grp, state, n):
        if grp.seq >= n:
            return jnp.stack([a[grp.row0 + (b + 1) * grp.seq - n:grp.row0 + (b + 1) * grp.seq]
                              for b in range(grp.batch)])
        rows = a[grp.row0:grp.row0 + grp.batch * grp.seq].reshape(grp.batch, grp.seq, -1)
        return jnp.concatenate([state, rows], axis=1)[:, -n:]

    w_out16, w_pw1_16, w_pw2_16 = w_out_ab.astype(BF16), w_pw1.astype(BF16), w_pw2.astype(BF16)
    w_cq16, w_mk16, w_mv16, w_co16 = (w.astype(BF16) for w in (w_cq, w_mk, w_mv, w_co))
    w_up16, w_down16 = w_up.astype(BF16), w_down.astype(BF16)
    q_scale = HEAD_DIM ** -0.5 * LOG2E
    col_scale = jnp.concatenate([jnp.full((width[n],), q_scale if n in ("q_sb", "q_d") else 1.0, F32)
                                 for n in order])

    for l in range(depth):
        i = l // 2
        if l % 2 == 0:
            w_in = w_in_ab[i]
            w_main = jnp.concatenate([w_in[:, src[n]:src[n] + width[n]] for n in order], axis=1).astype(BF16)
            tail_w = w_in.shape[1] - int(src["k_i"])
            w_tail = jnp.pad(w_in[:, src["k_i"]:], ((0, 0), (0, HEAD_DIM - tail_w))).astype(BF16)
            proj, kv32, tail = _in_proj(x, w_main, norm_mix[l], col_scale, w_tail, n32=n32)

            for name, key, nh in (("sb_k", "k_sb", n_sb), ("sb_v", "v_sb", n_sb),
                                  ("d_k", "k_d", n_dsa), ("d_v", "v_d", n_dsa)):
                ap, as_ = split_rows(kv32[:, col[key]:col[key] + width[key]])
                outs["p_" + name].append(ap.reshape(Bp, Tp, nh, HEAD_DIM))
                outs["s_" + name].append(as_.reshape(Bs, Ts, nh, HEAD_DIM))
            k_i = tail[:, :IDX_DIM]
            kip, kis = split_rows(k_i)
            outs["p_ki"].append(kip)
            outs["s_ki"].append(kis)

            tk_s = ATTN_TILE
            s_pad_s = -(-(past + Ts) // tk_s) * tk_s
            kidx_p = kip.astype(BF16)
            kidx_s = jnp.concatenate([cache_idx_k[i].astype(BF16), kis.astype(BF16),
                                      jnp.zeros((Bs, s_pad_s - past - Ts, IDX_DIM), BF16)], axis=1)

            o_shape = (M, sbw + dsw)
            sb_cols = (col["q_sb"], col["k_sb"], col["v_sb"])
            d_cols = (col["q_i"], col["q_d"], col["k_d"], col["v_d"])
            o = _sb_attention(prompt, proj, sb_cols, cache_sb_k[i], cache_sb_v[i], None, o_shape, 0,
                              hp=SB_HEADS_PER_STEP)
            o = _dsa_attention(prompt, proj, tail, d_cols, kidx_p, cache_dsa_k[i], cache_dsa_v[i], rel_bias,
                               o, o_shape, sbw, hp=DSA_HEADS_PER_STEP)
            o = _sb_attention(sample, proj, sb_cols, cache_sb_k[i], cache_sb_v[i], o, o_shape, 0,
                              hp=SB_HEADS_PER_STEP)
            o = _dsa_attention(sample, proj, tail, d_cols, kidx_s, cache_dsa_k[i], cache_dsa_v[i], rel_bias,
                               o, o_shape, sbw, hp=n_dsa)
            x, qc = _residual_matmul_then_proj(o, w_out16, i, x, None, norm_cross[l], w_cq16, l)
        else:
            u = _matmul(x, w_pw1_16, layer=i, gain=norm_mix[l], bias=b_pw1[i], glu=True)
            front = CONV_HALO - (CONV_WIDTH - 1)
            state_p = jnp.zeros((Bp, CONV_HALO, D), F32)
            state_s = jnp.concatenate([jnp.zeros((Bs, front, D), F32), state_conv[i]], axis=1)
            y = _conv_norm_swish(prompt, u, state_p, w_dw[i], b_dw[i], g_conv_norm[i], None)
            y = _conv_norm_swish(sample, u, state_s, w_dw[i], b_dw[i], g_conv_norm[i], y)
            outs["p_conv"].append(last_rows(u, prompt, state_p, CONV_WIDTH - 1))
            outs["s_conv"].append(last_rows(u, sample, state_s, CONV_WIDTH - 1))
            x, qc = _residual_matmul_then_proj(y, w_pw2_16, i, x, b_pw2[i], norm_cross[l], w_cq16, l)

        mk = _matmul(mem_flat, w_mk16, layer=l)
        mv = _matmul(mem_flat, w_mv16, layer=l)
        outs["p_mem_k"].append(mk.reshape(Bp, n_mem, cw // HEAD_DIM, HEAD_DIM))
        outs["p_mem_v"].append(mv.reshape(Bp, n_mem, cw // HEAD_DIM, HEAD_DIM))
        oc = _cross_attention(prompt, qc, mk.astype(BF16).reshape(Bp, n_mem, cw),
                              mv.astype(BF16).reshape(Bp, n_mem, cw), None)
        oc = _cross_attention(sample, qc, cache_mem_k[l].astype(BF16).reshape(Bs, n_mem, cw),
                              cache_mem_v[l].astype(BF16).reshape(Bs, n_mem, cw), oc)
        if l < depth - 1:
            x = _cross_out_mlp(x, oc, w_co16, norm_mlp[l], w_up16, w_down16, l)
        else:
            y_p, y_s = _cross_out_mlp(x, oc, w_co16, norm_mlp[l], w_up16, w_down16, l,
                                      final_gain=norm_final, m_prompt=Mp)

    st = lambda k: jnp.stack(outs[k])
    return (y_p.reshape(Bp, Tp, D), y_s.reshape(Bs, Ts, D), st("p_sb_k"), st("p_sb_v"), st("p_d_k"), st("p_d_v"),
            st("p_ki"), st("p_mem_k"), st("p_mem_v"), st("p_conv"), st("s_sb_k"), st("s_sb_v"), st("s_d_k"),
            st("s_d_v"), st("s_ki"), st("s_conv"))
```

```python
import functools
import math
from typing import NamedTuple

import numpy as np
import jax
import jax.numpy as jnp
from jax import lax
from jax.experimental import pallas as pl
from jax.experimental.pallas import tpu as pltpu

F32 = jnp.float32
BF16 = jnp.bfloat16

HEAD_DIM = 128
IDX_DIM = 64
N_IDX_HEADS = 16
CHUNK = 64
TOPK_MAX = 256
N_BUCKETS = 32
MAX_DISTANCE = 128
CONV_WIDTH = 31
CONV_HALO = 32
RMS_EPS = 1e-6
MASKED_LOGIT = -1e30
LOG2E = math.log2(math.e)
_INT32_MIN = -2 ** 31
VMEM_LIMIT_BYTES = 56 * 1024 * 1024
SB_HEADS_PER_STEP = 8
DSA_HEADS_PER_STEP = 4
SB_DEAD_BITS = 160.0
SB_INTERLEAVE = 4
ATTN_TILE = 256
SELECT_ROWS_PER_STEP = 512
FAR_TILES_PER_TRIP = 4


class _Group(NamedTuple):
    batch: int
    seq: int
    row0: int
    past: int


def _params(*semantics):
    return pltpu.CompilerParams(dimension_semantics=semantics, vmem_limit_bytes=VMEM_LIMIT_BYTES)


def _pick_tile(n, pref):
    if n <= pref:
        return n
    t = pref
    while n % t:
        t //= 2
    return t


def _row_tile(m, pref):
    t = min(m, pref) // 8 * 8
    while m % t:
        t -= 8
    return t


def _chained(prev):
    if prev is None:
        return [], [], None
    return [prev], [pl.BlockSpec(memory_space=pl.ANY)], True


def _rms_rows(x, g):
    ms = jnp.mean(x * x, axis=-1, keepdims=True)
    return x * lax.rsqrt(ms + RMS_EPS) * g


def _mm_kernel(*refs, has_norm, has_bias, has_res, glu):
    it = iter(refs)
    x_ref = next(it)
    g_ref = next(it) if has_norm else None
    w_ref = next(it)
    w2_ref = next(it) if glu else None
    b_ref = next(it) if has_bias else None
    b2_ref = next(it) if (has_bias and glu) else None
    r_ref = next(it) if has_res else None
    o_ref = next(it)
    xn_ref = next(it) if has_norm else None

    if has_norm:
        @pl.when(pl.program_id(1) == 0)
        def _():
            xn_ref[...] = _rms_rows(x_ref[...], g_ref[...]).astype(BF16)
        xb = xn_ref[...]
    else:
        xb = x_ref[...].astype(BF16)

    y = jnp.dot(xb, w_ref[...], preferred_element_type=F32)
    if has_bias:
        y = y + b_ref[...]
    if glu:
        gate = jnp.dot(xb, w2_ref[...], preferred_element_type=F32)
        if has_bias:
            gate = gate + b2_ref[...]
        y = y * (1.0 / (1.0 + jnp.exp(-gate)))
    if has_res:
        y = y + r_ref[...]
    o_ref[...] = y.astype(o_ref.dtype)


def _matmul(x, w, *, layer=0, gain=None, bias=None, residual=None, glu=False, out_dtype=F32, tm=1152, tn=512):
    M, K = x.shape
    n_out = w.shape[2] // 2 if glu else w.shape[2]
    tm = _row_tile(M, tm)
    tn = _pick_tile(n_out, tn)
    nj = n_out // tn
    has_norm, has_bias, has_res = gain is not None, bias is not None, residual is not None

    args = [x]
    specs = [pl.BlockSpec((tm, K), lambda i, j: (i, 0))]
    if has_norm:
        args.append(gain.reshape(1, K))
        specs.append(pl.BlockSpec((1, K), lambda i, j: (0, 0)))
    args.append(w)
    specs.append(pl.BlockSpec((None, K, tn), lambda i, j: (layer, 0, j)))
    if glu:
        args.append(w)
        specs.append(pl.BlockSpec((None, K, tn), lambda i, j: (layer, 0, j + nj)))
    if has_bias:
        b2d = bias.reshape(1, -1)
        args.append(b2d)
        specs.append(pl.BlockSpec((1, tn), lambda i, j: (0, j)))
        if glu:
            args.append(b2d)
            specs.append(pl.BlockSpec((1, tn), lambda i, j: (0, j + nj)))
    if has_res:
        args.append(residual)
        specs.append(pl.BlockSpec((tm, tn), lambda i, j: (i, j)))

    return pl.pallas_call(
        functools.partial(_mm_kernel, has_norm=has_norm, has_bias=has_bias, has_res=has_res, glu=glu),
        grid=(M // tm, nj),
        in_specs=specs,
        out_specs=pl.BlockSpec((tm, tn), lambda i, j: (i, j)),
        out_shape=jax.ShapeDtypeStruct((M, n_out), out_dtype),
        scratch_shapes=[pltpu.VMEM((tm, K), BF16)] if has_norm else [],
        compiler_params=_params("parallel", "arbitrary"),
        name="norm_matmul",
    )(*args)


def _res_proj_kernel(y_ref, w_ref, *refs, has_bias):
    it = iter(refs)
    b_ref = next(it) if has_bias else None
    r_ref, g_ref, wq_ref, o_ref, q_ref, x1_ref = (next(it) for _ in range(6))
    j = pl.program_id(1)
    y = jnp.dot(y_ref[...], w_ref[...], preferred_element_type=F32) + r_ref[...]
    if has_bias:
        y = y + b_ref[...]
    o_ref[...] = y
    x1_ref[j] = y

    @pl.when(j == pl.num_programs(1) - 1)
    def _():
        x1 = jnp.concatenate([x1_ref[t] for t in range(x1_ref.shape[0])], axis=1)
        q_ref[...] = jnp.dot(_rms_rows(x1, g_ref[...]).astype(BF16), wq_ref[...],
                             preferred_element_type=F32).astype(q_ref.dtype)


def _residual_matmul_then_proj(y, w, layer, residual, bias, gain, wq, q_layer, *, tm=1152, tn=512):
    M, K = y.shape
    N, nq = w.shape[2], wq.shape[2]
    tm = _row_tile(M, tm)
    assert N % tn == 0
    nj = N // tn
    has_bias = bias is not None
    args = [y, w]
    specs = [pl.BlockSpec((tm, K), lambda i, j: (i, 0)), pl.BlockSpec((None, K, tn), lambda i, j: (layer, 0, j))]
    if has_bias:
        args.append(bias.reshape(1, N))
        specs.append(pl.BlockSpec((1, tn), lambda i, j: (0, j)))
    args += [residual, gain.reshape(1, N), wq]
    specs += [pl.BlockSpec((tm, tn), lambda i, j: (i, j)), pl.BlockSpec((1, N), lambda i, j: (0, 0)),
              pl.BlockSpec((None, N, nq), lambda i, j: (q_layer, 0, 0))]
    return pl.pallas_call(
        functools.partial(_res_proj_kernel, has_bias=has_bias),
        grid=(M // tm, nj),
        in_specs=specs,
        out_specs=[pl.BlockSpec((tm, tn), lambda i, j: (i, j)), pl.BlockSpec((tm, nq), lambda i, j: (i, 0))],
        out_shape=[jax.ShapeDtypeStruct((M, N), F32), jax.ShapeDtypeStruct((M, nq), BF16)],
        scratch_shapes=[pltpu.VMEM((nj, tm, tn), F32)],
        compiler_params=_params("parallel", "arbitrary"),
        name="residual_matmul_proj",
    )(*args)


def _in_proj_kernel(x_ref, g_ref, w_ref, s_ref, wt_ref, o16_ref, o32_ref, ot_ref, xn_ref, *, n32_tiles):
    j = pl.program_id(1)

    @pl.when(j == 0)
    def _():
        xn_ref[...] = _rms_rows(x_ref[...], g_ref[...]).astype(BF16)
        ot_ref[...] = jnp.dot(xn_ref[...], wt_ref[...], preferred_element_type=F32)

    y = jnp.dot(xn_ref[...], w_ref[...], preferred_element_type=F32)
    o16_ref[...] = (y * s_ref[...]).astype(BF16)

    @pl.when(j < n32_tiles)
    def _():
        o32_ref[...] = y


def _in_proj(x, w, gain, col_scale, w_tail, *, n32, tm=1152, tn=512):
    M, K = x.shape
    N = w.shape[1]
    nt = w_tail.shape[1]
    tm = _row_tile(M, tm)
    assert N % tn == 0 and n32 % tn == 0
    n32_tiles = n32 // tn
    return pl.pallas_call(
        functools.partial(_in_proj_kernel, n32_tiles=n32_tiles),
        grid=(M // tm, N // tn),
        in_specs=[
            pl.BlockSpec((tm, K), lambda i, j: (i, 0)),
            pl.BlockSpec((1, K), lambda i, j: (0, 0)),
            pl.BlockSpec((K, tn), lambda i, j: (0, j)),
            pl.BlockSpec((1, tn), lambda i, j: (0, j)),
            pl.BlockSpec((K, nt), lambda i, j: (0, 0)),
        ],
        out_specs=[
            pl.BlockSpec((tm, tn), lambda i, j: (i, j)),
            pl.BlockSpec((tm, tn), lambda i, j: (i, jnp.minimum(j, n32_tiles - 1))),
            pl.BlockSpec((tm, nt), lambda i, j: (i, 0)),
        ],
        out_shape=[jax.ShapeDtypeStruct((M, N), BF16), jax.ShapeDtypeStruct((M, n32), F32),
                   jax.ShapeDtypeStruct((M, nt), F32)],
        scratch_shapes=[pltpu.VMEM((tm, K), BF16)],
        compiler_params=_params("parallel", "arbitrary"),
        name="in_proj",
    )(x, gain.reshape(1, K), w, col_scale.reshape(1, N), w_tail)


def _mlp_kernel(x_ref, oc_ref, wco_ref, g_ref, wu_ref, wd_ref, *refs, n_prompt_tiles):
    final = n_prompt_tiles is not None
    if final:
        gf_ref, op_ref, os_ref, x2_ref, xn_ref, acc_ref = refs
    else:
        o_ref, x2_ref, xn_ref, acc_ref = refs
    i, f = pl.program_id(0), pl.program_id(1)

    @pl.when(f == 0)
    def _():
        x2 = x_ref[...] + jnp.dot(oc_ref[...], wco_ref[...], preferred_element_type=F32)
        x2_ref[...] = x2
        xn_ref[...] = _rms_rows(x2, g_ref[...]).astype(BF16)
        acc_ref[...] = jnp.zeros_like(acc_ref)

    h = jnp.dot(xn_ref[...], wu_ref[...], preferred_element_type=F32)
    h = jnp.maximum(h, 0.0)
    h = (h * h).astype(BF16)
    acc_ref[...] += jnp.dot(h, wd_ref[...], preferred_element_type=F32)

    last = f == pl.num_programs(1) - 1
    if not final:
        @pl.when(last)
        def _():
            o_ref[...] = x2_ref[...] + acc_ref[...]
    else:
        @pl.when(last & (i < n_prompt_tiles))
        def _():
            op_ref[...] = _rms_rows(x2_ref[...] + acc_ref[...], gf_ref[...])

        @pl.when(last & (i >= n_prompt_tiles))
        def _():
            os_ref[...] = _rms_rows(x2_ref[...] + acc_ref[...], gf_ref[...])


def _cross_out_mlp(x, oc, w_co, gain, w_up, w_down, layer, *, final_gain=None, m_prompt=None, tm=512, tf=512):
    M, D = x.shape
    F = w_up.shape[2]
    cw = oc.shape[1]
    final = final_gain is not None
    tm = _pick_tile(M - m_prompt, tm) if final else _pick_tile(M, tm)
    tf = _pick_tile(F, tf)
    args = [x, oc, w_co, gain.reshape(1, D), w_up, w_down]
    specs = [
        pl.BlockSpec((tm, D), lambda i, f: (i, 0)),
        pl.BlockSpec((tm, cw), lambda i, f: (i, 0)),
        pl.BlockSpec((None, cw, D), lambda i, f: (layer, 0, 0)),
        pl.BlockSpec((1, D), lambda i, f: (0, 0)),
        pl.BlockSpec((None, D, tf), lambda i, f: (layer, 0, f)),
        pl.BlockSpec((None, tf, D), lambda i, f: (layer, f, 0)),
    ]
    if final:
        assert m_prompt % tm == 0 and (M - m_prompt) % tm == 0
        npt = m_prompt // tm
        args.append(final_gain.reshape(1, D))
        specs.append(pl.BlockSpec((1, D), lambda i, f: (0, 0)))
        out_specs = [pl.BlockSpec((tm, D), lambda i, f: (jnp.minimum(i, npt - 1), 0)),
                     pl.BlockSpec((tm, D), lambda i, f: (jnp.maximum(i - npt, 0), 0))]
        out_shape = [jax.ShapeDtypeStruct((m_prompt, D), F32), jax.ShapeDtypeStruct((M - m_prompt, D), F32)]
        semantics = ("arbitrary", "arbitrary")
    else:
        npt = None
        out_specs = pl.BlockSpec((tm, D), lambda i, f: (i, 0))
        out_shape = jax.ShapeDtypeStruct((M, D), F32)
        semantics = ("parallel", "arbitrary")
    return pl.pallas_call(
        functools.partial(_mlp_kernel, n_prompt_tiles=npt),
        grid=(M // tm, F // tf),
        in_specs=specs,
        out_specs=out_specs,
        out_shape=out_shape,
        scratch_shapes=[pltpu.VMEM((tm, D), F32), pltpu.VMEM((tm, D), BF16), pltpu.VMEM((tm, D), F32)],
        compiler_params=_params(*semantics),
        name="cross_out_mlp",
    )(*args)


def _nt_dot(a, b):
    return lax.dot_general(a, b, (((1,), (1,)), ((), ())), preferred_element_type=F32)


def _kv_operands(grp, proj, k_col, v_col, k_cache, v_cache, *, tk, hp):
    wp = hp * HEAD_DIM
    if grp.past == 0:
        blk0 = grp.row0 // grp.seq
        args = [proj, proj]
        specs = [pl.BlockSpec((grp.seq, wp), lambda b, h, i, c=k_col // wp: (blk0 + b, c + h),
                              pipeline_mode=pl.Buffered(1)),
                 pl.BlockSpec((grp.seq, wp), lambda b, h, i, c=v_col // wp: (blk0 + b, c + h),
                              pipeline_mode=pl.Buffered(1))]
        return args, specs

    def new_tile(col):
        rows = lax.slice(proj, (grp.row0, col), (grp.row0 + grp.batch * grp.seq, col + k_cache.shape[2] * HEAD_DIM))
        rows = rows.reshape(grp.batch, grp.seq, -1)
        return jnp.pad(rows, ((0, 0), (0, tk - grp.seq), (0, 0)))

    args = [new_tile(k_col), new_tile(v_col)]
    specs = [pl.BlockSpec((None, tk, wp), lambda b, h, i: (b, 0, h))] * 2
    n_heads = k_cache.shape[2]
    for cache in (k_cache, v_cache):
        args.append(cache.reshape(grp.batch, grp.past * n_heads, HEAD_DIM))
        specs.append(pl.BlockSpec((None, grp.past * n_heads, HEAD_DIM), lambda b, h, i: (b, 0, 0)))
    return args, specs


def _split_kv_refs(refs, has_past):
    if not has_past:
        return refs[0], refs[1], None, None, refs[2:]
    return refs[0], refs[1], refs[2], refs[3], refs[4:]


def _tile_reader(new_ref, past_ref, *, tk, hp, n_heads):
    def read(j, hh, diag, n=1):
        hs = slice(hh * HEAD_DIM, (hh + 1) * HEAD_DIM)
        if past_ref is None:
            return new_ref[pl.ds(pl.multiple_of(j * tk, tk), n * tk), hs]
        if diag:
            return new_ref[:, hs]
        head = pl.program_id(1) * hp + hh
        return past_ref[pl.ds(j * (tk * n_heads) + head, n * tk, stride=n_heads), :].astype(BF16)
    return read


def _sb_kernel(q_ref, *refs, tq, tk, hp, n_heads, past, has_past):
    k_new, v_new, k_past, v_past, rest = _split_kv_refs(refs, has_past)
    u_ref, o_ref, acc_ref, carry_ref = rest[0], rest[-3], rest[-2], rest[-1]
    read_k = _tile_reader(k_new, k_past, tk=tk, hp=hp, n_heads=n_heads)
    read_v = _tile_reader(v_new, v_past, tk=tk, hp=hp, n_heads=n_heads)

    i = pl.program_id(2)
    q_start = past + i * tq
    jd = q_start // tk
    reps = tk // HEAD_DIM

    acc_ref[...] = jnp.zeros_like(acc_ref)
    carry_ref[...] = jnp.zeros_like(carry_ref)

    def step(j, masked):
        if masked:
            t_pos = q_start + lax.broadcasted_iota(jnp.int32, (tq, tk), 0)
            s_pos = j * tk + lax.broadcasted_iota(jnp.int32, (tq, tk), 1)
            causal = t_pos > s_pos
        for h0 in range(0, hp, SB_INTERLEAVE):
            group_step(j, masked, causal if masked else None, range(h0, h0 + SB_INTERLEAVE))

    def group_step(j, masked, causal, group):
        heads = [(hh, slice(hh * HEAD_DIM, (hh + 1) * HEAD_DIM)) for hh in group]
        z2s = [_nt_dot(q_ref[:, hs], read_k(j, hh, masked)) for hh, hs in heads]
        ns, log2_betas, css = [], [], []
        for z2 in z2s:
            neg_abs = lax.bitcast_convert_type(
                lax.bitcast_convert_type(z2, jnp.int32) | jnp.int32(_INT32_MIN), F32)
            n = jnp.maximum(z2, 0.0) + jnp.log(1.0 + jnp.exp2(neg_abs)) * LOG2E
            log2_betas.append(z2 - n)
            if masked:
                n = jnp.where(causal, n, 0.0)
            n16 = n.astype(BF16)
            css.append(jnp.dot(n16, u_ref[...], preferred_element_type=F32))
            ns.append(n16)
        for (hh, hs), n16, log2_beta, cs in zip(heads, ns, log2_betas, css):
            a = jnp.exp2(log2_beta - cs)
            if masked:
                a = jnp.where(causal, a, 0.0)
            carry = carry_ref[:, hs]
            pv = jnp.dot(a.astype(BF16), read_v(j, hh, masked), preferred_element_type=F32)
            acc_ref[:, hs] += jnp.exp2(-carry) * pv
            carry_ref[:, hs] = carry + jnp.broadcast_to(cs[:, 0:1] + n16[:, 0:1].astype(F32), (tq, HEAD_DIM))

    step(jd, True)

    def live(state):
        t, min_carry = state
        return (t < jd) & (min_carry < SB_DEAD_BITS)

    def body(state):
        t, _ = state
        step(jd - 1 - t, False)
        return t + 1, jnp.min(carry_ref[...])

    lax.while_loop(live, body, (jnp.int32(0), jnp.min(carry_ref[...])))
    o_ref[...] = acc_ref[...].astype(o_ref.dtype)


def _suffix_sum_matrix(tk):
    r = np.arange(tk)
    return jnp.asarray((r[:, None] > r[None, :]).astype(np.float32), dtype=BF16)


def _attn_tiles(grp):
    tq = _pick_tile(grp.seq, ATTN_TILE)
    tk = ATTN_TILE if grp.past else tq
    nq = grp.seq // tq
    assert grp.seq % tq == 0 and tk % tq == 0 and grp.past % tk == 0 and grp.row0 % tq == 0
    assert grp.past == 0 or nq == 1
    assert grp.past > 0 or grp.row0 % grp.seq == 0
    return tq, tk, nq


def _sb_attention(grp, proj, cols, k_cache, v_cache, o_prev, o_shape, o_col, *, hp):
    H = k_cache.shape[2]
    tq, tk, nq = _attn_tiles(grp)
    wp = hp * HEAD_DIM
    q_blk0 = grp.row0 // tq
    kv_args, kv_specs = _kv_operands(grp, proj, cols[1], cols[2], k_cache, v_cache, tk=tk, hp=hp)
    prev_args, prev_specs, aliased = _chained(o_prev)
    n_in = 1 + len(kv_args) + 1
    return pl.pallas_call(
        functools.partial(_sb_kernel, tq=tq, tk=tk, hp=hp, n_heads=H, past=grp.past, has_past=grp.past > 0),
        grid=(grp.batch, H // hp, nq),
        in_specs=[pl.BlockSpec((tq, wp), lambda b, h, i: (q_blk0 + b * nq + i, cols[0] // wp + h))]
        + kv_specs
        + [pl.BlockSpec((tk, tk), lambda b, h, i: (0, 0), pipeline_mode=pl.Buffered(1))]
        + prev_specs,
        out_specs=pl.BlockSpec((tq, wp), lambda b, h, i: (q_blk0 + b * nq + i, o_col // wp + h)),
        out_shape=jax.ShapeDtypeStruct(o_shape, BF16),
        input_output_aliases={n_in: 0} if aliased else {},
        scratch_shapes=[pltpu.VMEM((tq, wp), F32), pltpu.VMEM((tq, wp), F32)],
        compiler_params=_params("parallel", "parallel", "arbitrary"),
        name="sb_attention",
    )(proj, *kv_args, _suffix_sum_matrix(tk), *prev_args)


def _sortable_key(x):
    b = lax.bitcast_convert_type(x, jnp.int32)
    return b ^ ((b >> 31) & jnp.int32(0x7FFFFFFF))


_KEY_NEG_INF = int(np.int32(np.array(-np.inf, np.float32).view(np.int32) ^ 0x7FFFFFFF))
_KEY_LOWEST_FINITE = int(np.int32(np.array(np.finfo(np.float32).min, np.float32).view(np.int32) ^ 0x7FFFFFFF))
COUNT_ROW_GROUP = 128


def _dsa_select_kernel(qi_ref, wi_ref, ki2_ref, mask_ref, keys_ref, wb_ref,
                       *, tq, tk, nkt, nsub, by_batch, past, topk):
    i = pl.program_id(1)
    rows = nsub * tq
    first_start = past + i * tq * (1 if by_batch else nsub)
    jd_min = first_start // tk
    n_cut = 1 if by_batch else nsub
    jd = jd_min + n_cut - 1
    reps = tk // HEAD_DIM

    w = wi_ref[:, IDX_DIM:IDX_DIM + N_IDX_HEADS] * (IDX_DIM ** -0.5 * N_IDX_HEADS ** -0.5)
    for hh in range(N_IDX_HEADS):
        wb_ref[hh] = jnp.broadcast_to(w[:, hh:hh + 1], (rows, HEAD_DIM))

    def idx_rows(j, r0, nr, kt):
        tot = jnp.zeros((nr, tk), F32)
        for p in range(N_IDX_HEADS // 2):
            sc = jnp.dot(qi_ref[r0:r0 + nr, p * 2 * IDX_DIM:(p + 1) * 2 * IDX_DIM], kt,
                         preferred_element_type=F32)
            w0 = jnp.concatenate([wb_ref[2 * p, r0:r0 + nr, :]] * reps, axis=1)
            w1 = jnp.concatenate([wb_ref[2 * p + 1, r0:r0 + nr, :]] * reps, axis=1)
            tot = tot + w0 * jnp.maximum(sc[:, :tk], 0.0) + w1 * jnp.maximum(sc[:, tk:], 0.0)
        return _sortable_key(tot)

    def idx_tile(j):
        if by_batch:
            return jnp.concatenate([idx_rows(j, s * tq, tq, ki2_ref[s, j]) for s in range(nsub)], axis=0)
        return idx_rows(j, 0, rows, ki2_ref[0, j])

    def idx_body(j, c):
        keys_ref[j] = idx_tile(j)
        return c

    lax.fori_loop(0, jd_min, idx_body, 0)
    row = lax.broadcasted_iota(jnp.int32, (rows, tk), 0)
    t_pos = first_start + (row % tq if by_batch else row)
    for d in range(n_cut):
        s_pos = (jd_min + d) * tk + lax.broadcasted_iota(jnp.int32, (rows, tk), 1)
        visible = (t_pos // CHUNK) >= (s_pos // CHUNK)
        keys_ref[jd_min + d] = jnp.where(visible, idx_tile(jd_min + d), jnp.int32(_KEY_NEG_INF))

    n_vis = jd + 1

    @pl.when(n_vis % 2 == 1)
    def _():
        keys_ref[n_vis] = jnp.full((rows, tk), _INT32_MIN, jnp.int32)

    n_pairs = (n_vis + 1) // 2
    rg = min(COUNT_ROW_GROUP, rows)

    def count_ge(cand):
        parts = []
        for g in range(rows // rg):
            cg = cand[g * rg:(g + 1) * rg]

            def body(t, cnt, g=g, cg=cg):
                for jj in (2 * t, 2 * t + 1):
                    kj = keys_ref[jj, pl.ds(g * rg, rg), :]
                    for r in range(reps):
                        cnt = cnt + jnp.where(kj[:, r * HEAD_DIM:(r + 1) * HEAD_DIM] >= cg, 1.0, 0.0)
                return cnt

            parts.append(lax.fori_loop(0, n_pairs, body, jnp.zeros((rg, HEAD_DIM), F32)))
        return jnp.sum(jnp.concatenate(parts, axis=0), axis=-1, keepdims=True)

    kf = float(topk)
    zero = jnp.zeros((rows, HEAD_DIM), jnp.int32)
    c = count_ge(zero)
    res = jnp.where(c >= kf, zero, jnp.int32(_INT32_MIN))
    cnt = jnp.where(c >= kf, c, float(tk) * (2 * n_pairs).astype(F32))
    for bit in range(30, -1, -1):
        cand = res + jnp.int32(1 << bit)
        c = count_ge(cand)
        res = jnp.where(c >= kf, cand, res)
        cnt = jnp.where(c >= kf, c, cnt)

    tied_cut = jnp.max(jnp.where((cnt != kf) & (res[:, 0:1] > jnp.int32(_KEY_NEG_INF)), 1.0, 0.0)) > 0.0

    @pl.when(tied_cut)
    def _():
        col = lax.broadcasted_iota(jnp.int32, (rg, tk), 1)

        def count_tied(before, strict_above):
            parts = []
            for g in range(rows // rg):
                grp_rows = slice(g * rg, (g + 1) * rg)
                res_g = jnp.concatenate([res[grp_rows]] * reps, axis=1)
                bef_g = None if strict_above else jnp.concatenate([before[grp_rows]] * reps, axis=1)

                def body(j, acc, g=g, res_g=res_g, bef_g=bef_g):
                    kj = keys_ref[j, pl.ds(g * rg, rg), :]
                    if strict_above:
                        hit = jnp.where(kj > res_g, 1.0, 0.0)
                    else:
                        hit = jnp.where(kj == res_g, jnp.where(j * tk + col < bef_g, 1.0, 0.0), 0.0)
                    return acc + sum(hit[:, r * HEAD_DIM:(r + 1) * HEAD_DIM] for r in range(reps))

                parts.append(lax.fori_loop(0, n_vis, body, jnp.zeros((rg, HEAD_DIM), F32)))
            return jnp.sum(jnp.concatenate(parts, axis=0), axis=-1, keepdims=True)

        need = kf - count_tied(None, True)
        r_keep = jnp.zeros((rows, HEAD_DIM), jnp.int32)
        for bit in range((nkt * tk).bit_length() - 1, -1, -1):
            cand = r_keep | jnp.int32(1 << bit)
            r_keep = jnp.where(count_tied(cand, False) < need, cand, r_keep)

        def drop_late_ties(j, c):
            for g in range(rows // rg):
                grp_rows = slice(g * rg, (g + 1) * rg)
                res_g = jnp.concatenate([res[grp_rows]] * reps, axis=1)
                keep_g = jnp.concatenate([r_keep[grp_rows]] * reps, axis=1)
                kj = keys_ref[j, pl.ds(g * rg, rg), :]
                keys_ref[j, pl.ds(g * rg, rg), :] = jnp.where(
                    kj == res_g, jnp.where(j * tk + col > keep_g, res_g - 1, kj), kj)
            return c

        lax.fori_loop(0, n_vis, drop_late_ties, 0)

    thr = jnp.maximum(res, jnp.int32(_KEY_LOWEST_FINITE))
    thr = jnp.concatenate([thr] * reps, axis=1)

    def write_mask(j, c):
        tile = jnp.where(keys_ref[j] >= thr, 0.0, MASKED_LOGIT).astype(mask_ref.dtype)
        for s_ in range(nsub):
            mask_ref[s_, j] = tile[s_ * tq:(s_ + 1) * tq]
        return c

    lax.fori_loop(0, jd + 1, write_mask, 0)

    def write_hidden(j, c):
        for s_ in range(nsub):
            mask_ref[s_, j] = jnp.full((tq, tk), MASKED_LOGIT, mask_ref.dtype)
        return c

    lax.fori_loop(jd + 1, nkt, write_hidden, 0)


def _dsa_flash_kernel(q_ref, *refs, tq, tk, hp, n_heads, past, has_past):
    k_new, v_new, k_past, v_past, rest = _split_kv_refs(refs, has_past)
    mask_ref, bnear_ref = rest[0], rest[1]
    o_ref, m_ref, l_ref, acc_ref = rest[-4], rest[-3], rest[-2], rest[-1]
    read_k = _tile_reader(k_new, k_past, tk=tk, hp=hp, n_heads=n_heads)
    read_v = _tile_reader(v_new, v_past, tk=tk, hp=hp, n_heads=n_heads)

    i = pl.program_id(2)
    jd = (past + i * tq) // tk
    reps = tk // HEAD_DIM
    heads = [slice(hh * HEAD_DIM, (hh + 1) * HEAD_DIM) for hh in range(hp)]

    m_ref[...] = jnp.full_like(m_ref, MASKED_LOGIT)
    l_ref[...] = jnp.zeros_like(l_ref)
    acc_ref[...] = jnp.zeros_like(acc_ref)

    def step(j, near, n=1):
        diag = near == (0,)
        maskf = jnp.concatenate([mask_ref[j + t].astype(F32) for t in range(n)], axis=1)
        logits = []
        for hh, hs in enumerate(heads):
            s = _nt_dot(q_ref[:, hs], read_k(j, hh, diag, n)) + maskf
            if near is not None:
                s = s + jnp.concatenate([bnear_ref[hh, d] for d in near], axis=1)
            logits.append(s)
        probs, alphas = [], []
        for hs, s in zip(heads, logits):
            m_old = m_ref[:, hs]
            m_new = jnp.maximum(m_old, jnp.max(s, axis=-1, keepdims=True))
            alpha = jnp.exp2(m_old - m_new)
            p = jnp.exp2(s - jnp.concatenate([m_new] * (n * reps), axis=1))
            l_ref[:, hs] = alpha * l_ref[:, hs] + sum(p[:, r * HEAD_DIM:(r + 1) * HEAD_DIM]
                                                      for r in range(n * reps))
            m_ref[:, hs] = m_new
            probs.append(p.astype(BF16))
            alphas.append(alpha)
        for hh, (hs, p, alpha) in enumerate(zip(heads, probs, alphas)):
            pv = jnp.dot(p, read_v(j, hh, diag, n), preferred_element_type=F32)
            acc_ref[:, hs] = alpha * acc_ref[:, hs] + pv

    if has_past:
        step(jd, (0,))

        @pl.when(jd >= 1)
        def _():
            step(jd - 1, (1,))
    else:
        @pl.when(jd >= 1)
        def _():
            step(jd - 1, (1, 0), 2)

        @pl.when(jd == 0)
        def _():
            step(jd, (0,))

    n_far = jnp.maximum(jd - 1, 0)
    n_trips = n_far // FAR_TILES_PER_TRIP

    def single(j, c):
        step(j, None)
        return c

    lax.fori_loop(n_trips * FAR_TILES_PER_TRIP, n_far, single, 0)

    def body(t, c):
        step(FAR_TILES_PER_TRIP * t, None, FAR_TILES_PER_TRIP)
        return c

    lax.fori_loop(0, n_trips, body, 0)
    for hs in heads:
        o_ref[:, hs] = (acc_ref[:, hs] / jnp.sum(l_ref[:, hs], axis=-1, keepdims=True)).astype(o_ref.dtype)


def _t5_bucket(rel):
    nb = N_BUCKETS // 2
    max_exact = nb // 2
    side = jnp.where(rel > 0, nb, 0)
    n = jnp.abs(rel)
    nf = jnp.maximum(n, 1).astype(F32)
    large = max_exact + (jnp.log(nf / max_exact) / math.log(MAX_DISTANCE / max_exact) * (nb - max_exact)).astype(jnp.int32)
    large = jnp.minimum(large, nb - 1)
    return side + jnp.where(n < max_exact, n, large)


def _near_bias(rel_bias, *, tq, tk):
    r = jnp.arange(tq, dtype=jnp.int32)[:, None]
    c = jnp.arange(tk, dtype=jnp.int32)[None, :]
    bucket = _t5_bucket(jnp.stack([c - r, c - r - tk]))
    far_bucket = _t5_bucket(jnp.int32(-2 * tk))
    table = (rel_bias - rel_bias[far_bucket][None, :]) * LOG2E
    onehot = (bucket[..., None] == jnp.arange(N_BUCKETS, dtype=jnp.int32)).astype(F32)
    return jnp.einsum("dqkb,bh->hdqk", onehot, table, precision=lax.Precision.HIGHEST)


def _dsa_attention(grp, proj, tail, cols, k_idx_all, k_cache, v_cache, rel_bias, o_prev, o_shape, o_col, *, hp):
    H = k_cache.shape[2]
    tq, tk, nq = _attn_tiles(grp)
    s_pad = k_idx_all.shape[1]
    nkt = s_pad // tk
    assert tk % CHUNK == 0 and tq % CHUNK == 0 and s_pad % tk == 0
    assert tk >= MAX_DISTANCE and H % hp == 0
    topk = min(TOPK_MAX, (grp.past + grp.seq) // 4)
    q_blk0 = grp.row0 // tq
    qiw = N_IDX_HEADS * IDX_DIM

    kt = jnp.transpose(k_idx_all.reshape(grp.batch, nkt, tk, IDX_DIM), (0, 1, 3, 2))
    z = jnp.zeros_like(kt)
    ki2 = jnp.concatenate([jnp.concatenate([kt, z], axis=3), jnp.concatenate([z, kt], axis=3)], axis=2)
    resident = pl.Buffered(1) if nq > 1 else None

    by_batch = nq == 1
    nsub = max(SELECT_ROWS_PER_STEP // tq, 1)
    while (grp.batch if by_batch else nq) % nsub or (grp.row0 // tq) % nsub or (not by_batch and tq != tk):
        nsub //= 2
    rows = nsub * tq
    sel_blk0 = grp.row0 // rows
    if by_batch:
        grid = (grp.batch // nsub, 1)
        row_map = lambda b, i: sel_blk0 + b
        ki2_spec = pl.BlockSpec((nsub, nkt, 2 * IDX_DIM, 2 * tk), lambda b, i: (b, 0, 0, 0))
        mask_spec = pl.BlockSpec((nsub, None, nkt, tq, tk), lambda b, i: (b, 0, 0, 0, 0))
    else:
        grid = (grp.batch, nq // nsub)
        row_map = lambda b, i: sel_blk0 + b * (nq // nsub) + i
        ki2_spec = pl.BlockSpec((1, nkt, 2 * IDX_DIM, 2 * tk), lambda b, i: (b, 0, 0, 0), pipeline_mode=resident)
        mask_spec = pl.BlockSpec((None, nsub, nkt, tq, tk), lambda b, i: (b, i, 0, 0, 0))

    mask = pl.pallas_call(
        functools.partial(_dsa_select_kernel, tq=tq, tk=tk, nkt=nkt, nsub=nsub, by_batch=by_batch,
                          past=grp.past, topk=topk),
        grid=grid,
        in_specs=[
            pl.BlockSpec((rows, qiw), lambda b, i: (row_map(b, i), cols[0] // qiw)),
            pl.BlockSpec((rows, HEAD_DIM), lambda b, i: (row_map(b, i), 0)),
            ki2_spec,
        ],
        out_specs=mask_spec,
        out_shape=jax.ShapeDtypeStruct((grp.batch, nq, nkt, tq, tk), BF16),
        scratch_shapes=[
            pltpu.VMEM((nkt + 1, rows, tk), jnp.int32),
            pltpu.VMEM((N_IDX_HEADS, rows, HEAD_DIM), F32),
        ],
        compiler_params=_params("parallel", "parallel"),
        name="dsa_select",
    )(proj, tail, ki2)

    wp = hp * HEAD_DIM
    kv_args, kv_specs = _kv_operands(grp, proj, cols[2], cols[3], k_cache, v_cache, tk=tk, hp=hp)
    prev_args, prev_specs, aliased = _chained(o_prev)
    n_in = 1 + len(kv_args) + 2
    return pl.pallas_call(
        functools.partial(_dsa_flash_kernel, tq=tq, tk=tk, hp=hp, n_heads=H, past=grp.past, has_past=grp.past > 0),
        grid=(grp.batch, H // hp, nq),
        in_specs=[pl.BlockSpec((tq, wp), lambda b, h, i: (q_blk0 + b * nq + i, cols[1] // wp + h))]
        + kv_specs
        + [pl.BlockSpec((None, None, nkt, tq, tk), lambda b, h, i: (b, i, 0, 0, 0)),
           pl.BlockSpec((hp, 2, tq, tk), lambda b, h, i: (h, 0, 0, 0), pipeline_mode=resident)]
        + prev_specs,
        out_specs=pl.BlockSpec((tq, wp), lambda b, h, i: (q_blk0 + b * nq + i, o_col // wp + h)),
        out_shape=jax.ShapeDtypeStruct(o_shape, BF16),
        input_output_aliases={n_in: 0} if aliased else {},
        scratch_shapes=[pltpu.VMEM((tq, wp), F32), pltpu.VMEM((tq, wp), F32), pltpu.VMEM((tq, wp), F32)],
        compiler_params=_params("parallel", "parallel", "arbitrary"),
        name="dsa_flash",
    )(proj, *kv_args, mask, _near_bias(rel_bias, tq=tq, tk=tk), *prev_args)


def _cross_kernel(q_ref, k_ref, v_ref, *rest, n_heads, scale):
    o_ref = rest[-1]
    for hh in range(n_heads):
        sl = slice(hh * HEAD_DIM, (hh + 1) * HEAD_DIM)
        s = _nt_dot(q_ref[:, sl], k_ref[:, sl]) * scale
        p = jnp.exp(s - jnp.max(s, axis=-1, keepdims=True))
        denom = jnp.sum(p, axis=-1, keepdims=True)
        o = jnp.dot(p.astype(BF16), v_ref[:, sl], preferred_element_type=F32)
        o_ref[:, sl] = (o / denom).astype(o_ref.dtype)


def _cross_attention(grp, q, mem_k, mem_v, o_prev, *, tq=512):
    M, W = q.shape
    n_mem = mem_k.shape[1]
    tq = _pick_tile(grp.seq, tq)
    nq = grp.seq // tq
    assert grp.row0 % tq == 0
    blk0 = grp.row0 // tq
    prev_args, prev_specs, aliased = _chained(o_prev)
    return pl.pallas_call(
        functools.partial(_cross_kernel, n_heads=W // HEAD_DIM, scale=HEAD_DIM ** -0.5),
        grid=(grp.batch, nq),
        in_specs=[
            pl.BlockSpec((tq, W), lambda b, i: (blk0 + b * nq + i, 0)),
            pl.BlockSpec((None, n_mem, W), lambda b, i: (b, 0, 0)),
            pl.BlockSpec((None, n_mem, W), lambda b, i: (b, 0, 0)),
        ] + prev_specs,
        out_specs=pl.BlockSpec((tq, W), lambda b, i: (blk0 + b * nq + i, 0)),
        out_shape=jax.ShapeDtypeStruct((M, W), BF16),
        input_output_aliases={3: 0} if aliased else {},
        compiler_params=_params("parallel", "parallel"),
        name="cross_attention",
    )(q, mem_k, mem_v, *prev_args)


CONV_ROW_CHUNK = 128
SUBLANES = 8


def _conv_kernel(a_ref, prev_ref, state_ref, w_ref, bias_ref, g_ref, *rest, tt):
    o_ref, buf_ref, q_ref, y_ref = rest[-4], rest[-3], rest[-2], rest[-1]
    D = a_ref.shape[-1]

    @pl.when(pl.program_id(1) == 0)
    def _():
        buf_ref[0:CONV_HALO, :] = state_ref[...]

    @pl.when(pl.program_id(1) > 0)
    def _():
        buf_ref[0:CONV_HALO, :] = prev_ref[...]

    buf_ref[CONV_HALO:CONV_HALO + tt, :] = a_ref[...]
    buf_ref[CONV_HALO + tt:CONV_HALO + tt + SUBLANES, :] = jnp.zeros((SUBLANES, D), F32)

    first = CONV_HALO - (CONV_WIDTH - 1)
    rc = min(CONV_ROW_CHUNK, tt)
    for c0 in range(0, D, HEAD_DIM):
        cs = slice(c0, c0 + HEAD_DIM)
        for t0 in range(0, tt, rc):
            out = jnp.zeros((rc, HEAD_DIM), F32)
            for r in range(SUBLANES):
                q = None
                for k in range(CONV_WIDTH):
                    if (first + k) % SUBLANES != r:
                        continue
                    base = t0 + first + k - r
                    term = buf_ref[base:base + rc + SUBLANES, cs] * w_ref[k:k + 1, cs]
                    q = term if q is None else q + term
                if r == 0:
                    out = out + q[0:rc]
                else:
                    q_ref[...] = q
                    out = out + q_ref[r:r + rc, :]
            y_ref[t0:t0 + rc, cs] = out + bias_ref[:, cs]
    y = _rms_rows(y_ref[...], g_ref[...])
    o_ref[...] = (y * (1.0 / (1.0 + jnp.exp(-y)))).astype(o_ref.dtype)


def _conv_norm_swish(grp, u, state, w_dw, b_dw, g_norm, o_prev):
    M, D = u.shape
    tt = _pick_tile(grp.seq, 256)
    nt = grp.seq // tt
    assert grp.seq % tt == 0 and tt % CONV_HALO == 0 and grp.row0 % tt == 0
    blk0 = grp.row0 // tt
    hb = tt // CONV_HALO
    prev_args, prev_specs, aliased = _chained(o_prev)
    return pl.pallas_call(
        functools.partial(_conv_kernel, tt=tt),
        grid=(grp.batch, nt),
        in_specs=[
            pl.BlockSpec((tt, D), lambda b, i: (blk0 + b * nt + i, 0)),
            pl.BlockSpec((CONV_HALO, D), lambda b, i: (jnp.maximum((blk0 + b * nt + i) * hb - 1, 0), 0)),
            pl.BlockSpec((None, CONV_HALO, D), lambda b, i: (b, 0, 0)),
            pl.BlockSpec((CONV_WIDTH, D), lambda b, i: (0, 0)),
            pl.BlockSpec((1, D), lambda b, i: (0, 0)),
            pl.BlockSpec((1, D), lambda b, i: (0, 0)),
        ] + prev_specs,
        out_specs=pl.BlockSpec((tt, D), lambda b, i: (blk0 + b * nt + i, 0)),
        out_shape=jax.ShapeDtypeStruct((M, D), BF16),
        input_output_aliases={6: 0} if aliased else {},
        scratch_shapes=[pltpu.VMEM((tt + CONV_HALO + SUBLANES, D), F32),
                        pltpu.VMEM((min(CONV_ROW_CHUNK, tt) + SUBLANES, HEAD_DIM), F32),
                        pltpu.VMEM((tt, D), F32)],
        compiler_params=_params("parallel", "arbitrary"),
        name="conv_norm_swish",
    )(u, u, state, w_dw, b_dw.reshape(1, D), g_norm.reshape(1, D), *prev_args)


def kernel(x_prompt, x_sample, mem_prompt, cache_sb_k, cache_sb_v, cache_dsa_k, cache_dsa_v, cache_idx_k, cache_mem_k, cache_mem_v, state_conv, norm_mix, norm_cross, norm_mlp, norm_final, w_in_ab, w_out_ab, rel_bias, w_pw1, b_pw1, w_dw, b_dw, g_conv_norm, w_pw2, b_pw2, w_cq, w_mk, w_mv, w_co, w_up, w_down):
    Bp, Tp, D = x_prompt.shape
    Bs, Ts, _ = x_sample.shape
    past = cache_sb_k.shape[2]
    depth = norm_mix.shape[0]
    n_mem = mem_prompt.shape[1]
    n_sb, n_dsa = cache_sb_k.shape[3], cache_dsa_k.shape[3]
    sbw, dsw = n_sb * HEAD_DIM, n_dsa * HEAD_DIM
    qiw = N_IDX_HEADS * IDX_DIM
    cw = w_cq.shape[2]
    Mp, Ms = Bp * Tp, Bs * Ts
    M = Mp + Ms
    prompt = _Group(Bp, Tp, 0, 0)
    sample = _Group(Bs, Ts, Mp, past)

    x = jnp.concatenate([x_prompt.reshape(Mp, D), x_sample.reshape(Ms, D)], axis=0)
    mem_flat = mem_prompt.reshape(Bp * n_mem, D)

    src = dict(zip(("q_sb", "k_sb", "v_sb", "q_d", "k_d", "v_d", "q_i", "k_i", "w_i"),
                   np.cumsum([0, sbw, sbw, sbw, dsw, dsw, dsw, qiw, IDX_DIM])))
    order = ("k_sb", "v_sb", "k_d", "v_d", "q_sb", "q_d", "q_i")
    width = dict(q_sb=sbw, k_sb=sbw, v_sb=sbw, q_d=dsw, k_d=dsw, v_d=dsw, q_i=qiw)
    col = dict(zip(order, np.cumsum([0] + [width[n] for n in order])[:-1]))
    col = {k: int(v) for k, v in col.items()}
    n32 = 2 * sbw + 2 * dsw

    outs = {k: [] for k in ("p_sb_k", "p_sb_v", "p_d_k", "p_d_v", "p_ki", "p_mem_k", "p_mem_v", "p_conv",
                            "s_sb_k", "s_sb_v", "s_d_k", "s_d_v", "s_ki", "s_conv")}

    def split_rows(a):
        return a[:Mp].reshape(Bp, Tp, -1), a[Mp:].reshape(Bs, Ts, -1)

    def last_rows(a, grp, state, n):
        if grp.seq >= n:
            return jnp.stack([a[grp.row0 + (b + 1) * grp.seq - n:grp.row0 + (b + 1) * grp.seq]
                              for b in range(grp.batch)])
        rows = a[grp.row0:grp.row0 + grp.batch * grp.seq].reshape(grp.batch, grp.seq, -1)
        return jnp.concatenate([state, rows], axis=1)[:, -n:]

    w_out16, w_pw1_16, w_pw2_16 = w_out_ab.astype(BF16), w_pw1.astype(BF16), w_pw2.astype(BF16)
    w_cq16, w_mk16, w_mv16, w_co16 = (w.astype(BF16) for w in (w_cq, w_mk, w_mv, w_co))
    w_up16, w_down16 = w_up.astype(BF16), w_down.astype(BF16)
    q_scale = HEAD_DIM ** -0.5 * LOG2E
    col_scale = jnp.concatenate([jnp.full((width[n],), q_scale if n in ("q_sb", "q_d") else 1.0, F32)
                                 for n in order])

    for l in range(depth):
        i = l // 2
        if l % 2 == 0:
            w_in = w_in_ab[i]
            w_main = jnp.concatenate([w_in[:, src[n]:src[n] + width[n]] for n in order], axis=1).astype(BF16)
            tail_w = w_in.shape[1] - int(src["k_i"])
            w_tail = jnp.pad(w_in[:, src["k_i"]:], ((0, 0), (0, HEAD_DIM - tail_w))).astype(BF16)
            proj, kv32, tail = _in_proj(x, w_main, norm_mix[l], col_scale, w_tail, n32=n32)

            for name, key, nh in (("sb_k", "k_sb", n_sb), ("sb_v", "v_sb", n_sb),
                                  ("d_k", "k_d", n_dsa), ("d_v", "v_d", n_dsa)):
                ap, as_ = split_rows(kv32[:, col[key]:col[key] + width[key]])
                outs["p_" + name].append(ap.reshape(Bp, Tp, nh, HEAD_DIM))
                outs["s_" + name].append(as_.reshape(Bs, Ts, nh, HEAD_DIM))
            k_i = tail[:, :IDX_DIM]
            kip, kis = split_rows(k_i)
            outs["p_ki"].append(kip)
            outs["s_ki"].append(kis)

            tk_s = ATTN_TILE
            s_pad_s = -(-(past + Ts) // tk_s) * tk_s
            kidx_p = kip.astype(BF16)
            kidx_s = jnp.concatenate([cache_idx_k[i].astype(BF16), kis.astype(BF16),
                                      jnp.zeros((Bs, s_pad_s - past - Ts, IDX_DIM), BF16)], axis=1)

            o_shape = (M, sbw + dsw)
            sb_cols = (col["q_sb"], col["k_sb"], col["v_sb"])
            d_cols = (col["q_i"], col["q_d"], col["k_d"], col["v_d"])
            o = _sb_attention(prompt, proj, sb_cols, cache_sb_k[i], cache_sb_v[i], None, o_shape, 0,
                              hp=SB_HEADS_PER_STEP)
            o = _dsa_attention(prompt, proj, tail, d_cols, kidx_p, cache_dsa_k[i], cache_dsa_v[i], rel_bias,
                               o, o_shape, sbw, hp=DSA_HEADS_PER_STEP)
            o = _sb_attention(sample, proj, sb_cols, cache_sb_k[i], cache_sb_v[i], o, o_shape, 0,
                              hp=SB_HEADS_PER_STEP)
            o = _dsa_attention(sample, proj, tail, d_cols, kidx_s, cache_dsa_k[i], cache_dsa_v[i], rel_bias,
                               o, o_shape, sbw, hp=n_dsa)
            x, qc = _residual_matmul_then_proj(o, w_out16, i, x, None, norm_cross[l], w_cq16, l)
        else:
            u = _matmul(x, w_pw1_16, layer=i, gain=norm_mix[l], bias=b_pw1[i], glu=True)
            front = CONV_HALO - (CONV_WIDTH - 1)
            state_p = jnp.zeros((Bp, CONV_HALO, D), F32)
            state_s = jnp.concatenate([jnp.zeros((Bs, front, D), F32), state_conv[i]], axis=1)
            y = _conv_norm_swish(prompt, u, state_p, w_dw[i], b_dw[i], g_conv_norm[i], None)
            y = _conv_norm_swish(sample, u, state_s, w_dw[i], b_dw[i], g_conv_norm[i], y)
            outs["p_conv"].append(last_rows(u, prompt, state_p, CONV_WIDTH - 1))
            outs["s_conv"].append(last_rows(u, sample, state_s, CONV_WIDTH - 1))
            x, qc = _residual_matmul_then_proj(y, w_pw2_16, i, x, b_pw2[i], norm_cross[l], w_cq16, l)

        mk = _matmul(mem_flat, w_mk16, layer=l)
        mv = _matmul(mem_flat, w_mv16, layer=l)
        outs["p_mem_k"].append(mk.reshape(Bp, n_mem, cw // HEAD_DIM, HEAD_DIM))
        outs["p_mem_v"].append(mv.reshape(Bp, n_mem, cw // HEAD_DIM, HEAD_DIM))
        oc = _cross_attention(prompt, qc, mk.astype(BF16).reshape(Bp, n_mem, cw),
                              mv.astype(BF16).reshape(Bp, n_mem, cw), None)
        oc = _cross_attention(sample, qc, cache_mem_k[l].astype(BF16).reshape(Bs, n_mem, cw),
                              cache_mem_v[l].astype(BF16).reshape(Bs, n_mem, cw), oc)
        if l < depth - 1:
            x = _cross_out_mlp(x, oc, w_co16, norm_mlp[l], w_up16, w_down16, l)
        else:
            y_p, y_s = _cross_out_mlp(x, oc, w_co16, norm_mlp[l], w_up16, w_down16, l,
                                      final_gain=norm_final, m_prompt=Mp)

    st = lambda k: jnp.stack(outs[k])
    return (y_p.reshape(Bp, Tp, D), y_s.reshape(Bs, Ts, D), st("p_sb_k"), st("p_sb_v"), st("p_d_k"), st("p_d_v"),
            st("p_ki"), st("p_mem_k"), st("p_mem_v"), st("p_conv"), st("s_sb_k"), st("s_sb_v"), st("s_d_k"),
            st("s_d_v"), st("s_ki"), st("s_conv"))
```

```python
import functools
import math
from typing import NamedTuple

import numpy as np
import jax
import jax.numpy as jnp
from jax import lax
from jax.experimental import pallas as pl
from jax.experimental.pallas import tpu as pltpu

F32 = jnp.float32
BF16 = jnp.bfloat16

HEAD_DIM = 128
IDX_DIM = 64
N_IDX_HEADS = 16
CHUNK = 64
TOPK_MAX = 256
N_BUCKETS = 32
MAX_DISTANCE = 128
CONV_WIDTH = 31
CONV_HALO = 32
RMS_EPS = 1e-6
MASKED_LOGIT = -1e30
LOG2E = math.log2(math.e)
_INT32_MIN = -2 ** 31
VMEM_LIMIT_BYTES = 56 * 1024 * 1024
SB_HEADS_PER_STEP = 8
DSA_HEADS_PER_STEP = 4
SB_DEAD_BITS = 160.0
SB_INTERLEAVE = 4
ATTN_TILE = 256
SELECT_ROWS_PER_STEP = 512
FAR_TILES_PER_TRIP = 4


class _Group(NamedTuple):
    batch: int
    seq: int
    row0: int
    past: int


def _params(*semantics):
    return pltpu.CompilerParams(dimension_semantics=semantics, vmem_limit_bytes=VMEM_LIMIT_BYTES)


def _pick_tile(n, pref):
    if n <= pref:
        return n
    t = pref
    while n % t:
        t //= 2
    return t


def _row_tile(m, pref):
    t = min(m, pref) // 8 * 8
    while m % t:
        t -= 8
    return t


def _chained(prev):
    if prev is None:
        return [], [], None
    return [prev], [pl.BlockSpec(memory_space=pl.ANY)], True


def _rms_rows(x, g):
    ms = jnp.mean(x * x, axis=-1, keepdims=True)
    return x * lax.rsqrt(ms + RMS_EPS) * g


def _mm_kernel(*refs, has_norm, has_bias, has_res, glu):
    it = iter(refs)
    x_ref = next(it)
    g_ref = next(it) if has_norm else None
    w_ref = next(it)
    w2_ref = next(it) if glu else None
    b_ref = next(it) if has_bias else None
    b2_ref = next(it) if (has_bias and glu) else None
    r_ref = next(it) if has_res else None
    o_ref = next(it)
    xn_ref = next(it) if has_norm else None

    if has_norm:
        @pl.when(pl.program_id(1) == 0)
        def _():
            xn_ref[...] = _rms_rows(x_ref[...], g_ref[...]).astype(BF16)
        xb = xn_ref[...]
    else:
        xb = x_ref[...].astype(BF16)

    y = jnp.dot(xb, w_ref[...], preferred_element_type=F32)
    if has_bias:
        y = y + b_ref[...]
    if glu:
        gate = jnp.dot(xb, w2_ref[...], preferred_element_type=F32)
        if has_bias:
            gate = gate + b2_ref[...]
        y = y * (1.0 / (1.0 + jnp.exp(-gate)))
    if has_res:
        y = y + r_ref[...]
    o_ref[...] = y.astype(o_ref.dtype)


def _matmul(x, w, *, layer=0, gain=None, bias=None, residual=None, glu=False, out_dtype=F32, tm=1152, tn=1024):
    M, K = x.shape
    n_out = w.shape[2] // 2 if glu else w.shape[2]
    tm = _row_tile(M, tm)
    tn = _pick_tile(n_out, tn)
    nj = n_out // tn
    has_norm, has_bias, has_res = gain is not None, bias is not None, residual is not None

    args = [x]
    specs = [pl.BlockSpec((tm, K), lambda i, j: (i, 0))]
    if has_norm:
        args.append(gain.reshape(1, K))
        specs.append(pl.BlockSpec((1, K), lambda i, j: (0, 0)))
    args.append(w)
    specs.append(pl.BlockSpec((None, K, tn), lambda i, j: (layer, 0, j)))
    if glu:
        args.append(w)
        specs.append(pl.BlockSpec((None, K, tn), lambda i, j: (layer, 0, j + nj)))
    if has_bias:
        b2d = bias.reshape(1, -1)
        args.append(b2d)
        specs.append(pl.BlockSpec((1, tn), lambda i, j: (0, j)))
        if glu:
            args.append(b2d)
            specs.append(pl.BlockSpec((1, tn), lambda i, j: (0, j + nj)))
    if has_res:
        args.append(residual)
        specs.append(pl.BlockSpec((tm, tn), lambda i, j: (i, j)))

    return pl.pallas_call(
        functools.partial(_mm_kernel, has_norm=has_norm, has_bias=has_bias, has_res=has_res, glu=glu),
        grid=(M // tm, nj),
        in_specs=specs,
        out_specs=pl.BlockSpec((tm, tn), lambda i, j: (i, j)),
        out_shape=jax.ShapeDtypeStruct((M, n_out), out_dtype),
        scratch_shapes=[pltpu.VMEM((tm, K), BF16)] if has_norm else [],
        compiler_params=_params("parallel", "arbitrary"),
        name="norm_matmul",
    )(*args)


def _res_proj_kernel(y_ref, w_ref, *refs, has_bias):
    it = iter(refs)
    b_ref = next(it) if has_bias else None
    r_ref, g_ref, wq_ref, o_ref, q_ref, x1_ref = (next(it) for _ in range(6))
    j = pl.program_id(1)
    y = jnp.dot(y_ref[...], w_ref[...], preferred_element_type=F32) + r_ref[...]
    if has_bias:
        y = y + b_ref[...]
    o_ref[...] = y
    x1_ref[j] = y

    @pl.when(j == pl.num_programs(1) - 1)
    def _():
        x1 = jnp.concatenate([x1_ref[t] for t in range(x1_ref.shape[0])], axis=1)
        q_ref[...] = jnp.dot(_rms_rows(x1, g_ref[...]).astype(BF16), wq_ref[...],
                             preferred_element_type=F32).astype(q_ref.dtype)


def _residual_matmul_then_proj(y, w, layer, residual, bias, gain, wq, q_layer, *, tm=1152, tn=1024):
    M, K = y.shape
    N, nq = w.shape[2], wq.shape[2]
    tm = _row_tile(M, tm)
    assert N % tn == 0
    nj = N // tn
    has_bias = bias is not None
    args = [y, w]
    specs = [pl.BlockSpec((tm, K), lambda i, j: (i, 0)), pl.BlockSpec((None, K, tn), lambda i, j: (layer, 0, j))]
    if has_bias:
        args.append(bias.reshape(1, N))
        specs.append(pl.BlockSpec((1, tn), lambda i, j: (0, j)))
    args += [residual, gain.reshape(1, N), wq]
    specs += [pl.BlockSpec((tm, tn), lambda i, j: (i, j)), pl.BlockSpec((1, N), lambda i, j: (0, 0)),
              pl.BlockSpec((None, N, nq), lambda i, j: (q_layer, 0, 0))]
    return pl.pallas_call(
        functools.partial(_res_proj_kernel, has_bias=has_bias),
        grid=(M // tm, nj),
        in_specs=specs,
        out_specs=[pl.BlockSpec((tm, tn), lambda i, j: (i, j)), pl.BlockSpec((tm, nq), lambda i, j: (i, 0))],
        out_shape=[jax.ShapeDtypeStruct((M, N), F32), jax.ShapeDtypeStruct((M, nq), BF16)],
        scratch_shapes=[pltpu.VMEM((nj, tm, tn), F32)],
        compiler_params=_params("parallel", "arbitrary"),
        name="residual_matmul_proj",
    )(*args)


def _in_proj_kernel(x_ref, g_ref, w_ref, s_ref, wt_ref, o16_ref, o32_ref, ot_ref, xn_ref, *, n32_tiles):
    j = pl.program_id(1)

    @pl.when(j == 0)
    def _():
        xn_ref[...] = _rms_rows(x_ref[...], g_ref[...]).astype(BF16)
        ot_ref[...] = jnp.dot(xn_ref[...], wt_ref[...], preferred_element_type=F32)

    y = jnp.dot(xn_ref[...], w_ref[...], preferred_element_type=F32)
    o16_ref[...] = (y * s_ref[...]).astype(BF16)

    @pl.when(j < n32_tiles)
    def _():
        o32_ref[...] = y


def _in_proj(x, w, gain, col_scale, w_tail, *, n32, tm=1152, tn=1024):
    M, K = x.shape
    N = w.shape[1]
    nt = w_tail.shape[1]
    tm = _row_tile(M, tm)
    assert N % tn == 0 and n32 % tn == 0
    n32_tiles = n32 // tn
    return pl.pallas_call(
        functools.partial(_in_proj_kernel, n32_tiles=n32_tiles),
        grid=(M // tm, N // tn),
        in_specs=[
            pl.BlockSpec((tm, K), lambda i, j: (i, 0)),
            pl.BlockSpec((1, K), lambda i, j: (0, 0)),
            pl.BlockSpec((K, tn), lambda i, j: (0, j)),
            pl.BlockSpec((1, tn), lambda i, j: (0, j)),
            pl.BlockSpec((K, nt), lambda i, j: (0, 0)),
        ],
        out_specs=[
            pl.BlockSpec((tm, tn), lambda i, j: (i, j)),
            pl.BlockSpec((tm, tn), lambda i, j: (i, jnp.minimum(j, n32_tiles - 1))),
            pl.BlockSpec((tm, nt), lambda i, j: (i, 0)),
        ],
        out_shape=[jax.ShapeDtypeStruct((M, N), BF16), jax.ShapeDtypeStruct((M, n32), F32),
                   jax.ShapeDtypeStruct((M, nt), F32)],
        scratch_shapes=[pltpu.VMEM((tm, K), BF16)],
        compiler_params=_params("parallel", "arbitrary"),
        name="in_proj",
    )(x, gain.reshape(1, K), w, col_scale.reshape(1, N), w_tail)


def _mlp_kernel(x_ref, oc_ref, wco_ref, g_ref, wu_ref, wd_ref, *refs, n_prompt_tiles):
    final = n_prompt_tiles is not None
    if final:
        gf_ref, op_ref, os_ref, x2_ref, xn_ref, acc_ref = refs
    else:
        o_ref, x2_ref, xn_ref, acc_ref = refs
    i, f = pl.program_id(0), pl.program_id(1)

    @pl.when(f == 0)
    def _():
        x2 = x_ref[...] + jnp.dot(oc_ref[...], wco_ref[...], preferred_element_type=F32)
        x2_ref[...] = x2
        xn_ref[...] = _rms_rows(x2, g_ref[...]).astype(BF16)
        acc_ref[...] = jnp.zeros_like(acc_ref)

    h = jnp.dot(xn_ref[...], wu_ref[...], preferred_element_type=F32)
    h = jnp.maximum(h, 0.0)
    h = (h * h).astype(BF16)
    acc_ref[...] += jnp.dot(h, wd_ref[...], preferred_element_type=F32)

    last = f == pl.num_programs(1) - 1
    if not final:
        @pl.when(last)
        def _():
            o_ref[...] = x2_ref[...] + acc_ref[...]
    else:
        @pl.when(last & (i < n_prompt_tiles))
        def _():
            op_ref[...] = _rms_rows(x2_ref[...] + acc_ref[...], gf_ref[...])

        @pl.when(last & (i >= n_prompt_tiles))
        def _():
            os_ref[...] = _rms_rows(x2_ref[...] + acc_ref[...], gf_ref[...])


def _cross_out_mlp(x, oc, w_co, gain, w_up, w_down, layer, *, final_gain=None, m_prompt=None, tm=512, tf=512):
    M, D = x.shape
    F = w_up.shape[2]
    cw = oc.shape[1]
    final = final_gain is not None
    tm = _pick_tile(M - m_prompt, tm) if final else _pick_tile(M, tm)
    tf = _pick_tile(F, tf)
    args = [x, oc, w_co, gain.reshape(1, D), w_up, w_down]
    specs = [
        pl.BlockSpec((tm, D), lambda i, f: (i, 0)),
        pl.BlockSpec((tm, cw), lambda i, f: (i, 0)),
        pl.BlockSpec((None, cw, D), lambda i, f: (layer, 0, 0)),
        pl.BlockSpec((1, D), lambda i, f: (0, 0)),
        pl.BlockSpec((None, D, tf), lambda i, f: (layer, 0, f)),
        pl.BlockSpec((None, tf, D), lambda i, f: (layer, f, 0)),
    ]
    if final:
        assert m_prompt % tm == 0 and (M - m_prompt) % tm == 0
        npt = m_prompt // tm
        args.append(final_gain.reshape(1, D))
        specs.append(pl.BlockSpec((1, D), lambda i, f: (0, 0)))
        out_specs = [pl.BlockSpec((tm, D), lambda i, f: (jnp.minimum(i, npt - 1), 0)),
                     pl.BlockSpec((tm, D), lambda i, f: (jnp.maximum(i - npt, 0), 0))]
        out_shape = [jax.ShapeDtypeStruct((m_prompt, D), F32), jax.ShapeDtypeStruct((M - m_prompt, D), F32)]
        semantics = ("arbitrary", "arbitrary")
    else:
        npt = None
        out_specs = pl.BlockSpec((tm, D), lambda i, f: (i, 0))
        out_shape = jax.ShapeDtypeStruct((M, D), F32)
        semantics = ("parallel", "arbitrary")
    return pl.pallas_call(
        functools.partial(_mlp_kernel, n_prompt_tiles=npt),
        grid=(M // tm, F // tf),
        in_specs=specs,
        out_specs=out_specs,
        out_shape=out_shape,
        scratch_shapes=[pltpu.VMEM((tm, D), F32), pltpu.VMEM((tm, D), BF16), pltpu.VMEM((tm, D), F32)],
        compiler_params=_params(*semantics),
        name="cross_out_mlp",
    )(*args)


def _nt_dot(a, b):
    return lax.dot_general(a, b, (((1,), (1,)), ((), ())), preferred_element_type=F32)


def _kv_operands(grp, proj, k_col, v_col, k_cache, v_cache, *, tk, hp):
    wp = hp * HEAD_DIM
    if grp.past == 0:
        blk0 = grp.row0 // grp.seq
        args = [proj, proj]
        specs = [pl.BlockSpec((grp.seq, wp), lambda b, h, i, c=k_col // wp: (blk0 + b, c + h),
                              pipeline_mode=pl.Buffered(1)),
                 pl.BlockSpec((grp.seq, wp), lambda b, h, i, c=v_col // wp: (blk0 + b, c + h),
                              pipeline_mode=pl.Buffered(1))]
        return args, specs

    def new_tile(col):
        rows = lax.slice(proj, (grp.row0, col), (grp.row0 + grp.batch * grp.seq, col + k_cache.shape[2] * HEAD_DIM))
        rows = rows.reshape(grp.batch, grp.seq, -1)
        return jnp.pad(rows, ((0, 0), (0, tk - grp.seq), (0, 0)))

    args = [new_tile(k_col), new_tile(v_col)]
    specs = [pl.BlockSpec((None, tk, wp), lambda b, h, i: (b, 0, h))] * 2
    n_heads = k_cache.shape[2]
    for cache in (k_cache, v_cache):
        args.append(cache.reshape(grp.batch, grp.past * n_heads, HEAD_DIM))
        specs.append(pl.BlockSpec((None, grp.past * n_heads, HEAD_DIM), lambda b, h, i: (b, 0, 0)))
    return args, specs


def _split_kv_refs(refs, has_past):
    if not has_past:
        return refs[0], refs[1], None, None, refs[2:]
    return refs[0], refs[1], refs[2], refs[3], refs[4:]


def _tile_reader(new_ref, past_ref, *, tk, hp, n_heads):
    def read(j, hh, diag, n=1):
        hs = slice(hh * HEAD_DIM, (hh + 1) * HEAD_DIM)
        if past_ref is None:
            return new_ref[pl.ds(pl.multiple_of(j * tk, tk), n * tk), hs]
        if diag:
            return new_ref[:, hs]
        head = pl.program_id(1) * hp + hh
        return past_ref[pl.ds(j * (tk * n_heads) + head, n * tk, stride=n_heads), :].astype(BF16)
    return read


def _sb_kernel(q_ref, *refs, tq, tk, hp, n_heads, past, has_past):
    k_new, v_new, k_past, v_past, rest = _split_kv_refs(refs, has_past)
    u_ref, o_ref, acc_ref, carry_ref = rest[0], rest[-3], rest[-2], rest[-1]
    read_k = _tile_reader(k_new, k_past, tk=tk, hp=hp, n_heads=n_heads)
    read_v = _tile_reader(v_new, v_past, tk=tk, hp=hp, n_heads=n_heads)

    i = pl.program_id(2)
    q_start = past + i * tq
    jd = q_start // tk
    reps = tk // HEAD_DIM

    acc_ref[...] = jnp.zeros_like(acc_ref)
    carry_ref[...] = jnp.zeros_like(carry_ref)

    def step(j, masked):
        if masked:
            t_pos = q_start + lax.broadcasted_iota(jnp.int32, (tq, tk), 0)
            s_pos = j * tk + lax.broadcasted_iota(jnp.int32, (tq, tk), 1)
            causal = t_pos > s_pos
        for h0 in range(0, hp, SB_INTERLEAVE):
            group_step(j, masked, causal if masked else None, range(h0, h0 + SB_INTERLEAVE))

    def group_step(j, masked, causal, group):
        heads = [(hh, slice(hh * HEAD_DIM, (hh + 1) * HEAD_DIM)) for hh in group]
        z2s = [_nt_dot(q_ref[:, hs], read_k(j, hh, masked)) for hh, hs in heads]
        ns, log2_betas, css = [], [], []
        for z2 in z2s:
            neg_abs = lax.bitcast_convert_type(
                lax.bitcast_convert_type(z2, jnp.int32) | jnp.int32(_INT32_MIN), F32)
            n = jnp.maximum(z2, 0.0) + jnp.log(1.0 + jnp.exp2(neg_abs)) * LOG2E
            log2_betas.append(z2 - n)
            if masked:
                n = jnp.where(causal, n, 0.0)
            n16 = n.astype(BF16)
            css.append(jnp.dot(n16, u_ref[...], preferred_element_type=F32))
            ns.append(n16)
        for (hh, hs), n16, log2_beta, cs in zip(heads, ns, log2_betas, css):
            a = jnp.exp2(log2_beta - cs)
            if masked:
                a = jnp.where(causal, a, 0.0)
            carry = carry_ref[:, hs]
            pv = jnp.dot(a.astype(BF16), read_v(j, hh, masked), preferred_element_type=F32)
            acc_ref[:, hs] += jnp.exp2(-carry) * pv
            carry_ref[:, hs] = carry + jnp.broadcast_to(cs[:, 0:1] + n16[:, 0:1].astype(F32), (tq, HEAD_DIM))

    step(jd, True)

    def live(state):
        t, min_carry = state
        return (t < jd) & (min_carry < SB_DEAD_BITS)

    def body(state):
        t, _ = state
        step(jd - 1 - t, False)
        return t + 1, jnp.min(carry_ref[...])

    lax.while_loop(live, body, (jnp.int32(0), jnp.min(carry_ref[...])))
    o_ref[...] = acc_ref[...].astype(o_ref.dtype)


def _suffix_sum_matrix(tk):
    r = np.arange(tk)
    return jnp.asarray((r[:, None] > r[None, :]).astype(np.float32), dtype=BF16)


def _attn_tiles(grp):
    tq = _pick_tile(grp.seq, ATTN_TILE)
    tk = ATTN_TILE if grp.past else tq
    nq = grp.seq // tq
    assert grp.seq % tq == 0 and tk % tq == 0 and grp.past % tk == 0 and grp.row0 % tq == 0
    assert grp.past == 0 or nq == 1
    assert grp.past > 0 or grp.row0 % grp.seq == 0
    return tq, tk, nq


def _sb_attention(grp, proj, cols, k_cache, v_cache, o_prev, o_shape, o_col, *, hp):
    H = k_cache.shape[2]
    tq, tk, nq = _attn_tiles(grp)
    wp = hp * HEAD_DIM
    q_blk0 = grp.row0 // tq
    kv_args, kv_specs = _kv_operands(grp, proj, cols[1], cols[2], k_cache, v_cache, tk=tk, hp=hp)
    prev_args, prev_specs, aliased = _chained(o_prev)
    n_in = 1 + len(kv_args) + 1
    return pl.pallas_call(
        functools.partial(_sb_kernel, tq=tq, tk=tk, hp=hp, n_heads=H, past=grp.past, has_past=grp.past > 0),
        grid=(grp.batch, H // hp, nq),
        in_specs=[pl.BlockSpec((tq, wp), lambda b, h, i: (q_blk0 + b * nq + i, cols[0] // wp + h))]
        + kv_specs
        + [pl.BlockSpec((tk, tk), lambda b, h, i: (0, 0), pipeline_mode=pl.Buffered(1))]
        + prev_specs,
        out_specs=pl.BlockSpec((tq, wp), lambda b, h, i: (q_blk0 + b * nq + i, o_col // wp + h)),
        out_shape=jax.ShapeDtypeStruct(o_shape, BF16),
        input_output_aliases={n_in: 0} if aliased else {},
        scratch_shapes=[pltpu.VMEM((tq, wp), F32), pltpu.VMEM((tq, wp), F32)],
        compiler_params=_params("parallel", "parallel", "arbitrary"),
        name="sb_attention",
    )(proj, *kv_args, _suffix_sum_matrix(tk), *prev_args)


def _sortable_key(x):
    b = lax.bitcast_convert_type(x, jnp.int32)
    return b ^ ((b >> 31) & jnp.int32(0x7FFFFFFF))


_KEY_NEG_INF = int(np.int32(np.array(-np.inf, np.float32).view(np.int32) ^ 0x7FFFFFFF))
_KEY_LOWEST_FINITE = int(np.int32(np.array(np.finfo(np.float32).min, np.float32).view(np.int32) ^ 0x7FFFFFFF))
COUNT_ROW_GROUP = 128


def _dsa_select_kernel(qi_ref, wi_ref, ki2_ref, mask_ref, keys_ref, wb_ref,
                       *, tq, tk, nkt, nsub, by_batch, past, topk):
    i = pl.program_id(1)
    rows = nsub * tq
    first_start = past + i * tq * (1 if by_batch else nsub)
    jd_min = first_start // tk
    n_cut = 1 if by_batch else nsub
    jd = jd_min + n_cut - 1
    reps = tk // HEAD_DIM

    w = wi_ref[:, IDX_DIM:IDX_DIM + N_IDX_HEADS] * (IDX_DIM ** -0.5 * N_IDX_HEADS ** -0.5)
    for hh in range(N_IDX_HEADS):
        wb_ref[hh] = jnp.broadcast_to(w[:, hh:hh + 1], (rows, HEAD_DIM))

    def idx_rows(j, r0, nr, kt):
        tot = jnp.zeros((nr, tk), F32)
        for p in range(N_IDX_HEADS // 2):
            sc = jnp.dot(qi_ref[r0:r0 + nr, p * 2 * IDX_DIM:(p + 1) * 2 * IDX_DIM], kt,
                         preferred_element_type=F32)
            w0 = jnp.concatenate([wb_ref[2 * p, r0:r0 + nr, :]] * reps, axis=1)
            w1 = jnp.concatenate([wb_ref[2 * p + 1, r0:r0 + nr, :]] * reps, axis=1)
            tot = tot + w0 * jnp.maximum(sc[:, :tk], 0.0) + w1 * jnp.maximum(sc[:, tk:], 0.0)
        return _sortable_key(tot)

    def idx_tile(j):
        if by_batch:
            return jnp.concatenate([idx_rows(j, s * tq, tq, ki2_ref[s, j]) for s in range(nsub)], axis=0)
        return idx_rows(j, 0, rows, ki2_ref[0, j])

    def idx_body(j, c):
        keys_ref[j] = idx_tile(j)
        return c

    lax.fori_loop(0, jd_min, idx_body, 0)
    row = lax.broadcasted_iota(jnp.int32, (rows, tk), 0)
    t_pos = first_start + (row % tq if by_batch else row)
    for d in range(n_cut):
        s_pos = (jd_min + d) * tk + lax.broadcasted_iota(jnp.int32, (rows, tk), 1)
        visible = (t_pos // CHUNK) >= (s_pos // CHUNK)
        keys_ref[jd_min + d] = jnp.where(visible, idx_tile(jd_min + d), jnp.int32(_KEY_NEG_INF))

    n_vis = jd + 1

    @pl.when(n_vis % 2 == 1)
    def _():
        keys_ref[n_vis] = jnp.full((rows, tk), _INT32_MIN, jnp.int32)

    n_pairs = (n_vis + 1) // 2
    rg = min(COUNT_ROW_GROUP, rows)

    def count_ge(cand):
        parts = []
        for g in range(rows // rg):
            cg = cand[g * rg:(g + 1) * rg]

            def body(t, cnt, g=g, cg=cg):
                for jj in (2 * t, 2 * t + 1):
                    kj = keys_ref[jj, pl.ds(g * rg, rg), :]
                    for r in range(reps):
                        cnt = cnt + jnp.where(kj[:, r * HEAD_DIM:(r + 1) * HEAD_DIM] >= cg, 1.0, 0.0)
                return cnt

            parts.append(lax.fori_loop(0, n_pairs, body, jnp.zeros((rg, HEAD_DIM), F32)))
        return jnp.sum(jnp.concatenate(parts, axis=0), axis=-1, keepdims=True)

    kf = float(topk)
    zero = jnp.zeros((rows, HEAD_DIM), jnp.int32)
    c = count_ge(zero)
    res = jnp.where(c >= kf, zero, jnp.int32(_INT32_MIN))
    cnt = jnp.where(c >= kf, c, float(tk) * (2 * n_pairs).astype(F32))
    for bit in range(30, -1, -1):
        cand = res + jnp.int32(1 << bit)
        c = count_ge(cand)
        res = jnp.where(c >= kf, cand, res)
        cnt = jnp.where(c >= kf, c, cnt)

    tied_cut = jnp.max(jnp.where((cnt != kf) & (res[:, 0:1] > jnp.int32(_KEY_NEG_INF)), 1.0, 0.0)) > 0.0

    @pl.when(tied_cut)
    def _():
        col = lax.broadcasted_iota(jnp.int32, (rg, tk), 1)

        def count_tied(before, strict_above):
            parts = []
            for g in range(rows // rg):
                grp_rows = slice(g * rg, (g + 1) * rg)
                res_g = jnp.concatenate([res[grp_rows]] * reps, axis=1)
                bef_g = None if strict_above else jnp.concatenate([before[grp_rows]] * reps, axis=1)

                def body(j, acc, g=g, res_g=res_g, bef_g=bef_g):
                    kj = keys_ref[j, pl.ds(g * rg, rg), :]
                    if strict_above:
                        hit = jnp.where(kj > res_g, 1.0, 0.0)
                    else:
                        hit = jnp.where(kj == res_g, jnp.where(j * tk + col < bef_g, 1.0, 0.0), 0.0)
                    return acc + sum(hit[:, r * HEAD_DIM:(r + 1) * HEAD_DIM] for r in range(reps))

                parts.append(lax.fori_loop(0, n_vis, body, jnp.zeros((rg, HEAD_DIM), F32)))
            return jnp.sum(jnp.concatenate(parts, axis=0), axis=-1, keepdims=True)

        need = kf - count_tied(None, True)
        r_keep = jnp.zeros((rows, HEAD_DIM), jnp.int32)
        for bit in range((nkt * tk).bit_length() - 1, -1, -1):
            cand = r_keep | jnp.int32(1 << bit)
            r_keep = jnp.where(count_tied(cand, False) < need, cand, r_keep)

        def drop_late_ties(j, c):
            for g in range(rows // rg):
                grp_rows = slice(g * rg, (g + 1) * rg)
                res_g = jnp.concatenate([res[grp_rows]] * reps, axis=1)
                keep_g = jnp.concatenate([r_keep[grp_rows]] * reps, axis=1)
                kj = keys_ref[j, pl.ds(g * rg, rg), :]
                keys_ref[j, pl.ds(g * rg, rg), :] = jnp.where(
                    kj == res_g, jnp.where(j * tk + col > keep_g, res_g - 1, kj), kj)
            return c

        lax.fori_loop(0, n_vis, drop_late_ties, 0)

    thr = jnp.maximum(res, jnp.int32(_KEY_LOWEST_FINITE))
    thr = jnp.concatenate([thr] * reps, axis=1)

    def write_mask(j, c):
        tile = jnp.where(keys_ref[j] >= thr, 0.0, MASKED_LOGIT).astype(mask_ref.dtype)
        for s_ in range(nsub):
            mask_ref[s_, j] = tile[s_ * tq:(s_ + 1) * tq]
        return c

    lax.fori_loop(0, jd + 1, write_mask, 0)

    def write_hidden(j, c):
        for s_ in range(nsub):
            mask_ref[s_, j] = jnp.full((tq, tk), MASKED_LOGIT, mask_ref.dtype)
        return c

    lax.fori_loop(jd + 1, nkt, write_hidden, 0)


def _dsa_flash_kernel(q_ref, *refs, tq, tk, hp, n_heads, past, has_past):
    k_new, v_new, k_past, v_past, rest = _split_kv_refs(refs, has_past)
    mask_ref, bnear_ref = rest[0], rest[1]
    o_ref, m_ref, l_ref, acc_ref = rest[-4], rest[-3], rest[-2], rest[-1]
    read_k = _tile_reader(k_new, k_past, tk=tk, hp=hp, n_heads=n_heads)
    read_v = _tile_reader(v_new, v_past, tk=tk, hp=hp, n_heads=n_heads)

    i = pl.program_id(2)
    jd = (past + i * tq) // tk
    reps = tk // HEAD_DIM
    heads = [slice(hh * HEAD_DIM, (hh + 1) * HEAD_DIM) for hh in range(hp)]

    m_ref[...] = jnp.full_like(m_ref, MASKED_LOGIT)
    l_ref[...] = jnp.zeros_like(l_ref)
    acc_ref[...] = jnp.zeros_like(acc_ref)

    def step(j, near, n=1):
        diag = near == (0,)
        maskf = jnp.concatenate([mask_ref[j + t].astype(F32) for t in range(n)], axis=1)
        logits = []
        for hh, hs in enumerate(heads):
            s = _nt_dot(q_ref[:, hs], read_k(j, hh, diag, n)) + maskf
            if near is not None:
                s = s + jnp.concatenate([bnear_ref[hh, d] for d in near], axis=1)
            logits.append(s)
        probs, alphas = [], []
        for hs, s in zip(heads, logits):
            m_old = m_ref[:, hs]
            m_new = jnp.maximum(m_old, jnp.max(s, axis=-1, keepdims=True))
            alpha = jnp.exp2(m_old - m_new)
            p = jnp.exp2(s - jnp.concatenate([m_new] * (n * reps), axis=1))
            l_ref[:, hs] = alpha * l_ref[:, hs] + sum(p[:, r * HEAD_DIM:(r + 1) * HEAD_DIM]
                                                      for r in range(n * reps))
            m_ref[:, hs] = m_new
            probs.append(p.astype(BF16))
            alphas.append(alpha)
        for hh, (hs, p, alpha) in enumerate(zip(heads, probs, alphas)):
            pv = jnp.dot(p, read_v(j, hh, diag, n), preferred_element_type=F32)
            acc_ref[:, hs] = alpha * acc_ref[:, hs] + pv

    if has_past:
        step(jd, (0,))

        @pl.when(jd >= 1)
        def _():
            step(jd - 1, (1,))
    else:
        @pl.when(jd >= 1)
        def _():
            step(jd - 1, (1, 0), 2)

        @pl.when(jd == 0)
        def _():
            step(jd, (0,))

    n_far = jnp.maximum(jd - 1, 0)
    n_trips = n_far // FAR_TILES_PER_TRIP

    def single(j, c):
        step(j, None)
        return c

    lax.fori_loop(n_trips * FAR_TILES_PER_TRIP, n_far, single, 0)

    def body(t, c):
        step(FAR_TILES_PER_TRIP * t, None, FAR_TILES_PER_TRIP)
        return c

    lax.fori_loop(0, n_trips, body, 0)
    for hs in heads:
        o_ref[:, hs] = (acc_ref[:, hs] / jnp.sum(l_ref[:, hs], axis=-1, keepdims=True)).astype(o_ref.dtype)


def _t5_bucket(rel):
    nb = N_BUCKETS // 2
    max_exact = nb // 2
    side = jnp.where(rel > 0, nb, 0)
    n = jnp.abs(rel)
    nf = jnp.maximum(n, 1).astype(F32)
    large = max_exact + (jnp.log(nf / max_exact) / math.log(MAX_DISTANCE / max_exact) * (nb - max_exact)).astype(jnp.int32)
    large = jnp.minimum(large, nb - 1)
    return side + jnp.where(n < max_exact, n, large)


def _near_bias(rel_bias, *, tq, tk):
    r = jnp.arange(tq, dtype=jnp.int32)[:, None]
    c = jnp.arange(tk, dtype=jnp.int32)[None, :]
    bucket = _t5_bucket(jnp.stack([c - r, c - r - tk]))
    far_bucket = _t5_bucket(jnp.int32(-2 * tk))
    table = (rel_bias - rel_bias[far_bucket][None, :]) * LOG2E
    onehot = (bucket[..., None] == jnp.arange(N_BUCKETS, dtype=jnp.int32)).astype(F32)
    return jnp.einsum("dqkb,bh->hdqk", onehot, table, precision=lax.Precision.HIGHEST)


def _dsa_attention(grp, proj, tail, cols, k_idx_all, k_cache, v_cache, rel_bias, o_prev, o_shape, o_col, *, hp):
    H = k_cache.shape[2]
    tq, tk, nq = _attn_tiles(grp)
    s_pad = k_idx_all.shape[1]
    nkt = s_pad // tk
    assert tk % CHUNK == 0 and tq % CHUNK == 0 and s_pad % tk == 0
    assert tk >= MAX_DISTANCE and H % hp == 0
    topk = min(TOPK_MAX, (grp.past + grp.seq) // 4)
    q_blk0 = grp.row0 // tq
    qiw = N_IDX_HEADS * IDX_DIM

    kt = jnp.transpose(k_idx_all.reshape(grp.batch, nkt, tk, IDX_DIM), (0, 1, 3, 2))
    z = jnp.zeros_like(kt)
    ki2 = jnp.concatenate([jnp.concatenate([kt, z], axis=3), jnp.concatenate([z, kt], axis=3)], axis=2)
    resident = pl.Buffered(1) if nq > 1 else None

    by_batch = nq == 1
    nsub = max(SELECT_ROWS_PER_STEP // tq, 1)
    while (grp.batch if by_batch else nq) % nsub or (grp.row0 // tq) % nsub or (not by_batch and tq != tk):
        nsub //= 2
    rows = nsub * tq
    sel_blk0 = grp.row0 // rows
    if by_batch:
        grid = (grp.batch // nsub, 1)
        row_map = lambda b, i: sel_blk0 + b
        ki2_spec = pl.BlockSpec((nsub, nkt, 2 * IDX_DIM, 2 * tk), lambda b, i: (b, 0, 0, 0))
        mask_spec = pl.BlockSpec((nsub, None, nkt, tq, tk), lambda b, i: (b, 0, 0, 0, 0))
    else:
        grid = (grp.batch, nq // nsub)
        row_map = lambda b, i: sel_blk0 + b * (nq // nsub) + i
        ki2_spec = pl.BlockSpec((1, nkt, 2 * IDX_DIM, 2 * tk), lambda b, i: (b, 0, 0, 0), pipeline_mode=resident)
        mask_spec = pl.BlockSpec((None, nsub, nkt, tq, tk), lambda b, i: (b, i, 0, 0, 0))

    mask = pl.pallas_call(
        functools.partial(_dsa_select_kernel, tq=tq, tk=tk, nkt=nkt, nsub=nsub, by_batch=by_batch,
                          past=grp.past, topk=topk),
        grid=grid,
        in_specs=[
            pl.BlockSpec((rows, qiw), lambda b, i: (row_map(b, i), cols[0] // qiw)),
            pl.BlockSpec((rows, HEAD_DIM), lambda b, i: (row_map(b, i), 0)),
            ki2_spec,
        ],
        out_specs=mask_spec,
        out_shape=jax.ShapeDtypeStruct((grp.batch, nq, nkt, tq, tk), BF16),
        scratch_shapes=[
            pltpu.VMEM((nkt + 1, rows, tk), jnp.int32),
            pltpu.VMEM((N_IDX_HEADS, rows, HEAD_DIM), F32),
        ],
        compiler_params=_params("parallel", "parallel"),
        name="dsa_select",
    )(proj, tail, ki2)

    wp = hp * HEAD_DIM
    kv_args, kv_specs = _kv_operands(grp, proj, cols[2], cols[3], k_cache, v_cache, tk=tk, hp=hp)
    prev_args, prev_specs, aliased = _chained(o_prev)
    n_in = 1 + len(kv_args) + 2
    return pl.pallas_call(
        functools.partial(_dsa_flash_kernel, tq=tq, tk=tk, hp=hp, n_heads=H, past=grp.past, has_past=grp.past > 0),
        grid=(grp.batch, H // hp, nq),
        in_specs=[pl.BlockSpec((tq, wp), lambda b, h, i: (q_blk0 + b * nq + i, cols[1] // wp + h))]
        + kv_specs
        + [pl.BlockSpec((None, None, nkt, tq, tk), lambda b, h, i: (b, i, 0, 0, 0)),
           pl.BlockSpec((hp, 2, tq, tk), lambda b, h, i: (h, 0, 0, 0), pipeline_mode=resident)]
        + prev_specs,
        out_specs=pl.BlockSpec((tq, wp), lambda b, h, i: (q_blk0 + b * nq + i, o_col // wp + h)),
        out_shape=jax.ShapeDtypeStruct(o_shape, BF16),
        input_output_aliases={n_in: 0} if aliased else {},
        scratch_shapes=[pltpu.VMEM((tq, wp), F32), pltpu.VMEM((tq, wp), F32), pltpu.VMEM((tq, wp), F32)],
        compiler_params=_params("parallel", "parallel", "arbitrary"),
        name="dsa_flash",
    )(proj, *kv_args, mask, _near_bias(rel_bias, tq=tq, tk=tk), *prev_args)


def _cross_kernel(q_ref, k_ref, v_ref, *rest, n_heads, scale):
    o_ref = rest[-1]
    for hh in range(n_heads):
        sl = slice(hh * HEAD_DIM, (hh + 1) * HEAD_DIM)
        s = _nt_dot(q_ref[:, sl], k_ref[:, sl]) * scale
        p = jnp.exp(s - jnp.max(s, axis=-1, keepdims=True))
        denom = jnp.sum(p, axis=-1, keepdims=True)
        o = jnp.dot(p.astype(BF16), v_ref[:, sl], preferred_element_type=F32)
        o_ref[:, sl] = (o / denom).astype(o_ref.dtype)


def _cross_attention(grp, q, mem_k, mem_v, o_prev, *, tq=512):
    M, W = q.shape
    n_mem = mem_k.shape[1]
    tq = _pick_tile(grp.seq, tq)
    nq = grp.seq // tq
    assert grp.row0 % tq == 0
    blk0 = grp.row0 // tq
    prev_args, prev_specs, aliased = _chained(o_prev)
    return pl.pallas_call(
        functools.partial(_cross_kernel, n_heads=W // HEAD_DIM, scale=HEAD_DIM ** -0.5),
        grid=(grp.batch, nq),
        in_specs=[
            pl.BlockSpec((tq, W), lambda b, i: (blk0 + b * nq + i, 0)),
            pl.BlockSpec((None, n_mem, W), lambda b, i: (b, 0, 0)),
            pl.BlockSpec((None, n_mem, W), lambda b, i: (b, 0, 0)),
        ] + prev_specs,
        out_specs=pl.BlockSpec((tq, W), lambda b, i: (blk0 + b * nq + i, 0)),
        out_shape=jax.ShapeDtypeStruct((M, W), BF16),
        input_output_aliases={3: 0} if aliased else {},
        compiler_params=_params("parallel", "parallel"),
        name="cross_attention",
    )(q, mem_k, mem_v, *prev_args)


CONV_ROW_CHUNK = 128
SUBLANES = 8


def _conv_kernel(a_ref, prev_ref, state_ref, w_ref, bias_ref, g_ref, *rest, tt):
    o_ref, buf_ref, q_ref, y_ref = rest[-4], rest[-3], rest[-2], rest[-1]
    D = a_ref.shape[-1]

    @pl.when(pl.program_id(1) == 0)
    def _():
        buf_ref[0:CONV_HALO, :] = state_ref[...]

    @pl.when(pl.program_id(1) > 0)
    def _():
        buf_ref[0:CONV_HALO, :] = prev_ref[...]

    buf_ref[CONV_HALO:CONV_HALO + tt, :] = a_ref[...]
    buf_ref[CONV_HALO + tt:CONV_HALO + tt + SUBLANES, :] = jnp.zeros((SUBLANES, D), F32)

    first = CONV_HALO - (CONV_WIDTH - 1)
    rc = min(CONV_ROW_CHUNK, tt)
    for c0 in range(0, D, HEAD_DIM):
        cs = slice(c0, c0 + HEAD_DIM)
        for t0 in range(0, tt, rc):
            out = jnp.zeros((rc, HEAD_DIM), F32)
            for r in range(SUBLANES):
                q = None
                for k in range(CONV_WIDTH):
                    if (first + k) % SUBLANES != r:
                        continue
                    base = t0 + first + k - r
                    term = buf_ref[base:base + rc + SUBLANES, cs] * w_ref[k:k + 1, cs]
                    q = term if q is None else q + term
                if r == 0:
                    out = out + q[0:rc]
                else:
                    q_ref[...] = q
                    out = out + q_ref[r:r + rc, :]
            y_ref[t0:t0 + rc, cs] = out + bias_ref[:, cs]
    y = _rms_rows(y_ref[...], g_ref[...])
    o_ref[...] = (y * (1.0 / (1.0 + jnp.exp(-y)))).astype(o_ref.dtype)


def _conv_norm_swish(grp, u, state, w_dw, b_dw, g_norm, o_prev):
    M, D = u.shape
    tt = _pick_tile(grp.seq, 256)
    nt = grp.seq // tt
    assert grp.seq % tt == 0 and tt % CONV_HALO == 0 and grp.row0 % tt == 0
    blk0 = grp.row0 // tt
    hb = tt // CONV_HALO
    prev_args, prev_specs, aliased = _chained(o_prev)
    return pl.pallas_call(
        functools.partial(_conv_kernel, tt=tt),
        grid=(grp.batch, nt),
        in_specs=[
            pl.BlockSpec((tt, D), lambda b, i: (blk0 + b * nt + i, 0)),
            pl.BlockSpec((CONV_HALO, D), lambda b, i: (jnp.maximum((blk0 + b * nt + i) * hb - 1, 0), 0)),
            pl.BlockSpec((None, CONV_HALO, D), lambda b, i: (b, 0, 0)),
            pl.BlockSpec((CONV_WIDTH, D), lambda b, i: (0, 0)),
            pl.BlockSpec((1, D), lambda b, i: (0, 0)),
            pl.BlockSpec((1, D), lambda b, i: (0, 0)),
        ] + prev_specs,
        out_specs=pl.BlockSpec((tt, D), lambda b, i: (blk0 + b * nt + i, 0)),
        out_shape=jax.ShapeDtypeStruct((M, D), BF16),
        input_output_aliases={6: 0} if aliased else {},
        scratch_shapes=[pltpu.VMEM((tt + CONV_HALO + SUBLANES, D), F32),
                        pltpu.VMEM((min(CONV_ROW_CHUNK, tt) + SUBLANES, HEAD_DIM), F32),
                        pltpu.VMEM((tt, D), F32)],
        compiler_params=_params("parallel", "arbitrary"),
        name="conv_norm_swish",
    )(u, u, state, w_dw, b_dw.reshape(1, D), g_norm.reshape(1, D), *prev_args)


def kernel(x_prompt, x_sample, mem_prompt, cache_sb_k, cache_sb_v, cache_dsa_k, cache_dsa_v, cache_idx_k, cache_mem_k, cache_mem_v, state_conv, norm_mix, norm_cross, norm_mlp, norm_final, w_in_ab, w_out_ab, rel_bias, w_pw1, b_pw1, w_dw, b_dw, g_conv_norm, w_pw2, b_pw2, w_cq, w_mk, w_mv, w_co, w_up, w_down):
    Bp, Tp, D = x_prompt.shape
    Bs, Ts, _ = x_sample.shape
    past = cache_sb_k.shape[2]
    depth = norm_mix.shape[0]
    n_mem = mem_prompt.shape[1]
    n_sb, n_dsa = cache_sb_k.shape[3], cache_dsa_k.shape[3]
    sbw, dsw = n_sb * HEAD_DIM, n_dsa * HEAD_DIM
    qiw = N_IDX_HEADS * IDX_DIM
    cw = w_cq.shape[2]
    Mp, Ms = Bp * Tp, Bs * Ts
    M = Mp + Ms
    prompt = _Group(Bp, Tp, 0, 0)
    sample = _Group(Bs, Ts, Mp, past)

    x = jnp.concatenate([x_prompt.reshape(Mp, D), x_sample.reshape(Ms, D)], axis=0)
    mem_flat = mem_prompt.reshape(Bp * n_mem, D)

    src = dict(zip(("q_sb", "k_sb", "v_sb", "q_d", "k_d", "v_d", "q_i", "k_i", "w_i"),
                   np.cumsum([0, sbw, sbw, sbw, dsw, dsw, dsw, qiw, IDX_DIM])))
    order = ("k_sb", "v_sb", "k_d", "v_d", "q_sb", "q_d", "q_i")
    width = dict(q_sb=sbw, k_sb=sbw, v_sb=sbw, q_d=dsw, k_d=dsw, v_d=dsw, q_i=qiw)
    col = dict(zip(order, np.cumsum([0] + [width[n] for n in order])[:-1]))
    col = {k: int(v) for k, v in col.items()}
    n32 = 2 * sbw + 2 * dsw

    outs = {k: [] for k in ("p_sb_k", "p_sb_v", "p_d_k", "p_d_v", "p_ki", "p_mem_k", "p_mem_v", "p_conv",
                            "s_sb_k", "s_sb_v", "s_d_k", "s_d_v", "s_ki", "s_conv")}

    def split_rows(a):
        return a[:Mp].reshape(Bp, Tp, -1), a[Mp:].reshape(Bs, Ts, -1)

    def last_rows(a, grp, state, n):
        if grp.seq >= n:
            return jnp.stack([a[grp.row0 + (b + 1) * grp.seq - n:grp.row0 + (b + 1) * grp.seq]
                              for b in range(grp.batch)])
        rows = a[grp.row0:grp.row0 + grp.batch * grp.seq].reshape(grp.batch, grp.seq, -1)
        return jnp.concatenate([state, rows], axis=1)[:, -n:]

    w_out16, w_pw1_16, w_pw2_16 = w_out_ab.astype(BF16), w_pw1.astype(BF16), w_pw2.astype(BF16)
    w_cq16, w_mk16, w_mv16, w_co16 = (w.astype(BF16) for w in (w_cq, w_mk, w_mv, w_co))
    w_up16, w_down16 = w_up.astype(BF16), w_down.astype(BF16)
    q_scale = HEAD_DIM ** -0.5 * LOG2E
    col_scale = jnp.concatenate([jnp.full((width[n],), q_scale if n in ("q_sb", "q_d") else 1.0, F32)
                                 for n in order])

    for l in range(depth):
        i = l // 2
        if l % 2 == 0:
            w_in = w_in_ab[i]
            w_main = jnp.concatenate([w_in[:, src[n]:src[n] + width[n]] for n in order], axis=1).astype(BF16)
            tail_w = w_in.shape[1] - int(src["k_i"])
            w_tail = jnp.pad(w_in[:, src["k_i"]:], ((0, 0), (0, HEAD_DIM - tail_w))).astype(BF16)
            proj, kv32, tail = _in_proj(x, w_main, norm_mix[l], col_scale, w_tail, n32=n32)

            for name, key, nh in (("sb_k", "k_sb", n_sb), ("sb_v", "v_sb", n_sb),
                                  ("d_k", "k_d", n_dsa), ("d_v", "v_d", n_dsa)):
                ap, as_ = split_rows(kv32[:, col[key]:col[key] + width[key]])
                outs["p_" + name].append(ap.reshape(Bp, Tp, nh, HEAD_DIM))
                outs["s_" + name].append(as_.reshape(Bs, Ts, nh, HEAD_DIM))
            k_i = tail[:, :IDX_DIM]
            kip, kis = split_rows(k_i)
            outs["p_ki"].append(kip)
            outs["s_ki"].append(kis)

            tk_s = ATTN_TILE
            s_pad_s = -(-(past + Ts) // tk_s) * tk_s
            kidx_p = kip.astype(BF16)
            kidx_s = jnp.concatenate([cache_idx_k[i].astype(BF16), kis.astype(BF16),
                                      jnp.zeros((Bs, s_pad_s - past - Ts, IDX_DIM), BF16)], axis=1)

            o_shape = (M, sbw + dsw)
            sb_cols = (col["q_sb"], col["k_sb"], col["v_sb"])
            d_cols = (col["q_i"], col["q_d"], col["k_d"], col["v_d"])
            o = _sb_attention(prompt, proj, sb_cols, cache_sb_k[i], cache_sb_v[i], None, o_shape, 0,
                              hp=SB_HEADS_PER_STEP)
            o = _dsa_attention(prompt, proj, tail, d_cols, kidx_p, cache_dsa_k[i], cache_dsa_v[i], rel_bias,
                               o, o_shape, sbw, hp=DSA_HEADS_PER_STEP)
            o = _sb_attention(sample, proj, sb_cols, cache_sb_k[i], cache_sb_v[i], o, o_shape, 0,
                              hp=SB_HEADS_PER_STEP)
            o = _dsa_attention(sample, proj, tail, d_cols, kidx_s, cache_dsa_k[i], cache_dsa_v[i], rel_bias,
                               o, o_shape, sbw, hp=n_dsa)
            x, qc = _residual_matmul_then_proj(o, w_out16, i, x, None, norm_cross[l], w_cq16, l)
        else:
            u = _matmul(x, w_pw1_16, layer=i, gain=norm_mix[l], bias=b_pw1[i], glu=True)
            front = CONV_HALO - (CONV_WIDTH - 1)
            state_p = jnp.zeros((Bp, CONV_HALO, D), F32)
            state_s = jnp.concatenate([jnp.zeros((Bs, front, D), F32), state_conv[i]], axis=1)
            y = _conv_norm_swish(prompt, u, state_p, w_dw[i], b_dw[i], g_conv_norm[i], None)
            y = _conv_norm_swish(sample, u, state_s, w_dw[i], b_dw[i], g_conv_norm[i], y)
            outs["p_conv"].append(last_rows(u, prompt, state_p, CONV_WIDTH - 1))
            outs["s_conv"].append(last_rows(u, sample, state_s, CONV_WIDTH - 1))
            x, qc = _residual_matmul_then_proj(y, w_pw2_16, i, x, b_pw2[i], norm_cross[l], w_cq16, l)

        mk = _matmul(mem_flat, w_mk16, layer=l)
        mv = _matmul(mem_flat, w_mv16, layer=l)
        outs["p_mem_k"].append(mk.reshape(Bp, n_mem, cw // HEAD_DIM, HEAD_DIM))
        outs["p_mem_v"].append(mv.reshape(Bp, n_mem, cw // HEAD_DIM, HEAD_DIM))
        oc = _cross_attention(prompt, qc, mk.astype(BF16).reshape(Bp, n_mem, cw),
                              mv.astype(BF16).reshape(Bp, n_mem, cw), None)
        oc = _cross_attention(sample, qc, cache_mem_k[l].astype(BF16).reshape(Bs, n_mem, cw),
                              cache_mem_v[l].astype(BF16).reshape(Bs, n_mem, cw), oc)
        if l < depth - 1:
            x = _cross_out_mlp(x, oc, w_co16, norm_mlp[l], w_up16, w_down16, l)
        else:
            y_p, y_s = _cross_out_mlp(x, oc, w_co16, norm_mlp[l], w_up16, w_down16, l,
                                      final_gain=norm_final, m_prompt=Mp)

    st = lambda k: jnp.stack(outs[k])
    return (y_p.reshape(Bp, Tp, D), y_s.reshape(Bs, Ts, D), st("p_sb_k"), st("p_sb_v"), st("p_d_k"), st("p_d_v"),
            st("p_ki"), st("p_mem_k"), st("p_mem_v"), st("p_conv"), st("s_sb_k"), st("s_sb_v"), st("s_d_k"),
            st("s_d_v"), st("s_ki"), st("s_conv"))
```

```python
import functools
import math
from typing import NamedTuple

import numpy as np
import jax
import jax.numpy as jnp
from jax import lax
from jax.experimental import pallas as pl
from jax.experimental.pallas import tpu as pltpu

F32 = jnp.float32
BF16 = jnp.bfloat16

HEAD_DIM = 128
IDX_DIM = 64
N_IDX_HEADS = 16
CHUNK = 64
TOPK_MAX = 256
N_BUCKETS = 32
MAX_DISTANCE = 128
CONV_WIDTH = 31
CONV_HALO = 32
RMS_EPS = 1e-6
MASKED_LOGIT = -1e30
LOG2E = math.log2(math.e)
_INT32_MIN = -2 ** 31
VMEM_LIMIT_BYTES = 56 * 1024 * 1024
SB_HEADS_PER_STEP = 8
DSA_HEADS_PER_STEP = 4
SB_DEAD_BITS = 160.0
SB_INTERLEAVE = 4
ATTN_TILE = 256
SELECT_ROWS_PER_STEP = 512
FAR_TILES_PER_TRIP = 4


class _Group(NamedTuple):
    batch: int
    seq: int
    row0: int
    past: int


def _params(*semantics):
    return pltpu.CompilerParams(dimension_semantics=semantics, vmem_limit_bytes=VMEM_LIMIT_BYTES)


def _pick_tile(n, pref):
    if n <= pref:
        return n
    t = pref
    while n % t:
        t //= 2
    return t


def _row_tile(m, pref):
    t = min(m, pref) // 8 * 8
    while m % t:
        t -= 8
    return t


def _chained(prev):
    if prev is None:
        return [], [], None
    return [prev], [pl.BlockSpec(memory_space=pl.ANY)], True


def _rms_rows(x, g):
    ms = jnp.mean(x * x, axis=-1, keepdims=True)
    return x * lax.rsqrt(ms + RMS_EPS) * g


def _mm_kernel(*refs, has_norm, has_bias, has_res, glu):
    it = iter(refs)
    x_ref = next(it)
    g_ref = next(it) if has_norm else None
    w_ref = next(it)
    w2_ref = next(it) if glu else None
    b_ref = next(it) if has_bias else None
    b2_ref = next(it) if (has_bias and glu) else None
    r_ref = next(it) if has_res else None
    o_ref = next(it)
    xn_ref = next(it) if has_norm else None

    if has_norm:
        @pl.when(pl.program_id(1) == 0)
        def _():
            xn_ref[...] = _rms_rows(x_ref[...], g_ref[...]).astype(BF16)
        xb = xn_ref[...]
    else:
        xb = x_ref[...].astype(BF16)

    y = jnp.dot(xb, w_ref[...], preferred_element_type=F32)
    if has_bias:
        y = y + b_ref[...]
    if glu:
        gate = jnp.dot(xb, w2_ref[...], preferred_element_type=F32)
        if has_bias:
            gate = gate + b2_ref[...]
        y = y * (1.0 / (1.0 + jnp.exp(-gate)))
    if has_res:
        y = y + r_ref[...]
    o_ref[...] = y.astype(o_ref.dtype)


def _matmul(x, w, *, layer=0, gain=None, bias=None, residual=None, glu=False, out_dtype=F32, tm=1152, tn=1024):
    M, K = x.shape
    n_out = w.shape[2] // 2 if glu else w.shape[2]
    tm = _row_tile(M, tm)
    tn = _pick_tile(n_out, tn)
    nj = n_out // tn
    has_norm, has_bias, has_res = gain is not None, bias is not None, residual is not None

    args = [x]
    specs = [pl.BlockSpec((tm, K), lambda i, j: (i, 0))]
    if has_norm:
        args.append(gain.reshape(1, K))
        specs.append(pl.BlockSpec((1, K), lambda i, j: (0, 0)))
    args.append(w)
    specs.append(pl.BlockSpec((None, K, tn), lambda i, j: (layer, 0, j)))
    if glu:
        args.append(w)
        specs.append(pl.BlockSpec((None, K, tn), lambda i, j: (layer, 0, j + nj)))
    if has_bias:
        b2d = bias.reshape(1, -1)
        args.append(b2d)
        specs.append(pl.BlockSpec((1, tn), lambda i, j: (0, j)))
        if glu:
            args.append(b2d)
            specs.append(pl.BlockSpec((1, tn), lambda i, j: (0, j + nj)))
    if has_res:
        args.append(residual)
        specs.append(pl.BlockSpec((tm, tn), lambda i, j: (i, j)))

    return pl.pallas_call(
        functools.partial(_mm_kernel, has_norm=has_norm, has_bias=has_bias, has_res=has_res, glu=glu),
        grid=(M // tm, nj),
        in_specs=specs,
        out_specs=pl.BlockSpec((tm, tn), lambda i, j: (i, j)),
        out_shape=jax.ShapeDtypeStruct((M, n_out), out_dtype),
        scratch_shapes=[pltpu.VMEM((tm, K), BF16)] if has_norm else [],
        compiler_params=_params("parallel", "arbitrary"),
        name="norm_matmul",
    )(*args)


def _res_proj_kernel(y_ref, w_ref, *refs, has_bias):
    it = iter(refs)
    b_ref = next(it) if has_bias else None
    r_ref, g_ref, wq_ref, o_ref, q_ref, x1_ref = (next(it) for _ in range(6))
    j = pl.program_id(1)
    y = jnp.dot(y_ref[...], w_ref[...], preferred_element_type=F32) + r_ref[...]
    if has_bias:
        y = y + b_ref[...]
    o_ref[...] = y
    x1_ref[j] = y

    @pl.when(j == pl.num_programs(1) - 1)
    def _():
        x1 = jnp.concatenate([x1_ref[t] for t in range(x1_ref.shape[0])], axis=1)
        q_ref[...] = jnp.dot(_rms_rows(x1, g_ref[...]).astype(BF16), wq_ref[...],
                             preferred_element_type=F32).astype(q_ref.dtype)


def _residual_matmul_then_proj(y, w, layer, residual, bias, gain, wq, q_layer, *, tm=1152, tn=1024):
    M, K = y.shape
    N, nq = w.shape[2], wq.shape[2]
    tm = _row_tile(M, tm)
    assert N % tn == 0
    nj = N // tn
    has_bias = bias is not None
    args = [y, w]
    specs = [pl.BlockSpec((tm, K), lambda i, j: (i, 0)), pl.BlockSpec((None, K, tn), lambda i, j: (layer, 0, j))]
    if has_bias:
        args.append(bias.reshape(1, N))
        specs.append(pl.BlockSpec((1, tn), lambda i, j: (0, j)))
    args += [residual, gain.reshape(1, N), wq]
    specs += [pl.BlockSpec((tm, tn), lambda i, j: (i, j)), pl.BlockSpec((1, N), lambda i, j: (0, 0)),
              pl.BlockSpec((None, N, nq), lambda i, j: (q_layer, 0, 0))]
    return pl.pallas_call(
        functools.partial(_res_proj_kernel, has_bias=has_bias),
        grid=(M // tm, nj),
        in_specs=specs,
        out_specs=[pl.BlockSpec((tm, tn), lambda i, j: (i, j)), pl.BlockSpec((tm, nq), lambda i, j: (i, 0))],
        out_shape=[jax.ShapeDtypeStruct((M, N), F32), jax.ShapeDtypeStruct((M, nq), BF16)],
        scratch_shapes=[pltpu.VMEM((nj, tm, tn), F32)],
        compiler_params=_params("parallel", "arbitrary"),
        name="residual_matmul_proj",
    )(*args)


def _in_proj_kernel(x_ref, g_ref, w_ref, s_ref, wt_ref, o16_ref, o32_ref, ot_ref, xn_ref, *, n32_tiles):
    j = pl.program_id(1)

    @pl.when(j == 0)
    def _():
        xn_ref[...] = _rms_rows(x_ref[...], g_ref[...]).astype(BF16)
        ot_ref[...] = jnp.dot(xn_ref[...], wt_ref[...], preferred_element_type=F32)

    y = jnp.dot(xn_ref[...], w_ref[...], preferred_element_type=F32)
    o16_ref[...] = (y * s_ref[...]).astype(BF16)

    @pl.when(j < n32_tiles)
    def _():
        o32_ref[...] = y


def _in_proj(x, w, gain, col_scale, w_tail, *, n32, tm=1152, tn=1024):
    M, K = x.shape
    N = w.shape[1]
    nt = w_tail.shape[1]
    tm = _row_tile(M, tm)
    assert N % tn == 0 and n32 % tn == 0
    n32_tiles = n32 // tn
    return pl.pallas_call(
        functools.partial(_in_proj_kernel, n32_tiles=n32_tiles),
        grid=(M // tm, N // tn),
        in_specs=[
            pl.BlockSpec((tm, K), lambda i, j: (i, 0)),
            pl.BlockSpec((1, K), lambda i, j: (0, 0)),
            pl.BlockSpec((K, tn), lambda i, j: (0, j)),
            pl.BlockSpec((1, tn), lambda i, j: (0, j)),
            pl.BlockSpec((K, nt), lambda i, j: (0, 0)),
        ],
        out_specs=[
            pl.BlockSpec((tm, tn), lambda i, j: (i, j)),
            pl.BlockSpec((tm, tn), lambda i, j: (i, jnp.minimum(j, n32_tiles - 1))),
            pl.BlockSpec((tm, nt), lambda i, j: (i, 0)),
        ],
        out_shape=[jax.ShapeDtypeStruct((M, N), BF16), jax.ShapeDtypeStruct((M, n32), F32),
                   jax.ShapeDtypeStruct((M, nt), F32)],
        scratch_shapes=[pltpu.VMEM((tm, K), BF16)],
        compiler_params=_params("parallel", "arbitrary"),
        name="in_proj",
    )(x, gain.reshape(1, K), w, col_scale.reshape(1, N), w_tail)


def _mlp_kernel(x_ref, oc_ref, wco_ref, g_ref, wu_ref, wd_ref, *refs, n_prompt_tiles):
    final = n_prompt_tiles is not None
    if final:
        gf_ref, op_ref, os_ref, x2_ref, xn_ref, acc_ref = refs
    else:
        o_ref, x2_ref, xn_ref, acc_ref = refs
    i, f = pl.program_id(0), pl.program_id(1)

    @pl.when(f == 0)
    def _():
        x2 = x_ref[...] + jnp.dot(oc_ref[...], wco_ref[...], preferred_element_type=F32)
        x2_ref[...] = x2
        xn_ref[...] = _rms_rows(x2, g_ref[...]).astype(BF16)
        acc_ref[...] = jnp.zeros_like(acc_ref)

    h = jnp.dot(xn_ref[...], wu_ref[...], preferred_element_type=F32)
    h = jnp.maximum(h, 0.0)
    h = (h * h).astype(BF16)
    acc_ref[...] += jnp.dot(h, wd_ref[...], preferred_element_type=F32)

    last = f == pl.num_programs(1) - 1
    if not final:
        @pl.when(last)
        def _():
            o_ref[...] = x2_ref[...] + acc_ref[...]
    else:
        @pl.when(last & (i < n_prompt_tiles))
        def _():
            op_ref[...] = _rms_rows(x2_ref[...] + acc_ref[...], gf_ref[...])

        @pl.when(last & (i >= n_prompt_tiles))
        def _():
            os_ref[...] = _rms_rows(x2_ref[...] + acc_ref[...], gf_ref[...])


def _cross_out_mlp(x, oc, w_co, gain, w_up, w_down, layer, *, final_gain=None, m_prompt=None, tm=512, tf=512):
    M, D = x.shape
    F = w_up.shape[2]
    cw = oc.shape[1]
    final = final_gain is not None
    tm = _pick_tile(M - m_prompt, tm) if final else _pick_tile(M, tm)
    tf = _pick_tile(F, tf if final else 2 * tf)
    args = [x, oc, w_co, gain.reshape(1, D), w_up, w_down]
    specs = [
        pl.BlockSpec((tm, D), lambda i, f: (i, 0)),
        pl.BlockSpec((tm, cw), lambda i, f: (i, 0)),
        pl.BlockSpec((None, cw, D), lambda i, f: (layer, 0, 0)),
        pl.BlockSpec((1, D), lambda i, f: (0, 0)),
        pl.BlockSpec((None, D, tf), lambda i, f: (layer, 0, f)),
        pl.BlockSpec((None, tf, D), lambda i, f: (layer, f, 0)),
    ]
    if final:
        assert m_prompt % tm == 0 and (M - m_prompt) % tm == 0
        npt = m_prompt // tm
        args.append(final_gain.reshape(1, D))
        specs.append(pl.BlockSpec((1, D), lambda i, f: (0, 0)))
        out_specs = [pl.BlockSpec((tm, D), lambda i, f: (jnp.minimum(i, npt - 1), 0)),
                     pl.BlockSpec((tm, D), lambda i, f: (jnp.maximum(i - npt, 0), 0))]
        out_shape = [jax.ShapeDtypeStruct((m_prompt, D), F32), jax.ShapeDtypeStruct((M - m_prompt, D), F32)]
        semantics = ("arbitrary", "arbitrary")
    else:
        npt = None
        out_specs = pl.BlockSpec((tm, D), lambda i, f: (i, 0))
        out_shape = jax.ShapeDtypeStruct((M, D), F32)
        semantics = ("parallel", "arbitrary")
    return pl.pallas_call(
        functools.partial(_mlp_kernel, n_prompt_tiles=npt),
        grid=(M // tm, F // tf),
        in_specs=specs,
        out_specs=out_specs,
        out_shape=out_shape,
        scratch_shapes=[pltpu.VMEM((tm, D), F32), pltpu.VMEM((tm, D), BF16), pltpu.VMEM((tm, D), F32)],
        compiler_params=_params(*semantics),
        name="cross_out_mlp",
    )(*args)


def _nt_dot(a, b):
    return lax.dot_general(a, b, (((1,), (1,)), ((), ())), preferred_element_type=F32)


def _kv_operands(grp, proj, k_col, v_col, k_cache, v_cache, *, tk, hp):
    wp = hp * HEAD_DIM
    if grp.past == 0:
        blk0 = grp.row0 // grp.seq
        args = [proj, proj]
        specs = [pl.BlockSpec((grp.seq, wp), lambda b, h, i, c=k_col // wp: (blk0 + b, c + h),
                              pipeline_mode=pl.Buffered(1)),
                 pl.BlockSpec((grp.seq, wp), lambda b, h, i, c=v_col // wp: (blk0 + b, c + h),
                              pipeline_mode=pl.Buffered(1))]
        return args, specs

    def new_tile(col):
        rows = lax.slice(proj, (grp.row0, col), (grp.row0 + grp.batch * grp.seq, col + k_cache.shape[2] * HEAD_DIM))
        rows = rows.reshape(grp.batch, grp.seq, -1)
        return jnp.pad(rows, ((0, 0), (0, tk - grp.seq), (0, 0)))

    args = [new_tile(k_col), new_tile(v_col)]
    specs = [pl.BlockSpec((None, tk, wp), lambda b, h, i: (b, 0, h))] * 2
    n_heads = k_cache.shape[2]
    for cache in (k_cache, v_cache):
        args.append(cache.reshape(grp.batch, grp.past * n_heads, HEAD_DIM))
        specs.append(pl.BlockSpec((None, grp.past * n_heads, HEAD_DIM), lambda b, h, i: (b, 0, 0)))
    return args, specs


def _split_kv_refs(refs, has_past):
    if not has_past:
        return refs[0], refs[1], None, None, refs[2:]
    return refs[0], refs[1], refs[2], refs[3], refs[4:]


def _tile_reader(new_ref, past_ref, *, tk, hp, n_heads):
    def read(j, hh, diag, n=1):
        hs = slice(hh * HEAD_DIM, (hh + 1) * HEAD_DIM)
        if past_ref is None:
            return new_ref[pl.ds(pl.multiple_of(j * tk, tk), n * tk), hs]
        if diag:
            return new_ref[:, hs]
        head = pl.program_id(1) * hp + hh
        return past_ref[pl.ds(j * (tk * n_heads) + head, n * tk, stride=n_heads), :].astype(BF16)
    return read


def _sb_kernel(q_ref, *refs, tq, tk, hp, n_heads, past, has_past):
    k_new, v_new, k_past, v_past, rest = _split_kv_refs(refs, has_past)
    u_ref, o_ref, acc_ref, carry_ref = rest[0], rest[-3], rest[-2], rest[-1]
    read_k = _tile_reader(k_new, k_past, tk=tk, hp=hp, n_heads=n_heads)
    read_v = _tile_reader(v_new, v_past, tk=tk, hp=hp, n_heads=n_heads)

    i = pl.program_id(2)
    q_start = past + i * tq
    jd = q_start // tk
    reps = tk // HEAD_DIM

    acc_ref[...] = jnp.zeros_like(acc_ref)
    carry_ref[...] = jnp.zeros_like(carry_ref)

    def step(j, masked):
        if masked:
            t_pos = q_start + lax.broadcasted_iota(jnp.int32, (tq, tk), 0)
            s_pos = j * tk + lax.broadcasted_iota(jnp.int32, (tq, tk), 1)
            causal = t_pos > s_pos
        for h0 in range(0, hp, SB_INTERLEAVE):
            group_step(j, masked, causal if masked else None, range(h0, h0 + SB_INTERLEAVE))

    def group_step(j, masked, causal, group):
        heads = [(hh, slice(hh * HEAD_DIM, (hh + 1) * HEAD_DIM)) for hh in group]
        z2s = [_nt_dot(q_ref[:, hs], read_k(j, hh, masked)) for hh, hs in heads]
        ns, log2_betas, css = [], [], []
        for z2 in z2s:
            neg_abs = lax.bitcast_convert_type(
                lax.bitcast_convert_type(z2, jnp.int32) | jnp.int32(_INT32_MIN), F32)
            n = jnp.maximum(z2, 0.0) + jnp.log(1.0 + jnp.exp2(neg_abs)) * LOG2E
            log2_betas.append(z2 - n)
            if masked:
                n = jnp.where(causal, n, 0.0)
            n16 = n.astype(BF16)
            css.append(jnp.dot(n16, u_ref[...], preferred_element_type=F32))
            ns.append(n16)
        for (hh, hs), n16, log2_beta, cs in zip(heads, ns, log2_betas, css):
            a = jnp.exp2(log2_beta - cs)
            if masked:
                a = jnp.where(causal, a, 0.0)
            carry = carry_ref[:, hs]
            pv = jnp.dot(a.astype(BF16), read_v(j, hh, masked), preferred_element_type=F32)
            acc_ref[:, hs] += jnp.exp2(-carry) * pv
            carry_ref[:, hs] = carry + jnp.broadcast_to(cs[:, 0:1] + n16[:, 0:1].astype(F32), (tq, HEAD_DIM))

    step(jd, True)

    def live(state):
        t, min_carry = state
        return (t < jd) & (min_carry < SB_DEAD_BITS)

    def body(state):
        t, _ = state
        step(jd - 1 - t, False)
        return t + 1, jnp.min(carry_ref[...])

    lax.while_loop(live, body, (jnp.int32(0), jnp.min(carry_ref[...])))
    o_ref[...] = acc_ref[...].astype(o_ref.dtype)


def _suffix_sum_matrix(tk):
    r = np.arange(tk)
    return jnp.asarray((r[:, None] > r[None, :]).astype(np.float32), dtype=BF16)


def _attn_tiles(grp):
    tq = _pick_tile(grp.seq, ATTN_TILE)
    tk = ATTN_TILE if grp.past else tq
    nq = grp.seq // tq
    assert grp.seq % tq == 0 and tk % tq == 0 and grp.past % tk == 0 and grp.row0 % tq == 0
    assert grp.past == 0 or nq == 1
    assert grp.past > 0 or grp.row0 % grp.seq == 0
    return tq, tk, nq


def _sb_attention(grp, proj, cols, k_cache, v_cache, o_prev, o_shape, o_col, *, hp):
    H = k_cache.shape[2]
    tq, tk, nq = _attn_tiles(grp)
    wp = hp * HEAD_DIM
    q_blk0 = grp.row0 // tq
    kv_args, kv_specs = _kv_operands(grp, proj, cols[1], cols[2], k_cache, v_cache, tk=tk, hp=hp)
    prev_args, prev_specs, aliased = _chained(o_prev)
    n_in = 1 + len(kv_args) + 1
    return pl.pallas_call(
        functools.partial(_sb_kernel, tq=tq, tk=tk, hp=hp, n_heads=H, past=grp.past, has_past=grp.past > 0),
        grid=(grp.batch, H // hp, nq),
        in_specs=[pl.BlockSpec((tq, wp), lambda b, h, i: (q_blk0 + b * nq + i, cols[0] // wp + h))]
        + kv_specs
        + [pl.BlockSpec((tk, tk), lambda b, h, i: (0, 0), pipeline_mode=pl.Buffered(1))]
        + prev_specs,
        out_specs=pl.BlockSpec((tq, wp), lambda b, h, i: (q_blk0 + b * nq + i, o_col // wp + h)),
        out_shape=jax.ShapeDtypeStruct(o_shape, BF16),
        input_output_aliases={n_in: 0} if aliased else {},
        scratch_shapes=[pltpu.VMEM((tq, wp), F32), pltpu.VMEM((tq, wp), F32)],
        compiler_params=_params("parallel", "parallel", "arbitrary"),
        name="sb_attention",
    )(proj, *kv_args, _suffix_sum_matrix(tk), *prev_args)


def _sortable_key(x):
    b = lax.bitcast_convert_type(x, jnp.int32)
    return b ^ ((b >> 31) & jnp.int32(0x7FFFFFFF))


_KEY_NEG_INF = int(np.int32(np.array(-np.inf, np.float32).view(np.int32) ^ 0x7FFFFFFF))
_KEY_LOWEST_FINITE = int(np.int32(np.array(np.finfo(np.float32).min, np.float32).view(np.int32) ^ 0x7FFFFFFF))
COUNT_ROW_GROUP = 128


def _dsa_select_kernel(qi_ref, wi_ref, ki2_ref, mask_ref, keys_ref, wb_ref,
                       *, tq, tk, nkt, nsub, by_batch, past, topk):
    i = pl.program_id(1)
    rows = nsub * tq
    first_start = past + i * tq * (1 if by_batch else nsub)
    jd_min = first_start // tk
    n_cut = 1 if by_batch else nsub
    jd = jd_min + n_cut - 1
    reps = tk // HEAD_DIM

    w = wi_ref[:, IDX_DIM:IDX_DIM + N_IDX_HEADS] * (IDX_DIM ** -0.5 * N_IDX_HEADS ** -0.5)
    for hh in range(N_IDX_HEADS):
        wb_ref[hh] = jnp.broadcast_to(w[:, hh:hh + 1], (rows, HEAD_DIM))

    def idx_rows(j, r0, nr, kt):
        tot = jnp.zeros((nr, tk), F32)
        for p in range(N_IDX_HEADS // 2):
            sc = jnp.dot(qi_ref[r0:r0 + nr, p * 2 * IDX_DIM:(p + 1) * 2 * IDX_DIM], kt,
                         preferred_element_type=F32)
            w0 = jnp.concatenate([wb_ref[2 * p, r0:r0 + nr, :]] * reps, axis=1)
            w1 = jnp.concatenate([wb_ref[2 * p + 1, r0:r0 + nr, :]] * reps, axis=1)
            tot = tot + w0 * jnp.maximum(sc[:, :tk], 0.0) + w1 * jnp.maximum(sc[:, tk:], 0.0)
        return _sortable_key(tot)

    def idx_tile(j):
        if by_batch:
            return jnp.concatenate([idx_rows(j, s * tq, tq, ki2_ref[s, j]) for s in range(nsub)], axis=0)
        return idx_rows(j, 0, rows, ki2_ref[0, j])

    def idx_body(j, c):
        keys_ref[j] = idx_tile(j)
        return c

    lax.fori_loop(0, jd_min, idx_body, 0)
    row = lax.broadcasted_iota(jnp.int32, (rows, tk), 0)
    t_pos = first_start + (row % tq if by_batch else row)
    for d in range(n_cut):
        s_pos = (jd_min + d) * tk + lax.broadcasted_iota(jnp.int32, (rows, tk), 1)
        visible = (t_pos // CHUNK) >= (s_pos // CHUNK)
        keys_ref[jd_min + d] = jnp.where(visible, idx_tile(jd_min + d), jnp.int32(_KEY_NEG_INF))

    n_vis = jd + 1

    @pl.when(n_vis % 2 == 1)
    def _():
        keys_ref[n_vis] = jnp.full((rows, tk), _INT32_MIN, jnp.int32)

    n_pairs = (n_vis + 1) // 2
    rg = min(COUNT_ROW_GROUP, rows)

    def count_ge(cand):
        parts = []
        for g in range(rows // rg):
            cg = cand[g * rg:(g + 1) * rg]

            def body(t, cnt, g=g, cg=cg):
                for jj in (2 * t, 2 * t + 1):
                    kj = keys_ref[jj, pl.ds(g * rg, rg), :]
                    for r in range(reps):
                        cnt = cnt + jnp.where(kj[:, r * HEAD_DIM:(r + 1) * HEAD_DIM] >= cg, 1.0, 0.0)
                return cnt

            parts.append(lax.fori_loop(0, n_pairs, body, jnp.zeros((rg, HEAD_DIM), F32)))
        return jnp.sum(jnp.concatenate(parts, axis=0), axis=-1, keepdims=True)

    kf = float(topk)
    zero = jnp.zeros((rows, HEAD_DIM), jnp.int32)
    c = count_ge(zero)
    res = jnp.where(c >= kf, zero, jnp.int32(_INT32_MIN))
    cnt = jnp.where(c >= kf, c, float(tk) * (2 * n_pairs).astype(F32))
    for bit in range(30, -1, -1):
        cand = res + jnp.int32(1 << bit)
        c = count_ge(cand)
        res = jnp.where(c >= kf, cand, res)
        cnt = jnp.where(c >= kf, c, cnt)

    tied_cut = jnp.max(jnp.where((cnt != kf) & (res[:, 0:1] > jnp.int32(_KEY_NEG_INF)), 1.0, 0.0)) > 0.0

    @pl.when(tied_cut)
    def _():
        col = lax.broadcasted_iota(jnp.int32, (rg, tk), 1)

        def count_tied(before, strict_above):
            parts = []
            for g in range(rows // rg):
                grp_rows = slice(g * rg, (g + 1) * rg)
                res_g = jnp.concatenate([res[grp_rows]] * reps, axis=1)
                bef_g = None if strict_above else jnp.concatenate([before[grp_rows]] * reps, axis=1)

                def body(j, acc, g=g, res_g=res_g, bef_g=bef_g):
                    kj = keys_ref[j, pl.ds(g * rg, rg), :]
                    if strict_above:
                        hit = jnp.where(kj > res_g, 1.0, 0.0)
                    else:
                        hit = jnp.where(kj == res_g, jnp.where(j * tk + col < bef_g, 1.0, 0.0), 0.0)
                    return acc + sum(hit[:, r * HEAD_DIM:(r + 1) * HEAD_DIM] for r in range(reps))

                parts.append(lax.fori_loop(0, n_vis, body, jnp.zeros((rg, HEAD_DIM), F32)))
            return jnp.sum(jnp.concatenate(parts, axis=0), axis=-1, keepdims=True)

        need = kf - count_tied(None, True)
        r_keep = jnp.zeros((rows, HEAD_DIM), jnp.int32)
        for bit in range((nkt * tk).bit_length() - 1, -1, -1):
            cand = r_keep | jnp.int32(1 << bit)
            r_keep = jnp.where(count_tied(cand, False) < need, cand, r_keep)

        def drop_late_ties(j, c):
            for g in range(rows // rg):
                grp_rows = slice(g * rg, (g + 1) * rg)
                res_g = jnp.concatenate([res[grp_rows]] * reps, axis=1)
                keep_g = jnp.concatenate([r_keep[grp_rows]] * reps, axis=1)
                kj = keys_ref[j, pl.ds(g * rg, rg), :]
                keys_ref[j, pl.ds(g * rg, rg), :] = jnp.where(
                    kj == res_g, jnp.where(j * tk + col > keep_g, res_g - 1, kj), kj)
            return c

        lax.fori_loop(0, n_vis, drop_late_ties, 0)

    thr = jnp.maximum(res, jnp.int32(_KEY_LOWEST_FINITE))
    thr = jnp.concatenate([thr] * reps, axis=1)

    def write_mask(j, c):
        tile = jnp.where(keys_ref[j] >= thr, 0.0, MASKED_LOGIT).astype(mask_ref.dtype)
        for s_ in range(nsub):
            mask_ref[s_, j] = tile[s_ * tq:(s_ + 1) * tq]
        return c

    lax.fori_loop(0, jd + 1, write_mask, 0)

    def write_hidden(j, c):
        for s_ in range(nsub):
            mask_ref[s_, j] = jnp.full((tq, tk), MASKED_LOGIT, mask_ref.dtype)
        return c

    lax.fori_loop(jd + 1, nkt, write_hidden, 0)


def _dsa_flash_kernel(q_ref, *refs, tq, tk, hp, n_heads, past, has_past):
    k_new, v_new, k_past, v_past, rest = _split_kv_refs(refs, has_past)
    mask_ref, bnear_ref = rest[0], rest[1]
    o_ref, m_ref, l_ref, acc_ref = rest[-4], rest[-3], rest[-2], rest[-1]
    read_k = _tile_reader(k_new, k_past, tk=tk, hp=hp, n_heads=n_heads)
    read_v = _tile_reader(v_new, v_past, tk=tk, hp=hp, n_heads=n_heads)

    i = pl.program_id(2)
    jd = (past + i * tq) // tk
    reps = tk // HEAD_DIM
    heads = [slice(hh * HEAD_DIM, (hh + 1) * HEAD_DIM) for hh in range(hp)]

    m_ref[...] = jnp.full_like(m_ref, MASKED_LOGIT)
    l_ref[...] = jnp.zeros_like(l_ref)
    acc_ref[...] = jnp.zeros_like(acc_ref)

    def step(j, near, n=1):
        diag = near == (0,)
        maskf = jnp.concatenate([mask_ref[j + t].astype(F32) for t in range(n)], axis=1)
        logits = []
        for hh, hs in enumerate(heads):
            s = _nt_dot(q_ref[:, hs], read_k(j, hh, diag, n)) + maskf
            if near is not None:
                s = s + jnp.concatenate([bnear_ref[hh, d] for d in near], axis=1)
            logits.append(s)
        probs, alphas = [], []
        for hs, s in zip(heads, logits):
            m_old = m_ref[:, hs]
            m_new = jnp.maximum(m_old, jnp.max(s, axis=-1, keepdims=True))
            alpha = jnp.exp2(m_old - m_new)
            p = jnp.exp2(s - jnp.concatenate([m_new] * (n * reps), axis=1))
            l_ref[:, hs] = alpha * l_ref[:, hs] + sum(p[:, r * HEAD_DIM:(r + 1) * HEAD_DIM]
                                                      for r in range(n * reps))
            m_ref[:, hs] = m_new
            probs.append(p.astype(BF16))
            alphas.append(alpha)
        for hh, (hs, p, alpha) in enumerate(zip(heads, probs, alphas)):
            pv = jnp.dot(p, read_v(j, hh, diag, n), preferred_element_type=F32)
            acc_ref[:, hs] = alpha * acc_ref[:, hs] + pv

    if has_past:
        step(jd, (0,))

        @pl.when(jd >= 1)
        def _():
            step(jd - 1, (1,))
    else:
        @pl.when(jd >= 1)
        def _():
            step(jd - 1, (1, 0), 2)

        @pl.when(jd == 0)
        def _():
            step(jd, (0,))

    n_far = jnp.maximum(jd - 1, 0)
    n_trips = n_far // FAR_TILES_PER_TRIP

    def single(j, c):
        step(j, None)
        return c

    lax.fori_loop(n_trips * FAR_TILES_PER_TRIP, n_far, single, 0)

    def body(t, c):
        step(FAR_TILES_PER_TRIP * t, None, FAR_TILES_PER_TRIP)
        return c

    lax.fori_loop(0, n_trips, body, 0)
    for hs in heads:
        o_ref[:, hs] = (acc_ref[:, hs] / jnp.sum(l_ref[:, hs], axis=-1, keepdims=True)).astype(o_ref.dtype)


def _t5_bucket(rel):
    nb = N_BUCKETS // 2
    max_exact = nb // 2
    side = jnp.where(rel > 0, nb, 0)
    n = jnp.abs(rel)
    nf = jnp.maximum(n, 1).astype(F32)
    large = max_exact + (jnp.log(nf / max_exact) / math.log(MAX_DISTANCE / max_exact) * (nb - max_exact)).astype(jnp.int32)
    large = jnp.minimum(large, nb - 1)
    return side + jnp.where(n < max_exact, n, large)


def _near_bias(rel_bias, *, tq, tk):
    r = jnp.arange(tq, dtype=jnp.int32)[:, None]
    c = jnp.arange(tk, dtype=jnp.int32)[None, :]
    bucket = _t5_bucket(jnp.stack([c - r, c - r - tk]))
    far_bucket = _t5_bucket(jnp.int32(-2 * tk))
    table = (rel_bias - rel_bias[far_bucket][None, :]) * LOG2E
    onehot = (bucket[..., None] == jnp.arange(N_BUCKETS, dtype=jnp.int32)).astype(F32)
    return jnp.einsum("dqkb,bh->hdqk", onehot, table, precision=lax.Precision.HIGHEST)


def _dsa_attention(grp, proj, tail, cols, k_idx_all, k_cache, v_cache, rel_bias, o_prev, o_shape, o_col, *, hp):
    H = k_cache.shape[2]
    tq, tk, nq = _attn_tiles(grp)
    s_pad = k_idx_all.shape[1]
    nkt = s_pad // tk
    assert tk % CHUNK == 0 and tq % CHUNK == 0 and s_pad % tk == 0
    assert tk >= MAX_DISTANCE and H % hp == 0
    topk = min(TOPK_MAX, (grp.past + grp.seq) // 4)
    q_blk0 = grp.row0 // tq
    qiw = N_IDX_HEADS * IDX_DIM

    kt = jnp.transpose(k_idx_all.reshape(grp.batch, nkt, tk, IDX_DIM), (0, 1, 3, 2))
    z = jnp.zeros_like(kt)
    ki2 = jnp.concatenate([jnp.concatenate([kt, z], axis=3), jnp.concatenate([z, kt], axis=3)], axis=2)
    resident = pl.Buffered(1) if nq > 1 else None

    by_batch = nq == 1
    nsub = max(SELECT_ROWS_PER_STEP // tq, 1)
    while (grp.batch if by_batch else nq) % nsub or (grp.row0 // tq) % nsub or (not by_batch and tq != tk):
        nsub //= 2
    rows = nsub * tq
    sel_blk0 = grp.row0 // rows
    if by_batch:
        grid = (grp.batch // nsub, 1)
        row_map = lambda b, i: sel_blk0 + b
        ki2_spec = pl.BlockSpec((nsub, nkt, 2 * IDX_DIM, 2 * tk), lambda b, i: (b, 0, 0, 0))
        mask_spec = pl.BlockSpec((nsub, None, nkt, tq, tk), lambda b, i: (b, 0, 0, 0, 0))
    else:
        grid = (grp.batch, nq // nsub)
        row_map = lambda b, i: sel_blk0 + b * (nq // nsub) + i
        ki2_spec = pl.BlockSpec((1, nkt, 2 * IDX_DIM, 2 * tk), lambda b, i: (b, 0, 0, 0), pipeline_mode=resident)
        mask_spec = pl.BlockSpec((None, nsub, nkt, tq, tk), lambda b, i: (b, i, 0, 0, 0))

    mask = pl.pallas_call(
        functools.partial(_dsa_select_kernel, tq=tq, tk=tk, nkt=nkt, nsub=nsub, by_batch=by_batch,
                          past=grp.past, topk=topk),
        grid=grid,
        in_specs=[
            pl.BlockSpec((rows, qiw), lambda b, i: (row_map(b, i), cols[0] // qiw)),
            pl.BlockSpec((rows, HEAD_DIM), lambda b, i: (row_map(b, i), 0)),
            ki2_spec,
        ],
        out_specs=mask_spec,
        out_shape=jax.ShapeDtypeStruct((grp.batch, nq, nkt, tq, tk), BF16),
        scratch_shapes=[
            pltpu.VMEM((nkt + 1, rows, tk), jnp.int32),
            pltpu.VMEM((N_IDX_HEADS, rows, HEAD_DIM), F32),
        ],
        compiler_params=_params("parallel", "parallel"),
        name="dsa_select",
    )(proj, tail, ki2)

    wp = hp * HEAD_DIM
    kv_args, kv_specs = _kv_operands(grp, proj, cols[2], cols[3], k_cache, v_cache, tk=tk, hp=hp)
    prev_args, prev_specs, aliased = _chained(o_prev)
    n_in = 1 + len(kv_args) + 2
    return pl.pallas_call(
        functools.partial(_dsa_flash_kernel, tq=tq, tk=tk, hp=hp, n_heads=H, past=grp.past, has_past=grp.past > 0),
        grid=(grp.batch, H // hp, nq),
        in_specs=[pl.BlockSpec((tq, wp), lambda b, h, i: (q_blk0 + b * nq + i, cols[1] // wp + h))]
        + kv_specs
        + [pl.BlockSpec((None, None, nkt, tq, tk), lambda b, h, i: (b, i, 0, 0, 0)),
           pl.BlockSpec((hp, 2, tq, tk), lambda b, h, i: (h, 0, 0, 0), pipeline_mode=resident)]
        + prev_specs,
        out_specs=pl.BlockSpec((tq, wp), lambda b, h, i: (q_blk0 + b * nq + i, o_col // wp + h)),
        out_shape=jax.ShapeDtypeStruct(o_shape, BF16),
        input_output_aliases={n_in: 0} if aliased else {},
        scratch_shapes=[pltpu.VMEM((tq, wp), F32), pltpu.VMEM((tq, wp), F32), pltpu.VMEM((tq, wp), F32)],
        compiler_params=_params("parallel", "parallel", "arbitrary"),
        name="dsa_flash",
    )(proj, *kv_args, mask, _near_bias(rel_bias, tq=tq, tk=tk), *prev_args)


def _cross_kernel(q_ref, k_ref, v_ref, *rest, n_heads, scale):
    o_ref = rest[-1]
    for hh in range(n_heads):
        sl = slice(hh * HEAD_DIM, (hh + 1) * HEAD_DIM)
        s = _nt_dot(q_ref[:, sl], k_ref[:, sl]) * scale
        p = jnp.exp(s - jnp.max(s, axis=-1, keepdims=True))
        denom = jnp.sum(p, axis=-1, keepdims=True)
        o = jnp.dot(p.astype(BF16), v_ref[:, sl], preferred_element_type=F32)
        o_ref[:, sl] = (o / denom).astype(o_ref.dtype)


def _cross_attention(grp, q, mem_k, mem_v, o_prev, *, tq=512):
    M, W = q.shape
    n_mem = mem_k.shape[1]
    tq = _pick_tile(grp.seq, tq)
    nq = grp.seq // tq
    assert grp.row0 % tq == 0
    blk0 = grp.row0 // tq
    prev_args, prev_specs, aliased = _chained(o_prev)
    return pl.pallas_call(
        functools.partial(_cross_kernel, n_heads=W // HEAD_DIM, scale=HEAD_DIM ** -0.5),
        grid=(grp.batch, nq),
        in_specs=[
            pl.BlockSpec((tq, W), lambda b, i: (blk0 + b * nq + i, 0)),
            pl.BlockSpec((None, n_mem, W), lambda b, i: (b, 0, 0)),
            pl.BlockSpec((None, n_mem, W), lambda b, i: (b, 0, 0)),
        ] + prev_specs,
        out_specs=pl.BlockSpec((tq, W), lambda b, i: (blk0 + b * nq + i, 0)),
        out_shape=jax.ShapeDtypeStruct((M, W), BF16),
        input_output_aliases={3: 0} if aliased else {},
        compiler_params=_params("parallel", "parallel"),
        name="cross_attention",
    )(q, mem_k, mem_v, *prev_args)


CONV_ROW_CHUNK = 128
SUBLANES = 8


def _conv_kernel(a_ref, prev_ref, state_ref, w_ref, bias_ref, g_ref, *rest, tt):
    o_ref, buf_ref, q_ref, y_ref = rest[-4], rest[-3], rest[-2], rest[-1]
    D = a_ref.shape[-1]

    @pl.when(pl.program_id(1) == 0)
    def _():
        buf_ref[0:CONV_HALO, :] = state_ref[...]

    @pl.when(pl.program_id(1) > 0)
    def _():
        buf_ref[0:CONV_HALO, :] = prev_ref[...]

    buf_ref[CONV_HALO:CONV_HALO + tt, :] = a_ref[...]
    buf_ref[CONV_HALO + tt:CONV_HALO + tt + SUBLANES, :] = jnp.zeros((SUBLANES, D), F32)

    first = CONV_HALO - (CONV_WIDTH - 1)
    rc = min(CONV_ROW_CHUNK, tt)
    for c0 in range(0, D, HEAD_DIM):
        cs = slice(c0, c0 + HEAD_DIM)
        for t0 in range(0, tt, rc):
            out = jnp.zeros((rc, HEAD_DIM), F32)
            for r in range(SUBLANES):
                q = None
                for k in range(CONV_WIDTH):
                    if (first + k) % SUBLANES != r:
                        continue
                    base = t0 + first + k - r
                    term = buf_ref[base:base + rc + SUBLANES, cs] * w_ref[k:k + 1, cs]
                    q = term if q is None else q + term
                if r == 0:
                    out = out + q[0:rc]
                else:
                    q_ref[...] = q
                    out = out + q_ref[r:r + rc, :]
            y_ref[t0:t0 + rc, cs] = out + bias_ref[:, cs]
    y = _rms_rows(y_ref[...], g_ref[...])
    o_ref[...] = (y * (1.0 / (1.0 + jnp.exp(-y)))).astype(o_ref.dtype)


def _conv_norm_swish(grp, u, state, w_dw, b_dw, g_norm, o_prev):
    M, D = u.shape
    tt = _pick_tile(grp.seq, 256)
    nt = grp.seq // tt
    assert grp.seq % tt == 0 and tt % CONV_HALO == 0 and grp.row0 % tt == 0
    blk0 = grp.row0 // tt
    hb = tt // CONV_HALO
    prev_args, prev_specs, aliased = _chained(o_prev)
    return pl.pallas_call(
        functools.partial(_conv_kernel, tt=tt),
        grid=(grp.batch, nt),
        in_specs=[
            pl.BlockSpec((tt, D), lambda b, i: (blk0 + b * nt + i, 0)),
            pl.BlockSpec((CONV_HALO, D), lambda b, i: (jnp.maximum((blk0 + b * nt + i) * hb - 1, 0), 0)),
            pl.BlockSpec((None, CONV_HALO, D), lambda b, i: (b, 0, 0)),
            pl.BlockSpec((CONV_WIDTH, D), lambda b, i: (0, 0)),
            pl.BlockSpec((1, D), lambda b, i: (0, 0)),
            pl.BlockSpec((1, D), lambda b, i: (0, 0)),
        ] + prev_specs,
        out_specs=pl.BlockSpec((tt, D), lambda b, i: (blk0 + b * nt + i, 0)),
        out_shape=jax.ShapeDtypeStruct((M, D), BF16),
        input_output_aliases={6: 0} if aliased else {},
        scratch_shapes=[pltpu.VMEM((tt + CONV_HALO + SUBLANES, D), F32),
                        pltpu.VMEM((min(CONV_ROW_CHUNK, tt) + SUBLANES, HEAD_DIM), F32),
                        pltpu.VMEM((tt, D), F32)],
        compiler_params=_params("parallel", "arbitrary"),
        name="conv_norm_swish",
    )(u, u, state, w_dw, b_dw.reshape(1, D), g_norm.reshape(1, D), *prev_args)


def kernel(x_prompt, x_sample, mem_prompt, cache_sb_k, cache_sb_v, cache_dsa_k, cache_dsa_v, cache_idx_k, cache_mem_k, cache_mem_v, state_conv, norm_mix, norm_cross, norm_mlp, norm_final, w_in_ab, w_out_ab, rel_bias, w_pw1, b_pw1, w_dw, b_dw, g_conv_norm, w_pw2, b_pw2, w_cq, w_mk, w_mv, w_co, w_up, w_down):
    Bp, Tp, D = x_prompt.shape
    Bs, Ts, _ = x_sample.shape
    past = cache_sb_k.shape[2]
    depth = norm_mix.shape[0]
    n_mem = mem_prompt.shape[1]
    n_sb, n_dsa = cache_sb_k.shape[3], cache_dsa_k.shape[3]
    sbw, dsw = n_sb * HEAD_DIM, n_dsa * HEAD_DIM
    qiw = N_IDX_HEADS * IDX_DIM
    cw = w_cq.shape[2]
    Mp, Ms = Bp * Tp, Bs * Ts
    M = Mp + Ms
    prompt = _Group(Bp, Tp, 0, 0)
    sample = _Group(Bs, Ts, Mp, past)

    x = jnp.concatenate([x_prompt.reshape(Mp, D), x_sample.reshape(Ms, D)], axis=0)
    mem_flat = mem_prompt.reshape(Bp * n_mem, D)

    src = dict(zip(("q_sb", "k_sb", "v_sb", "q_d", "k_d", "v_d", "q_i", "k_i", "w_i"),
                   np.cumsum([0, sbw, sbw, sbw, dsw, dsw, dsw, qiw, IDX_DIM])))
    order = ("k_sb", "v_sb", "k_d", "v_d", "q_sb", "q_d", "q_i")
    width = dict(q_sb=sbw, k_sb=sbw, v_sb=sbw, q_d=dsw, k_d=dsw, v_d=dsw, q_i=qiw)
    col = dict(zip(order, np.cumsum([0] + [width[n] for n in order])[:-1]))
    col = {k: int(v) for k, v in col.items()}
    n32 = 2 * sbw + 2 * dsw

    outs = {k: [] for k in ("p_sb_k", "p_sb_v", "p_d_k", "p_d_v", "p_ki", "p_mem_k", "p_mem_v", "p_conv",
                            "s_sb_k", "s_sb_v", "s_d_k", "s_d_v", "s_ki", "s_conv")}

    def split_rows(a):
        return a[:Mp].reshape(Bp, Tp, -1), a[Mp:].reshape(Bs, Ts, -1)

    def last_rows(a, grp, state, n):
        if grp.seq >= n:
            return jnp.stack([a[grp.row0 + (b + 1) * grp.seq - n:grp.row0 + (b + 1) * grp.seq]
                              for b in range(grp.batch)])
        rows = a[grp.row0:grp.row0 + grp.batch * grp.seq].reshape(grp.batch, grp.seq, -1)
        return jnp.concatenate([state, rows], axis=1)[:, -n:]

    w_out16, w_pw1_16, w_pw2_16 = w_out_ab.astype(BF16), w_pw1.astype(BF16), w_pw2.astype(BF16)
    w_cq16, w_mk16, w_mv16, w_co16 = (w.astype(BF16) for w in (w_cq, w_mk, w_mv, w_co))
    w_up16, w_down16 = w_up.astype(BF16), w_down.astype(BF16)
    q_scale = HEAD_DIM ** -0.5 * LOG2E
    col_scale = jnp.concatenate([jnp.full((width[n],), q_scale if n in ("q_sb", "q_d") else 1.0, F32)
                                 for n in order])

    for l in range(depth):
        i = l // 2
        if l % 2 == 0:
            w_in = w_in_ab[i]
            w_main = jnp.concatenate([w_in[:, src[n]:src[n] + width[n]] for n in order], axis=1).astype(BF16)
            tail_w = w_in.shape[1] - int(src["k_i"])
            w_tail = jnp.pad(w_in[:, src["k_i"]:], ((0, 0), (0, HEAD_DIM - tail_w))).astype(BF16)
            proj, kv32, tail = _in_proj(x, w_main, norm_mix[l], col_scale, w_tail, n32=n32)

            for name, key, nh in (("sb_k", "k_sb", n_sb), ("sb_v", "v_sb", n_sb),
                                  ("d_k", "k_d", n_dsa), ("d_v", "v_d", n_dsa)):
                ap, as_ = split_rows(kv32[:, col[key]:col[key] + width[key]])
                outs["p_" + name].append(ap.reshape(Bp, Tp, nh, HEAD_DIM))
                outs["s_" + name].append(as_.reshape(Bs, Ts, nh, HEAD_DIM))
            k_i = tail[:, :IDX_DIM]
            kip, kis = split_rows(k_i)
            outs["p_ki"].append(kip)
            outs["s_ki"].append(kis)

            tk_s = ATTN_TILE
            s_pad_s = -(-(past + Ts) // tk_s) * tk_s
            kidx_p = kip.astype(BF16)
            kidx_s = jnp.concatenate([cache_idx_k[i].astype(BF16), kis.astype(BF16),
                                      jnp.zeros((Bs, s_pad_s - past - Ts, IDX_DIM), BF16)], axis=1)

            o_shape = (M, sbw + dsw)
            sb_cols = (col["q_sb"], col["k_sb"], col["v_sb"])
            d_cols = (col["q_i"], col["q_d"], col["k_d"], col["v_d"])
            o = _sb_attention(prompt, proj, sb_cols, cache_sb_k[i], cache_sb_v[i], None, o_shape, 0,
                              hp=SB_HEADS_PER_STEP)
            o = _dsa_attention(prompt, proj, tail, d_cols, kidx_p, cache_dsa_k[i], cache_dsa_v[i], rel_bias,
                               o, o_shape, sbw, hp=DSA_HEADS_PER_STEP)
            o = _sb_attention(sample, proj, sb_cols, cache_sb_k[i], cache_sb_v[i], o, o_shape, 0,
                              hp=SB_HEADS_PER_STEP)
            o = _dsa_attention(sample, proj, tail, d_cols, kidx_s, cache_dsa_k[i], cache_dsa_v[i], rel_bias,
                               o, o_shape, sbw, hp=n_dsa)
            x, qc = _residual_matmul_then_proj(o, w_out16, i, x, None, norm_cross[l], w_cq16, l)
        else:
            u = _matmul(x, w_pw1_16, layer=i, gain=norm_mix[l], bias=b_pw1[i], glu=True)
            front = CONV_HALO - (CONV_WIDTH - 1)
            state_p = jnp.zeros((Bp, CONV_HALO, D), F32)
            state_s = jnp.concatenate([jnp.zeros((Bs, front, D), F32), state_conv[i]], axis=1)
            y = _conv_norm_swish(prompt, u, state_p, w_dw[i], b_dw[i], g_conv_norm[i], None)
            y = _conv_norm_swish(sample, u, state_s, w_dw[i], b_dw[i], g_conv_norm[i], y)
            outs["p_conv"].append(last_rows(u, prompt, state_p, CONV_WIDTH - 1))
            outs["s_conv"].append(last_rows(u, sample, state_s, CONV_WIDTH - 1))
            x, qc = _residual_matmul_then_proj(y, w_pw2_16, i, x, b_pw2[i], norm_cross[l], w_cq16, l)

        mk = _matmul(mem_flat, w_mk16, layer=l)
        mv = _matmul(mem_flat, w_mv16, layer=l)
        outs["p_mem_k"].append(mk.reshape(Bp, n_mem, cw // HEAD_DIM, HEAD_DIM))
        outs["p_mem_v"].append(mv.reshape(Bp, n_mem, cw // HEAD_DIM, HEAD_DIM))
        oc = _cross_attention(prompt, qc, mk.astype(BF16).reshape(Bp, n_mem, cw),
                              mv.astype(BF16).reshape(Bp, n_mem, cw), None)
        oc = _cross_attention(sample, qc, cache_mem_k[l].astype(BF16).reshape(Bs, n_mem, cw),
                              cache_mem_v[l].astype(BF16).reshape(Bs, n_mem, cw), oc)
        if l < depth - 1:
            x = _cross_out_mlp(x, oc, w_co16, norm_mlp[l], w_up16, w_down16, l)
        else:
            y_p, y_s = _cross_out_mlp(x, oc, w_co16, norm_mlp[l], w_up16, w_down16, l,
                                      final_gain=norm_final, m_prompt=Mp)

    st = lambda k: jnp.stack(outs[k])
    return (y_p.reshape(Bp, Tp, D), y_s.reshape(Bs, Ts, D), st("p_sb_k"), st("p_sb_v"), st("p_d_k"), st("p_d_v"),
            st("p_ki"), st("p_mem_k"), st("p_mem_v"), st("p_conv"), st("s_sb_k"), st("s_sb_v"), st("s_d_k"),
            st("s_d_v"), st("s_ki"), st("s_conv"))
```
